```python
import jax, jax.numpy as jnp
from jax import lax
import numpy as np

D_MODEL = 1024
BATCH = 32
SEQ = 256
DEPTH = 2
DEC_BATCH = 4
DEC_SEQ = 1024
PAST_LEN = 256

GRID_W = 64
HEAD_DIM = 64
N_RWKV_HEADS = 8
RWKV_WIDTH = N_RWKV_HEADS * HEAD_DIM
N_Q_HEADS = 8
N_KV_HEADS = 2
GQA_GROUP = N_Q_HEADS // N_KV_HEADS
ATTN_WIDTH = N_Q_HEADS * HEAD_DIM
KV_WIDTH = N_KV_HEADS * HEAD_DIM
DECAY_RANK = 64
ICLR_RANK = 64
GATE_RANK = 128
D_FF = 4 * D_MODEL
Q_BLOCK = 128
ROPE_THETA = 10000.0
ROPE_FREQS = HEAD_DIM // 4
N_DIRS = 2
N_MOD = 6
NORM_EPS = 1e-6
GN_EPS = 64e-5
DECAY_SCALE = 0.606531
SPLIT_SIZES = (RWKV_WIDTH, RWKV_WIDTH, RWKV_WIDTH, DECAY_RANK, ICLR_RANK, GATE_RANK,
               ATTN_WIDTH, KV_WIDTH, KV_WIDTH, 2 * D_MODEL)
D_IN = 3 * RWKV_WIDTH + DECAY_RANK + ICLR_RANK + GATE_RANK + ATTN_WIDTH + 2 * KV_WIDTH + 2 * D_MODEL

kernel_name = "hybrid_rwkv7_gqa_dit_step"

F32 = jnp.float32


def rms_norm(x, g):
    xf = x.astype(F32)
    y = xf * lax.rsqrt(jnp.mean(xf * xf, axis=-1, keepdims=True) + NORM_EPS)
    return (y * g.astype(F32)).astype(x.dtype)


def centred_shift(z):
    zp = jnp.pad(z, ((0, 0), (1, 1), (0, 0)))
    return 0.5 * (zp[:, :-2] + zp[:, 2:])


def axial_rope_tables(n_rows):
    rows = jnp.repeat(jnp.arange(n_rows, dtype=F32), GRID_W)
    cols = jnp.tile(jnp.arange(GRID_W, dtype=F32), n_rows)
    half = HEAD_DIM // 2
    freqs = 1.0 / (ROPE_THETA ** (jnp.arange(0, half, 2, dtype=F32) / half))
    ang = jnp.stack([rows[:, None] * freqs, cols[:, None] * freqs], axis=1)
    return jnp.cos(ang), jnp.sin(ang)


def apply_rope(x, cos, sin):
    B, T, H, _ = x.shape
    xr = x.astype(F32).reshape(B, T, H, 2, 2, ROPE_FREQS)
    x1, x2 = xr[..., 0, :], xr[..., 1, :]
    c = cos[None, :, None]
    s = sin[None, :, None]
    out = jnp.stack([x1 * c - x2 * s, x2 * c + x1 * s], axis=-2)
    return out.reshape(B, T, H, HEAD_DIM).astype(x.dtype)


def block_attention(q, k, v):
    B, T = q.shape[0], q.shape[1]
    nb = T // Q_BLOCK
    qb = q.reshape(B, nb, Q_BLOCK, N_KV_HEADS, GQA_GROUP, HEAD_DIM).transpose(1, 0, 2, 3, 4, 5)
    scale = HEAD_DIM ** -0.5

    def one_block(q_blk):
        s = jnp.einsum('bqhgd,bkhd->bhgqk', q_blk, k).astype(F32) * scale
        p = jax.nn.softmax(s, axis=-1)
        return jnp.einsum('bhgqk,bkhd->bqhgd', p.astype(v.dtype), v)

    o = lax.map(one_block, qb)
    return o.transpose(1, 0, 2, 3, 4, 5).reshape(B, T, ATTN_WIDTH)


def wkv_scan(S0, r, w, kt, v, kh, a):
    xs = tuple(jnp.moveaxis(z.astype(F32), 2, 0) for z in (r, w, kt, v, kh, a))

    def step(S, inp):
        r_t, w_t, kt_t, v_t, kh_t, a_t = inp
        S_kh = jnp.einsum('bdhvk,bdhk->bdhv', S, kh_t)
        S = S * w_t[..., None, :] - S_kh[..., :, None] * (a_t * kh_t)[..., None, :] \
            + v_t[..., :, None] * kt_t[..., None, :]
        return S, jnp.einsum('bdhvk,bdhk->bdhv', S, r_t)

    S_fin, ys = lax.scan(step, S0.astype(F32), xs)
    return S_fin, jnp.moveaxis(ys, 0, 2)


def rwkv_branch(r_p, k_p, v_p, lw, la, lg, S0, P, l):
    B, T, C = r_p.shape
    dt = r_p.dtype
    mu = P['rwkv_mu'][l]
    r = r_p + mu[0] * (centred_shift(r_p) - r_p)
    k = k_p + mu[1] * (centred_shift(k_p) - k_p)
    v = v_p + mu[2] * (centred_shift(v_p) - v_p)
    heads = lambda z: z.reshape(z.shape[:-1] + (N_RWKV_HEADS, HEAD_DIM))
    kkf = heads(k * P['rwkv_k_k'][l]).astype(F32)
    kh = kkf * lax.rsqrt(jnp.sum(kkf * kkf, axis=-1, keepdims=True) + 1e-12)
    w_raw = jnp.einsum('btr,drc->bdtc', jnp.tanh(lw), P['decay_up'][l]) + P['decay_w0'][l][None, :, None, :]
    w = jnp.exp(-DECAY_SCALE * jax.nn.sigmoid(w_raw.astype(F32)))
    a = jax.nn.sigmoid((jnp.einsum('btr,drc->bdtc', la, P['iclr_up'][l])
                        + P['iclr_a0'][l][None, :, None, :]).astype(F32))
    kt = k[:, None].astype(F32) * (1.0 + (a - 1.0) * P['rwkv_k_a'][l].astype(F32))
    both = lambda z: jnp.stack([z, jnp.flip(z, axis=1)], axis=1)
    orient = lambda z: jnp.stack([z[:, 0], jnp.flip(z[:, 1], axis=1)], axis=1)
    rh, vh = heads(r), heads(v)
    kt_h = heads(kt)
    S_fin, y = wkv_scan(S0, both(rh), orient(heads(w)), orient(kt_h), both(vh), both(kh), orient(heads(a)))
    y = y[:, 0] + jnp.flip(y[:, 1], axis=1)
    mean = jnp.mean(y, axis=-1, keepdims=True)
    var = jnp.mean(jnp.square(y - mean), axis=-1, keepdims=True)
    yn = (y - mean) * lax.rsqrt(var + GN_EPS) * P['gn_w'][l].reshape(N_RWKV_HEADS, HEAD_DIM) \
        + P['gn_b'][l].reshape(N_RWKV_HEADS, HEAD_DIM)
    bonus = jnp.einsum('bthn,bdthn,hn->bth', rh.astype(F32), kt_h, P['rwkv_r_k'][l].astype(F32))[..., None] \
        * vh.astype(F32)
    g = jax.nn.sigmoid(lg) @ P['gate_up'][l]
    out = (yn + bonus).reshape(B, T, C).astype(dt) * g
    return out, S_fin.astype(dt)


def mixer(h, l, P, rope, ctx_k, ctx_v, S0):
    B, T, _ = h.shape
    proj = h @ P['w_in'][l]
    idx = np.cumsum(SPLIT_SIZES)[:-1].tolist()
    r_p, k_p, v_p, lw, la, lg, q_p, ka_p, va_p, gates = jnp.split(proj, idx, axis=-1)
    o_r, S_fin = rwkv_branch(r_p, k_p, v_p, lw, la, lg, S0, P, l)
    q = rms_norm(q_p.reshape(B, T, N_Q_HEADS, HEAD_DIM), P['q_gain'][l])
    k = rms_norm(ka_p.reshape(B, T, N_KV_HEADS, HEAD_DIM), P['k_gain'][l])
    v = va_p.reshape(B, T, N_KV_HEADS, HEAD_DIM)
    if rope is not None:
        q = apply_rope(q, rope[0], rope[1])
        k = apply_rope(k, rope[0], rope[1])
    k_own, v_own = k, v
    if ctx_k is not None:
        k = jnp.concatenate([ctx_k, k], axis=1)
        v = jnp.concatenate([ctx_v, v], axis=1)
    o_a = block_attention(q, k, v)
    g_r, g_a = jnp.split(gates, 2, axis=-1)
    merged = jax.nn.sigmoid(g_r) * (o_r @ P['w_br'][l, 0]) + jax.nn.sigmoid(g_a) * (o_a @ P['w_br'][l, 1])
    return merged @ P['w_out'][l], k_own, v_own, S_fin


def trunk_layer(x, cvec, l, P, rope, ctx_k, ctx_v, S0):
    mod = (jax.nn.silu(cvec) @ P['w_mod'][l] + P['b_mod'][l]).reshape(-1, 1, N_MOD * D_MODEL)
    sh1, sc1, g1, sh2, sc2, g2 = jnp.split(mod, N_MOD, axis=-1)
    ng = P['norm_g'][l]
    h = rms_norm(x, ng[0]) * (1.0 + sc1) + sh1
    m, k_new, v_new, S_fin = mixer(h, l, P, rope, ctx_k, ctx_v, S0)
    x = x + g1 * rms_norm(m, ng[1])
    h = rms_norm(x, ng[2]) * (1.0 + sc2) + sh2
    f = jnp.square(jax.nn.relu(h @ P['mlp_up'][l])) @ P['mlp_down'][l]
    x = x + g2 * rms_norm(f, ng[3])
    return x, k_new, v_new, S_fin


def setup_inputs(seed: int = 0) -> dict:
    key = jax.random.key(seed)
    ks = jax.random.split(key, 32)
    nrm = lambda k, shape, s: jax.random.normal(k, shape, F32) * s
    H, N, C = N_RWKV_HEADS, HEAD_DIM, RWKV_WIDTH
    return {
        'x_prompt': nrm(ks[0], (BATCH, SEQ, D_MODEL), 1.0),
        'x_sample': nrm(ks[1], (DEC_BATCH, DEC_SEQ, D_MODEL), 1.0),
        'cache_k': nrm(ks[2], (DEC_BATCH, DEPTH, PAST_LEN, N_KV_HEADS, HEAD_DIM), 1.0),
        'cache_v': nrm(ks[3], (DEC_BATCH, DEPTH, PAST_LEN, N_KV_HEADS, HEAD_DIM), 1.0),
        'state_wkv': nrm(ks[4], (DEC_BATCH, DEPTH, N_DIRS, H, N, N), 0.1),
        'c': nrm(ks[5], (DEC_BATCH, D_MODEL), 1.0),
        'c_ctx': nrm(ks[6], (D_MODEL,), 1.0),
        'w_in': nrm(ks[7], (DEPTH, D_MODEL, D_IN), D_MODEL ** -0.5),
        'w_br': nrm(ks[8], (DEPTH, 2, C, D_MODEL), C ** -0.5),
        'w_out': nrm(ks[9], (DEPTH, D_MODEL, D_MODEL), D_MODEL ** -0.5),
        'w_mod': nrm(ks[10], (DEPTH, D_MODEL, N_MOD * D_MODEL), D_MODEL ** -0.5),
        'b_mod': nrm(ks[11], (DEPTH, N_MOD * D_MODEL), 0.02),
        'norm_g': 1.0 + nrm(ks[12], (DEPTH, 4, D_MODEL), 0.05),
        'mlp_up': nrm(ks[13], (DEPTH, D_MODEL, D_FF), D_MODEL ** -0.5),
        'mlp_down': nrm(ks[14], (DEPTH, D_FF, D_MODEL), D_FF ** -0.5),
        'rwkv_mu': 0.5 + nrm(ks[15], (DEPTH, 3, C), 0.1),
        'rwkv_k_k': 0.85 + nrm(ks[16], (DEPTH, C), 0.05),
        'rwkv_k_a': 1.0 + nrm(ks[17], (DEPTH, C), 0.05),
        'rwkv_r_k': nrm(ks[18], (DEPTH, H, N), 0.1),
        'decay_w0': -1.0 + nrm(ks[19], (DEPTH, N_DIRS, C), 0.5),
        'decay_up': nrm(ks[20], (DEPTH, N_DIRS, DECAY_RANK, C), 0.1),
        'iclr_a0': nrm(ks[21], (DEPTH, N_DIRS, C), 0.3),
        'iclr_up': nrm(ks[22], (DEPTH, N_DIRS, ICLR_RANK, C), 0.1),
        'gate_up': nrm(ks[23], (DEPTH, GATE_RANK, C), GATE_RANK ** -0.5),
        'gn_w': 1.0 + nrm(ks[24], (DEPTH, C), 0.05),
        'gn_b': nrm(ks[25], (DEPTH, C), 0.02),
        'q_gain': 1.0 + nrm(ks[26], (DEPTH, HEAD_DIM), 0.05),
        'k_gain': 1.0 + nrm(ks[27], (DEPTH, HEAD_DIM), 0.05),
    }


def reference(x_prompt, x_sample, cache_k, cache_v, state_wkv, c, c_ctx, w_in, w_br, w_out, w_mod, b_mod,
              norm_g, mlp_up, mlp_down, rwkv_mu, rwkv_k_k, rwkv_k_a, rwkv_r_k, decay_w0, decay_up, iclr_a0,
              iclr_up, gate_up, gn_w, gn_b, q_gain, k_gain):
    P = dict(w_in=w_in, w_br=w_br, w_out=w_out, w_mod=w_mod, b_mod=b_mod, norm_g=norm_g, mlp_up=mlp_up,
             mlp_down=mlp_down, rwkv_mu=rwkv_mu, rwkv_k_k=rwkv_k_k, rwkv_k_a=rwkv_k_a, rwkv_r_k=rwkv_r_k,
             decay_w0=decay_w0, decay_up=decay_up, iclr_a0=iclr_a0, iclr_up=iclr_up, gate_up=gate_up,
             gn_w=gn_w, gn_b=gn_b, q_gain=q_gain, k_gain=k_gain)

    b_ctx = x_prompt.shape[0]
    S_zero = jnp.zeros((b_ctx, N_DIRS, N_RWKV_HEADS, HEAD_DIM, HEAD_DIM), x_prompt.dtype)
    y_prompt = x_prompt
    ks_list, vs_list, ss_list = [], [], []
    for l in range(DEPTH):
        y_prompt, k_l, v_l, s_l = trunk_layer(y_prompt, c_ctx, l, P, None, None, None, S_zero)
        ks_list.append(k_l)
        vs_list.append(v_l)
        ss_list.append(s_l)
    new_cache_k = jnp.stack(ks_list, axis=1)
    new_cache_v = jnp.stack(vs_list, axis=1)
    new_state_wkv = jnp.stack(ss_list, axis=1)

    n_rows = x_sample.shape[1] // GRID_W
    rope = axial_rope_tables(n_rows)
    y_sample = x_sample
    for l in range(DEPTH):
        y_sample, _, _, _ = trunk_layer(y_sample, c, l, P, rope, cache_k[:, l], cache_v[:, l], state_wkv[:, l])

    return (y_prompt, y_sample, new_cache_k, new_cache_v, new_state_wkv)
```

```python
import functools

import numpy as np
import jax
import jax.numpy as jnp
from jax import lax
from jax.experimental import pallas as pl
from jax.experimental.pallas import tpu as pltpu

F32 = jnp.float32
BF16 = jnp.bfloat16

D_MODEL = 1024
DEPTH = 2
GRID_W = 64
HEAD_DIM = 64
N_RWKV_HEADS = 8
RWKV_WIDTH = N_RWKV_HEADS * HEAD_DIM
N_Q_HEADS = 8
N_KV_HEADS = 2
ATTN_WIDTH = N_Q_HEADS * HEAD_DIM
KV_WIDTH = N_KV_HEADS * HEAD_DIM
DECAY_RANK = 64
ICLR_RANK = 64
GATE_RANK = 128
D_FF = 4 * D_MODEL
ROPE_THETA = 10000.0
ROPE_FREQS = HEAD_DIM // 4
N_MOD = 6
NORM_EPS = 1e-6
GN_EPS = 64e-5
DECAY_SCALE = 0.606531
D_IN = 3 * RWKV_WIDTH + DECAY_RANK + ICLR_RANK + GATE_RANK + ATTN_WIDTH + 2 * KV_WIDTH + 2 * D_MODEL

LANES = 128
SUBLANES = 8
VMEM_LIMIT_BYTES = 56 * 1024 * 1024

PROJ_W = D_IN
COL_RKV = 0
COL_Q = 3 * RWKV_WIDTH
COL_GR = COL_Q + ATTN_WIDTH
COL_GA = COL_GR + D_MODEL
COL_LORA = COL_GA + D_MODEL
COL_KV = COL_LORA + 2 * LANES

WKV_CHUNK = 64
PAIR = 2 * HEAD_DIM
N_PAIRS = N_RWKV_HEADS // 2

NN = (((1,), (0,)), ((), ()))
NT = (((1,), (1,)), ((), ()))
TN = (((0,), (0,)), ((), ()))


def _mm(a, b, dims=NN):
    return lax.dot_general(a.astype(BF16), b.astype(BF16), dims, preferred_element_type=F32)


def _split3(x):
    hi = x.astype(BF16)
    r1 = x - hi.astype(F32)
    mid = r1.astype(BF16)
    lo = (r1 - mid.astype(F32)).astype(BF16)
    return hi, mid, lo


def _mm_exact_rhs(x, m_bf16):
    hi, mid, lo = _split3(x)
    d = lambda p: lax.dot_general(p, m_bf16, NN, preferred_element_type=F32)
    return d(hi) + d(mid) + d(lo)


def _div_pow2(i, n):
    return lax.shift_right_logical(i, int(np.log2(n)))


def _mod_pow2(i, n):
    return lax.bitwise_and(i, n - 1)


def _group_ones():
    i = lax.broadcasted_iota(jnp.int32, (LANES, LANES), 0)
    j = lax.broadcasted_iota(jnp.int32, (LANES, LANES), 1)
    return jnp.where(_div_pow2(i, HEAD_DIM) == _div_pow2(j, HEAD_DIM), 1.0, 0.0).astype(BF16)


def _group_sum(x, ones_bd):
    blocks = [
        _mm_exact_rhs(x[:, i * LANES:(i + 1) * LANES], ones_bd)
        for i in range(x.shape[1] // LANES)
    ]
    return blocks[0] if len(blocks) == 1 else jnp.concatenate(blocks, axis=1)


def _rms(x, g):
    return x * lax.rsqrt(jnp.mean(x * x, axis=-1, keepdims=True) + NORM_EPS) * g


def _mod_kernel(c_ref, w_ref, b_ref, o_ref):
    c = c_ref[...]
    s = c * jax.nn.sigmoid(c)
    o_ref[0] = _mm(s, w_ref[0]) + b_ref[0]


def _modulation(cvec8, w_mod, b_mod):
    tn = D_MODEL
    n = N_MOD * D_MODEL
    return pl.pallas_call(
        _mod_kernel,
        grid=(DEPTH, n // tn),
        in_specs=[
            pl.BlockSpec((SUBLANES, D_MODEL), lambda l, j: (0, 0)),
            pl.BlockSpec((1, D_MODEL, tn), lambda l, j: (l, 0, j)),
            pl.BlockSpec((1, 1, tn), lambda l, j: (l, 0, j)),
        ],
        out_specs=pl.BlockSpec((1, SUBLANES, tn), lambda l, j: (l, 0, j)),
        out_shape=jax.ShapeDtypeStruct((DEPTH, SUBLANES, n), F32),
        name="modulation",
    )(cvec8, w_mod, b_mod.reshape(DEPTH, 1, n))


def _inproj_kernel(x_ref, mod_ref, ng_ref, w_ref, o_ref, h_scr, *, tm):
    @pl.when(pl.program_id(1) == 0)
    def _():
        sh = mod_ref[0, 0:1, :]
        sc = mod_ref[0, 1:2, :]
        ng = ng_ref[...]
        sub = 256
        for i in range(tm // sub):
            x = x_ref[i * sub:(i + 1) * sub, :]
            h_scr[i * sub:(i + 1) * sub, :] = (_rms(x, ng) * (1.0 + sc) + sh).astype(BF16)

    o_ref[...] = lax.dot_general(h_scr[...], w_ref[...], NN, preferred_element_type=F32)


def _in_projection(x2d, mod8, ng, w_in_bf, rows_per_mod):
    n_tok = x2d.shape[0]
    tm = 1024
    tn = 1536
    mod_idx = (lambda i, j: (i * tm // rows_per_mod, 0, 0)) if rows_per_mod else (lambda i, j: (0, 0, 0))
    return pl.pallas_call(
        functools.partial(_inproj_kernel, tm=tm),
        grid=(n_tok // tm, PROJ_W // tn),
        in_specs=[
            pl.BlockSpec((tm, D_MODEL), lambda i, j: (i, 0)),
            pl.BlockSpec((1, SUBLANES, D_MODEL), mod_idx),
            pl.BlockSpec((1, D_MODEL), lambda i, j: (0, 0)),
            pl.BlockSpec((D_MODEL, tn), lambda i, j: (0, j)),
        ],
        out_specs=pl.BlockSpec((tm, tn), lambda i, j: (i, j)),
        out_shape=jax.ShapeDtypeStruct((n_tok, PROJ_W), F32),
        scratch_shapes=[pltpu.VMEM((tm, D_MODEL), BF16)],
        compiler_params=pltpu.CompilerParams(
            dimension_semantics=("parallel", "arbitrary"), vmem_limit_bytes=VMEM_LIMIT_BYTES),
        name="in_projection",
    )(x2d, mod8, ng, w_in_bf)


def _shift_mix(cur, prev_row, next_row, mu):
    n = cur.shape[0]
    row = lax.broadcasted_iota(jnp.int32, cur.shape, 0)
    x_prev = jnp.where(row == 0, prev_row, pltpu.roll(cur, 1, 0))
    x_next = jnp.where(row == n - 1, next_row, pltpu.roll(cur, n - 1, 0))
    return cur + mu * (0.5 * (x_prev + x_next) - cur)


def _wkv_masks(reverse):
    i = lax.broadcasted_iota(jnp.int32, (PAIR, PAIR), 0)
    j = lax.broadcasted_iota(jnp.int32, (PAIR, PAIR), 1)
    same = _div_pow2(i, WKV_CHUNK) == _div_pow2(j, WKV_CHUNK)
    ti = _mod_pow2(i, WKV_CHUNK)
    tj = _mod_pow2(j, WKV_CHUNK)
    if reverse:
        strict = same & (tj > ti)
        incl = same & (tj >= ti)
    else:
        strict = same & (tj < ti)
        incl = same & (tj <= ti)
    blk = {bs: _div_pow2(i, bs) == _div_pow2(j, bs) for bs in (8, 16, 32, 64)}
    return strict, incl, blk, i == j


def _wkv_pair(S2, r, kt, v, kh, b, cum, cum_ex, tot, masks):
    strict, incl, blk, eye = masks
    lane = lax.broadcasted_iota(jnp.int32, (WKV_CHUNK, PAIR), 1)
    left = lane < HEAD_DIM
    st = lambda x: jnp.concatenate([jnp.where(left, x, 0.0), jnp.where(left, 0.0, x)], axis=0)

    e_in = jnp.exp(cum)
    e_inv = jnp.exp(-cum)
    e_end = jnp.exp(tot - cum)
    Kh2 = st(kh * jnp.exp(cum_ex))
    R2 = st(r * e_in)
    B2 = st(b * e_inv)
    Kt2 = st(kt * e_inv)
    V2 = st(v)
    Be2 = st(b * e_end)
    Kte2 = st(kt * e_end)

    V2b = V2.astype(BF16)
    A = _mm(jnp.concatenate([Kh2, R2], axis=0), jnp.concatenate([B2, Kt2], axis=0), NT)
    A_ub = jnp.where(strict, A[:PAIR, :PAIR], 0.0)
    A_uv = jnp.where(strict, A[:PAIR, PAIR:], 0.0)
    A_rb = jnp.where(incl, A[PAIR:, :PAIR], 0.0)
    A_rv = jnp.where(incl, A[PAIR:, PAIR:], 0.0)

    A8 = jnp.where(blk[8], A_ub, 0.0)
    A8_2 = _mm(A8, A8)
    A8_4 = _mm(A8_2, A8_2)
    X = jnp.where(eye, 1.0, 0.0) - A8
    X = X + _mm(X, A8_2)
    T = X + _mm(X, A8_4)
    for bs in (8, 16, 32):
        off = jnp.where(blk[2 * bs] & jnp.logical_not(blk[bs]), A_ub, 0.0)
        T = T - _mm(_mm(T, off), T)

    AV = _mm(jnp.concatenate([A_uv, A_rv], axis=0), V2b)
    Q = _mm(T, jnp.concatenate([Kh2, AV[:PAIR]], axis=1))
    AQ = _mm(A_rb, Q)
    G1 = R2 - AQ[:, :PAIR]
    G2 = AV[PAIR:] - AQ[:, PAIR:]
    QB = _mm(Q, Be2, TN)
    M2 = jnp.where(eye, jnp.exp(tot), 0.0) - QB[:PAIR]
    N2 = _mm(V2b, Kte2, TN) - QB[PAIR:]
    Y2 = _mm(G1, S2, NT) + G2
    y = Y2[:WKV_CHUNK] + Y2[WKV_CHUNK:]
    return y, _mm(S2, M2) + N2


def _wkv_kernel(cur0, prv0, nxt0, lor0, cur1, prv1, nxt1, lor1,
                mu_ref, kk_ref, ka_ref, rk_ref, w0_ref, wup_ref, a0_ref, aup_ref, s0_ref,
                y0_ref, y1_ref, bo0_ref, bo1_ref, sfin_ref, s_scr, *, zero_state):
    c = pl.program_id(1)
    nc = pl.num_programs(1)

    @pl.when(c == 0)
    def _():
        if zero_state:
            s_scr[...] = jnp.zeros(s_scr.shape, F32)
        else:
            s_scr[...] = s0_ref[0]

    ones_bd = _group_ones()
    C = RWKV_WIDTH
    L = WKV_CHUNK
    ti = lax.broadcasted_iota(jnp.int32, (L, L), 0)
    tj = lax.broadcasted_iota(jnp.int32, (L, L), 1)

    for d, (cur, prv, nxt, lor, y_ref, bo_ref) in enumerate(
            ((cur0, prv0, nxt0, lor0, y0_ref, bo0_ref), (cur1, prv1, nxt1, lor1, y1_ref, bo1_ref))):
        reverse = d == 1
        cd = nc - 1 - c if reverse else c
        not_first = jnp.where(cd == 0, 0.0, 1.0)
        not_last = jnp.where(cd == nc - 1, 0.0, 1.0)
        x = cur[...]
        prev_row = prv[SUBLANES - 1:SUBLANES, :] * not_first
        next_row = nxt[0:1, :] * not_last
        mixed = [
            _shift_mix(x[:, i * C:(i + 1) * C], prev_row[:, i * C:(i + 1) * C],
                       next_row[:, i * C:(i + 1) * C], mu_ref[i:i + 1, :])
            for i in range(3)
        ]
        r, k, v = mixed
        lo = lor[...]
        kk = k * kk_ref[...]
        kh = kk * lax.rsqrt(_group_sum(kk * kk, ones_bd) + 1e-12)
        lo_dec = lo[:, :LANES]
        w_raw = _mm(jnp.tanh(lo_dec), wup_ref[d]) + w0_ref[d:d + 1, :]
        lw = -DECAY_SCALE * jax.nn.sigmoid(w_raw)
        a = jax.nn.sigmoid(_mm(lo_dec, aup_ref[d]) + a0_ref[d:d + 1, :])
        kt = k * (1.0 + (a - 1.0) * ka_ref[...])
        bo_ref[...] = _group_sum(r * kt * rk_ref[...], ones_bd) * v
        b = a * kh

        tri = jnp.where((tj >= ti) if reverse else (tj <= ti), 1.0, 0.0).astype(BF16)
        hi = lw.astype(BF16)
        lo2 = (lw - hi.astype(F32)).astype(BF16)
        cum = (lax.dot_general(tri, hi, NN, preferred_element_type=F32)
               + lax.dot_general(tri, lo2, NN, preferred_element_type=F32))
        cum_ex = cum - lw
        tot = cum[0:1, :] if reverse else cum[L - 1:L, :]
        masks = _wkv_masks(reverse)
        for p in range(N_PAIRS):
            sl = slice(p * PAIR, (p + 1) * PAIR)
            y, s_new = _wkv_pair(s_scr[d, p], r[:, sl], kt[:, sl], v[:, sl], kh[:, sl], b[:, sl],
                                 cum[:, sl], cum_ex[:, sl], tot[:, sl], masks)
            y_ref[:, sl] = y
            s_scr[d, p] = s_new

    @pl.when(c == nc - 1)
    def _():
        sfin_ref[0] = s_scr[...]


def _wkv(proj, s0_pairs, n_batch, seq, mu, k_k, k_a, r_k, w0, wup_bf, a0, aup_bf):
    L = WKV_CHUNK
    nc = seq // L
    n_tok = n_batch * seq
    rows8 = L // SUBLANES
    n_blk8 = n_tok // SUBLANES
    zero_state = s0_pairs is None
    if zero_state:
        s0_pairs = jnp.zeros((1, 2, N_PAIRS, PAIR, PAIR), F32)

    def chunk(d):
        return (lambda b, c: b * nc + c) if d == 0 else (lambda b, c: b * nc + nc - 1 - c)

    in_specs = []
    for d in (0, 1):
        g = chunk(d)
        in_specs += [
            pl.BlockSpec((L, 3 * RWKV_WIDTH), lambda b, c, g=g: (g(b, c), COL_RKV // (3 * RWKV_WIDTH))),
            pl.BlockSpec((SUBLANES, 3 * RWKV_WIDTH),
                         lambda b, c, g=g: (jnp.maximum(g(b, c) * rows8 - 1, 0), 0)),
            pl.BlockSpec((SUBLANES, 3 * RWKV_WIDTH),
                         lambda b, c, g=g: (jnp.minimum((g(b, c) + 1) * rows8, n_blk8 - 1), 0)),
            pl.BlockSpec((L, 2 * LANES), lambda b, c, g=g: (g(b, c), COL_LORA // (2 * LANES))),
        ]
    const = lambda shape: pl.BlockSpec(shape, lambda b, c: (0,) * len(shape))
    in_specs += [
        const((3, RWKV_WIDTH)), const((1, RWKV_WIDTH)), const((1, RWKV_WIDTH)), const((1, RWKV_WIDTH)),
        const((2, RWKV_WIDTH)), const((2, LANES, RWKV_WIDTH)),
        const((2, RWKV_WIDTH)), const((2, LANES, RWKV_WIDTH)),
        pl.BlockSpec((1, 2, N_PAIRS, PAIR, PAIR),
                     (lambda b, c: (0, 0, 0, 0, 0)) if zero_state else (lambda b, c: (b, 0, 0, 0, 0))),
    ]
    tok_spec = lambda d: pl.BlockSpec((L, RWKV_WIDTH), lambda b, c, g=chunk(d): (g(b, c), 0))
    out_specs = [tok_spec(0), tok_spec(1), tok_spec(0), tok_spec(1),
                 pl.BlockSpec((1, 2, N_PAIRS, PAIR, PAIR), lambda b, c: (b, 0, 0, 0, 0))]
    tok = jax.ShapeDtypeStruct((n_tok, RWKV_WIDTH), F32)
    return pl.pallas_call(
        functools.partial(_wkv_kernel, zero_state=zero_state),
        grid=(n_batch, nc),
        in_specs=in_specs,
        out_specs=out_specs,
        out_shape=[tok, tok, tok, tok, jax.ShapeDtypeStruct((n_batch, 2, N_PAIRS, PAIR, PAIR), F32)],
        scratch_shapes=[pltpu.VMEM((2, N_PAIRS, PAIR, PAIR), F32)],
        compiler_params=pltpu.CompilerParams(
            dimension_semantics=("parallel", "arbitrary"), vmem_limit_bytes=VMEM_LIMIT_BYTES),
        name="wkv_scan",
    )(proj, proj, proj, proj, proj, proj, proj, proj, mu, k_k, k_a, r_k, w0, wup_bf, a0, aup_bf, s0_pairs)


def _rope(x, cos, sin_signed):
    w = x.shape[1]
    lane = lax.broadcasted_iota(jnp.int32, x.shape, 1)
    partner = jnp.where(_mod_pow2(lane, 2 * ROPE_FREQS) < ROPE_FREQS,
                        pltpu.roll(x, w - ROPE_FREQS, 1), pltpu.roll(x, ROPE_FREQS, 1))
    return x * cos + partner * sin_signed


def _attn_kernel(*refs, tq, seq, past, use_rope):
    it = iter(refs)
    q_ref, kv_ref, qg_ref, kg_ref = next(it), next(it), next(it), next(it)
    if use_rope:
        cq_ref, sq_ref, ck_ref, sk_ref = next(it), next(it), next(it), next(it)
    if past:
        pk_ref, pv_ref = next(it), next(it)
    o_ref, ko_ref, vo_ref = next(it), next(it), next(it)
    k_scr, v_scr = next(it), next(it)

    ones_bd = _group_ones()
    kv = kv_ref[...]
    k_raw = kv[:, :KV_WIDTH]
    kn = k_raw * lax.rsqrt(_group_sum(k_raw * k_raw, ones_bd) * (1.0 / HEAD_DIM) + NORM_EPS) * kg_ref[...]
    if use_rope:
        kn = _rope(kn, ck_ref[...], sk_ref[...])
    if past:
        k_scr[0:past, :] = pk_ref[0]
        v_scr[0:past, :] = pv_ref[0]
    k_scr[past:past + seq, :] = kn
    v_scr[past:past + seq, :] = kv[:, KV_WIDTH:]
    q0 = pl.multiple_of(pl.program_id(1) * tq, tq)
    ko_ref[...] = k_scr[pl.ds(past + q0, tq), :]
    vo_ref[...] = v_scr[pl.ds(past + q0, tq), :]

    k_all = k_scr[...]
    v_all = v_scr[...]
    lane = lax.broadcasted_iota(jnp.int32, k_all.shape, 1)
    left = lane < HEAD_DIM
    k_sw = pltpu.roll(k_all, HEAD_DIM, 1)
    v_sw = pltpu.roll(v_all, HEAD_DIM, 1)
    kvar = [[jnp.where(left, k_all, 0.0).astype(BF16), jnp.where(left, 0.0, k_sw).astype(BF16)],
            [jnp.where(left, k_sw, 0.0).astype(BF16), jnp.where(left, 0.0, k_all).astype(BF16)]]
    vvar = [[jnp.where(left, v_all, 0.0).astype(BF16), jnp.where(left, 0.0, v_sw).astype(BF16)],
            [jnp.where(left, v_sw, 0.0).astype(BF16), jnp.where(left, 0.0, v_all).astype(BF16)]]

    scale = HEAD_DIM ** -0.5
    for jb in range(ATTN_WIDTH // LANES):
        hk = jb // (ATTN_WIDTH // LANES // N_KV_HEADS)
        qb = q_ref[:, jb * LANES:(jb + 1) * LANES]
        qn = qb * lax.rsqrt(_group_sum(qb * qb, ones_bd) * (1.0 / HEAD_DIM) + NORM_EPS) * qg_ref[...]
        if use_rope:
            qn = _rope(qn, cq_ref[...], sq_ref[...])
        qn = qn.astype(BF16)
        acc = None
        for side in (0, 1):
            s = lax.dot_general(qn, kvar[hk][side], NT, preferred_element_type=F32) * scale
            e = jnp.exp(s - jnp.max(s, axis=-1, keepdims=True))
            o = lax.dot_general(e.astype(BF16), vvar[hk][side], NN, preferred_element_type=F32)
            o = o / jnp.sum(e, axis=-1, keepdims=True)
            acc = o if acc is None else acc + o
        o_ref[:, jb * LANES:(jb + 1) * LANES] = acc


def _attention(proj, n_batch, seq, q_gain128, k_gain128, rope128, past_k, past_v):
    tq = 256
    nq = seq // tq
    n_tok = n_batch * seq
    use_rope = rope128 is not None
    past = 0 if past_k is None else past_k.shape[1]
    in_specs = [
        pl.BlockSpec((tq, ATTN_WIDTH), lambda b, i: (b * nq + i, COL_Q // ATTN_WIDTH)),
        pl.BlockSpec((seq, 2 * KV_WIDTH), lambda b, i: (b, COL_KV // (2 * KV_WIDTH))),
        pl.BlockSpec((1, LANES), lambda b, i: (0, 0)),
        pl.BlockSpec((1, LANES), lambda b, i: (0, 0)),
    ]
    args = [proj, proj, q_gain128, k_gain128]
    if use_rope:
        cos, sin = rope128
        in_specs += [pl.BlockSpec((tq, LANES), lambda b, i: (i, 0)),
                     pl.BlockSpec((tq, LANES), lambda b, i: (i, 0)),
                     pl.BlockSpec((seq, LANES), lambda b, i: (0, 0)),
                     pl.BlockSpec((seq, LANES), lambda b, i: (0, 0))]
        args += [cos, sin, cos, sin]
    if past:
        in_specs += [pl.BlockSpec((1, past, KV_WIDTH), lambda b, i: (b, 0, 0)),
                     pl.BlockSpec((1, past, KV_WIDTH), lambda b, i: (b, 0, 0))]
        args += [past_k, past_v]
    out_specs = [pl.BlockSpec((tq, ATTN_WIDTH), lambda b, i: (b * nq + i, 0)),
                 pl.BlockSpec((tq, KV_WIDTH), lambda b, i: (b * nq + i, 0)),
                 pl.BlockSpec((tq, KV_WIDTH), lambda b, i: (b * nq + i, 0))]
    return pl.pallas_call(
        functools.partial(_attn_kernel, tq=tq, seq=seq, past=past, use_rope=use_rope),
        grid=(n_batch, nq),
        in_specs=in_specs,
        out_specs=out_specs,
        out_shape=[jax.ShapeDtypeStruct((n_tok, ATTN_WIDTH), F32),
                   jax.ShapeDtypeStruct((n_tok, KV_WIDTH), F32),
                   jax.ShapeDtypeStruct((n_tok, KV_WIDTH), F32)],
        scratch_shapes=[pltpu.VMEM((past + seq, KV_WIDTH), F32), pltpu.VMEM((past + seq, KV_WIDTH), F32)],
        compiler_params=pltpu.CompilerParams(
            dimension_semantics=("parallel", "arbitrary"), vmem_limit_bytes=VMEM_LIMIT_BYTES),
        name="attention",
    )(*args)


def _post_kernel(y0_ref, y1_ref, bo0_ref, bo1_ref, lor_ref, oa_ref, gr_ref, ga_ref, x_ref, mod_ref,
                 gup_ref, gnw_ref, gnb_ref, wbr_ref, wout_ref, ng_ref, x1_ref, h2_ref):
    ones_bd = _group_ones()
    y = y0_ref[...] + y1_ref[...]
    mean = _group_sum(y, ones_bd) * (1.0 / HEAD_DIM)
    yc = y - mean
    var = _group_sum(yc * yc, ones_bd) * (1.0 / HEAD_DIM)
    yn = yc * lax.rsqrt(var + GN_EPS) * gnw_ref[...] + gnb_ref[...]
    lg = lor_ref[:, LANES:]
    g = _mm(jax.nn.sigmoid(lg), gup_ref[...])
    o_r = (yn + (bo0_ref[...] + bo1_ref[...])) * g
    merged = (jax.nn.sigmoid(gr_ref[...]) * _mm(o_r, wbr_ref[0])
              + jax.nn.sigmoid(ga_ref[...]) * _mm(oa_ref[...], wbr_ref[1]))
    m = _mm(merged, wout_ref[...])
    g1 = mod_ref[0, 2:3, :]
    sh2 = mod_ref[0, 3:4, :]
    sc2 = mod_ref[0, 4:5, :]
    x1 = x_ref[...] + g1 * _rms(m, ng_ref[1:2, :])
    x1_ref[...] = x1
    h2_ref[...] = (_rms(x1, ng_ref[2:3, :]) * (1.0 + sc2) + sh2).astype(BF16)


def _post(y0, y1, bo0, bo1, proj, o_a, x2d, mod8, gup_bf, gn_w, gn_b, wbr_bf, wout_bf, ng, rows_per_mod):
    n_tok = x2d.shape[0]
    tm = 256
    mod_idx = (lambda i: (i * tm // rows_per_mod, 0, 0)) if rows_per_mod else (lambda i: (0, 0, 0))
    tok = lambda w: pl.BlockSpec((tm, w), lambda i: (i, 0))
    const = lambda shape: pl.BlockSpec(shape, lambda i: (0,) * len(shape))
    return pl.pallas_call(
        _post_kernel,
        grid=(n_tok // tm,),
        in_specs=[
            tok(RWKV_WIDTH), tok(RWKV_WIDTH), tok(RWKV_WIDTH), tok(RWKV_WIDTH),
            pl.BlockSpec((tm, 2 * LANES), lambda i: (i, COL_LORA // (2 * LANES))),
            tok(ATTN_WIDTH),
            pl.BlockSpec((tm, D_MODEL), lambda i: (i, COL_GR // D_MODEL)),
            pl.BlockSpec((tm, D_MODEL), lambda i: (i, COL_GA // D_MODEL)),
            tok(D_MODEL),
            pl.BlockSpec((1, SUBLANES, D_MODEL), mod_idx),
            const((GATE_RANK, RWKV_WIDTH)), const((1, RWKV_WIDTH)), const((1, RWKV_WIDTH)),
            const((2, RWKV_WIDTH, D_MODEL)), const((D_MODEL, D_MODEL)), const((4, D_MODEL)),
        ],
        out_specs=[tok(D_MODEL), tok(D_MODEL)],
        out_shape=[jax.ShapeDtypeStruct((n_tok, D_MODEL), F32), jax.ShapeDtypeStruct((n_tok, D_MODEL), BF16)],
        compiler_params=pltpu.CompilerParams(
            dimension_semantics=("parallel",), vmem_limit_bytes=VMEM_LIMIT_BYTES),
        name="merge_outproj",
    )(y0, y1, bo0, bo1, proj, o_a, proj, proj, x2d, mod8, gup_bf, gn_w, gn_b, wbr_bf, wout_bf, ng)


def _mlp_kernel(h_ref, x1_ref, mod_ref, up_ref, down_ref, ng_ref, o_ref):
    h = h_ref[...]
    f = None
    ff = D_MODEL
    for j in range(D_FF // ff):
        u = lax.dot_general(h, up_ref[:, j * ff:(j + 1) * ff], NN, preferred_element_type=F32)
        u = jnp.square(jnp.maximum(u, 0.0)).astype(BF16)
        part = lax.dot_general(u, down_ref[j * ff:(j + 1) * ff, :], NN, preferred_element_type=F32)
        f = part if f is None else f + part
    g2 = mod_ref[0, 5:6, :]
    o_ref[...] = x1_ref[...] + g2 * _rms(f, ng_ref[3:4, :])


def _mlp(h2, x1, mod8, up_bf, down_bf, ng, rows_per_mod):
    n_tok = x1.shape[0]
    tm = 512
    mod_idx = (lambda i: (i * tm // rows_per_mod, 0, 0)) if rows_per_mod else (lambda i: (0, 0, 0))
    tok = lambda: pl.BlockSpec((tm, D_MODEL), lambda i: (i, 0))
    return pl.pallas_call(
        _mlp_kernel,
        grid=(n_tok // tm,),
        in_specs=[
            tok(), tok(),
            pl.BlockSpec((1, SUBLANES, D_MODEL), mod_idx),
            pl.BlockSpec((D_MODEL, D_FF), lambda i: (0, 0)),
            pl.BlockSpec((D_FF, D_MODEL), lambda i: (0, 0)),
            pl.BlockSpec((4, D_MODEL), lambda i: (0, 0)),
        ],
        out_specs=tok(),
        out_shape=jax.ShapeDtypeStruct((n_tok, D_MODEL), F32),
        compiler_params=pltpu.CompilerParams(
            dimension_semantics=("parallel",), vmem_limit_bytes=VMEM_LIMIT_BYTES),
        name="mlp",
    )(h2, x1, mod8, up_bf, down_bf, ng)


def _rope_tables(seq):
    n_rows = seq // GRID_W
    rows = np.repeat(np.arange(n_rows, dtype=np.float32), GRID_W)
    cols = np.tile(np.arange(GRID_W, dtype=np.float32), n_rows)
    half = HEAD_DIM // 2
    freqs = 1.0 / (jnp.asarray(ROPE_THETA, F32) ** (jnp.arange(0, half, 2, dtype=F32) / half))
    ang_r = jnp.asarray(rows)[:, None] * freqs
    ang_c = jnp.asarray(cols)[:, None] * freqs
    cr, sr, cc, sc = jnp.cos(ang_r), jnp.sin(ang_r), jnp.cos(ang_c), jnp.sin(ang_c)
    cos64 = jnp.concatenate([cr, cr, cc, cc], axis=1)
    sin64 = jnp.concatenate([-sr, sr, -sc, sc], axis=1)
    return jnp.tile(cos64, (1, LANES // HEAD_DIM)), jnp.tile(sin64, (1, LANES // HEAD_DIM))


def _pairs_from_state(s):
    b = s.shape[0]
    s = s.reshape(b, 2, N_PAIRS, 2, HEAD_DIM, HEAD_DIM)
    z = jnp.zeros_like(s[:, :, :, 0])
    top = jnp.concatenate([s[:, :, :, 0], z], axis=-1)
    bot = jnp.concatenate([z, s[:, :, :, 1]], axis=-1)
    return jnp.concatenate([top, bot], axis=-2)


def _state_from_pairs(s2):
    b = s2.shape[0]
    s = jnp.stack([s2[:, :, :, :HEAD_DIM, :HEAD_DIM], s2[:, :, :, HEAD_DIM:, HEAD_DIM:]], axis=3)
    return s.reshape(b, 2, N_RWKV_HEADS, HEAD_DIM, HEAD_DIM)


def _layer(x2d, n_batch, seq, mod8, rows_per_mod, W, rope128, past_k, past_v, s0):
    proj = _in_projection(x2d, mod8, W['ng'][0:1], W['w_in'], rows_per_mod)
    s0_pairs = None if s0 is None else _pairs_from_state(s0)
    y0, y1, bo0, bo1, s_fin = _wkv(proj, s0_pairs, n_batch, seq, W['mu'], W['k_k'], W['k_a'], W['r_k'],
                                   W['w0'], W['wup'], W['a0'], W['aup'])
    o_a, k_own, v_own = _attention(proj, n_batch, seq, W['q_gain'], W['k_gain'], rope128, past_k, past_v)
    x1, h2 = _post(y0, y1, bo0, bo1, proj, o_a, x2d, mod8, W['gup'], W['gn_w'], W['gn_b'],
                   W['w_br'], W['w_out'], W['ng'], rows_per_mod)
    x2 = _mlp(h2, x1, mod8, W['up'], W['down'], W['ng'], rows_per_mod)
    return x2, k_own, v_own, s_fin


def kernel(x_prompt, x_sample, cache_k, cache_v, state_wkv, c, c_ctx, w_in, w_br, w_out, w_mod, b_mod, norm_g, mlp_up, mlp_down, rwkv_mu, rwkv_k_k, rwkv_k_a, rwkv_r_k, decay_w0, decay_up, iclr_a0, iclr_up, gate_up, gn_w, gn_b, q_gain, k_gain):
    n_ctx, seq_ctx, _ = x_prompt.shape
    n_dec, seq_dec, _ = x_sample.shape
    past = cache_k.shape[2]

    cvec8 = jnp.zeros((SUBLANES, D_MODEL), F32).at[0].set(c_ctx).at[1:1 + n_dec].set(c)
    mod = _modulation(cvec8, w_mod, b_mod)
    mod = jnp.pad(mod.reshape(DEPTH, SUBLANES, N_MOD, D_MODEL), ((0, 0), (0, 0), (0, SUBLANES - N_MOD), (0, 0)))

    o = np.cumsum((0, RWKV_WIDTH, RWKV_WIDTH, RWKV_WIDTH, DECAY_RANK, ICLR_RANK, GATE_RANK,
                   ATTN_WIDTH, KV_WIDTH, KV_WIDTH, 2 * D_MODEL))
    w_in_bf = jnp.concatenate([w_in[:, :, o[0]:o[3]], w_in[:, :, o[6]:o[7]], w_in[:, :, o[9]:o[10]],
                               w_in[:, :, o[3]:o[6]], w_in[:, :, o[7]:o[9]]], axis=-1).astype(BF16)
    zpad = jnp.zeros((DEPTH, 2, DECAY_RANK, RWKV_WIDTH), F32)
    wup_bf = jnp.concatenate([decay_up, zpad], axis=2).astype(BF16)
    aup_bf = jnp.concatenate([zpad, iclr_up], axis=2).astype(BF16)

    layers = []
    for l in range(DEPTH):
        layers.append(dict(
            w_in=w_in_bf[l], ng=norm_g[l], mu=rwkv_mu[l], k_k=rwkv_k_k[l][None], k_a=rwkv_k_a[l][None],
            r_k=rwkv_r_k[l].reshape(1, RWKV_WIDTH), w0=decay_w0[l], wup=wup_bf[l],
            a0=iclr_a0[l], aup=aup_bf[l], gup=gate_up[l].astype(BF16),
            gn_w=gn_w[l][None], gn_b=gn_b[l][None], w_br=w_br[l].astype(BF16), w_out=w_out[l].astype(BF16),
            up=mlp_up[l].astype(BF16), down=mlp_down[l].astype(BF16),
            q_gain=jnp.tile(q_gain[l], LANES // HEAD_DIM)[None], k_gain=jnp.tile(k_gain[l], LANES // HEAD_DIM)[None],
        ))

    x = x_prompt.reshape(n_ctx * seq_ctx, D_MODEL)
    ks, vs, ss = [], [], []
    for l in range(DEPTH):
        x, k_l, v_l, s_l = _layer(x, n_ctx, seq_ctx, mod[l, 0:1], 0, layers[l], None, None, None, None)
        ks.append(k_l.reshape(n_ctx, seq_ctx, N_KV_HEADS, HEAD_DIM))
        vs.append(v_l.reshape(n_ctx, seq_ctx, N_KV_HEADS, HEAD_DIM))
        ss.append(_state_from_pairs(s_l))
    y_prompt = x.reshape(n_ctx, seq_ctx, D_MODEL)

    rope128 = _rope_tables(seq_dec)
    x = x_sample.reshape(n_dec * seq_dec, D_MODEL)
    for l in range(DEPTH):
        x, _, _, _ = _layer(x, n_dec, seq_dec, mod[l, 1:1 + n_dec], seq_dec, layers[l], rope128,
                            cache_k[:, l].reshape(n_dec, past, KV_WIDTH),
                            cache_v[:, l].reshape(n_dec, past, KV_WIDTH), state_wkv[:, l])
    y_sample = x.reshape(n_dec, seq_dec, D_MODEL)

    return (y_prompt, y_sample, jnp.stack(ks, axis=1), jnp.stack(vs, axis=1), jnp.stack(ss, axis=1))
```

```python
import functools

import numpy as np
import jax
import jax.numpy as jnp
from jax import lax
from jax.experimental import pallas as pl
from jax.experimental.pallas import tpu as pltpu

F32 = jnp.float32
BF16 = jnp.bfloat16

D_MODEL = 1024
DEPTH = 2
GRID_W = 64
HEAD_DIM = 64
N_RWKV_HEADS = 8
RWKV_WIDTH = N_RWKV_HEADS * HEAD_DIM
N_Q_HEADS = 8
N_KV_HEADS = 2
ATTN_WIDTH = N_Q_HEADS * HEAD_DIM
KV_WIDTH = N_KV_HEADS * HEAD_DIM
DECAY_RANK = 64
ICLR_RANK = 64
GATE_RANK = 128
D_FF = 4 * D_MODEL
ROPE_THETA = 10000.0
ROPE_FREQS = HEAD_DIM // 4
N_MOD = 6
NORM_EPS = 1e-6
GN_EPS = 64e-5
DECAY_SCALE = 0.606531
D_IN = 3 * RWKV_WIDTH + DECAY_RANK + ICLR_RANK + GATE_RANK + ATTN_WIDTH + 2 * KV_WIDTH + 2 * D_MODEL

LANES = 128
SUBLANES = 8
VMEM_LIMIT_BYTES = 56 * 1024 * 1024

PROJ_W = D_IN
COL_RKV = 0
COL_Q = 3 * RWKV_WIDTH
COL_GR = COL_Q + ATTN_WIDTH
COL_GA = COL_GR + D_MODEL
COL_LORA = COL_GA + D_MODEL
COL_KV = COL_LORA + 2 * LANES

WKV_CHUNK = 64
PAIR = 2 * HEAD_DIM
N_PAIRS = N_RWKV_HEADS // 2

NN = (((1,), (0,)), ((), ()))
NT = (((1,), (1,)), ((), ()))
TN = (((0,), (0,)), ((), ()))


def _mm(a, b, dims=NN):
    return lax.dot_general(a.astype(BF16), b.astype(BF16), dims, preferred_element_type=F32)


def _split3(x):
    hi = x.astype(BF16)
    r1 = x - hi.astype(F32)
    mid = r1.astype(BF16)
    lo = (r1 - mid.astype(F32)).astype(BF16)
    return hi, mid, lo


def _mm_exact_rhs(x, m_bf16):
    hi, mid, lo = _split3(x)
    d = lambda p: lax.dot_general(p, m_bf16, NN, preferred_element_type=F32)
    return d(hi) + d(mid) + d(lo)


def _div_pow2(i, n):
    return lax.shift_right_logical(i, int(np.log2(n)))


def _mod_pow2(i, n):
    return lax.bitwise_and(i, n - 1)


def _group_ones():
    i = lax.broadcasted_iota(jnp.int32, (LANES, LANES), 0)
    j = lax.broadcasted_iota(jnp.int32, (LANES, LANES), 1)
    return jnp.where(_div_pow2(i, HEAD_DIM) == _div_pow2(j, HEAD_DIM), 1.0, 0.0).astype(BF16)


def _group_sum(x, ones_bd):
    blocks = [
        _mm_exact_rhs(x[:, i * LANES:(i + 1) * LANES], ones_bd)
        for i in range(x.shape[1] // LANES)
    ]
    return blocks[0] if len(blocks) == 1 else jnp.concatenate(blocks, axis=1)


def _rms(x, g):
    return x * lax.rsqrt(jnp.mean(x * x, axis=-1, keepdims=True) + NORM_EPS) * g


def _mod_kernel(c_ref, w_ref, b_ref, o_ref):
    c = c_ref[...]
    s = c * jax.nn.sigmoid(c)
    o_ref[0] = _mm(s, w_ref[0]) + b_ref[0]


def _modulation(cvec8, w_mod, b_mod):
    tn = D_MODEL
    n = N_MOD * D_MODEL
    return pl.pallas_call(
        _mod_kernel,
        grid=(DEPTH, n // tn),
        in_specs=[
            pl.BlockSpec((SUBLANES, D_MODEL), lambda l, j: (0, 0)),
            pl.BlockSpec((1, D_MODEL, tn), lambda l, j: (l, 0, j)),
            pl.BlockSpec((1, 1, tn), lambda l, j: (l, 0, j)),
        ],
        out_specs=pl.BlockSpec((1, SUBLANES, tn), lambda l, j: (l, 0, j)),
        out_shape=jax.ShapeDtypeStruct((DEPTH, SUBLANES, n), F32),
        name="modulation",
    )(cvec8, w_mod, b_mod.reshape(DEPTH, 1, n))


def _inproj_kernel(x_ref, mod_ref, ng_ref, w_ref, o_ref, h_scr, *, tm):
    @pl.when(pl.program_id(1) == 0)
    def _():
        sh = mod_ref[0, 0:1, :]
        sc = mod_ref[0, 1:2, :]
        ng = ng_ref[...]
        sub = 256
        for i in range(tm // sub):
            x = x_ref[i * sub:(i + 1) * sub, :]
            h_scr[i * sub:(i + 1) * sub, :] = (_rms(x, ng) * (1.0 + sc) + sh).astype(BF16)

    o_ref[...] = lax.dot_general(h_scr[...], w_ref[...], NN, preferred_element_type=F32)


def _in_projection(x2d, mod8, ng, w_in_bf, rows_per_mod):
    n_tok = x2d.shape[0]
    tm = 1024
    tn = 1536
    mod_idx = (lambda i, j: (i * tm // rows_per_mod, 0, 0)) if rows_per_mod else (lambda i, j: (0, 0, 0))
    return pl.pallas_call(
        functools.partial(_inproj_kernel, tm=tm),
        grid=(n_tok // tm, PROJ_W // tn),
        in_specs=[
            pl.BlockSpec((tm, D_MODEL), lambda i, j: (i, 0)),
            pl.BlockSpec((1, SUBLANES, D_MODEL), mod_idx),
            pl.BlockSpec((1, D_MODEL), lambda i, j: (0, 0)),
            pl.BlockSpec((D_MODEL, tn), lambda i, j: (0, j)),
        ],
        out_specs=pl.BlockSpec((tm, tn), lambda i, j: (i, j)),
        out_shape=jax.ShapeDtypeStruct((n_tok, PROJ_W), F32),
        scratch_shapes=[pltpu.VMEM((tm, D_MODEL), BF16)],
        compiler_params=pltpu.CompilerParams(
            dimension_semantics=("parallel", "arbitrary"), vmem_limit_bytes=VMEM_LIMIT_BYTES),
        name="in_projection",
    )(x2d, mod8, ng, w_in_bf)


def _shift_mix(cur, prev_row, next_row, mu):
    n = cur.shape[0]
    row = lax.broadcasted_iota(jnp.int32, cur.shape, 0)
    x_prev = jnp.where(row == 0, prev_row, pltpu.roll(cur, 1, 0))
    x_next = jnp.where(row == n - 1, next_row, pltpu.roll(cur, n - 1, 0))
    return cur + mu * (0.5 * (x_prev + x_next) - cur)


def _wkv_masks(reverse):
    i = lax.broadcasted_iota(jnp.int32, (PAIR, PAIR), 0)
    j = lax.broadcasted_iota(jnp.int32, (PAIR, PAIR), 1)
    same = _div_pow2(i, WKV_CHUNK) == _div_pow2(j, WKV_CHUNK)
    ti = _mod_pow2(i, WKV_CHUNK)
    tj = _mod_pow2(j, WKV_CHUNK)
    if reverse:
        strict = same & (tj > ti)
        incl = same & (tj >= ti)
    else:
        strict = same & (tj < ti)
        incl = same & (tj <= ti)
    blk = {bs: _div_pow2(i, bs) == _div_pow2(j, bs) for bs in (8, 16, 32, 64)}
    return strict, incl, blk, i == j


def _wkv_chains(chains):
    lane = lax.broadcasted_iota(jnp.int32, (WKV_CHUNK, PAIR), 1)
    left = lane < HEAD_DIM
    st = lambda x: jnp.concatenate([jnp.where(left, x, 0.0), jnp.where(left, 0.0, x)], axis=0)
    each = lambda f: [f(ch) for ch in chains]

    def operands(ch):
        cum, tot = ch['cum'], ch['tot']
        e_inv = jnp.exp(-cum)
        e_end = jnp.exp(tot - cum)
        ch['Kh2'] = st(ch['kh'] * jnp.exp(ch['cum_ex']))
        ch['R2'] = st(ch['r'] * jnp.exp(cum))
        ch['Be2'] = st(ch['b'] * e_end).astype(BF16)
        ch['Kte2'] = st(ch['kt'] * e_end).astype(BF16)
        ch['V2b'] = st(ch['v']).astype(BF16)
        ch['lhsA'] = jnp.concatenate([ch['Kh2'], ch['R2']], axis=0).astype(BF16)
        ch['rhsA'] = jnp.concatenate([st(ch['b'] * e_inv), st(ch['kt'] * e_inv)], axis=0).astype(BF16)
    each(operands)

    def a_blocks(ch):
        strict, incl, blk, eye = ch['masks']
        A = _mm(ch['lhsA'], ch['rhsA'], NT)
        ch['A_ub'] = jnp.where(strict, A[:PAIR, :PAIR], 0.0)
        ch['A_uvrv'] = jnp.concatenate([jnp.where(strict, A[:PAIR, PAIR:], 0.0),
                                        jnp.where(incl, A[PAIR:, PAIR:], 0.0)], axis=0).astype(BF16)
        ch['A_rb'] = jnp.where(incl, A[PAIR:, :PAIR], 0.0).astype(BF16)
        ch['A8'] = jnp.where(blk[8], ch['A_ub'], 0.0).astype(BF16)
    each(a_blocks)

    def sq1(ch):
        ch['A8_2'] = _mm(ch['A8'], ch['A8']).astype(BF16)
    each(sq1)

    def sq2(ch):
        ch['A8_4'] = _mm(ch['A8_2'], ch['A8_2']).astype(BF16)
        ch['AV'] = _mm(ch['A_uvrv'], ch['V2b'])
    each(sq2)

    def neu1(ch):
        eye = ch['masks'][3]
        x = jnp.where(eye, 1.0, 0.0) - ch['A8'].astype(F32)
        ch['X'] = x + _mm(x, ch['A8_2'])
    each(neu1)

    def neu2(ch):
        ch['T'] = ch['X'] + _mm(ch['X'], ch['A8_4'])
    each(neu2)

    for bs in (8, 16, 32):
        def merge_a(ch, bs=bs):
            blk = ch['masks'][2]
            off = jnp.where(blk[2 * bs] & jnp.logical_not(blk[bs]), ch['A_ub'], 0.0)
            ch['Tb'] = ch['T'].astype(BF16)
            ch['TO'] = _mm(ch['Tb'], off)
        each(merge_a)

        def merge_b(ch):
            ch['T'] = ch['T'] - _mm(ch['TO'], ch['Tb'])
        each(merge_b)

    def q_stage(ch):
        ch['Q'] = _mm(ch['T'], jnp.concatenate([ch['Kh2'], ch['AV'][:PAIR]], axis=1)).astype(BF16)
    each(q_stage)

    def mn_stage(ch):
        eye = ch['masks'][3]
        QB = _mm(ch['Q'], ch['Be2'], TN)
        ch['M2'] = jnp.where(eye, jnp.exp(ch['tot']), 0.0) - QB[:PAIR]
        ch['N2'] = _mm(ch['V2b'], ch['Kte2'], TN) - QB[PAIR:]
        AQ = _mm(ch['A_rb'], ch['Q'])
        ch['G1'] = ch['R2'] - AQ[:, :PAIR]
        ch['G2'] = ch['AV'][PAIR:] - AQ[:, PAIR:]
    each(mn_stage)

    def out_stage(ch):
        S2 = ch['S2'].astype(BF16)
        Y2 = _mm(ch['G1'], S2, NT) + ch['G2']
        ch['y'] = Y2[:WKV_CHUNK] + Y2[WKV_CHUNK:]
        ch['S2n'] = _mm(S2, ch['M2']) + ch['N2']
    each(out_stage)


def _wkv_kernel(cur0, prv0, nxt0, lor0, cur1, prv1, nxt1, lor1,
                mu_ref, kk_ref, ka_ref, rk_ref, w0_ref, wup_ref, a0_ref, aup_ref, s0_ref,
                y0_ref, y1_ref, bo0_ref, bo1_ref, sfin_ref, s_scr, *, zero_state):
    c = pl.program_id(1)
    nc = pl.num_programs(1)

    @pl.when(c == 0)
    def _():
        if zero_state:
            s_scr[...] = jnp.zeros(s_scr.shape, F32)
        else:
            s_scr[...] = s0_ref[0]

    ones_bd = _group_ones()
    C = RWKV_WIDTH
    L = WKV_CHUNK
    ti = lax.broadcasted_iota(jnp.int32, (L, L), 0)
    tj = lax.broadcasted_iota(jnp.int32, (L, L), 1)

    chains = []
    for d, (cur, prv, nxt, lor, y_ref, bo_ref) in enumerate(
            ((cur0, prv0, nxt0, lor0, y0_ref, bo0_ref), (cur1, prv1, nxt1, lor1, y1_ref, bo1_ref))):
        reverse = d == 1
        cd = nc - 1 - c if reverse else c
        not_first = jnp.where(cd == 0, 0.0, 1.0)
        not_last = jnp.where(cd == nc - 1, 0.0, 1.0)
        x = cur[...]
        prev_row = prv[SUBLANES - 1:SUBLANES, :] * not_first
        next_row = nxt[0:1, :] * not_last
        mixed = [
            _shift_mix(x[:, i * C:(i + 1) * C], prev_row[:, i * C:(i + 1) * C],
                       next_row[:, i * C:(i + 1) * C], mu_ref[i:i + 1, :])
            for i in range(3)
        ]
        r, k, v = mixed
        lo = lor[...]
        kk = k * kk_ref[...]
        kh = kk * lax.rsqrt(_group_sum(kk * kk, ones_bd) + 1e-12)
        lo_dec = lo[:, :LANES]
        w_raw = _mm(jnp.tanh(lo_dec), wup_ref[d]) + w0_ref[d:d + 1, :]
        lw = -DECAY_SCALE * jax.nn.sigmoid(w_raw)
        a = jax.nn.sigmoid(_mm(lo_dec, aup_ref[d]) + a0_ref[d:d + 1, :])
        kt = k * (1.0 + (a - 1.0) * ka_ref[...])
        bo_ref[...] = _group_sum(r * kt * rk_ref[...], ones_bd) * v
        b = a * kh

        tri = jnp.where((tj >= ti) if reverse else (tj <= ti), 1.0, 0.0).astype(BF16)
        hi = lw.astype(BF16)
        lo2 = (lw - hi.astype(F32)).astype(BF16)
        cum = (lax.dot_general(tri, hi, NN, preferred_element_type=F32)
               + lax.dot_general(tri, lo2, NN, preferred_element_type=F32))
        cum_ex = cum - lw
        tot = cum[0:1, :] if reverse else cum[L - 1:L, :]
        masks = _wkv_masks(reverse)
        for p in range(N_PAIRS):
            sl = slice(p * PAIR, (p + 1) * PAIR)
            chains.append(dict(d=d, p=p, sl=sl, y_ref=y_ref, masks=masks, S2=s_scr[d, p],
                               r=r[:, sl], kt=kt[:, sl], v=v[:, sl], kh=kh[:, sl], b=b[:, sl],
                               cum=cum[:, sl], cum_ex=cum_ex[:, sl], tot=tot[:, sl]))

    _wkv_chains(chains)
    for ch in chains:
        ch['y_ref'][:, ch['sl']] = ch['y']
        s_scr[ch['d'], ch['p']] = ch['S2n']

    @pl.when(c == nc - 1)
    def _():
        sfin_ref[0] = s_scr[...]


def _wkv(proj, s0_pairs, n_batch, seq, mu, k_k, k_a, r_k, w0, wup_bf, a0, aup_bf):
    L = WKV_CHUNK
    nc = seq // L
    n_tok = n_batch * seq
    rows8 = L // SUBLANES
    n_blk8 = n_tok // SUBLANES
    zero_state = s0_pairs is None
    if zero_state:
        s0_pairs = jnp.zeros((1, 2, N_PAIRS, PAIR, PAIR), F32)

    def chunk(d):
        return (lambda b, c: b * nc + c) if d == 0 else (lambda b, c: b * nc + nc - 1 - c)

    in_specs = []
    for d in (0, 1):
        g = chunk(d)
        in_specs += [
            pl.BlockSpec((L, 3 * RWKV_WIDTH), lambda b, c, g=g: (g(b, c), COL_RKV // (3 * RWKV_WIDTH))),
            pl.BlockSpec((SUBLANES, 3 * RWKV_WIDTH),
                         lambda b, c, g=g: (jnp.maximum(g(b, c) * rows8 - 1, 0), 0)),
            pl.BlockSpec((SUBLANES, 3 * RWKV_WIDTH),
                         lambda b, c, g=g: (jnp.minimum((g(b, c) + 1) * rows8, n_blk8 - 1), 0)),
            pl.BlockSpec((L, 2 * LANES), lambda b, c, g=g: (g(b, c), COL_LORA // (2 * LANES))),
        ]
    const = lambda shape: pl.BlockSpec(shape, lambda b, c: (0,) * len(shape))
    in_specs += [
        const((3, RWKV_WIDTH)), const((1, RWKV_WIDTH)), const((1, RWKV_WIDTH)), const((1, RWKV_WIDTH)),
        const((2, RWKV_WIDTH)), const((2, LANES, RWKV_WIDTH)),
        const((2, RWKV_WIDTH)), const((2, LANES, RWKV_WIDTH)),
        pl.BlockSpec((1, 2, N_PAIRS, PAIR, PAIR),
                     (lambda b, c: (0, 0, 0, 0, 0)) if zero_state else (lambda b, c: (b, 0, 0, 0, 0))),
    ]
    tok_spec = lambda d: pl.BlockSpec((L, RWKV_WIDTH), lambda b, c, g=chunk(d): (g(b, c), 0))
    out_specs = [tok_spec(0), tok_spec(1), tok_spec(0), tok_spec(1),
                 pl.BlockSpec((1, 2, N_PAIRS, PAIR, PAIR), lambda b, c: (b, 0, 0, 0, 0))]
    tok = jax.ShapeDtypeStruct((n_tok, RWKV_WIDTH), F32)
    return pl.pallas_call(
        functools.partial(_wkv_kernel, zero_state=zero_state),
        grid=(n_batch, nc),
        in_specs=in_specs,
        out_specs=out_specs,
        out_shape=[tok, tok, tok, tok, jax.ShapeDtypeStruct((n_batch, 2, N_PAIRS, PAIR, PAIR), F32)],
        scratch_shapes=[pltpu.VMEM((2, N_PAIRS, PAIR, PAIR), F32)],
        compiler_params=pltpu.CompilerParams(
            dimension_semantics=("parallel", "arbitrary"), vmem_limit_bytes=VMEM_LIMIT_BYTES),
        name="wkv_scan",
    )(proj, proj, proj, proj, proj, proj, proj, proj, mu, k_k, k_a, r_k, w0, wup_bf, a0, aup_bf, s0_pairs)


def _rope(x, cos, sin_signed):
    w = x.shape[1]
    lane = lax.broadcasted_iota(jnp.int32, x.shape, 1)
    partner = jnp.where(_mod_pow2(lane, 2 * ROPE_FREQS) < ROPE_FREQS,
                        pltpu.roll(x, w - ROPE_FREQS, 1), pltpu.roll(x, ROPE_FREQS, 1))
    return x * cos + partner * sin_signed


def _attn_kernel(*refs, tq, seq, past, use_rope):
    it = iter(refs)
    q_ref, kv_ref, qg_ref, kg_ref = next(it), next(it), next(it), next(it)
    if use_rope:
        cq_ref, sq_ref, ck_ref, sk_ref = next(it), next(it), next(it), next(it)
    if past:
        pk_ref, pv_ref = next(it), next(it)
    o_ref, ko_ref, vo_ref = next(it), next(it), next(it)
    k_scr, v_scr = next(it), next(it)

    ones_bd = _group_ones()
    kv = kv_ref[...]
    k_raw = kv[:, :KV_WIDTH]
    kn = k_raw * lax.rsqrt(_group_sum(k_raw * k_raw, ones_bd) * (1.0 / HEAD_DIM) + NORM_EPS) * kg_ref[...]
    if use_rope:
        kn = _rope(kn, ck_ref[...], sk_ref[...])
    if past:
        k_scr[0:past, :] = pk_ref[0]
        v_scr[0:past, :] = pv_ref[0]
    k_scr[past:past + seq, :] = kn
    v_scr[past:past + seq, :] = kv[:, KV_WIDTH:]
    q0 = pl.multiple_of(pl.program_id(1) * tq, tq)
    ko_ref[...] = k_scr[pl.ds(past + q0, tq), :]
    vo_ref[...] = v_scr[pl.ds(past + q0, tq), :]

    k_all = k_scr[...]
    v_all = v_scr[...]
    lane = lax.broadcasted_iota(jnp.int32, k_all.shape, 1)
    left = lane < HEAD_DIM
    k_sw = pltpu.roll(k_all, HEAD_DIM, 1)
    v_sw = pltpu.roll(v_all, HEAD_DIM, 1)
    kvar = [[jnp.where(left, k_all, 0.0).astype(BF16), jnp.where(left, 0.0, k_sw).astype(BF16)],
            [jnp.where(left, k_sw, 0.0).astype(BF16), jnp.where(left, 0.0, k_all).astype(BF16)]]
    vvar = [[jnp.where(left, v_all, 0.0).astype(BF16), jnp.where(left, 0.0, v_sw).astype(BF16)],
            [jnp.where(left, v_sw, 0.0).astype(BF16), jnp.where(left, 0.0, v_all).astype(BF16)]]

    scale = HEAD_DIM ** -0.5
    for jb in range(ATTN_WIDTH // LANES):
        hk = jb // (ATTN_WIDTH // LANES // N_KV_HEADS)
        qb = q_ref[:, jb * LANES:(jb + 1) * LANES]
        qn = qb * lax.rsqrt(_group_sum(qb * qb, ones_bd) * (1.0 / HEAD_DIM) + NORM_EPS) * qg_ref[...]
        if use_rope:
            qn = _rope(qn, cq_ref[...], sq_ref[...])
        qn = qn.astype(BF16)
        acc = None
        for side in (0, 1):
            s = lax.dot_general(qn, kvar[hk][side], NT, preferred_element_type=F32) * scale
            e = jnp.exp(s - jnp.max(s, axis=-1, keepdims=True))
            o = lax.dot_general(e.astype(BF16), vvar[hk][side], NN, preferred_element_type=F32)
            o = o / jnp.sum(e, axis=-1, keepdims=True)
            acc = o if acc is None else acc + o
        o_ref[:, jb * LANES:(jb + 1) * LANES] = acc


def _attention(proj, n_batch, seq, q_gain128, k_gain128, rope128, past_k, past_v):
    tq = 256
    nq = seq // tq
    n_tok = n_batch * seq
    use_rope = rope128 is not None
    past = 0 if past_k is None else past_k.shape[1]
    in_specs = [
        pl.BlockSpec((tq, ATTN_WIDTH), lambda b, i: (b * nq + i, COL_Q // ATTN_WIDTH)),
        pl.BlockSpec((seq, 2 * KV_WIDTH), lambda b, i: (b, COL_KV // (2 * KV_WIDTH))),
        pl.BlockSpec((1, LANES), lambda b, i: (0, 0)),
        pl.BlockSpec((1, LANES), lambda b, i: (0, 0)),
    ]
    args = [proj, proj, q_gain128, k_gain128]
    if use_rope:
        cos, sin = rope128
        in_specs += [pl.BlockSpec((tq, LANES), lambda b, i: (i, 0)),
                     pl.BlockSpec((tq, LANES), lambda b, i: (i, 0)),
                     pl.BlockSpec((seq, LANES), lambda b, i: (0, 0)),
                     pl.BlockSpec((seq, LANES), lambda b, i: (0, 0))]
        args += [cos, sin, cos, sin]
    if past:
        in_specs += [pl.BlockSpec((1, past, KV_WIDTH), lambda b, i: (b, 0, 0)),
                     pl.BlockSpec((1, past, KV_WIDTH), lambda b, i: (b, 0, 0))]
        args += [past_k, past_v]
    out_specs = [pl.BlockSpec((tq, ATTN_WIDTH), lambda b, i: (b * nq + i, 0)),
                 pl.BlockSpec((tq, KV_WIDTH), lambda b, i: (b * nq + i, 0)),
                 pl.BlockSpec((tq, KV_WIDTH), lambda b, i: (b * nq + i, 0))]
    return pl.pallas_call(
        functools.partial(_attn_kernel, tq=tq, seq=seq, past=past, use_rope=use_rope),
        grid=(n_batch, nq),
        in_specs=in_specs,
        out_specs=out_specs,
        out_shape=[jax.ShapeDtypeStruct((n_tok, ATTN_WIDTH), F32),
                   jax.ShapeDtypeStruct((n_tok, KV_WIDTH), F32),
                   jax.ShapeDtypeStruct((n_tok, KV_WIDTH), F32)],
        scratch_shapes=[pltpu.VMEM((past + seq, KV_WIDTH), F32), pltpu.VMEM((past + seq, KV_WIDTH), F32)],
        compiler_params=pltpu.CompilerParams(
            dimension_semantics=("parallel", "arbitrary"), vmem_limit_bytes=VMEM_LIMIT_BYTES),
        name="attention",
    )(*args)


def _post_kernel(y0_ref, y1_ref, bo0_ref, bo1_ref, lor_ref, oa_ref, gr_ref, ga_ref, x_ref, mod_ref,
                 gup_ref, gnw_ref, gnb_ref, wbr_ref, wout_ref, ng_ref, x1_ref, h2_ref):
    ones_bd = _group_ones()
    y = y0_ref[...] + y1_ref[...]
    mean = _group_sum(y, ones_bd) * (1.0 / HEAD_DIM)
    yc = y - mean
    var = _group_sum(yc * yc, ones_bd) * (1.0 / HEAD_DIM)
    yn = yc * lax.rsqrt(var + GN_EPS) * gnw_ref[...] + gnb_ref[...]
    lg = lor_ref[:, LANES:]
    g = _mm(jax.nn.sigmoid(lg), gup_ref[...])
    o_r = (yn + (bo0_ref[...] + bo1_ref[...])) * g
    merged = (jax.nn.sigmoid(gr_ref[...]) * _mm(o_r, wbr_ref[0])
              + jax.nn.sigmoid(ga_ref[...]) * _mm(oa_ref[...], wbr_ref[1]))
    m = _mm(merged, wout_ref[...])
    g1 = mod_ref[0, 2:3, :]
    sh2 = mod_ref[0, 3:4, :]
    sc2 = mod_ref[0, 4:5, :]
    x1 = x_ref[...] + g1 * _rms(m, ng_ref[1:2, :])
    x1_ref[...] = x1
    h2_ref[...] = (_rms(x1, ng_ref[2:3, :]) * (1.0 + sc2) + sh2).astype(BF16)


def _post(y0, y1, bo0, bo1, proj, o_a, x2d, mod8, gup_bf, gn_w, gn_b, wbr_bf, wout_bf, ng, rows_per_mod):
    n_tok = x2d.shape[0]
    tm = 256
    mod_idx = (lambda i: (i * tm // rows_per_mod, 0, 0)) if rows_per_mod else (lambda i: (0, 0, 0))
    tok = lambda w: pl.BlockSpec((tm, w), lambda i: (i, 0))
    const = lambda shape: pl.BlockSpec(shape, lambda i: (0,) * len(shape))
    return pl.pallas_call(
        _post_kernel,
        grid=(n_tok // tm,),
        in_specs=[
            tok(RWKV_WIDTH), tok(RWKV_WIDTH), tok(RWKV_WIDTH), tok(RWKV_WIDTH),
            pl.BlockSpec((tm, 2 * LANES), lambda i: (i, COL_LORA // (2 * LANES))),
            tok(ATTN_WIDTH),
            pl.BlockSpec((tm, D_MODEL), lambda i: (i, COL_GR // D_MODEL)),
            pl.BlockSpec((tm, D_MODEL), lambda i: (i, COL_GA // D_MODEL)),
            tok(D_MODEL),
            pl.BlockSpec((1, SUBLANES, D_MODEL), mod_idx),
            const((GATE_RANK, RWKV_WIDTH)), const((1, RWKV_WIDTH)), const((1, RWKV_WIDTH)),
            const((2, RWKV_WIDTH, D_MODEL)), const((D_MODEL, D_MODEL)), const((4, D_MODEL)),
        ],
        out_specs=[tok(D_MODEL), tok(D_MODEL)],
        out_shape=[jax.ShapeDtypeStruct((n_tok, D_MODEL), F32), jax.ShapeDtypeStruct((n_tok, D_MODEL), BF16)],
        compiler_params=pltpu.CompilerParams(
            dimension_semantics=("parallel",), vmem_limit_bytes=VMEM_LIMIT_BYTES),
        name="merge_outproj",
    )(y0, y1, bo0, bo1, proj, o_a, proj, proj, x2d, mod8, gup_bf, gn_w, gn_b, wbr_bf, wout_bf, ng)


def _mlp_kernel(h_ref, x1_ref, mod_ref, up_ref, down_ref, ng_ref, o_ref):
    h = h_ref[...]
    f = None
    ff = D_MODEL
    for j in range(D_FF // ff):
        u = lax.dot_general(h, up_ref[:, j * ff:(j + 1) * ff], NN, preferred_element_type=F32)
        u = jnp.square(jnp.maximum(u, 0.0)).astype(BF16)
        part = lax.dot_general(u, down_ref[j * ff:(j + 1) * ff, :], NN, preferred_element_type=F32)
        f = part if f is None else f + part
    g2 = mod_ref[0, 5:6, :]
    o_ref[...] = x1_ref[...] + g2 * _rms(f, ng_ref[3:4, :])


def _mlp(h2, x1, mod8, up_bf, down_bf, ng, rows_per_mod):
    n_tok = x1.shape[0]
    tm = 512
    mod_idx = (lambda i: (i * tm // rows_per_mod, 0, 0)) if rows_per_mod else (lambda i: (0, 0, 0))
    tok = lambda: pl.BlockSpec((tm, D_MODEL), lambda i: (i, 0))
    return pl.pallas_call(
        _mlp_kernel,
        grid=(n_tok // tm,),
        in_specs=[
            tok(), tok(),
            pl.BlockSpec((1, SUBLANES, D_MODEL), mod_idx),
            pl.BlockSpec((D_MODEL, D_FF), lambda i: (0, 0)),
            pl.BlockSpec((D_FF, D_MODEL), lambda i: (0, 0)),
            pl.BlockSpec((4, D_MODEL), lambda i: (0, 0)),
        ],
        out_specs=tok(),
        out_shape=jax.ShapeDtypeStruct((n_tok, D_MODEL), F32),
        compiler_params=pltpu.CompilerParams(
            dimension_semantics=("parallel",), vmem_limit_bytes=VMEM_LIMIT_BYTES),
        name="mlp",
    )(h2, x1, mod8, up_bf, down_bf, ng)


def _rope_tables(seq):
    n_rows = seq // GRID_W
    rows = np.repeat(np.arange(n_rows, dtype=np.float32), GRID_W)
    cols = np.tile(np.arange(GRID_W, dtype=np.float32), n_rows)
    half = HEAD_DIM // 2
    freqs = 1.0 / (jnp.asarray(ROPE_THETA, F32) ** (jnp.arange(0, half, 2, dtype=F32) / half))
    ang_r = jnp.asarray(rows)[:, None] * freqs
    ang_c = jnp.asarray(cols)[:, None] * freqs
    cr, sr, cc, sc = jnp.cos(ang_r), jnp.sin(ang_r), jnp.cos(ang_c), jnp.sin(ang_c)
    cos64 = jnp.concatenate([cr, cr, cc, cc], axis=1)
    sin64 = jnp.concatenate([-sr, sr, -sc, sc], axis=1)
    return jnp.tile(cos64, (1, LANES // HEAD_DIM)), jnp.tile(sin64, (1, LANES // HEAD_DIM))


def _pairs_from_state(s):
    b = s.shape[0]
    s = s.reshape(b, 2, N_PAIRS, 2, HEAD_DIM, HEAD_DIM)
    z = jnp.zeros_like(s[:, :, :, 0])
    top = jnp.concatenate([s[:, :, :, 0], z], axis=-1)
    bot = jnp.concatenate([z, s[:, :, :, 1]], axis=-1)
    return jnp.concatenate([top, bot], axis=-2)


def _state_from_pairs(s2):
    b = s2.shape[0]
    s = jnp.stack([s2[:, :, :, :HEAD_DIM, :HEAD_DIM], s2[:, :, :, HEAD_DIM:, HEAD_DIM:]], axis=3)
    return s.reshape(b, 2, N_RWKV_HEADS, HEAD_DIM, HEAD_DIM)


def _layer(x2d, n_batch, seq, mod8, rows_per_mod, W, rope128, past_k, past_v, s0):
    proj = _in_projection(x2d, mod8, W['ng'][0:1], W['w_in'], rows_per_mod)
    s0_pairs = None if s0 is None else _pairs_from_state(s0)
    y0, y1, bo0, bo1, s_fin = _wkv(proj, s0_pairs, n_batch, seq, W['mu'], W['k_k'], W['k_a'], W['r_k'],
                                   W['w0'], W['wup'], W['a0'], W['aup'])
    o_a, k_own, v_own = _attention(proj, n_batch, seq, W['q_gain'], W['k_gain'], rope128, past_k, past_v)
    x1, h2 = _post(y0, y1, bo0, bo1, proj, o_a, x2d, mod8, W['gup'], W['gn_w'], W['gn_b'],
                   W['w_br'], W['w_out'], W['ng'], rows_per_mod)
    x2 = _mlp(h2, x1, mod8, W['up'], W['down'], W['ng'], rows_per_mod)
    return x2, k_own, v_own, s_fin


def kernel(x_prompt, x_sample, cache_k, cache_v, state_wkv, c, c_ctx, w_in, w_br, w_out, w_mod, b_mod, norm_g, mlp_up, mlp_down, rwkv_mu, rwkv_k_k, rwkv_k_a, rwkv_r_k, decay_w0, decay_up, iclr_a0, iclr_up, gate_up, gn_w, gn_b, q_gain, k_gain):
    n_ctx, seq_ctx, _ = x_prompt.shape
    n_dec, seq_dec, _ = x_sample.shape
    past = cache_k.shape[2]

    cvec8 = jnp.zeros((SUBLANES, D_MODEL), F32).at[0].set(c_ctx).at[1:1 + n_dec].set(c)
    mod = _modulation(cvec8, w_mod, b_mod)
    mod = jnp.pad(mod.reshape(DEPTH, SUBLANES, N_MOD, D_MODEL), ((0, 0), (0, 0), (0, SUBLANES - N_MOD), (0, 0)))

    o = np.cumsum((0, RWKV_WIDTH, RWKV_WIDTH, RWKV_WIDTH, DECAY_RANK, ICLR_RANK, GATE_RANK,
                   ATTN_WIDTH, KV_WIDTH, KV_WIDTH, 2 * D_MODEL))
    w_in_bf = jnp.concatenate([w_in[:, :, o[0]:o[3]], w_in[:, :, o[6]:o[7]], w_in[:, :, o[9]:o[10]],
                               w_in[:, :, o[3]:o[6]], w_in[:, :, o[7]:o[9]]], axis=-1).astype(BF16)
    zpad = jnp.zeros((DEPTH, 2, DECAY_RANK, RWKV_WIDTH), F32)
    wup_bf = jnp.concatenate([decay_up, zpad], axis=2).astype(BF16)
    aup_bf = jnp.concatenate([zpad, iclr_up], axis=2).astype(BF16)

    layers = []
    for l in range(DEPTH):
        layers.append(dict(
            w_in=w_in_bf[l], ng=norm_g[l], mu=rwkv_mu[l], k_k=rwkv_k_k[l][None], k_a=rwkv_k_a[l][None],
            r_k=rwkv_r_k[l].reshape(1, RWKV_WIDTH), w0=decay_w0[l], wup=wup_bf[l],
            a0=iclr_a0[l], aup=aup_bf[l], gup=gate_up[l].astype(BF16),
            gn_w=gn_w[l][None], gn_b=gn_b[l][None], w_br=w_br[l].astype(BF16), w_out=w_out[l].astype(BF16),
            up=mlp_up[l].astype(BF16), down=mlp_down[l].astype(BF16),
            q_gain=jnp.tile(q_gain[l], LANES // HEAD_DIM)[None], k_gain=jnp.tile(k_gain[l], LANES // HEAD_DIM)[None],
        ))

    x = x_prompt.reshape(n_ctx * seq_ctx, D_MODEL)
    ks, vs, ss = [], [], []
    for l in range(DEPTH):
        x, k_l, v_l, s_l = _layer(x, n_ctx, seq_ctx, mod[l, 0:1], 0, layers[l], None, None, None, None)
        ks.append(k_l.reshape(n_ctx, seq_ctx, N_KV_HEADS, HEAD_DIM))
        vs.append(v_l.reshape(n_ctx, seq_ctx, N_KV_HEADS, HEAD_DIM))
        ss.append(_state_from_pairs(s_l))
    y_prompt = x.reshape(n_ctx, seq_ctx, D_MODEL)

    rope128 = _rope_tables(seq_dec)
    x = x_sample.reshape(n_dec * seq_dec, D_MODEL)
    for l in range(DEPTH):
        x, _, _, _ = _layer(x, n_dec, seq_dec, mod[l, 1:1 + n_dec], seq_dec, layers[l], rope128,
                            cache_k[:, l].reshape(n_dec, past, KV_WIDTH),
                            cache_v[:, l].reshape(n_dec, past, KV_WIDTH), state_wkv[:, l])
    y_sample = x.reshape(n_dec, seq_dec, D_MODEL)

    return (y_prompt, y_sample, jnp.stack(ks, axis=1), jnp.stack(vs, axis=1), jnp.stack(ss, axis=1))
```

```python
import functools

import numpy as np
import jax
import jax.numpy as jnp
from jax import lax
from jax.experimental import pallas as pl
from jax.experimental.pallas import tpu as pltpu

F32 = jnp.float32
BF16 = jnp.bfloat16

D_MODEL = 1024
DEPTH = 2
GRID_W = 64
HEAD_DIM = 64
N_RWKV_HEADS = 8
RWKV_WIDTH = N_RWKV_HEADS * HEAD_DIM
N_Q_HEADS = 8
N_KV_HEADS = 2
ATTN_WIDTH = N_Q_HEADS * HEAD_DIM
KV_WIDTH = N_KV_HEADS * HEAD_DIM
DECAY_RANK = 64
ICLR_RANK = 64
GATE_RANK = 128
D_FF = 4 * D_MODEL
ROPE_THETA = 10000.0
ROPE_FREQS = HEAD_DIM // 4
N_MOD = 6
NORM_EPS = 1e-6
GN_EPS = 64e-5
DECAY_SCALE = 0.606531
D_IN = 3 * RWKV_WIDTH + DECAY_RANK + ICLR_RANK + GATE_RANK + ATTN_WIDTH + 2 * KV_WIDTH + 2 * D_MODEL

LANES = 128
SUBLANES = 8
VMEM_LIMIT_BYTES = 56 * 1024 * 1024

RKV_W = 3 * RWKV_WIDTH
COL_GR = 0
COL_GA = COL_GR + D_MODEL
COL_Q = COL_GA + D_MODEL
COL_LORA = COL_Q + ATTN_WIDTH
COL_KV = COL_LORA + 2 * LANES
REST_W = COL_KV + 2 * KV_WIDTH

WKV_CHUNK = 64
PAIR = 2 * HEAD_DIM
N_PAIRS = N_RWKV_HEADS // 2

NN = (((1,), (0,)), ((), ()))
NT = (((1,), (1,)), ((), ()))
TN = (((0,), (0,)), ((), ()))


def _mm(a, b, dims=NN):
    return lax.dot_general(a.astype(BF16), b.astype(BF16), dims, preferred_element_type=F32)


def _split2(x):
    hi = x.astype(BF16)
    lo = (x - hi.astype(F32)).astype(BF16)
    return hi, lo


def _mm_exact_rhs(x, m_bf16):
    hi, lo = _split2(x)
    d = lambda p: lax.dot_general(p, m_bf16, NN, preferred_element_type=F32)
    return d(hi) + d(lo)


def _div_pow2(i, n):
    return lax.shift_right_logical(i, int(np.log2(n)))


def _mod_pow2(i, n):
    return lax.bitwise_and(i, n - 1)


def _group_ones():
    i = lax.broadcasted_iota(jnp.int32, (LANES, LANES), 0)
    j = lax.broadcasted_iota(jnp.int32, (LANES, LANES), 1)
    return jnp.where(_div_pow2(i, HEAD_DIM) == _div_pow2(j, HEAD_DIM), 1.0, 0.0).astype(BF16)


def _group_sum(x, ones_bd):
    blocks = [
        _mm_exact_rhs(x[:, i * LANES:(i + 1) * LANES], ones_bd)
        for i in range(x.shape[1] // LANES)
    ]
    return blocks[0] if len(blocks) == 1 else jnp.concatenate(blocks, axis=1)


def _rms(x, g):
    return x * lax.rsqrt(jnp.mean(x * x, axis=-1, keepdims=True) + NORM_EPS) * g


def _mod_kernel(c_ref, w_ref, b_ref, o_ref):
    c = c_ref[...]
    s = c * jax.nn.sigmoid(c)
    o_ref[0] = _mm(s, w_ref[0]) + b_ref[0]


def _modulation(cvec8, w_mod, b_mod):
    tn = D_MODEL
    n = N_MOD * D_MODEL
    return pl.pallas_call(
        _mod_kernel,
        grid=(DEPTH, n // tn),
        in_specs=[
            pl.BlockSpec((SUBLANES, D_MODEL), lambda l, j: (0, 0)),
            pl.BlockSpec((1, D_MODEL, tn), lambda l, j: (l, 0, j)),
            pl.BlockSpec((1, 1, tn), lambda l, j: (l, 0, j)),
        ],
        out_specs=pl.BlockSpec((1, SUBLANES, tn), lambda l, j: (l, 0, j)),
        out_shape=jax.ShapeDtypeStruct((DEPTH, SUBLANES, n), F32),
        name="modulation",
    )(cvec8, w_mod, b_mod.reshape(DEPTH, 1, n))


def _shift_mix(x, mu):
    n = x.shape[0]
    row = lax.broadcasted_iota(jnp.int32, x.shape, 0)
    x_prev = jnp.where(row == 0, 0.0, pltpu.roll(x, 1, 0))
    x_next = jnp.where(row == n - 1, 0.0, pltpu.roll(x, n - 1, 0))
    return x * (1.0 - mu) + (0.5 * mu) * (x_prev + x_next)


def _inproj_kernel(x_ref, mod_ref, ng_ref, wrkv_ref, wrest_ref, mu_ref, kk_ref,
                   rkv_ref, rest_ref, kh_ref, h_scr, *, tm, seq):
    j = pl.program_id(1)
    C = RWKV_WIDTH

    def mix_in_place(i):
        for s in range(tm // seq):
            rows = slice(s * seq, (s + 1) * seq)
            cols = slice(i * C, (i + 1) * C)
            mixed = _shift_mix(rkv_ref[rows, cols], mu_ref[i:i + 1, :])
            rkv_ref[rows, cols] = mixed
            if i == 1:
                kk = mixed * kk_ref[...]
                kh_ref[rows, :] = kk * lax.rsqrt(_group_sum(kk * kk, _group_ones()) + 1e-12)

    @pl.when(j == 0)
    def _():
        sh = mod_ref[0, 0:1, :]
        sc = mod_ref[0, 1:2, :]
        ng = ng_ref[...]
        sub = 256
        for i in range(tm // sub):
            x = x_ref[i * sub:(i + 1) * sub, :]
            h_scr[i * sub:(i + 1) * sub, :] = (_rms(x, ng) * (1.0 + sc) + sh).astype(BF16)
        rkv_ref[...] = lax.dot_general(h_scr[...], wrkv_ref[...], NN, preferred_element_type=F32)

    for jj in range(1, 4):
        @pl.when(j == jj)
        def _(jj=jj):
            mix_in_place(jj - 1)
            rest_ref[...] = lax.dot_general(h_scr[...], wrest_ref[...], NN, preferred_element_type=F32)


def _in_projection(x2d, mod8, ng, w_rkv_bf, w_rest_bf, mu, k_k, rows_per_mod, seq):
    n_tok = x2d.shape[0]
    tm = 1024
    tn = REST_W // 3
    assert tm % seq == 0
    mod_idx = (lambda i, j: (i * tm // rows_per_mod, 0, 0)) if rows_per_mod else (lambda i, j: (0, 0, 0))
    rest_blk = lambda i, j: (i, jnp.maximum(j - 1, 0))
    return pl.pallas_call(
        functools.partial(_inproj_kernel, tm=tm, seq=seq),
        grid=(n_tok // tm, 4),
        in_specs=[
            pl.BlockSpec((tm, D_MODEL), lambda i, j: (i, 0)),
            pl.BlockSpec((1, SUBLANES, D_MODEL), mod_idx),
            pl.BlockSpec((1, D_MODEL), lambda i, j: (0, 0)),
            pl.BlockSpec((D_MODEL, RKV_W), lambda i, j: (0, 0)),
            pl.BlockSpec((D_MODEL, tn), lambda i, j: (0, jnp.maximum(j - 1, 0))),
            pl.BlockSpec((3, RWKV_WIDTH), lambda i, j: (0, 0)),
            pl.BlockSpec((1, RWKV_WIDTH), lambda i, j: (0, 0)),
        ],
        out_specs=[pl.BlockSpec((tm, RKV_W), lambda i, j: (i, 0)),
                   pl.BlockSpec((tm, tn), rest_blk),
                   pl.BlockSpec((tm, RWKV_WIDTH), lambda i, j: (i, 0))],
        out_shape=[jax.ShapeDtypeStruct((n_tok, RKV_W), F32),
                   jax.ShapeDtypeStruct((n_tok, REST_W), F32),
                   jax.ShapeDtypeStruct((n_tok, RWKV_WIDTH), F32)],
        scratch_shapes=[pltpu.VMEM((tm, D_MODEL), BF16)],
        compiler_params=pltpu.CompilerParams(
            dimension_semantics=("parallel", "arbitrary"), vmem_limit_bytes=VMEM_LIMIT_BYTES),
        name="in_projection",
    )(x2d, mod8, ng, w_rkv_bf, w_rest_bf, mu, k_k)


def _wkv_masks(reverse):
    i = lax.broadcasted_iota(jnp.int32, (PAIR, PAIR), 0)
    j = lax.broadcasted_iota(jnp.int32, (PAIR, PAIR), 1)
    same = _div_pow2(i, WKV_CHUNK) == _div_pow2(j, WKV_CHUNK)
    ti = _mod_pow2(i, WKV_CHUNK)
    tj = _mod_pow2(j, WKV_CHUNK)
    if reverse:
        strict = same & (tj > ti)
        incl = same & (tj >= ti)
    else:
        strict = same & (tj < ti)
        incl = same & (tj <= ti)
    blk = {bs: _div_pow2(i, bs) == _div_pow2(j, bs) for bs in (8, 16, 32, 64)}
    off = {bs: blk[2 * bs] & jnp.logical_not(blk[bs]) for bs in (8, 16, 32)}
    return dict(strict=strict, incl=incl, blk8=blk[8], off=off, eye=i == j)


def _wkv_chains(chains):
    lane = lax.broadcasted_iota(jnp.int32, (WKV_CHUNK, PAIR), 1)
    left = lane < HEAD_DIM
    st = lambda x: jnp.concatenate([jnp.where(left, x, 0.0), jnp.where(left, 0.0, x)], axis=0)
    each = lambda f: [f(ch) for ch in chains]

    def operands(ch):
        cum, tot = ch['cum'], ch['tot']
        e_inv = jnp.exp(-cum)
        e_end = jnp.exp(tot - cum)
        ch['Kh2'] = st(ch['kh'] * jnp.exp(ch['cum_ex']))
        ch['R2'] = st(ch['r'] * jnp.exp(cum))
        ch['Be2'] = st(ch['b'] * e_end).astype(BF16)
        ch['Kte2'] = st(ch['kt'] * e_end).astype(BF16)
        ch['V2b'] = st(ch['v']).astype(BF16)
        ch['lhsA'] = jnp.concatenate([ch['Kh2'], ch['R2']], axis=0).astype(BF16)
        bi = (ch['b'] * e_inv).astype(BF16)
        ki = (ch['kt'] * e_inv).astype(BF16)
        ch['rhsA'] = jnp.concatenate([bi, bi, ki, ki], axis=0)
    each(operands)

    def a_blocks(ch):
        m = ch['masks']
        A = _mm(ch['lhsA'], ch['rhsA'], NT)
        ch['A_ub'] = jnp.where(m['strict'], A[:PAIR, :PAIR], 0.0)
        ch['A_uvrv'] = jnp.concatenate([jnp.where(m['strict'], A[:PAIR, PAIR:], 0.0),
                                        jnp.where(m['incl'], A[PAIR:, PAIR:], 0.0)], axis=0).astype(BF16)
        ch['A_rb'] = jnp.where(m['incl'], A[PAIR:, :PAIR], 0.0).astype(BF16)
        a8 = jnp.where(m['blk8'], ch['A_ub'], 0.0)
        ch['A8'] = a8.astype(BF16)
        ch['X0'] = jnp.where(m['eye'], 1.0, -a8)
    each(a_blocks)

    def sq1(ch):
        ch['A8_2'] = _mm(ch['A8'], ch['A8']).astype(BF16)
    each(sq1)

    def sq2(ch):
        ch['A8_4'] = _mm(ch['A8_2'], ch['A8_2']).astype(BF16)
        ch['AV'] = _mm(ch['A_uvrv'], ch['V2b'])
    each(sq2)

    def neu1(ch):
        ch['X'] = ch['X0'] + _mm(ch['X0'], ch['A8_2'])
    each(neu1)

    def neu2(ch):
        ch['T'] = ch['X'] + _mm(ch['X'], ch['A8_4'])
    each(neu2)

    for bs in (8, 16, 32):
        def merge_a(ch, bs=bs):
            off = jnp.where(ch['masks']['off'][bs], ch['A_ub'], 0.0)
            ch['Tb'] = ch['T'].astype(BF16)
            ch['TO'] = _mm(ch['Tb'], off)
        each(merge_a)

        def merge_b(ch):
            ch['T'] = ch['T'] - _mm(ch['TO'], ch['Tb'])
        each(merge_b)

    def q_stage(ch):
        ch['Q'] = _mm(ch['T'], jnp.concatenate([ch['Kh2'], ch['AV'][:PAIR]], axis=1)).astype(BF16)
    each(q_stage)

    def mn_stage(ch):
        QB = _mm(ch['Q'], ch['Be2'], TN)
        ch['M2'] = jnp.where(ch['masks']['eye'], jnp.exp(ch['tot']), 0.0) - QB[:PAIR]
        ch['N2'] = _mm(ch['V2b'], ch['Kte2'], TN) - QB[PAIR:]
        AQ = _mm(ch['A_rb'], ch['Q'])
        ch['G1'] = ch['R2'] - AQ[:, :PAIR]
        ch['G2'] = ch['AV'][PAIR:] - AQ[:, PAIR:]
    each(mn_stage)

    def out_stage(ch):
        S2 = ch['S2'].astype(BF16)
        Y2 = _mm(ch['G1'], S2, NT) + ch['G2']
        ch['y'] = Y2[:WKV_CHUNK] + Y2[WKV_CHUNK:]
        ch['S2n'] = _mm(S2, ch['M2']) + ch['N2']
    each(out_stage)


def _wkv_kernel(rkv0, khr0, lor0, rkv1, khr1, lor1,
                ka_ref, rk_ref, w0_ref, wup_ref, a0_ref, aup_ref, s0_ref,
                y0_ref, y1_ref, bo0_ref, bo1_ref, sfin_ref, s_scr, *, zero_state, nb):
    c = pl.program_id(1)
    nc = pl.num_programs(1)

    @pl.when(c == 0)
    def _():
        if zero_state:
            s_scr[...] = jnp.zeros(s_scr.shape, F32)
        else:
            s_scr[...] = s0_ref[...]

    ones_bd = _group_ones()
    C = RWKV_WIDTH
    L = WKV_CHUNK
    ti = lax.broadcasted_iota(jnp.int32, (L, L), 0)
    tj = lax.broadcasted_iota(jnp.int32, (L, L), 1)
    masks = [_wkv_masks(False), _wkv_masks(True)]
    tri = [jnp.where(tj <= ti, 1.0, 0.0).astype(BF16), jnp.where(tj >= ti, 1.0, 0.0).astype(BF16)]

    chains = []
    for e in range(nb):
        for d, (rkv, khr, lor, y_ref, bo_ref) in enumerate(
                ((rkv0, khr0, lor0, y0_ref, bo0_ref), (rkv1, khr1, lor1, y1_ref, bo1_ref))):
            reverse = d == 1
            r = rkv[e, 0, :, 0:C]
            k = rkv[e, 0, :, C:2 * C]
            v = rkv[e, 0, :, 2 * C:3 * C]
            kh = khr[e, 0]
            lo_dec = lor[e, 0, :, 0:LANES]
            w_raw = _mm(jnp.tanh(lo_dec), wup_ref[d]) + w0_ref[d:d + 1, :]
            lw = -DECAY_SCALE * jax.nn.sigmoid(w_raw)
            a = jax.nn.sigmoid(_mm(lo_dec, aup_ref[d]) + a0_ref[d:d + 1, :])
            kt = k * (1.0 + (a - 1.0) * ka_ref[...])
            bo_ref[e, 0] = _group_sum(r * kt * rk_ref[...], ones_bd) * v
            b = a * kh

            hi, lo2 = _split2(lw)
            cum = (lax.dot_general(tri[d], hi, NN, preferred_element_type=F32)
                   + lax.dot_general(tri[d], lo2, NN, preferred_element_type=F32))
            cum_ex = cum - lw
            tot = cum[0:1, :] if reverse else cum[L - 1:L, :]
            for p in range(N_PAIRS):
                sl = slice(p * PAIR, (p + 1) * PAIR)
                chains.append(dict(e=e, d=d, p=p, sl=sl, y_ref=y_ref, masks=masks[d], S2=s_scr[e, d, p],
                                   r=r[:, sl], kt=kt[:, sl], v=v[:, sl], kh=kh[:, sl], b=b[:, sl],
                                   cum=cum[:, sl], cum_ex=cum_ex[:, sl], tot=tot[:, sl]))

    _wkv_chains(chains)
    for ch in chains:
        ch['y_ref'][ch['e'], 0, :, ch['sl']] = ch['y']
        s_scr[ch['e'], ch['d'], ch['p']] = ch['S2n']

    @pl.when(c == nc - 1)
    def _():
        sfin_ref[...] = s_scr[...]


def _wkv(rkv, rest, kh, s0_pairs, n_batch, seq, k_a, r_k, w0, wup_bf, a0, aup_bf):
    L = WKV_CHUNK
    nc = seq // L
    nb = 2
    n_tok = n_batch * seq
    zero_state = s0_pairs is None
    if zero_state:
        s0_pairs = jnp.zeros((nb, 2, N_PAIRS, PAIR, PAIR), F32)
    rkv4 = rkv.reshape(n_batch, nc, L, RKV_W)
    rest4 = rest.reshape(n_batch, nc, L, REST_W)
    kh4 = kh.reshape(n_batch, nc, L, RWKV_WIDTH)

    def chunk(d):
        return (lambda c: c) if d == 0 else (lambda c: nc - 1 - c)

    in_specs = []
    for d in (0, 1):
        g = chunk(d)
        in_specs += [
            pl.BlockSpec((nb, 1, L, RKV_W), lambda b, c, g=g: (b, g(c), 0, 0)),
            pl.BlockSpec((nb, 1, L, RWKV_WIDTH), lambda b, c, g=g: (b, g(c), 0, 0)),
            pl.BlockSpec((nb, 1, L, 2 * LANES), lambda b, c, g=g: (b, g(c), 0, COL_LORA // (2 * LANES))),
        ]
    const = lambda shape: pl.BlockSpec(shape, lambda b, c: (0,) * len(shape))
    state_spec = lambda idx: pl.BlockSpec((nb, 2, N_PAIRS, PAIR, PAIR), idx)
    in_specs += [
        const((1, RWKV_WIDTH)), const((1, RWKV_WIDTH)),
        const((2, RWKV_WIDTH)), const((2, LANES, RWKV_WIDTH)),
        const((2, RWKV_WIDTH)), const((2, LANES, RWKV_WIDTH)),
        state_spec((lambda b, c: (0, 0, 0, 0, 0)) if zero_state else (lambda b, c: (b, 0, 0, 0, 0))),
    ]
    tok_spec = lambda d: pl.BlockSpec((nb, 1, L, RWKV_WIDTH), lambda b, c, g=chunk(d): (b, g(c), 0, 0))
    out_specs = [tok_spec(0), tok_spec(1), tok_spec(0), tok_spec(1), state_spec(lambda b, c: (b, 0, 0, 0, 0))]
    tok = jax.ShapeDtypeStruct((n_batch, nc, L, RWKV_WIDTH), F32)
    outs = pl.pallas_call(
        functools.partial(_wkv_kernel, zero_state=zero_state, nb=nb),
        grid=(n_batch // nb, nc),
        in_specs=in_specs,
        out_specs=out_specs,
        out_shape=[tok, tok, tok, tok, jax.ShapeDtypeStruct((n_batch, 2, N_PAIRS, PAIR, PAIR), F32)],
        scratch_shapes=[pltpu.VMEM((nb, 2, N_PAIRS, PAIR, PAIR), F32)],
        compiler_params=pltpu.CompilerParams(
            dimension_semantics=("parallel", "arbitrary"), vmem_limit_bytes=VMEM_LIMIT_BYTES),
        name="wkv_scan",
    )(rkv4, kh4, rest4, rkv4, kh4, rest4, k_a, r_k, w0, wup_bf, a0, aup_bf, s0_pairs)
    y0, y1, bo0, bo1, s_fin = outs
    flat = lambda t: t.reshape(n_tok, RWKV_WIDTH)
    return flat(y0), flat(y1), flat(bo0), flat(bo1), s_fin


def _rope(x, cos, sin_signed):
    w = x.shape[1]
    lane = lax.broadcasted_iota(jnp.int32, x.shape, 1)
    partner = jnp.where(_mod_pow2(lane, 2 * ROPE_FREQS) < ROPE_FREQS,
                        pltpu.roll(x, w - ROPE_FREQS, 1), pltpu.roll(x, ROPE_FREQS, 1))
    return x * cos + partner * sin_signed


def _attn_kernel(*refs, tq, seq, past, use_rope):
    it = iter(refs)
    q_ref, kv_ref, qg_ref, kg_ref = next(it), next(it), next(it), next(it)
    if use_rope:
        cq_ref, sq_ref, ck_ref, sk_ref = next(it), next(it), next(it), next(it)
    if past:
        pk_ref, pv_ref = next(it), next(it)
    o_ref, ko_ref, vo_ref = next(it), next(it), next(it)
    kvar_scr, vvar_scr = next(it), next(it)

    ones_bd = _group_ones()

    @pl.when(pl.program_id(1) == 0)
    def _():
        kv = kv_ref[...]
        k_raw = kv[:, :KV_WIDTH]
        kn = k_raw * lax.rsqrt(_group_sum(k_raw * k_raw, ones_bd) * (1.0 / HEAD_DIM) + NORM_EPS) * kg_ref[...]
        if use_rope:
            kn = _rope(kn, ck_ref[...], sk_ref[...])
        vn = kv[:, KV_WIDTH:]
        ko_ref[...] = kn
        vo_ref[...] = vn
        pieces = [(past, seq, kn, vn)]
        if past:
            pieces.append((0, past, pk_ref[0], pv_ref[0]))
        for start, n, kx, vx in pieces:
            left = lax.broadcasted_iota(jnp.int32, kx.shape, 1) < HEAD_DIM
            for x, scr in ((kx, kvar_scr), (vx, vvar_scr)):
                sw = pltpu.roll(x, HEAD_DIM, 1)
                scr[0, start:start + n, :] = jnp.where(left, x, 0.0).astype(BF16)
                scr[1, start:start + n, :] = jnp.where(left, 0.0, sw).astype(BF16)
                scr[2, start:start + n, :] = jnp.where(left, sw, 0.0).astype(BF16)
                scr[3, start:start + n, :] = jnp.where(left, 0.0, x).astype(BF16)

    scale = HEAD_DIM ** -0.5
    n_blk = ATTN_WIDTH // LANES
    qn = []
    for jb in range(n_blk):
        qb = q_ref[:, jb * LANES:(jb + 1) * LANES]
        x = qb * lax.rsqrt(_group_sum(qb * qb, ones_bd) * (1.0 / HEAD_DIM) + NORM_EPS) * qg_ref[...]
        if use_rope:
            x = _rope(x, cq_ref[...], sq_ref[...])
        qn.append((x * scale).astype(BF16))
    heads = [(jb, 2 * (jb // (n_blk // N_KV_HEADS)) + side) for jb in range(n_blk) for side in (0, 1)]
    scores = [lax.dot_general(qn[jb], kvar_scr[var], NT, preferred_element_type=F32) for jb, var in heads]
    exps = [jnp.exp(s - jnp.max(s, axis=-1, keepdims=True)) for s in scores]
    outs = [lax.dot_general(e.astype(BF16), vvar_scr[var], NN, preferred_element_type=F32)
            for e, (jb, var) in zip(exps, heads)]
    outs = [o / jnp.sum(e, axis=-1, keepdims=True) for o, e in zip(outs, exps)]
    for jb in range(n_blk):
        o_ref[:, jb * LANES:(jb + 1) * LANES] = outs[2 * jb] + outs[2 * jb + 1]


def _attention(rest, n_batch, seq, q_gain128, k_gain128, rope128, past_k, past_v):
    tq = 256
    nq = seq // tq
    n_tok = n_batch * seq
    use_rope = rope128 is not None
    past = 0 if past_k is None else past_k.shape[1]
    in_specs = [
        pl.BlockSpec((tq, ATTN_WIDTH), lambda b, i: (b * nq + i, COL_Q // ATTN_WIDTH)),
        pl.BlockSpec((seq, 2 * KV_WIDTH), lambda b, i: (b, COL_KV // (2 * KV_WIDTH))),
        pl.BlockSpec((1, LANES), lambda b, i: (0, 0)),
        pl.BlockSpec((1, LANES), lambda b, i: (0, 0)),
    ]
    args = [rest, rest, q_gain128, k_gain128]
    if use_rope:
        cos, sin = rope128
        in_specs += [pl.BlockSpec((tq, LANES), lambda b, i: (i, 0)),
                     pl.BlockSpec((tq, LANES), lambda b, i: (i, 0)),
                     pl.BlockSpec((seq, LANES), lambda b, i: (0, 0)),
                     pl.BlockSpec((seq, LANES), lambda b, i: (0, 0))]
        args += [cos, sin, cos, sin]
    if past:
        in_specs += [pl.BlockSpec((1, past, KV_WIDTH), lambda b, i: (b, 0, 0)),
                     pl.BlockSpec((1, past, KV_WIDTH), lambda b, i: (b, 0, 0))]
        args += [past_k, past_v]
    out_specs = [pl.BlockSpec((tq, ATTN_WIDTH), lambda b, i: (b * nq + i, 0)),
                 pl.BlockSpec((seq, KV_WIDTH), lambda b, i: (b, 0)),
                 pl.BlockSpec((seq, KV_WIDTH), lambda b, i: (b, 0))]
    return pl.pallas_call(
        functools.partial(_attn_kernel, tq=tq, seq=seq, past=past, use_rope=use_rope),
        grid=(n_batch, nq),
        in_specs=in_specs,
        out_specs=out_specs,
        out_shape=[jax.ShapeDtypeStruct((n_tok, ATTN_WIDTH), F32),
                   jax.ShapeDtypeStruct((n_tok, KV_WIDTH), F32),
                   jax.ShapeDtypeStruct((n_tok, KV_WIDTH), F32)],
        scratch_shapes=[pltpu.VMEM((2 * N_KV_HEADS, past + seq, KV_WIDTH), BF16),
                        pltpu.VMEM((2 * N_KV_HEADS, past + seq, KV_WIDTH), BF16)],
        compiler_params=pltpu.CompilerParams(
            dimension_semantics=("parallel", "arbitrary"), vmem_limit_bytes=VMEM_LIMIT_BYTES),
        name="attention",
    )(*args)


def _post_kernel(y0_ref, y1_ref, bo0_ref, bo1_ref, lor_ref, oa_ref, gr_ref, ga_ref, x_ref, mod_ref,
                 gup_ref, gnw_ref, gnb_ref, wbr_ref, wout_ref, ng_ref, x1_ref, h2_ref):
    ones_bd = _group_ones()
    y = y0_ref[...] + y1_ref[...]
    mean = _group_sum(y, ones_bd) * (1.0 / HEAD_DIM)
    yc = y - mean
    var = _group_sum(yc * yc, ones_bd) * (1.0 / HEAD_DIM)
    yn = yc * lax.rsqrt(var + GN_EPS) * gnw_ref[...] + gnb_ref[...]
    lg = lor_ref[:, LANES:]
    g = _mm(jax.nn.sigmoid(lg), gup_ref[...])
    o_r = (yn + (bo0_ref[...] + bo1_ref[...])) * g
    merged = (jax.nn.sigmoid(gr_ref[...]) * _mm(o_r, wbr_ref[0])
              + jax.nn.sigmoid(ga_ref[...]) * _mm(oa_ref[...], wbr_ref[1]))
    m = _mm(merged, wout_ref[...])
    g1 = mod_ref[0, 2:3, :]
    sh2 = mod_ref[0, 3:4, :]
    sc2 = mod_ref[0, 4:5, :]
    x1 = x_ref[...] + g1 * _rms(m, ng_ref[1:2, :])
    x1_ref[...] = x1
    h2_ref[...] = (_rms(x1, ng_ref[2:3, :]) * (1.0 + sc2) + sh2).astype(BF16)


def _post(y0, y1, bo0, bo1, rest, o_a, x2d, mod8, gup_bf, gn_w, gn_b, wbr_bf, wout_bf, ng, rows_per_mod):
    n_tok = x2d.shape[0]
    tm = 512
    mod_idx = (lambda i: (i * tm // rows_per_mod, 0, 0)) if rows_per_mod else (lambda i: (0, 0, 0))
    tok = lambda w: pl.BlockSpec((tm, w), lambda i: (i, 0))
    const = lambda shape: pl.BlockSpec(shape, lambda i: (0,) * len(shape))
    return pl.pallas_call(
        _post_kernel,
        grid=(n_tok // tm,),
        in_specs=[
            tok(RWKV_WIDTH), tok(RWKV_WIDTH), tok(RWKV_WIDTH), tok(RWKV_WIDTH),
            pl.BlockSpec((tm, 2 * LANES), lambda i: (i, COL_LORA // (2 * LANES))),
            tok(ATTN_WIDTH),
            pl.BlockSpec((tm, D_MODEL), lambda i: (i, COL_GR // D_MODEL)),
            pl.BlockSpec((tm, D_MODEL), lambda i: (i, COL_GA // D_MODEL)),
            tok(D_MODEL),
            pl.BlockSpec((1, SUBLANES, D_MODEL), mod_idx),
            const((GATE_RANK, RWKV_WIDTH)), const((1, RWKV_WIDTH)), const((1, RWKV_WIDTH)),
            const((2, RWKV_WIDTH, D_MODEL)), const((D_MODEL, D_MODEL)), const((4, D_MODEL)),
        ],
        out_specs=[tok(D_MODEL), tok(D_MODEL)],
        out_shape=[jax.ShapeDtypeStruct((n_tok, D_MODEL), F32), jax.ShapeDtypeStruct((n_tok, D_MODEL), BF16)],
        compiler_params=pltpu.CompilerParams(
            dimension_semantics=("parallel",), vmem_limit_bytes=VMEM_LIMIT_BYTES),
        name="merge_outproj",
    )(y0, y1, bo0, bo1, rest, o_a, rest, rest, x2d, mod8, gup_bf, gn_w, gn_b, wbr_bf, wout_bf, ng)


def _mlp_kernel(h_ref, x1_ref, mod_ref, up_ref, down_ref, ng_ref, o_ref):
    h = h_ref[...]
    f = None
    ff = D_MODEL
    for j in range(D_FF // ff):
        u = lax.dot_general(h, up_ref[:, j * ff:(j + 1) * ff], NN, preferred_element_type=F32)
        u = jnp.square(jnp.maximum(u, 0.0)).astype(BF16)
        part = lax.dot_general(u, down_ref[j * ff:(j + 1) * ff, :], NN, preferred_element_type=F32)
        f = part if f is None else f + part
    g2 = mod_ref[0, 5:6, :]
    o_ref[...] = x1_ref[...] + g2 * _rms(f, ng_ref[3:4, :])


def _mlp(h2, x1, mod8, up_bf, down_bf, ng, rows_per_mod):
    n_tok = x1.shape[0]
    tm = 512
    mod_idx = (lambda i: (i * tm // rows_per_mod, 0, 0)) if rows_per_mod else (lambda i: (0, 0, 0))
    tok = lambda: pl.BlockSpec((tm, D_MODEL), lambda i: (i, 0))
    return pl.pallas_call(
        _mlp_kernel,
        grid=(n_tok // tm,),
        in_specs=[
            tok(), tok(),
            pl.BlockSpec((1, SUBLANES, D_MODEL), mod_idx),
            pl.BlockSpec((D_MODEL, D_FF), lambda i: (0, 0)),
            pl.BlockSpec((D_FF, D_MODEL), lambda i: (0, 0)),
            pl.BlockSpec((4, D_MODEL), lambda i: (0, 0)),
        ],
        out_specs=tok(),
        out_shape=jax.ShapeDtypeStruct((n_tok, D_MODEL), F32),
        compiler_params=pltpu.CompilerParams(
            dimension_semantics=("parallel",), vmem_limit_bytes=VMEM_LIMIT_BYTES),
        name="mlp",
    )(h2, x1, mod8, up_bf, down_bf, ng)


def _rope_tables(seq):
    n_rows = seq // GRID_W
    rows = np.repeat(np.arange(n_rows, dtype=np.float32), GRID_W)
    cols = np.tile(np.arange(GRID_W, dtype=np.float32), n_rows)
    half = HEAD_DIM // 2
    freqs = 1.0 / (jnp.asarray(ROPE_THETA, F32) ** (jnp.arange(0, half, 2, dtype=F32) / half))
    ang_r = jnp.asarray(rows)[:, None] * freqs
    ang_c = jnp.asarray(cols)[:, None] * freqs
    cr, sr, cc, sc = jnp.cos(ang_r), jnp.sin(ang_r), jnp.cos(ang_c), jnp.sin(ang_c)
    cos64 = jnp.concatenate([cr, cr, cc, cc], axis=1)
    sin64 = jnp.concatenate([-sr, sr, -sc, sc], axis=1)
    return jnp.tile(cos64, (1, LANES // HEAD_DIM)), jnp.tile(sin64, (1, LANES // HEAD_DIM))


def _pairs_from_state(s):
    b = s.shape[0]
    s = s.reshape(b, 2, N_PAIRS, 2, HEAD_DIM, HEAD_DIM)
    z = jnp.zeros_like(s[:, :, :, 0])
    top = jnp.concatenate([s[:, :, :, 0], z], axis=-1)
    bot = jnp.concatenate([z, s[:, :, :, 1]], axis=-1)
    return jnp.concatenate([top, bot], axis=-2)


def _state_from_pairs(s2):
    b = s2.shape[0]
    s = jnp.stack([s2[:, :, :, :HEAD_DIM, :HEAD_DIM], s2[:, :, :, HEAD_DIM:, HEAD_DIM:]], axis=3)
    return s.reshape(b, 2, N_RWKV_HEADS, HEAD_DIM, HEAD_DIM)


def _layer(x2d, n_batch, seq, mod8, rows_per_mod, W, rope128, past_k, past_v, s0):
    rkv, rest, kh = _in_projection(x2d, mod8, W['ng'][0:1], W['w_rkv'], W['w_rest'], W['mu'], W['k_k'],
                                   rows_per_mod, seq)
    s0_pairs = None if s0 is None else _pairs_from_state(s0)
    y0, y1, bo0, bo1, s_fin = _wkv(rkv, rest, kh, s0_pairs, n_batch, seq, W['k_a'], W['r_k'],
                                   W['w0'], W['wup'], W['a0'], W['aup'])
    o_a, k_own, v_own = _attention(rest, n_batch, seq, W['q_gain'], W['k_gain'], rope128, past_k, past_v)
    x1, h2 = _post(y0, y1, bo0, bo1, rest, o_a, x2d, mod8, W['gup'], W['gn_w'], W['gn_b'],
                   W['w_br'], W['w_out'], W['ng'], rows_per_mod)
    x2 = _mlp(h2, x1, mod8, W['up'], W['down'], W['ng'], rows_per_mod)
    return x2, k_own, v_own, s_fin


def kernel(x_prompt, x_sample, cache_k, cache_v, state_wkv, c, c_ctx, w_in, w_br, w_out, w_mod, b_mod, norm_g, mlp_up, mlp_down, rwkv_mu, rwkv_k_k, rwkv_k_a, rwkv_r_k, decay_w0, decay_up, iclr_a0, iclr_up, gate_up, gn_w, gn_b, q_gain, k_gain):
    n_ctx, seq_ctx, _ = x_prompt.shape
    n_dec, seq_dec, _ = x_sample.shape
    past = cache_k.shape[2]

    cvec8 = jnp.zeros((SUBLANES, D_MODEL), F32).at[0].set(c_ctx).at[1:1 + n_dec].set(c)
    mod = _modulation(cvec8, w_mod, b_mod)
    mod = jnp.pad(mod.reshape(DEPTH, SUBLANES, N_MOD, D_MODEL), ((0, 0), (0, 0), (0, SUBLANES - N_MOD), (0, 0)))

    o = np.cumsum((0, RWKV_WIDTH, RWKV_WIDTH, RWKV_WIDTH, DECAY_RANK, ICLR_RANK, GATE_RANK,
                   ATTN_WIDTH, KV_WIDTH, KV_WIDTH, 2 * D_MODEL))
    w_rkv_bf = w_in[:, :, o[0]:o[3]].astype(BF16)
    w_rest_bf = jnp.concatenate([w_in[:, :, o[9]:o[10]], w_in[:, :, o[6]:o[7]], w_in[:, :, o[3]:o[6]],
                                 w_in[:, :, o[7]:o[9]]], axis=-1).astype(BF16)
    zpad = jnp.zeros((DEPTH, 2, DECAY_RANK, RWKV_WIDTH), F32)
    wup_bf = jnp.concatenate([decay_up, zpad], axis=2).astype(BF16)
    aup_bf = jnp.concatenate([zpad, iclr_up], axis=2).astype(BF16)

    layers = []
    for l in range(DEPTH):
        layers.append(dict(
            w_rkv=w_rkv_bf[l], w_rest=w_rest_bf[l], ng=norm_g[l], mu=rwkv_mu[l], k_k=rwkv_k_k[l][None], k_a=rwkv_k_a[l][None],
            r_k=rwkv_r_k[l].reshape(1, RWKV_WIDTH), w0=decay_w0[l], wup=wup_bf[l],
            a0=iclr_a0[l], aup=aup_bf[l], gup=gate_up[l].astype(BF16),
            gn_w=gn_w[l][None], gn_b=gn_b[l][None], w_br=w_br[l].astype(BF16), w_out=w_out[l].astype(BF16),
            up=mlp_up[l].astype(BF16), down=mlp_down[l].astype(BF16),
            q_gain=jnp.tile(q_gain[l], LANES // HEAD_DIM)[None], k_gain=jnp.tile(k_gain[l], LANES // HEAD_DIM)[None],
        ))

    x = x_prompt.reshape(n_ctx * seq_ctx, D_MODEL)
    ks, vs, ss = [], [], []
    for l in range(DEPTH):
        x, k_l, v_l, s_l = _layer(x, n_ctx, seq_ctx, mod[l, 0:1], 0, layers[l], None, None, None, None)
        ks.append(k_l.reshape(n_ctx, seq_ctx, N_KV_HEADS, HEAD_DIM))
        vs.append(v_l.reshape(n_ctx, seq_ctx, N_KV_HEADS, HEAD_DIM))
        ss.append(_state_from_pairs(s_l))
    y_prompt = x.reshape(n_ctx, seq_ctx, D_MODEL)

    rope128 = _rope_tables(seq_dec)
    x = x_sample.reshape(n_dec * seq_dec, D_MODEL)
    for l in range(DEPTH):
        x, _, _, _ = _layer(x, n_dec, seq_dec, mod[l, 1:1 + n_dec], seq_dec, layers[l], rope128,
                            cache_k[:, l].reshape(n_dec, past, KV_WIDTH),
                            cache_v[:, l].reshape(n_dec, past, KV_WIDTH), state_wkv[:, l])
    y_sample = x.reshape(n_dec, seq_dec, D_MODEL)

    return (y_prompt, y_sample, jnp.stack(ks, axis=1), jnp.stack(vs, axis=1), jnp.stack(ss, axis=1))
```

```python
import functools

import numpy as np
import jax
import jax.numpy as jnp
from jax import lax
from jax.experimental import pallas as pl
from jax.experimental.pallas import tpu as pltpu

F32 = jnp.float32
BF16 = jnp.bfloat16

D_MODEL = 1024
DEPTH = 2
GRID_W = 64
HEAD_DIM = 64
N_RWKV_HEADS = 8
RWKV_WIDTH = N_RWKV_HEADS * HEAD_DIM
N_Q_HEADS = 8
N_KV_HEADS = 2
ATTN_WIDTH = N_Q_HEADS * HEAD_DIM
KV_WIDTH = N_KV_HEADS * HEAD_DIM
DECAY_RANK = 64
ICLR_RANK = 64
GATE_RANK = 128
D_FF = 4 * D_MODEL
ROPE_THETA = 10000.0
ROPE_FREQS = HEAD_DIM // 4
N_MOD = 6
NORM_EPS = 1e-6
GN_EPS = 64e-5
DECAY_SCALE = 0.606531
D_IN = 3 * RWKV_WIDTH + DECAY_RANK + ICLR_RANK + GATE_RANK + ATTN_WIDTH + 2 * KV_WIDTH + 2 * D_MODEL

LANES = 128
SUBLANES = 8
VMEM_LIMIT_BYTES = 56 * 1024 * 1024

RKV_W = 3 * RWKV_WIDTH
COL_GR = 0
COL_GA = COL_GR + D_MODEL
COL_Q = COL_GA + D_MODEL
COL_LORA = COL_Q + ATTN_WIDTH
COL_KV = COL_LORA + 2 * LANES
REST_W = COL_KV + 2 * KV_WIDTH

WKV_CHUNK = 64
PAIR = 2 * HEAD_DIM
N_PAIRS = N_RWKV_HEADS // 2

NN = (((1,), (0,)), ((), ()))
NT = (((1,), (1,)), ((), ()))
TN = (((0,), (0,)), ((), ()))


def _mm(a, b, dims=NN):
    return lax.dot_general(a.astype(BF16), b.astype(BF16), dims, preferred_element_type=F32)


def _split2(x):
    hi = x.astype(BF16)
    lo = (x - hi.astype(F32)).astype(BF16)
    return hi, lo


def _mm_exact_rhs(x, m_bf16):
    hi, lo = _split2(x)
    d = lambda p: lax.dot_general(p, m_bf16, NN, preferred_element_type=F32)
    return d(hi) + d(lo)


def _div_pow2(i, n):
    return lax.shift_right_logical(i, int(np.log2(n)))


def _mod_pow2(i, n):
    return lax.bitwise_and(i, n - 1)


def _group_ones():
    i = lax.broadcasted_iota(jnp.int32, (LANES, LANES), 0)
    j = lax.broadcasted_iota(jnp.int32, (LANES, LANES), 1)
    return jnp.where(_div_pow2(i, HEAD_DIM) == _div_pow2(j, HEAD_DIM), 1.0, 0.0).astype(BF16)


def _group_sum(x, ones_bd):
    blocks = [
        _mm_exact_rhs(x[:, i * LANES:(i + 1) * LANES], ones_bd)
        for i in range(x.shape[1] // LANES)
    ]
    return blocks[0] if len(blocks) == 1 else jnp.concatenate(blocks, axis=1)


def _lspec(l, shape, single=False):
    kw = dict(pipeline_mode=pl.Buffered(1)) if single else {}
    return pl.BlockSpec((None,) + tuple(shape), lambda *g: (l,) + (0,) * len(shape), **kw)


def _mod_spec(l, row0, rows_per_mod, tm):
    if rows_per_mod:
        return pl.BlockSpec((None, 1, SUBLANES, D_MODEL), lambda i, *g: (l, row0 + i * tm // rows_per_mod, 0, 0))
    return pl.BlockSpec((None, 1, SUBLANES, D_MODEL), lambda *g: (l, row0, 0, 0))


def _rms(x, g):
    return x * lax.rsqrt(jnp.mean(x * x, axis=-1, keepdims=True) + NORM_EPS) * g


def _mod_kernel(c_ref, w_ref, b_ref, o_ref):
    c = c_ref[...]
    s = c * jax.nn.sigmoid(c)
    o_ref[0] = _mm(s, w_ref[0]) + b_ref[0]


def _modulation(cvec8, w_mod, b_mod):
    tn = D_MODEL
    n = N_MOD * D_MODEL
    return pl.pallas_call(
        _mod_kernel,
        grid=(DEPTH, n // tn),
        in_specs=[
            pl.BlockSpec((SUBLANES, D_MODEL), lambda l, j: (0, 0)),
            pl.BlockSpec((1, D_MODEL, tn), lambda l, j: (l, 0, j)),
            pl.BlockSpec((1, 1, tn), lambda l, j: (l, 0, j)),
        ],
        out_specs=pl.BlockSpec((1, SUBLANES, tn), lambda l, j: (l, 0, j)),
        out_shape=jax.ShapeDtypeStruct((DEPTH, SUBLANES, n), F32),
        name="modulation",
    )(cvec8, w_mod, b_mod.reshape(DEPTH, 1, n))


def _shift_mix(x, mu):
    n = x.shape[0]
    row = lax.broadcasted_iota(jnp.int32, x.shape, 0)
    x_prev = jnp.where(row == 0, 0.0, pltpu.roll(x, 1, 0))
    x_next = jnp.where(row == n - 1, 0.0, pltpu.roll(x, n - 1, 0))
    return x * (1.0 - mu) + (0.5 * mu) * (x_prev + x_next)


def _inproj_kernel(x_ref, mod_ref, ng_ref, wrkv_ref, wrest_ref, mu_ref, kk_ref,
                   rkv_ref, rest_ref, kh_ref, h_scr, *, tm, seq):
    j = pl.program_id(1)
    C = RWKV_WIDTH

    def mix_in_place(i):
        for s in range(tm // seq):
            rows = slice(s * seq, (s + 1) * seq)
            cols = slice(i * C, (i + 1) * C)
            mixed = _shift_mix(rkv_ref[rows, cols], mu_ref[i:i + 1, :])
            rkv_ref[rows, cols] = mixed
            if i == 1:
                kk = mixed * kk_ref[...]
                kh_ref[rows, :] = kk * lax.rsqrt(_group_sum(kk * kk, _group_ones()) + 1e-12)

    @pl.when(j == 0)
    def _():
        sh = mod_ref[0, 0:1, :]
        sc = mod_ref[0, 1:2, :]
        ng = ng_ref[0:1, :]
        sub = 256
        for i in range(tm // sub):
            x = x_ref[i * sub:(i + 1) * sub, :]
            h_scr[i * sub:(i + 1) * sub, :] = (_rms(x, ng) * (1.0 + sc) + sh).astype(BF16)
        rkv_ref[...] = lax.dot_general(h_scr[...], wrkv_ref[...], NN, preferred_element_type=F32)

    for jj in range(1, 4):
        @pl.when(j == jj)
        def _(jj=jj):
            mix_in_place(jj - 1)
            rest_ref[...] = lax.dot_general(h_scr[...], wrest_ref[...], NN, preferred_element_type=F32)


def _in_projection(x2d, P, l, mod_row0, rows_per_mod, seq):
    n_tok = x2d.shape[0]
    tm = 1024
    tn = REST_W // 3
    assert tm % seq == 0
    rest_blk = lambda i, j: (i, jnp.maximum(j - 1, 0))
    return pl.pallas_call(
        functools.partial(_inproj_kernel, tm=tm, seq=seq),
        grid=(n_tok // tm, 4),
        in_specs=[
            pl.BlockSpec((tm, D_MODEL), lambda i, j: (i, 0)),
            _mod_spec(l, mod_row0, rows_per_mod, tm),
            _lspec(l, (4, D_MODEL)),
            _lspec(l, (D_MODEL, RKV_W), single=True),
            pl.BlockSpec((None, D_MODEL, tn), lambda i, j: (l, 0, jnp.maximum(j - 1, 0))),
            _lspec(l, (3, RWKV_WIDTH)),
            _lspec(l, (1, RWKV_WIDTH)),
        ],
        out_specs=[pl.BlockSpec((tm, RKV_W), lambda i, j: (i, 0)),
                   pl.BlockSpec((tm, tn), rest_blk),
                   pl.BlockSpec((tm, RWKV_WIDTH), lambda i, j: (i, 0))],
        out_shape=[jax.ShapeDtypeStruct((n_tok, RKV_W), F32),
                   jax.ShapeDtypeStruct((n_tok, REST_W), F32),
                   jax.ShapeDtypeStruct((n_tok, RWKV_WIDTH), F32)],
        scratch_shapes=[pltpu.VMEM((tm, D_MODEL), BF16)],
        compiler_params=pltpu.CompilerParams(
            dimension_semantics=("parallel", "arbitrary"), vmem_limit_bytes=VMEM_LIMIT_BYTES),
        name="in_projection",
    )(x2d, P['mod'], P['ng'], P['w_rkv'], P['w_rest'], P['mu'], P['k_k'])


def _wkv_masks(reverse):
    i = lax.broadcasted_iota(jnp.int32, (PAIR, PAIR), 0)
    j = lax.broadcasted_iota(jnp.int32, (PAIR, PAIR), 1)
    same = _div_pow2(i, WKV_CHUNK) == _div_pow2(j, WKV_CHUNK)
    ti = _mod_pow2(i, WKV_CHUNK)
    tj = _mod_pow2(j, WKV_CHUNK)
    before = (tj > ti) if reverse else (tj < ti)
    upto = (tj >= ti) if reverse else (tj <= ti)
    strict = same & before
    incl = same & upto
    other = jnp.logical_not(same)
    strict_x = other & before
    incl_x = other & upto
    blk = {bs: _div_pow2(i, bs) == _div_pow2(j, bs) for bs in (8, 16, 32, 64)}
    off = {bs: blk[2 * bs] & jnp.logical_not(blk[bs]) for bs in (8, 16, 32)}
    return dict(strict=strict, incl=incl, strict_x=strict_x, incl_x=incl_x, blk8=blk[8], off=off, eye=i == j)


def _wkv_chains(chains):
    lane = lax.broadcasted_iota(jnp.int32, (WKV_CHUNK, PAIR), 1)
    left = lane < HEAD_DIM
    L = WKV_CHUNK
    st = lambda x: jnp.concatenate([jnp.where(left, x, 0.0), jnp.where(left, 0.0, x)], axis=0)
    stx = lambda x: jnp.concatenate([jnp.where(left, 0.0, x), jnp.where(left, x, 0.0)], axis=0)
    each = lambda f: [f(ch) for ch in chains]

    def operands(ch):
        cum, tot = ch['cum'], ch['tot']
        e_inv = jnp.exp(-cum)
        e_end = jnp.exp(tot - cum)
        ch['Kh2'] = st(ch['kh'] * jnp.exp(ch['cum_ex']))
        ch['R2'] = st(ch['r'] * jnp.exp(cum))
        ch['Be2'] = st(ch['b'] * e_end).astype(BF16)
        ch['Kte2'] = stx(ch['kt'] * e_end).astype(BF16)
        ch['V2b'] = stx(ch['v']).astype(BF16)
        kh2b = ch['Kh2'].astype(BF16)
        r2b = ch['R2'].astype(BF16)
        ch['lhsA'] = [jnp.concatenate([kh2b[j * L:(j + 1) * L], r2b[j * L:(j + 1) * L]], axis=0) for j in (0, 1)]
        bi = (ch['b'] * e_inv).astype(BF16)
        ki = (ch['kt'] * e_inv).astype(BF16)
        ch['rhsA'] = [jnp.concatenate([bi, ki], axis=0), jnp.concatenate([ki, bi], axis=0)]
    each(operands)

    def a_blocks(ch):
        m = ch['masks']
        a0, a1 = (_mm(ch['lhsA'][j], ch['rhsA'][j], NT) for j in (0, 1))
        top = jnp.concatenate([a0[:L], a1[:L]], axis=0)
        bot = jnp.concatenate([a0[L:], a1[L:]], axis=0)
        ch['A_ub'] = jnp.where(m['strict'], top, 0.0)
        ch['A_uvrv'] = jnp.concatenate([jnp.where(m['strict_x'], top, 0.0),
                                        jnp.where(m['incl_x'], bot, 0.0)], axis=0).astype(BF16)
        ch['A_rb'] = jnp.where(m['incl'], bot, 0.0).astype(BF16)
        a8 = jnp.where(m['blk8'], ch['A_ub'], 0.0)
        ch['A8'] = a8.astype(BF16)
        ch['X0'] = jnp.where(m['eye'], 1.0, -a8)
    each(a_blocks)

    def sq1(ch):
        ch['A8_2'] = _mm(ch['A8'], ch['A8']).astype(BF16)
    each(sq1)

    def sq2(ch):
        ch['A8_4'] = _mm(ch['A8_2'], ch['A8_2']).astype(BF16)
        ch['AV'] = _mm(ch['A_uvrv'], ch['V2b'])
    each(sq2)

    def neu1(ch):
        ch['X'] = ch['X0'] + _mm(ch['X0'], ch['A8_2'])
    each(neu1)

    def neu2(ch):
        ch['T'] = ch['X'] + _mm(ch['X'], ch['A8_4'])
    each(neu2)

    for bs in (8, 16, 32):
        def merge_a(ch, bs=bs):
            off = jnp.where(ch['masks']['off'][bs], ch['A_ub'], 0.0)
            ch['Tb'] = ch['T'].astype(BF16)
            ch['TO'] = _mm(ch['Tb'], off)
        each(merge_a)

        def merge_b(ch):
            ch['T'] = ch['T'] - _mm(ch['TO'], ch['Tb'])
        each(merge_b)

    def q_stage(ch):
        ch['Q'] = _mm(ch['T'], jnp.concatenate([ch['Kh2'], ch['AV'][:PAIR]], axis=1)).astype(BF16)
    each(q_stage)

    def mn_stage(ch):
        QB = _mm(ch['Q'], ch['Be2'], TN)
        ch['M2'] = jnp.where(ch['masks']['eye'], jnp.exp(ch['tot']), 0.0) - QB[:PAIR]
        ch['N2'] = _mm(ch['V2b'], ch['Kte2'], TN) - QB[PAIR:]
        AQ = _mm(ch['A_rb'], ch['Q'])
        ch['G1'] = ch['R2'] - AQ[:, :PAIR]
        ch['G2'] = ch['AV'][PAIR:] - AQ[:, PAIR:]
    each(mn_stage)

    def out_stage(ch):
        S2 = ch['S2'].astype(BF16)
        Y2 = _mm(ch['G1'], S2, NT) + ch['G2']
        ch['y'] = Y2[:WKV_CHUNK] + Y2[WKV_CHUNK:]
        ch['S2n'] = _mm(S2, ch['M2']) + ch['N2']
    each(out_stage)


def _wkv_kernel(rkv0, khr0, lor0, rkv1, khr1, lor1,
                ka_ref, rk_ref, w0_ref, wup_ref, a0_ref, aup_ref, s0_ref,
                *rest, zero_state, nb):
    y0_ref, y1_ref, bo0_ref, bo1_ref, sfin_ref, s_scr = rest[-6:]
    c = pl.program_id(1)
    nc = pl.num_programs(1)

    @pl.when(c == 0)
    def _():
        if zero_state:
            s_scr[...] = jnp.zeros(s_scr.shape, F32)
        else:
            s_scr[...] = s0_ref[...]

    ones_bd = _group_ones()
    C = RWKV_WIDTH
    L = WKV_CHUNK
    ti = lax.broadcasted_iota(jnp.int32, (L, L), 0)
    tj = lax.broadcasted_iota(jnp.int32, (L, L), 1)
    masks = [_wkv_masks(False), _wkv_masks(True)]
    tri = [jnp.where(tj <= ti, 1.0, 0.0).astype(BF16), jnp.where(tj >= ti, 1.0, 0.0).astype(BF16)]

    chains = []
    for e in range(nb):
        for d, (rkv, khr, lor, y_ref, bo_ref) in enumerate(
                ((rkv0, khr0, lor0, y0_ref, bo0_ref), (rkv1, khr1, lor1, y1_ref, bo1_ref))):
            reverse = d == 1
            r = rkv[e, 0, :, 0:C]
            k = rkv[e, 0, :, C:2 * C]
            v = rkv[e, 0, :, 2 * C:3 * C]
            kh = khr[e, 0]
            lo_dec = lor[e, 0, :, 0:LANES]
            w_raw = _mm(jnp.tanh(lo_dec), wup_ref[d]) + w0_ref[d:d + 1, :]
            lw = -DECAY_SCALE * jax.nn.sigmoid(w_raw)
            a = jax.nn.sigmoid(_mm(lo_dec, aup_ref[d]) + a0_ref[d:d + 1, :])
            kt = k * (1.0 + (a - 1.0) * ka_ref[...])
            bo_ref[e, 0] = _group_sum(r * kt * rk_ref[...], ones_bd) * v
            b = a * kh

            hi, lo2 = _split2(lw)
            cum = (lax.dot_general(tri[d], hi, NN, preferred_element_type=F32)
                   + lax.dot_general(tri[d], lo2, NN, preferred_element_type=F32))
            cum_ex = cum - lw
            tot = cum[0:1, :] if reverse else cum[L - 1:L, :]
            for p in range(N_PAIRS):
                sl = slice(p * PAIR, (p + 1) * PAIR)
                chains.append(dict(e=e, d=d, p=p, sl=sl, y_ref=y_ref, masks=masks[d], S2=s_scr[e, d, p],
                                   r=r[:, sl], kt=kt[:, sl], v=v[:, sl], kh=kh[:, sl], b=b[:, sl],
                                   cum=cum[:, sl], cum_ex=cum_ex[:, sl], tot=tot[:, sl]))

    _wkv_chains(chains)
    for ch in chains:
        ch['y_ref'][ch['e'], 0, :, ch['sl']] = ch['y']
        s_scr[ch['e'], ch['d'], ch['p']] = ch['S2n']

    @pl.when(c == nc - 1)
    def _():
        for e in range(nb):
            for d in (0, 1):
                for p in range(N_PAIRS):
                    s2 = s_scr[e, d, p]
                    sfin_ref[e, d, 2 * p] = s2[:HEAD_DIM, :HEAD_DIM]
                    sfin_ref[e, d, 2 * p + 1] = pltpu.roll(s2[HEAD_DIM:, :], HEAD_DIM, 1)[:, :HEAD_DIM]


def _wkv(rkv, rest, kh, s0_pairs, n_batch, seq, P, l, s_all):
    L = WKV_CHUNK
    nc = seq // L
    nb = 2
    n_tok = n_batch * seq
    zero_state = s0_pairs is None
    if zero_state:
        s0_pairs = jnp.zeros((nb, 2, N_PAIRS, PAIR, PAIR), F32)
    rkv4 = rkv.reshape(n_batch, nc, L, RKV_W)
    rest4 = rest.reshape(n_batch, nc, L, REST_W)
    kh4 = kh.reshape(n_batch, nc, L, RWKV_WIDTH)

    def chunk(d):
        return (lambda c: c) if d == 0 else (lambda c: nc - 1 - c)

    in_specs = []
    for d in (0, 1):
        g = chunk(d)
        in_specs += [
            pl.BlockSpec((nb, 1, L, RKV_W), lambda b, c, g=g: (b, g(c), 0, 0)),
            pl.BlockSpec((nb, 1, L, RWKV_WIDTH), lambda b, c, g=g: (b, g(c), 0, 0)),
            pl.BlockSpec((nb, 1, L, 2 * LANES), lambda b, c, g=g: (b, g(c), 0, COL_LORA // (2 * LANES))),
        ]
    state_spec = lambda idx: pl.BlockSpec((nb, 2, N_PAIRS, PAIR, PAIR), idx)
    in_specs += [
        _lspec(l, (1, RWKV_WIDTH)), _lspec(l, (1, RWKV_WIDTH)),
        _lspec(l, (2, RWKV_WIDTH)), _lspec(l, (2, LANES, RWKV_WIDTH)),
        _lspec(l, (2, RWKV_WIDTH)), _lspec(l, (2, LANES, RWKV_WIDTH)),
        state_spec((lambda b, c: (0, 0, 0, 0, 0)) if zero_state else (lambda b, c: (b, 0, 0, 0, 0))),
    ]
    args = [rkv4, kh4, rest4, rkv4, kh4, rest4, P['k_a'], P['r_k'], P['w0'], P['wup'], P['a0'], P['aup'], s0_pairs]
    aliases = {}
    if s_all is not None:
        in_specs.append(pl.BlockSpec(memory_space=pl.ANY))
        aliases = {len(args): 4}
        args.append(s_all)
    tok_spec = lambda d: pl.BlockSpec((nb, 1, L, RWKV_WIDTH), lambda b, c, g=chunk(d): (b, g(c), 0, 0))
    sfin_spec = pl.BlockSpec((nb, None, 2, N_RWKV_HEADS, HEAD_DIM, HEAD_DIM), lambda b, c: (b, l, 0, 0, 0, 0))
    out_specs = [tok_spec(0), tok_spec(1), tok_spec(0), tok_spec(1), sfin_spec]
    tok = jax.ShapeDtypeStruct((n_batch, nc, L, RWKV_WIDTH), F32)
    outs = pl.pallas_call(
        functools.partial(_wkv_kernel, zero_state=zero_state, nb=nb),
        grid=(n_batch // nb, nc),
        in_specs=in_specs,
        out_specs=out_specs,
        out_shape=[tok, tok, tok, tok,
                   jax.ShapeDtypeStruct((n_batch, DEPTH, 2, N_RWKV_HEADS, HEAD_DIM, HEAD_DIM), F32)],
        input_output_aliases=aliases,
        scratch_shapes=[pltpu.VMEM((nb, 2, N_PAIRS, PAIR, PAIR), F32)],
        compiler_params=pltpu.CompilerParams(
            dimension_semantics=("parallel", "arbitrary"), vmem_limit_bytes=VMEM_LIMIT_BYTES),
        name="wkv_scan",
    )(*args)
    y0, y1, bo0, bo1, s_fin = outs
    flat = lambda t: t.reshape(n_tok, RWKV_WIDTH)
    return flat(y0), flat(y1), flat(bo0), flat(bo1), s_fin


def _rope(x, cos, sin_signed):
    w = x.shape[1]
    lane = lax.broadcasted_iota(jnp.int32, x.shape, 1)
    partner = jnp.where(_mod_pow2(lane, 2 * ROPE_FREQS) < ROPE_FREQS,
                        pltpu.roll(x, w - ROPE_FREQS, 1), pltpu.roll(x, ROPE_FREQS, 1))
    return x * cos + partner * sin_signed


def _attn_kernel(*refs, tq, seq, past, use_rope):
    it = iter(refs)
    q_ref, kv_ref, qg_ref, kg_ref = next(it), next(it), next(it), next(it)
    if use_rope:
        cq_ref, sq_ref, ck_ref, sk_ref = next(it), next(it), next(it), next(it)
    if past:
        pk_ref, pv_ref = next(it), next(it)
    o_ref, ko_ref, vo_ref, kvar_scr, vvar_scr = refs[-5:]

    ones_bd = _group_ones()

    @pl.when(pl.program_id(1) == 0)
    def _():
        kv = kv_ref[...]
        k_raw = kv[:, :KV_WIDTH]
        kn = k_raw * lax.rsqrt(_group_sum(k_raw * k_raw, ones_bd) * (1.0 / HEAD_DIM) + NORM_EPS) * kg_ref[...]
        if use_rope:
            kn = _rope(kn, ck_ref[...], sk_ref[...])
        vn = kv[:, KV_WIDTH:]
        ko_ref[...] = kn
        vo_ref[...] = vn
        pieces = [(past, seq, kn, vn)]
        if past:
            pieces.append((0, past, pk_ref[...], pv_ref[...]))
        for start, n, kx, vx in pieces:
            left = lax.broadcasted_iota(jnp.int32, kx.shape, 1) < HEAD_DIM
            for x, scr in ((kx, kvar_scr), (vx, vvar_scr)):
                sw = pltpu.roll(x, HEAD_DIM, 1)
                scr[0, start:start + n, :] = jnp.where(left, x, 0.0).astype(BF16)
                scr[1, start:start + n, :] = jnp.where(left, 0.0, sw).astype(BF16)
                scr[2, start:start + n, :] = jnp.where(left, sw, 0.0).astype(BF16)
                scr[3, start:start + n, :] = jnp.where(left, 0.0, x).astype(BF16)

    scale = HEAD_DIM ** -0.5
    n_blk = ATTN_WIDTH // LANES
    qn = []
    for jb in range(n_blk):
        qb = q_ref[:, jb * LANES:(jb + 1) * LANES]
        x = qb * lax.rsqrt(_group_sum(qb * qb, ones_bd) * (1.0 / HEAD_DIM) + NORM_EPS) * qg_ref[...]
        if use_rope:
            x = _rope(x, cq_ref[...], sq_ref[...])
        qn.append((x * scale).astype(BF16))
    heads = [(jb, 2 * (jb // (n_blk // N_KV_HEADS)) + side) for jb in range(n_blk) for side in (0, 1)]
    scores = [lax.dot_general(qn[jb], kvar_scr[var], NT, preferred_element_type=F32) for jb, var in heads]
    exps = [jnp.exp(s - jnp.max(s, axis=-1, keepdims=True)) for s in scores]
    outs = [lax.dot_general(e.astype(BF16), vvar_scr[var], NN, preferred_element_type=F32)
            for e, (jb, var) in zip(exps, heads)]
    outs = [o / jnp.sum(e, axis=-1, keepdims=True) for o, e in zip(outs, exps)]
    for jb in range(n_blk):
        o_ref[:, jb * LANES:(jb + 1) * LANES] = (outs[2 * jb] + outs[2 * jb + 1]).astype(BF16)


def _attention(rest, n_batch, seq, P, l, rope128, past_kv, kv_all):
    tq = 256
    nq = seq // tq
    n_tok = n_batch * seq
    use_rope = rope128 is not None
    past = 0 if past_kv is None else past_kv[0].shape[2]
    in_specs = [
        pl.BlockSpec((tq, ATTN_WIDTH), lambda b, i: (b * nq + i, COL_Q // ATTN_WIDTH)),
        pl.BlockSpec((seq, 2 * KV_WIDTH), lambda b, i: (b, COL_KV // (2 * KV_WIDTH))),
        _lspec(l, (1, LANES)),
        _lspec(l, (1, LANES)),
    ]
    args = [rest, rest, P['q_gain'], P['k_gain']]
    if use_rope:
        cos, sin = rope128
        in_specs += [pl.BlockSpec((tq, LANES), lambda b, i: (i, 0)),
                     pl.BlockSpec((tq, LANES), lambda b, i: (i, 0)),
                     pl.BlockSpec((seq, LANES), lambda b, i: (0, 0)),
                     pl.BlockSpec((seq, LANES), lambda b, i: (0, 0))]
        args += [cos, sin, cos, sin]
    if past:
        in_specs += [pl.BlockSpec((None, None, past, KV_WIDTH), lambda b, i: (b, l, 0, 0)),
                     pl.BlockSpec((None, None, past, KV_WIDTH), lambda b, i: (b, l, 0, 0))]
        args += list(past_kv)
    aliases = {}
    if kv_all is not None:
        in_specs += [pl.BlockSpec(memory_space=pl.ANY), pl.BlockSpec(memory_space=pl.ANY)]
        aliases = {len(args): 1, len(args) + 1: 2}
        args += list(kv_all)
    cache_spec = pl.BlockSpec((None, None, seq, KV_WIDTH), lambda b, i: (b, l, 0, 0))
    out_specs = [pl.BlockSpec((tq, ATTN_WIDTH), lambda b, i: (b * nq + i, 0)), cache_spec, cache_spec]
    return pl.pallas_call(
        functools.partial(_attn_kernel, tq=tq, seq=seq, past=past, use_rope=use_rope),
        grid=(n_batch, nq),
        in_specs=in_specs,
        out_specs=out_specs,
        out_shape=[jax.ShapeDtypeStruct((n_tok, ATTN_WIDTH), BF16),
                   jax.ShapeDtypeStruct((n_batch, DEPTH, seq, KV_WIDTH), F32),
                   jax.ShapeDtypeStruct((n_batch, DEPTH, seq, KV_WIDTH), F32)],
        input_output_aliases=aliases,
        scratch_shapes=[pltpu.VMEM((2 * N_KV_HEADS, past + seq, KV_WIDTH), BF16),
                        pltpu.VMEM((2 * N_KV_HEADS, past + seq, KV_WIDTH), BF16)],
        compiler_params=pltpu.CompilerParams(
            dimension_semantics=("parallel", "arbitrary"), vmem_limit_bytes=VMEM_LIMIT_BYTES),
        name="attention",
    )(*args)


def _post_kernel(y0_ref, y1_ref, bo0_ref, bo1_ref, lor_ref, oa_ref, gr_ref, ga_ref, x_ref, mod_ref,
                 gup_ref, gnw_ref, gnb_ref, wbr_ref, wout_ref, ng_ref, x1_ref, h2_ref):
    ones_bd = _group_ones()
    y = y0_ref[...] + y1_ref[...]
    mean = _group_sum(y, ones_bd) * (1.0 / HEAD_DIM)
    yc = y - mean
    var = _group_sum(yc * yc, ones_bd) * (1.0 / HEAD_DIM)
    yn = yc * lax.rsqrt(var + GN_EPS) * gnw_ref[...] + gnb_ref[...]
    lg = lor_ref[:, LANES:]
    g = _mm(jax.nn.sigmoid(lg), gup_ref[...])
    o_r = (yn + (bo0_ref[...] + bo1_ref[...])) * g
    merged = (jax.nn.sigmoid(gr_ref[...]) * _mm(o_r, wbr_ref[0])
              + jax.nn.sigmoid(ga_ref[...]) * _mm(oa_ref[...], wbr_ref[1]))
    m = _mm(merged, wout_ref[...])
    g1 = mod_ref[0, 2:3, :]
    sh2 = mod_ref[0, 3:4, :]
    sc2 = mod_ref[0, 4:5, :]
    x1 = x_ref[...] + g1 * _rms(m, ng_ref[1:2, :])
    x1_ref[...] = x1
    h2_ref[...] = (_rms(x1, ng_ref[2:3, :]) * (1.0 + sc2) + sh2).astype(BF16)


def _post(y0, y1, bo0, bo1, rest, o_a, x2d, P, l, mod_row0, rows_per_mod):
    n_tok = x2d.shape[0]
    tm = 512
    tok = lambda w: pl.BlockSpec((tm, w), lambda i: (i, 0))
    return pl.pallas_call(
        _post_kernel,
        grid=(n_tok // tm,),
        in_specs=[
            tok(RWKV_WIDTH), tok(RWKV_WIDTH), tok(RWKV_WIDTH), tok(RWKV_WIDTH),
            pl.BlockSpec((tm, 2 * LANES), lambda i: (i, COL_LORA // (2 * LANES))),
            tok(ATTN_WIDTH),
            pl.BlockSpec((tm, D_MODEL), lambda i: (i, COL_GR // D_MODEL)),
            pl.BlockSpec((tm, D_MODEL), lambda i: (i, COL_GA // D_MODEL)),
            tok(D_MODEL),
            _mod_spec(l, mod_row0, rows_per_mod, tm),
            _lspec(l, (GATE_RANK, RWKV_WIDTH)), _lspec(l, (1, RWKV_WIDTH)), _lspec(l, (1, RWKV_WIDTH)),
            _lspec(l, (2, RWKV_WIDTH, D_MODEL)), _lspec(l, (D_MODEL, D_MODEL)), _lspec(l, (4, D_MODEL)),
        ],
        out_specs=[tok(D_MODEL), tok(D_MODEL)],
        out_shape=[jax.ShapeDtypeStruct((n_tok, D_MODEL), F32), jax.ShapeDtypeStruct((n_tok, D_MODEL), BF16)],
        compiler_params=pltpu.CompilerParams(
            dimension_semantics=("parallel",), vmem_limit_bytes=VMEM_LIMIT_BYTES),
        name="merge_outproj",
    )(y0, y1, bo0, bo1, rest, o_a, rest, rest, x2d, P['mod'], P['gup'], P['gn_w'], P['gn_b'], P['w_br'], P['w_out'],
      P['ng'])


def _mlp_kernel(h_ref, x1_ref, mod_ref, up_ref, down_ref, ng_ref, o_ref):
    h = h_ref[...]
    f = None
    ff = D_MODEL
    for j in range(D_FF // ff):
        u = lax.dot_general(h, up_ref[:, j * ff:(j + 1) * ff], NN, preferred_element_type=F32)
        u = jnp.square(jnp.maximum(u, 0.0)).astype(BF16)
        part = lax.dot_general(u, down_ref[j * ff:(j + 1) * ff, :], NN, preferred_element_type=F32)
        f = part if f is None else f + part
    g2 = mod_ref[0, 5:6, :]
    o_ref[...] = x1_ref[...] + g2 * _rms(f, ng_ref[3:4, :])


def _mlp(h2, x1, P, l, mod_row0, rows_per_mod):
    n_tok = x1.shape[0]
    tm = 512
    tok = lambda: pl.BlockSpec((tm, D_MODEL), lambda i: (i, 0))
    return pl.pallas_call(
        _mlp_kernel,
        grid=(n_tok // tm,),
        in_specs=[
            tok(), tok(),
            _mod_spec(l, mod_row0, rows_per_mod, tm),
            _lspec(l, (D_MODEL, D_FF), single=True),
            _lspec(l, (D_FF, D_MODEL), single=True),
            _lspec(l, (4, D_MODEL)),
        ],
        out_specs=tok(),
        out_shape=jax.ShapeDtypeStruct((n_tok, D_MODEL), F32),
        compiler_params=pltpu.CompilerParams(
            dimension_semantics=("parallel",), vmem_limit_bytes=VMEM_LIMIT_BYTES),
        name="mlp",
    )(h2, x1, P['mod'], P['up'], P['down'], P['ng'])


def _rope_tables(seq):
    n_rows = seq // GRID_W
    rows = np.repeat(np.arange(n_rows, dtype=np.float32), GRID_W)
    cols = np.tile(np.arange(GRID_W, dtype=np.float32), n_rows)
    half = HEAD_DIM // 2
    freqs = 1.0 / (jnp.asarray(ROPE_THETA, F32) ** (jnp.arange(0, half, 2, dtype=F32) / half))
    ang_r = jnp.asarray(rows)[:, None] * freqs
    ang_c = jnp.asarray(cols)[:, None] * freqs
    cr, sr, cc, sc = jnp.cos(ang_r), jnp.sin(ang_r), jnp.cos(ang_c), jnp.sin(ang_c)
    cos64 = jnp.concatenate([cr, cr, cc, cc], axis=1)
    sin64 = jnp.concatenate([-sr, sr, -sc, sc], axis=1)
    return jnp.tile(cos64, (1, LANES // HEAD_DIM)), jnp.tile(sin64, (1, LANES // HEAD_DIM))


def _pairs_from_state(s):
    b = s.shape[0]
    s = s.reshape(b, 2, N_PAIRS, 2, HEAD_DIM, HEAD_DIM)
    z = jnp.zeros_like(s[:, :, :, 0])
    top = jnp.concatenate([s[:, :, :, 0], z], axis=-1)
    bot = jnp.concatenate([z, s[:, :, :, 1]], axis=-1)
    return jnp.concatenate([top, bot], axis=-2)


def _layer(x2d, n_batch, seq, P, l, mod_row0, rows_per_mod, rope128, past_kv, s0, carry):
    kv_all, s_all = (None, None) if carry is None else (carry[:2], carry[2])
    rkv, rest, kh = _in_projection(x2d, P, l, mod_row0, rows_per_mod, seq)
    s0_pairs = None if s0 is None else _pairs_from_state(s0)
    y0, y1, bo0, bo1, s_all = _wkv(rkv, rest, kh, s0_pairs, n_batch, seq, P, l, s_all)
    o_a, k_all, v_all = _attention(rest, n_batch, seq, P, l, rope128, past_kv, kv_all)
    x1, h2 = _post(y0, y1, bo0, bo1, rest, o_a, x2d, P, l, mod_row0, rows_per_mod)
    x2 = _mlp(h2, x1, P, l, mod_row0, rows_per_mod)
    return x2, (k_all, v_all, s_all)


def kernel(x_prompt, x_sample, cache_k, cache_v, state_wkv, c, c_ctx, w_in, w_br, w_out, w_mod, b_mod, norm_g, mlp_up, mlp_down, rwkv_mu, rwkv_k_k, rwkv_k_a, rwkv_r_k, decay_w0, decay_up, iclr_a0, iclr_up, gate_up, gn_w, gn_b, q_gain, k_gain):
    n_ctx, seq_ctx, _ = x_prompt.shape
    n_dec, seq_dec, _ = x_sample.shape
    past = cache_k.shape[2]

    cvec8 = jnp.zeros((SUBLANES, D_MODEL), F32).at[0].set(c_ctx).at[1:1 + n_dec].set(c)
    mod = _modulation(cvec8, w_mod, b_mod)
    mod = jnp.pad(mod.reshape(DEPTH, SUBLANES, N_MOD, D_MODEL), ((0, 0), (0, 0), (0, SUBLANES - N_MOD), (0, 0)))

    o = np.cumsum((0, RWKV_WIDTH, RWKV_WIDTH, RWKV_WIDTH, DECAY_RANK, ICLR_RANK, GATE_RANK,
                   ATTN_WIDTH, KV_WIDTH, KV_WIDTH, 2 * D_MODEL))
    w_rest = jnp.concatenate([w_in[:, :, o[9]:o[10]], w_in[:, :, o[6]:o[7]], w_in[:, :, o[3]:o[6]],
                              w_in[:, :, o[7]:o[9]]], axis=-1)
    zpad = jnp.zeros((DEPTH, 2, DECAY_RANK, RWKV_WIDTH), F32)
    row = lambda a: a.reshape(DEPTH, 1, -1)
    tile2 = lambda a: jnp.tile(a, (1, LANES // HEAD_DIM)).reshape(DEPTH, 1, LANES)
    P = dict(
        mod=mod, ng=norm_g, w_rkv=w_in[:, :, o[0]:o[3]].astype(BF16), w_rest=w_rest.astype(BF16),
        mu=rwkv_mu, k_k=row(rwkv_k_k), k_a=row(rwkv_k_a), r_k=row(rwkv_r_k),
        w0=decay_w0, wup=jnp.concatenate([decay_up, zpad], axis=2).astype(BF16),
        a0=iclr_a0, aup=jnp.concatenate([zpad, iclr_up], axis=2).astype(BF16),
        gup=gate_up.astype(BF16), gn_w=row(gn_w), gn_b=row(gn_b),
        w_br=w_br.astype(BF16), w_out=w_out.astype(BF16), up=mlp_up.astype(BF16), down=mlp_down.astype(BF16),
        q_gain=tile2(q_gain), k_gain=tile2(k_gain),
    )

    x = x_prompt.reshape(n_ctx * seq_ctx, D_MODEL)
    carry = None
    for l in range(DEPTH):
        x, carry = _layer(x, n_ctx, seq_ctx, P, l, 0, 0, None, None, None, carry)
    y_prompt = x.reshape(n_ctx, seq_ctx, D_MODEL)
    new_k, new_v, new_s = carry
    cache_shape = (n_ctx, DEPTH, seq_ctx, N_KV_HEADS, HEAD_DIM)

    rope128 = _rope_tables(seq_dec)
    past_kv = (cache_k.reshape(n_dec, DEPTH, past, KV_WIDTH), cache_v.reshape(n_dec, DEPTH, past, KV_WIDTH))
    x = x_sample.reshape(n_dec * seq_dec, D_MODEL)
    for l in range(DEPTH):
        x, _ = _layer(x, n_dec, seq_dec, P, l, 1, seq_dec, rope128, past_kv, state_wkv[:, l], None)
    y_sample = x.reshape(n_dec, seq_dec, D_MODEL)

    return (y_prompt, y_sample, new_k.reshape(cache_shape), new_v.reshape(cache_shape), new_s)
```

```python
import functools

import numpy as np
import jax
import jax.numpy as jnp
from jax import lax
from jax.experimental import pallas as pl
from jax.experimental.pallas import tpu as pltpu

F32 = jnp.float32
BF16 = jnp.bfloat16

D_MODEL = 1024
DEPTH = 2
GRID_W = 64
HEAD_DIM = 64
N_RWKV_HEADS = 8
RWKV_WIDTH = N_RWKV_HEADS * HEAD_DIM
N_Q_HEADS = 8
N_KV_HEADS = 2
ATTN_WIDTH = N_Q_HEADS * HEAD_DIM
KV_WIDTH = N_KV_HEADS * HEAD_DIM
DECAY_RANK = 64
ICLR_RANK = 64
GATE_RANK = 128
D_FF = 4 * D_MODEL
ROPE_THETA = 10000.0
ROPE_FREQS = HEAD_DIM // 4
N_MOD = 6
NORM_EPS = 1e-6
GN_EPS = 64e-5
DECAY_SCALE = 0.606531
D_IN = 3 * RWKV_WIDTH + DECAY_RANK + ICLR_RANK + GATE_RANK + ATTN_WIDTH + 2 * KV_WIDTH + 2 * D_MODEL

LANES = 128
SUBLANES = 8
VMEM_LIMIT_BYTES = 56 * 1024 * 1024

RKV_W = 3 * RWKV_WIDTH
COL_GR = 0
COL_GA = COL_GR + D_MODEL
COL_Q = COL_GA + D_MODEL
COL_LORA = COL_Q + ATTN_WIDTH
COL_KV = COL_LORA + 2 * LANES
REST_W = COL_KV + 2 * KV_WIDTH

WKV_CHUNK = 64
PAIR = 2 * HEAD_DIM
N_PAIRS = N_RWKV_HEADS // 2

NN = (((1,), (0,)), ((), ()))
NT = (((1,), (1,)), ((), ()))
TN = (((0,), (0,)), ((), ()))


def _mm(a, b, dims=NN):
    return lax.dot_general(a.astype(BF16), b.astype(BF16), dims, preferred_element_type=F32)


def _split2(x):
    hi = x.astype(BF16)
    lo = (x - hi.astype(F32)).astype(BF16)
    return hi, lo


def _mm_exact_rhs(x, m_bf16):
    hi, lo = _split2(x)
    d = lambda p: lax.dot_general(p, m_bf16, NN, preferred_element_type=F32)
    return d(hi) + d(lo)


def _div_pow2(i, n):
    return lax.shift_right_logical(i, int(np.log2(n)))


def _mod_pow2(i, n):
    return lax.bitwise_and(i, n - 1)


def _group_ones():
    i = lax.broadcasted_iota(jnp.int32, (LANES, LANES), 0)
    j = lax.broadcasted_iota(jnp.int32, (LANES, LANES), 1)
    return jnp.where(_div_pow2(i, HEAD_DIM) == _div_pow2(j, HEAD_DIM), 1.0, 0.0).astype(BF16)


def _group_sum(x, ones_bd):
    blocks = [
        _mm_exact_rhs(x[:, i * LANES:(i + 1) * LANES], ones_bd)
        for i in range(x.shape[1] // LANES)
    ]
    return blocks[0] if len(blocks) == 1 else jnp.concatenate(blocks, axis=1)


def _lspec(l, shape, single=False):
    kw = dict(pipeline_mode=pl.Buffered(1)) if single else {}
    return pl.BlockSpec((None,) + tuple(shape), lambda *g: (l,) + (0,) * len(shape), **kw)


def _mod_spec(l, row0, rows_per_mod, tm):
    if rows_per_mod:
        return pl.BlockSpec((None, 1, SUBLANES, D_MODEL), lambda i, *g: (l, row0 + i * tm // rows_per_mod, 0, 0))
    return pl.BlockSpec((None, 1, SUBLANES, D_MODEL), lambda *g: (l, row0, 0, 0))


def _rms(x, g):
    return x * lax.rsqrt(jnp.mean(x * x, axis=-1, keepdims=True) + NORM_EPS) * g


def _mod_kernel(c_ref, w_ref, b_ref, o_ref):
    c = c_ref[...]
    s = c * jax.nn.sigmoid(c)
    o_ref[0] = _mm(s, w_ref[0]) + b_ref[0]


def _modulation(cvec8, w_mod, b_mod):
    tn = D_MODEL
    n = N_MOD * D_MODEL
    return pl.pallas_call(
        _mod_kernel,
        grid=(DEPTH, n // tn),
        in_specs=[
            pl.BlockSpec((SUBLANES, D_MODEL), lambda l, j: (0, 0)),
            pl.BlockSpec((1, D_MODEL, tn), lambda l, j: (l, 0, j)),
            pl.BlockSpec((1, 1, tn), lambda l, j: (l, 0, j)),
        ],
        out_specs=pl.BlockSpec((1, SUBLANES, tn), lambda l, j: (l, 0, j)),
        out_shape=jax.ShapeDtypeStruct((DEPTH, SUBLANES, n), F32),
        name="modulation",
    )(cvec8, w_mod, b_mod.reshape(DEPTH, 1, n))


def _shift_mix(x, mu):
    n = x.shape[0]
    row = lax.broadcasted_iota(jnp.int32, x.shape, 0)
    x_prev = jnp.where(row == 0, 0.0, pltpu.roll(x, 1, 0))
    x_next = jnp.where(row == n - 1, 0.0, pltpu.roll(x, n - 1, 0))
    return x * (1.0 - mu) + (0.5 * mu) * (x_prev + x_next)


def _inproj_kernel(x_ref, mod_ref, ng_ref, wrkv_ref, wrest_ref, mu_ref, kk_ref,
                   rkv_ref, rest_ref, kh_ref, h_scr, *, tm, seq):
    j = pl.program_id(1)
    C = RWKV_WIDTH

    def mix_in_place(i):
        for s in range(tm // seq):
            rows = slice(s * seq, (s + 1) * seq)
            cols = slice(i * C, (i + 1) * C)
            mixed = _shift_mix(rkv_ref[rows, cols], mu_ref[i:i + 1, :])
            rkv_ref[rows, cols] = mixed
            if i == 1:
                kk = mixed * kk_ref[...]
                kh_ref[rows, :] = kk * lax.rsqrt(_group_sum(kk * kk, _group_ones()) + 1e-12)

    @pl.when(j == 0)
    def _():
        sh = mod_ref[0, 0:1, :]
        sc = mod_ref[0, 1:2, :]
        ng = ng_ref[0:1, :]
        sub = 256
        for i in range(tm // sub):
            x = x_ref[i * sub:(i + 1) * sub, :]
            h_scr[i * sub:(i + 1) * sub, :] = (_rms(x, ng) * (1.0 + sc) + sh).astype(BF16)
        rkv_ref[...] = lax.dot_general(h_scr[...], wrkv_ref[...], NN, preferred_element_type=F32)

    for jj in range(1, 4):
        @pl.when(j == jj)
        def _(jj=jj):
            mix_in_place(jj - 1)
            rest_ref[...] = lax.dot_general(h_scr[...], wrest_ref[...], NN, preferred_element_type=F32)


def _in_projection(x2d, P, l, mod_row0, rows_per_mod, seq):
    n_tok = x2d.shape[0]
    tm = 1024
    tn = REST_W // 3
    assert tm % seq == 0
    rest_blk = lambda i, j: (i, jnp.maximum(j - 1, 0))
    return pl.pallas_call(
        functools.partial(_inproj_kernel, tm=tm, seq=seq),
        grid=(n_tok // tm, 4),
        in_specs=[
            pl.BlockSpec((tm, D_MODEL), lambda i, j: (i, 0)),
            _mod_spec(l, mod_row0, rows_per_mod, tm),
            _lspec(l, (4, D_MODEL)),
            _lspec(l, (D_MODEL, RKV_W), single=True),
            pl.BlockSpec((None, D_MODEL, tn), lambda i, j: (l, 0, jnp.maximum(j - 1, 0))),
            _lspec(l, (3, RWKV_WIDTH)),
            _lspec(l, (1, RWKV_WIDTH)),
        ],
        out_specs=[pl.BlockSpec((tm, RKV_W), lambda i, j: (i, 0)),
                   pl.BlockSpec((tm, tn), rest_blk),
                   pl.BlockSpec((tm, RWKV_WIDTH), lambda i, j: (i, 0))],
        out_shape=[jax.ShapeDtypeStruct((n_tok, RKV_W), F32),
                   jax.ShapeDtypeStruct((n_tok, REST_W), F32),
                   jax.ShapeDtypeStruct((n_tok, RWKV_WIDTH), F32)],
        scratch_shapes=[pltpu.VMEM((tm, D_MODEL), BF16)],
        compiler_params=pltpu.CompilerParams(
            dimension_semantics=("parallel", "arbitrary"), vmem_limit_bytes=VMEM_LIMIT_BYTES),
        name="in_projection",
    )(x2d, P['mod'], P['ng'], P['w_rkv'], P['w_rest'], P['mu'], P['k_k'])


def _wkv_masks(reverse):
    L = WKV_CHUNK
    row = lax.broadcasted_iota(jnp.int32, (L, PAIR), 0)
    col = _mod_pow2(lax.broadcasted_iota(jnp.int32, (L, PAIR), 1), L)
    blk = {bs: _div_pow2(row, bs) == _div_pow2(col, bs) for bs in (8, 16, 32, 64)}
    return dict(
        strict=(col > row) if reverse else (col < row),
        incl=(col >= row) if reverse else (col <= row),
        blk8=blk[8],
        off={bs: blk[2 * bs] & jnp.logical_not(blk[bs]) for bs in (8, 16, 32)},
        eye=row == col,
    )


def _bd(x, left):
    x = x.astype(BF16)
    z = jnp.zeros_like(x)
    return jnp.concatenate([jnp.where(left, x, z), jnp.where(left, z, x)], axis=0)


def _wkv_operand_scratch(nb):
    C, L = RWKV_WIDTH, WKV_CHUNK
    full = lambda dt: pltpu.VMEM((nb, 2, L, C), dt)
    pair = lambda rows: pltpu.VMEM((nb, 2, N_PAIRS, rows, PAIR), BF16)
    return dict(khd=full(F32), rd=full(F32), bonus=full(F32), etot=pltpu.VMEM((nb, 2, SUBLANES, C), F32),
                be=full(BF16), kte=full(BF16), vb=full(BF16),
                lhsA=pair(2 * L), rhsA=pair(4 * L), vbd=pair(2 * L), khdbd=pair(2 * L))


def _wkv_prepare(e, d, rkv, khr, lor, ka_ref, rk_ref, w0_ref, wup_ref, a0_ref, aup_ref, ops, ones_bd, tri, left):
    C, L = RWKV_WIDTH, WKV_CHUNK
    reverse = d == 1
    r = rkv[e, 0, :, 0:C]
    k = rkv[e, 0, :, C:2 * C]
    v = rkv[e, 0, :, 2 * C:3 * C]
    kh = khr[e, 0]
    lo_dec = lor[e, 0, :, 0:LANES]
    w_raw = _mm(jnp.tanh(lo_dec), wup_ref[d]) + w0_ref[d:d + 1, :]
    lw = -DECAY_SCALE * jax.nn.sigmoid(w_raw)
    a = jax.nn.sigmoid(_mm(lo_dec, aup_ref[d]) + a0_ref[d:d + 1, :])
    kt = k * (1.0 + (a - 1.0) * ka_ref[...])
    ops['bonus'][e, d] = _group_sum(r * kt * rk_ref[...], ones_bd) * v
    b = a * kh

    hi, lo2 = _split2(lw)
    cum = (lax.dot_general(tri[d], hi, NN, preferred_element_type=F32)
           + lax.dot_general(tri[d], lo2, NN, preferred_element_type=F32))
    tot = cum[0:1, :] if reverse else cum[L - 1:L, :]
    e_inv = jnp.exp(-cum)
    e_end = jnp.exp(tot - cum)
    khd = kh * jnp.exp(cum - lw)
    rd = r * jnp.exp(cum)
    bi = b * e_inv
    ki = kt * e_inv
    ops['khd'][e, d] = khd
    ops['rd'][e, d] = rd
    ops['etot'][e, d] = jnp.broadcast_to(jnp.exp(tot), (SUBLANES, C))
    ops['be'][e, d] = (b * e_end).astype(BF16)
    ops['kte'][e, d] = (kt * e_end).astype(BF16)
    ops['vb'][e, d] = v.astype(BF16)
    for p in range(N_PAIRS):
        sl = slice(p * PAIR, (p + 1) * PAIR)
        ops['lhsA'][e, d, p] = jnp.concatenate([khd[:, sl], rd[:, sl]], axis=0).astype(BF16)
        ops['rhsA'][e, d, p] = jnp.concatenate([_bd(bi[:, sl], left), _bd(ki[:, sl], left)], axis=0)
        ops['vbd'][e, d, p] = _bd(v[:, sl], left)
        ops['khdbd'][e, d, p] = _bd(khd[:, sl], left)


def _wkv_chains(chains, left, fillers):
    L = WKV_CHUNK
    i2 = lax.broadcasted_iota(jnp.int32, (PAIR, PAIR), 0)
    j2 = lax.broadcasted_iota(jnp.int32, (PAIR, PAIR), 1)
    same = _div_pow2(i2, HEAD_DIM) == _div_pow2(j2, HEAD_DIM)
    eye2 = i2 == j2
    bd = lambda x: _bd(x, left)
    lp = lambda g: jnp.where(left, g[:L], g[L:])
    n_stages = 13
    due = {((k + 1) * n_stages) // (len(fillers) + 1): f for k, f in enumerate(fillers)}
    emitted = [0]

    def each(f):
        for ch in chains:
            f(ch)
        emitted[0] += 1
        if emitted[0] in due:
            due[emitted[0]]()

    def a_blocks(ch):
        m = ch['masks']
        A = lax.dot_general(ch['lhsA'], ch['rhsA'], NT, preferred_element_type=F32)
        ch['A_ub'] = jnp.where(m['strict'], A[:L, :PAIR], 0.0)
        ch['A_uvrv'] = jnp.concatenate([jnp.where(m['strict'], A[:L, PAIR:], 0.0),
                                        jnp.where(m['incl'], A[L:, PAIR:], 0.0)], axis=0).astype(BF16)
        ch['A_rb'] = jnp.where(m['incl'], A[L:, :PAIR], 0.0).astype(BF16)
        a8 = jnp.where(m['blk8'], ch['A_ub'], 0.0)
        ch['A8'] = a8
        ch['X0'] = jnp.where(m['eye'], 1.0, -a8)
    each(a_blocks)

    def sq1(ch):
        a8_2 = _mm(ch['A8'], bd(ch['A8']))
        ch['A8_2lp'] = a8_2.astype(BF16)
        ch['A8_2'] = bd(a8_2)
    each(sq1)

    def sq2(ch):
        ch['A8_4'] = bd(_mm(ch['A8_2lp'], ch['A8_2']))
        ch['AV'] = _mm(ch['A_uvrv'], ch['vbd'])
    each(sq2)

    def neu1(ch):
        ch['X'] = ch['X0'] + _mm(ch['X0'], ch['A8_2'])
    each(neu1)

    def neu2(ch):
        ch['T'] = ch['X'] + _mm(ch['X'], ch['A8_4'])
    each(neu2)

    for bs in (8, 16, 32):
        def merge_a(ch, bs=bs):
            ch['TO'] = _mm(ch['T'], bd(jnp.where(ch['masks']['off'][bs], ch['A_ub'], 0.0)))
        each(merge_a)

        def merge_b(ch):
            ch['T'] = ch['T'] - _mm(ch['TO'], bd(ch['T']))
        each(merge_b)

    def q_stage(ch):
        rhs = jnp.concatenate([ch['khdbd'], bd(ch['AV'][:L])], axis=1)
        ch['Q'] = _mm(ch['T'], rhs).astype(BF16)
    each(q_stage)

    def mn_stage(ch):
        q = ch['Q']
        QB = _mm(q, ch['be'], TN)
        ch['M'] = (jnp.where(eye2, ch['etot'], 0.0) - jnp.where(same, QB[:PAIR], 0.0)).astype(BF16)
        ch['N'] = lp(_mm(ch['vb'], ch['kte'], TN)) - lp(QB[PAIR:])
        AQ = _mm(ch['A_rb'], jnp.concatenate([bd(q[:, :PAIR]), bd(q[:, PAIR:])], axis=1))
        ch['G1'] = ch['rd'] - AQ[:, :PAIR]
        ch['G2'] = ch['AV'][L:] - AQ[:, PAIR:]
    each(mn_stage)

    def out_stage(ch):
        ch['y'] = _mm(ch['G1'], bd(ch['S']), NT) + ch['G2']
        ch['Sn'] = _mm(ch['S'], ch['M']) + ch['N']
    each(out_stage)


def _wkv_kernel(rkv0, khr0, lor0, rkv1, khr1, lor1,
                ka_ref, rk_ref, w0_ref, wup_ref, a0_ref, aup_ref, s0_ref,
                *rest, zero_state, nb, nc, op_names):
    n_ops = len(op_names)
    y0_ref, y1_ref, bo0_ref, bo1_ref, sfin_ref, s_scr = rest[-(6 + 2 * n_ops):len(rest) - 2 * n_ops]
    op_sets = [dict(zip(op_names, rest[len(rest) - (2 - i) * n_ops:len(rest) - (1 - i) * n_ops])) for i in (0, 1)]
    t = pl.program_id(0)
    c = lax.rem(jnp.maximum(t - 1, 0), nc)

    @pl.when(t == 0)
    def _():
        for ref in op_sets[1].values():
            ref[...] = jnp.zeros(ref.shape, ref.dtype)

    @pl.when(c == 0)
    def _():
        if zero_state:
            s_scr[...] = jnp.zeros(s_scr.shape, F32)
        else:
            s_scr[...] = s0_ref[...]

    def step(ops_in, ops_out):
        L = WKV_CHUNK
        left = lax.broadcasted_iota(jnp.int32, (L, PAIR), 1) < HEAD_DIM
        masks = [_wkv_masks(False), _wkv_masks(True)]

        chains = []
        for e in range(nb):
            for d, (y_ref, bo_ref) in enumerate(((y0_ref, bo0_ref), (y1_ref, bo1_ref))):
                bo_ref[e, 0] = ops_in['bonus'][e, d]
                for p in range(N_PAIRS):
                    sl = slice(p * PAIR, (p + 1) * PAIR)
                    chains.append(dict(
                        e=e, d=d, p=p, sl=sl, y_ref=y_ref, masks=masks[d], S=s_scr[e, d, p],
                        lhsA=ops_in['lhsA'][e, d, p], rhsA=ops_in['rhsA'][e, d, p], vbd=ops_in['vbd'][e, d, p],
                        khdbd=ops_in['khdbd'][e, d, p], rd=ops_in['rd'][e, d, :, sl], be=ops_in['be'][e, d, :, sl],
                        kte=ops_in['kte'][e, d, :, sl], vb=ops_in['vb'][e, d, :, sl],
                        etot=ops_in['etot'][e, d, 0:1, sl]))
        ones_bd = _group_ones()
        ti = lax.broadcasted_iota(jnp.int32, (L, L), 0)
        tj = lax.broadcasted_iota(jnp.int32, (L, L), 1)
        tri = [jnp.where(tj <= ti, 1.0, 0.0).astype(BF16), jnp.where(tj >= ti, 1.0, 0.0).astype(BF16)]
        prepare = [
            functools.partial(_wkv_prepare, e, d, rkv, khr, lor, ka_ref, rk_ref, w0_ref, wup_ref, a0_ref, aup_ref,
                              ops_out, ones_bd, tri, left)
            for e in range(nb) for d, (rkv, khr, lor) in enumerate(((rkv0, khr0, lor0), (rkv1, khr1, lor1)))]
        _wkv_chains(chains, left, prepare)
        for ch in chains:
            ch['y_ref'][ch['e'], 0, :, ch['sl']] = ch['y']
            s_scr[ch['e'], ch['d'], ch['p']] = ch['Sn']

    parity = lax.rem(t, 2)

    @pl.when(parity == 0)
    def _():
        step(op_sets[1], op_sets[0])

    @pl.when(parity == 1)
    def _():
        step(op_sets[0], op_sets[1])

    @pl.when((c == nc - 1) & (t > 0))
    def _():
        for e in range(nb):
            for d in (0, 1):
                for p in range(N_PAIRS):
                    s2 = s_scr[e, d, p]
                    sfin_ref[e, d, 2 * p] = s2[:, :HEAD_DIM]
                    sfin_ref[e, d, 2 * p + 1] = pltpu.roll(s2, HEAD_DIM, 1)[:, :HEAD_DIM]


def _wkv(rkv, rest, kh, s0_pairs, n_batch, seq, P, l, s_all):
    L = WKV_CHUNK
    nc = seq // L
    nb = 2
    n_steps = (n_batch // nb) * nc
    n_tok = n_batch * seq
    zero_state = s0_pairs is None
    if zero_state:
        s0_pairs = jnp.zeros((nb, 2, N_PAIRS, HEAD_DIM, PAIR), F32)
    rkv4 = rkv.reshape(n_batch, nc, L, RKV_W)
    rest4 = rest.reshape(n_batch, nc, L, REST_W)
    kh4 = kh.reshape(n_batch, nc, L, RWKV_WIDTH)

    def in_pos(t, d):
        s = jnp.minimum(t, n_steps - 1)
        c = lax.rem(s, nc)
        return lax.div(s, nc), (c if d == 0 else nc - 1 - c)

    def out_pos(t, d):
        s = jnp.maximum(t - 1, 0)
        c = lax.rem(s, nc)
        return lax.div(s, nc), (c if d == 0 else nc - 1 - c)

    in_specs = []
    for d in (0, 1):
        in_specs += [
            pl.BlockSpec((nb, 1, L, RKV_W), lambda t, d=d: (*in_pos(t, d), 0, 0)),
            pl.BlockSpec((nb, 1, L, RWKV_WIDTH), lambda t, d=d: (*in_pos(t, d), 0, 0)),
            pl.BlockSpec((nb, 1, L, 2 * LANES), lambda t, d=d: (*in_pos(t, d), 0, COL_LORA // (2 * LANES))),
        ]
    state_spec = lambda idx: pl.BlockSpec((nb, 2, N_PAIRS, HEAD_DIM, PAIR), idx)
    in_specs += [
        _lspec(l, (1, RWKV_WIDTH)), _lspec(l, (1, RWKV_WIDTH)),
        _lspec(l, (2, RWKV_WIDTH)), _lspec(l, (2, LANES, RWKV_WIDTH)),
        _lspec(l, (2, RWKV_WIDTH)), _lspec(l, (2, LANES, RWKV_WIDTH)),
        state_spec((lambda t: (0, 0, 0, 0, 0)) if zero_state else (lambda t: (out_pos(t, 0)[0], 0, 0, 0, 0))),
    ]
    args = [rkv4, kh4, rest4, rkv4, kh4, rest4, P['k_a'], P['r_k'], P['w0'], P['wup'], P['a0'], P['aup'], s0_pairs]
    aliases = {}
    if s_all is not None:
        in_specs.append(pl.BlockSpec(memory_space=pl.ANY))
        aliases = {len(args): 4}
        args.append(s_all)
    tok_spec = lambda d: pl.BlockSpec((nb, 1, L, RWKV_WIDTH), lambda t, d=d: (*out_pos(t, d), 0, 0))
    sfin_spec = pl.BlockSpec((nb, None, 2, N_RWKV_HEADS, HEAD_DIM, HEAD_DIM),
                             lambda t: (out_pos(t, 0)[0], l, 0, 0, 0, 0))
    out_specs = [tok_spec(0), tok_spec(1), tok_spec(0), tok_spec(1), sfin_spec]
    tok = jax.ShapeDtypeStruct((n_batch, nc, L, RWKV_WIDTH), F32)
    op_scratch = _wkv_operand_scratch(nb)
    outs = pl.pallas_call(
        functools.partial(_wkv_kernel, zero_state=zero_state, nb=nb, nc=nc, op_names=tuple(op_scratch)),
        grid=(n_steps + 1,),
        in_specs=in_specs,
        out_specs=out_specs,
        out_shape=[tok, tok, tok, tok,
                   jax.ShapeDtypeStruct((n_batch, DEPTH, 2, N_RWKV_HEADS, HEAD_DIM, HEAD_DIM), F32)],
        input_output_aliases=aliases,
        scratch_shapes=([pltpu.VMEM((nb, 2, N_PAIRS, HEAD_DIM, PAIR), F32)]
                        + list(op_scratch.values()) + list(_wkv_operand_scratch(nb).values())),
        compiler_params=pltpu.CompilerParams(
            dimension_semantics=("arbitrary",), vmem_limit_bytes=VMEM_LIMIT_BYTES),
        name="wkv_scan",
    )(*args)
    y0, y1, bo0, bo1, s_fin = outs
    flat = lambda t: t.reshape(n_tok, RWKV_WIDTH)
    return flat(y0), flat(y1), flat(bo0), flat(bo1), s_fin


def _rope(x, cos, sin_signed):
    w = x.shape[1]
    lane = lax.broadcasted_iota(jnp.int32, x.shape, 1)
    partner = jnp.where(_mod_pow2(lane, 2 * ROPE_FREQS) < ROPE_FREQS,
                        pltpu.roll(x, w - ROPE_FREQS, 1), pltpu.roll(x, ROPE_FREQS, 1))
    return x * cos + partner * sin_signed


def _attn_kernel(*refs, tq, seq, past, use_rope):
    it = iter(refs)
    q_ref, kv_ref, qg_ref, kg_ref = next(it), next(it), next(it), next(it)
    if use_rope:
        cq_ref, sq_ref, ck_ref, sk_ref = next(it), next(it), next(it), next(it)
    if past:
        pk_ref, pv_ref = next(it), next(it)
    o_ref, ko_ref, vo_ref, kvar_scr, vvar_scr = refs[-5:]

    ones_bd = _group_ones()

    @pl.when(pl.program_id(1) == 0)
    def _():
        kv = kv_ref[...]
        k_raw = kv[:, :KV_WIDTH]
        kn = k_raw * lax.rsqrt(_group_sum(k_raw * k_raw, ones_bd) * (1.0 / HEAD_DIM) + NORM_EPS) * kg_ref[...]
        if use_rope:
            kn = _rope(kn, ck_ref[...], sk_ref[...])
        vn = kv[:, KV_WIDTH:]
        ko_ref[...] = kn
        vo_ref[...] = vn
        pieces = [(past, seq, kn, vn)]
        if past:
            pieces.append((0, past, pk_ref[...], pv_ref[...]))
        for start, n, kx, vx in pieces:
            left = lax.broadcasted_iota(jnp.int32, kx.shape, 1) < HEAD_DIM
            for x, scr in ((kx, kvar_scr), (vx, vvar_scr)):
                sw = pltpu.roll(x, HEAD_DIM, 1)
                scr[0, start:start + n, :] = jnp.where(left, x, 0.0).astype(BF16)
                scr[1, start:start + n, :] = jnp.where(left, 0.0, sw).astype(BF16)
                scr[2, start:start + n, :] = jnp.where(left, sw, 0.0).astype(BF16)
                scr[3, start:start + n, :] = jnp.where(left, 0.0, x).astype(BF16)

    scale = HEAD_DIM ** -0.5
    n_blk = ATTN_WIDTH // LANES
    qn = []
    for jb in range(n_blk):
        qb = q_ref[:, jb * LANES:(jb + 1) * LANES]
        x = qb * lax.rsqrt(_group_sum(qb * qb, ones_bd) * (1.0 / HEAD_DIM) + NORM_EPS) * qg_ref[...]
        if use_rope:
            x = _rope(x, cq_ref[...], sq_ref[...])
        qn.append((x * scale).astype(BF16))
    heads = [(jb, 2 * (jb // (n_blk // N_KV_HEADS)) + side) for jb in range(n_blk) for side in (0, 1)]
    scores = [lax.dot_general(qn[jb], kvar_scr[var], NT, preferred_element_type=F32) for jb, var in heads]
    exps = [jnp.exp(s - jnp.max(s, axis=-1, keepdims=True)) for s in scores]
    outs = [lax.dot_general(e.astype(BF16), vvar_scr[var], NN, preferred_element_type=F32)
            for e, (jb, var) in zip(exps, heads)]
    outs = [o / jnp.sum(e, axis=-1, keepdims=True) for o, e in zip(outs, exps)]
    for jb in range(n_blk):
        o_ref[:, jb * LANES:(jb + 1) * LANES] = (outs[2 * jb] + outs[2 * jb + 1]).astype(BF16)


def _attention(rest, n_batch, seq, P, l, rope128, past_kv, kv_all):
    tq = 256
    nq = seq // tq
    n_tok = n_batch * seq
    use_rope = rope128 is not None
    past = 0 if past_kv is None else past_kv[0].shape[2]
    in_specs = [
        pl.BlockSpec((tq, ATTN_WIDTH), lambda b, i: (b * nq + i, COL_Q // ATTN_WIDTH)),
        pl.BlockSpec((seq, 2 * KV_WIDTH), lambda b, i: (b, COL_KV // (2 * KV_WIDTH))),
        _lspec(l, (1, LANES)),
        _lspec(l, (1, LANES)),
    ]
    args = [rest, rest, P['q_gain'], P['k_gain']]
    if use_rope:
        cos, sin = rope128
        in_specs += [pl.BlockSpec((tq, LANES), lambda b, i: (i, 0)),
                     pl.BlockSpec((tq, LANES), lambda b, i: (i, 0)),
                     pl.BlockSpec((seq, LANES), lambda b, i: (0, 0)),
                     pl.BlockSpec((seq, LANES), lambda b, i: (0, 0))]
        args += [cos, sin, cos, sin]
    if past:
        in_specs += [pl.BlockSpec((None, None, past, KV_WIDTH), lambda b, i: (b, l, 0, 0)),
                     pl.BlockSpec((None, None, past, KV_WIDTH), lambda b, i: (b, l, 0, 0))]
        args += list(past_kv)
    aliases = {}
    if kv_all is not None:
        in_specs += [pl.BlockSpec(memory_space=pl.ANY), pl.BlockSpec(memory_space=pl.ANY)]
        aliases = {len(args): 1, len(args) + 1: 2}
        args += list(kv_all)
    cache_spec = pl.BlockSpec((None, None, seq, KV_WIDTH), lambda b, i: (b, l, 0, 0))
    out_specs = [pl.BlockSpec((tq, ATTN_WIDTH), lambda b, i: (b * nq + i, 0)), cache_spec, cache_spec]
    return pl.pallas_call(
        functools.partial(_attn_kernel, tq=tq, seq=seq, past=past, use_rope=use_rope),
        grid=(n_batch, nq),
        in_specs=in_specs,
        out_specs=out_specs,
        out_shape=[jax.ShapeDtypeStruct((n_tok, ATTN_WIDTH), BF16),
                   jax.ShapeDtypeStruct((n_batch, DEPTH, seq, KV_WIDTH), F32),
                   jax.ShapeDtypeStruct((n_batch, DEPTH, seq, KV_WIDTH), F32)],
        input_output_aliases=aliases,
        scratch_shapes=[pltpu.VMEM((2 * N_KV_HEADS, past + seq, KV_WIDTH), BF16),
                        pltpu.VMEM((2 * N_KV_HEADS, past + seq, KV_WIDTH), BF16)],
        compiler_params=pltpu.CompilerParams(
            dimension_semantics=("parallel", "arbitrary"), vmem_limit_bytes=VMEM_LIMIT_BYTES),
        name="attention",
    )(*args)


def _post_kernel(y0_ref, y1_ref, bo0_ref, bo1_ref, lor_ref, oa_ref, gr_ref, ga_ref, x_ref, mod_ref,
                 gup_ref, gnw_ref, gnb_ref, wbr_ref, wout_ref, ng_ref, x1_ref, h2_ref):
    ones_bd = _group_ones()
    y = y0_ref[...] + y1_ref[...]
    mean = _group_sum(y, ones_bd) * (1.0 / HEAD_DIM)
    yc = y - mean
    var = _group_sum(yc * yc, ones_bd) * (1.0 / HEAD_DIM)
    yn = yc * lax.rsqrt(var + GN_EPS) * gnw_ref[...] + gnb_ref[...]
    lg = lor_ref[:, LANES:]
    g = _mm(jax.nn.sigmoid(lg), gup_ref[...])
    o_r = (yn + (bo0_ref[...] + bo1_ref[...])) * g
    merged = (jax.nn.sigmoid(gr_ref[...]) * _mm(o_r, wbr_ref[0])
              + jax.nn.sigmoid(ga_ref[...]) * _mm(oa_ref[...], wbr_ref[1]))
    m = _mm(merged, wout_ref[...])
    g1 = mod_ref[0, 2:3, :]
    sh2 = mod_ref[0, 3:4, :]
    sc2 = mod_ref[0, 4:5, :]
    x1 = x_ref[...] + g1 * _rms(m, ng_ref[1:2, :])
    x1_ref[...] = x1
    h2_ref[...] = (_rms(x1, ng_ref[2:3, :]) * (1.0 + sc2) + sh2).astype(BF16)


def _post(y0, y1, bo0, bo1, rest, o_a, x2d, P, l, mod_row0, rows_per_mod):
    n_tok = x2d.shape[0]
    tm = 512
    tok = lambda w: pl.BlockSpec((tm, w), lambda i: (i, 0))
    return pl.pallas_call(
        _post_kernel,
        grid=(n_tok // tm,),
        in_specs=[
            tok(RWKV_WIDTH), tok(RWKV_WIDTH), tok(RWKV_WIDTH), tok(RWKV_WIDTH),
            pl.BlockSpec((tm, 2 * LANES), lambda i: (i, COL_LORA // (2 * LANES))),
            tok(ATTN_WIDTH),
            pl.BlockSpec((tm, D_MODEL), lambda i: (i, COL_GR // D_MODEL)),
            pl.BlockSpec((tm, D_MODEL), lambda i: (i, COL_GA // D_MODEL)),
            tok(D_MODEL),
            _mod_spec(l, mod_row0, rows_per_mod, tm),
            _lspec(l, (GATE_RANK, RWKV_WIDTH)), _lspec(l, (1, RWKV_WIDTH)), _lspec(l, (1, RWKV_WIDTH)),
            _lspec(l, (2, RWKV_WIDTH, D_MODEL)), _lspec(l, (D_MODEL, D_MODEL)), _lspec(l, (4, D_MODEL)),
        ],
        out_specs=[tok(D_MODEL), tok(D_MODEL)],
        out_shape=[jax.ShapeDtypeStruct((n_tok, D_MODEL), F32), jax.ShapeDtypeStruct((n_tok, D_MODEL), BF16)],
        compiler_params=pltpu.CompilerParams(
            dimension_semantics=("parallel",), vmem_limit_bytes=VMEM_LIMIT_BYTES),
        name="merge_outproj",
    )(y0, y1, bo0, bo1, rest, o_a, rest, rest, x2d, P['mod'], P['gup'], P['gn_w'], P['gn_b'], P['w_br'], P['w_out'],
      P['ng'])


def _mlp_kernel(h_ref, x1_ref, mod_ref, up_ref, down_ref, ng_ref, o_ref):
    h = h_ref[...]
    f = None
    ff = D_MODEL
    for j in range(D_FF // ff):
        u = lax.dot_general(h, up_ref[:, j * ff:(j + 1) * ff], NN, preferred_element_type=F32)
        u = jnp.square(jnp.maximum(u, 0.0)).astype(BF16)
        part = lax.dot_general(u, down_ref[j * ff:(j + 1) * ff, :], NN, preferred_element_type=F32)
        f = part if f is None else f + part
    g2 = mod_ref[0, 5:6, :]
    o_ref[...] = x1_ref[...] + g2 * _rms(f, ng_ref[3:4, :])


def _mlp(h2, x1, P, l, mod_row0, rows_per_mod):
    n_tok = x1.shape[0]
    tm = 512
    tok = lambda: pl.BlockSpec((tm, D_MODEL), lambda i: (i, 0))
    return pl.pallas_call(
        _mlp_kernel,
        grid=(n_tok // tm,),
        in_specs=[
            tok(), tok(),
            _mod_spec(l, mod_row0, rows_per_mod, tm),
            _lspec(l, (D_MODEL, D_FF), single=True),
            _lspec(l, (D_FF, D_MODEL), single=True),
            _lspec(l, (4, D_MODEL)),
        ],
        out_specs=tok(),
        out_shape=jax.ShapeDtypeStruct((n_tok, D_MODEL), F32),
        compiler_params=pltpu.CompilerParams(
            dimension_semantics=("parallel",), vmem_limit_bytes=VMEM_LIMIT_BYTES),
        name="mlp",
    )(h2, x1, P['mod'], P['up'], P['down'], P['ng'])


def _rope_tables(seq):
    n_rows = seq // GRID_W
    rows = np.repeat(np.arange(n_rows, dtype=np.float32), GRID_W)
    cols = np.tile(np.arange(GRID_W, dtype=np.float32), n_rows)
    half = HEAD_DIM // 2
    freqs = 1.0 / (jnp.asarray(ROPE_THETA, F32) ** (jnp.arange(0, half, 2, dtype=F32) / half))
    ang_r = jnp.asarray(rows)[:, None] * freqs
    ang_c = jnp.asarray(cols)[:, None] * freqs
    cr, sr, cc, sc = jnp.cos(ang_r), jnp.sin(ang_r), jnp.cos(ang_c), jnp.sin(ang_c)
    cos64 = jnp.concatenate([cr, cr, cc, cc], axis=1)
    sin64 = jnp.concatenate([-sr, sr, -sc, sc], axis=1)
    return jnp.tile(cos64, (1, LANES // HEAD_DIM)), jnp.tile(sin64, (1, LANES // HEAD_DIM))


def _pairs_from_state(s):
    b = s.shape[0]
    s = s.reshape(b, 2, N_PAIRS, 2, HEAD_DIM, HEAD_DIM)
    return jnp.concatenate([s[:, :, :, 0], s[:, :, :, 1]], axis=-1)


def _layer(x2d, n_batch, seq, P, l, mod_row0, rows_per_mod, rope128, past_kv, s0, carry):
    kv_all, s_all = (None, None) if carry is None else (carry[:2], carry[2])
    rkv, rest, kh = _in_projection(x2d, P, l, mod_row0, rows_per_mod, seq)
    s0_pairs = None if s0 is None else _pairs_from_state(s0)
    y0, y1, bo0, bo1, s_all = _wkv(rkv, rest, kh, s0_pairs, n_batch, seq, P, l, s_all)
    o_a, k_all, v_all = _attention(rest, n_batch, seq, P, l, rope128, past_kv, kv_all)
    x1, h2 = _post(y0, y1, bo0, bo1, rest, o_a, x2d, P, l, mod_row0, rows_per_mod)
    x2 = _mlp(h2, x1, P, l, mod_row0, rows_per_mod)
    return x2, (k_all, v_all, s_all)


def kernel(x_prompt, x_sample, cache_k, cache_v, state_wkv, c, c_ctx, w_in, w_br, w_out, w_mod, b_mod, norm_g, mlp_up, mlp_down, rwkv_mu, rwkv_k_k, rwkv_k_a, rwkv_r_k, decay_w0, decay_up, iclr_a0, iclr_up, gate_up, gn_w, gn_b, q_gain, k_gain):
    n_ctx, seq_ctx, _ = x_prompt.shape
    n_dec, seq_dec, _ = x_sample.shape
    past = cache_k.shape[2]

    cvec8 = jnp.zeros((SUBLANES, D_MODEL), F32).at[0].set(c_ctx).at[1:1 + n_dec].set(c)
    mod = _modulation(cvec8, w_mod, b_mod)
    mod = jnp.pad(mod.reshape(DEPTH, SUBLANES, N_MOD, D_MODEL), ((0, 0), (0, 0), (0, SUBLANES - N_MOD), (0, 0)))

    o = np.cumsum((0, RWKV_WIDTH, RWKV_WIDTH, RWKV_WIDTH, DECAY_RANK, ICLR_RANK, GATE_RANK,
                   ATTN_WIDTH, KV_WIDTH, KV_WIDTH, 2 * D_MODEL))
    w_rest = jnp.concatenate([w_in[:, :, o[9]:o[10]], w_in[:, :, o[6]:o[7]], w_in[:, :, o[3]:o[6]],
                              w_in[:, :, o[7]:o[9]]], axis=-1)
    zpad = jnp.zeros((DEPTH, 2, DECAY_RANK, RWKV_WIDTH), F32)
    row = lambda a: a.reshape(DEPTH, 1, -1)
    tile2 = lambda a: jnp.tile(a, (1, LANES // HEAD_DIM)).reshape(DEPTH, 1, LANES)
    P = dict(
        mod=mod, ng=norm_g, w_rkv=w_in[:, :, o[0]:o[3]].astype(BF16), w_rest=w_rest.astype(BF16),
        mu=rwkv_mu, k_k=row(rwkv_k_k), k_a=row(rwkv_k_a), r_k=row(rwkv_r_k),
        w0=decay_w0, wup=jnp.concatenate([decay_up, zpad], axis=2).astype(BF16),
        a0=iclr_a0, aup=jnp.concatenate([zpad, iclr_up], axis=2).astype(BF16),
        gup=gate_up.astype(BF16), gn_w=row(gn_w), gn_b=row(gn_b),
        w_br=w_br.astype(BF16), w_out=w_out.astype(BF16), up=mlp_up.astype(BF16), down=mlp_down.astype(BF16),
        q_gain=tile2(q_gain), k_gain=tile2(k_gain),
    )

    x = x_prompt.reshape(n_ctx * seq_ctx, D_MODEL)
    carry = None
    for l in range(DEPTH):
        x, carry = _layer(x, n_ctx, seq_ctx, P, l, 0, 0, None, None, None, carry)
    y_prompt = x.reshape(n_ctx, seq_ctx, D_MODEL)
    new_k, new_v, new_s = carry
    cache_shape = (n_ctx, DEPTH, seq_ctx, N_KV_HEADS, HEAD_DIM)

    rope128 = _rope_tables(seq_dec)
    past_kv = (cache_k.reshape(n_dec, DEPTH, past, KV_WIDTH), cache_v.reshape(n_dec, DEPTH, past, KV_WIDTH))
    x = x_sample.reshape(n_dec * seq_dec, D_MODEL)
    for l in range(DEPTH):
        x, _ = _layer(x, n_dec, seq_dec, P, l, 1, seq_dec, rope128, past_kv, state_wkv[:, l], None)
    y_sample = x.reshape(n_dec, seq_dec, D_MODEL)

    return (y_prompt, y_sample, new_k.reshape(cache_shape), new_v.reshape(cache_shape), new_s)
```

```python
import functools

import numpy as np
import jax
import jax.numpy as jnp
from jax import lax
from jax.experimental import pallas as pl
from jax.experimental.pallas import tpu as pltpu

F32 = jnp.float32
BF16 = jnp.bfloat16

D_MODEL = 1024
DEPTH = 2
GRID_W = 64
HEAD_DIM = 64
N_RWKV_HEADS = 8
RWKV_WIDTH = N_RWKV_HEADS * HEAD_DIM
N_Q_HEADS = 8
N_KV_HEADS = 2
ATTN_WIDTH = N_Q_HEADS * HEAD_DIM
KV_WIDTH = N_KV_HEADS * HEAD_DIM
DECAY_RANK = 64
ICLR_RANK = 64
GATE_RANK = 128
D_FF = 4 * D_MODEL
ROPE_THETA = 10000.0
ROPE_FREQS = HEAD_DIM // 4
N_MOD = 6
NORM_EPS = 1e-6
GN_EPS = 64e-5
DECAY_SCALE = 0.606531
D_IN = 3 * RWKV_WIDTH + DECAY_RANK + ICLR_RANK + GATE_RANK + ATTN_WIDTH + 2 * KV_WIDTH + 2 * D_MODEL

LANES = 128
SUBLANES = 8
VMEM_LIMIT_BYTES = 56 * 1024 * 1024

RKV_W = 3 * RWKV_WIDTH
COL_GR = 0
COL_GA = COL_GR + D_MODEL
COL_Q = COL_GA + D_MODEL
COL_LORA = COL_Q + ATTN_WIDTH
COL_KV = COL_LORA + 2 * LANES
REST_W = COL_KV + 2 * KV_WIDTH

WKV_CHUNK = 64
PAIR = 2 * HEAD_DIM
N_PAIRS = N_RWKV_HEADS // 2

NN = (((1,), (0,)), ((), ()))
NT = (((1,), (1,)), ((), ()))
TN = (((0,), (0,)), ((), ()))


def _mm(a, b, dims=NN):
    return lax.dot_general(a.astype(BF16), b.astype(BF16), dims, preferred_element_type=F32)


def _split2(x):
    hi = x.astype(BF16)
    lo = (x - hi.astype(F32)).astype(BF16)
    return hi, lo


def _mm_exact_rhs(x, m_bf16):
    hi, lo = _split2(x)
    d = lambda p: lax.dot_general(p, m_bf16, NN, preferred_element_type=F32)
    return d(hi) + d(lo)


def _div_pow2(i, n):
    return lax.shift_right_logical(i, int(np.log2(n)))


def _mod_pow2(i, n):
    return lax.bitwise_and(i, n - 1)


def _group_ones():
    i = lax.broadcasted_iota(jnp.int32, (LANES, LANES), 0)
    j = lax.broadcasted_iota(jnp.int32, (LANES, LANES), 1)
    return jnp.where(_div_pow2(i, HEAD_DIM) == _div_pow2(j, HEAD_DIM), 1.0, 0.0).astype(BF16)


def _group_sum(x, ones_bd):
    blocks = [
        _mm_exact_rhs(x[:, i * LANES:(i + 1) * LANES], ones_bd)
        for i in range(x.shape[1] // LANES)
    ]
    return blocks[0] if len(blocks) == 1 else jnp.concatenate(blocks, axis=1)


def _lspec(l, shape, single=False):
    kw = dict(pipeline_mode=pl.Buffered(1)) if single else {}
    return pl.BlockSpec((None,) + tuple(shape), lambda *g: (l,) + (0,) * len(shape), **kw)


def _mod_spec(l, row0, rows_per_mod, tm):
    if rows_per_mod:
        return pl.BlockSpec((None, 1, SUBLANES, D_MODEL), lambda i, *g: (l, row0 + i * tm // rows_per_mod, 0, 0))
    return pl.BlockSpec((None, 1, SUBLANES, D_MODEL), lambda *g: (l, row0, 0, 0))


def _rms(x, g):
    return x * lax.rsqrt(jnp.mean(x * x, axis=-1, keepdims=True) + NORM_EPS) * g


def _mod_kernel(c_ref, w_ref, b_ref, o_ref):
    c = c_ref[...]
    s = c * jax.nn.sigmoid(c)
    o_ref[0] = _mm(s, w_ref[0]) + b_ref[0]


def _modulation(cvec8, w_mod, b_mod):
    tn = D_MODEL
    n = N_MOD * D_MODEL
    return pl.pallas_call(
        _mod_kernel,
        grid=(DEPTH, n // tn),
        in_specs=[
            pl.BlockSpec((SUBLANES, D_MODEL), lambda l, j: (0, 0)),
            pl.BlockSpec((1, D_MODEL, tn), lambda l, j: (l, 0, j)),
            pl.BlockSpec((1, 1, tn), lambda l, j: (l, 0, j)),
        ],
        out_specs=pl.BlockSpec((1, SUBLANES, tn), lambda l, j: (l, 0, j)),
        out_shape=jax.ShapeDtypeStruct((DEPTH, SUBLANES, n), F32),
        name="modulation",
    )(cvec8, w_mod, b_mod.reshape(DEPTH, 1, n))


def _shift_mix(x, mu):
    n = x.shape[0]
    row = lax.broadcasted_iota(jnp.int32, x.shape, 0)
    x_prev = jnp.where(row == 0, 0.0, pltpu.roll(x, 1, 0))
    x_next = jnp.where(row == n - 1, 0.0, pltpu.roll(x, n - 1, 0))
    return x * (1.0 - mu) + (0.5 * mu) * (x_prev + x_next)


def _inproj_kernel(x_ref, mod_ref, ng_ref, wrkv_ref, wrest_ref, mu_ref, kk_ref,
                   rkv_ref, rest_ref, kh_ref, h_scr, *, tm, seq):
    j = pl.program_id(1)
    C = RWKV_WIDTH

    def mix_in_place(i, piece):
        lanes = slice(piece * LANES, (piece + 1) * LANES)
        cols = slice(i * C + piece * LANES, i * C + (piece + 1) * LANES)
        for s in range(tm // seq):
            rows = slice(s * seq, (s + 1) * seq)
            mixed = _shift_mix(rkv_ref[rows, cols], mu_ref[i:i + 1, lanes])
            rkv_ref[rows, cols] = mixed
            if i == 1:
                kk = mixed * kk_ref[:, lanes]
                kh_ref[rows, lanes] = kk * lax.rsqrt(_group_sum(kk * kk, _group_ones()) + 1e-12)

    @pl.when(j == 0)
    def _():
        sh = mod_ref[0, 0:1, :]
        sc = mod_ref[0, 1:2, :]
        ng = ng_ref[0:1, :]
        sub = 256
        for i in range(tm // sub):
            x = x_ref[i * sub:(i + 1) * sub, :]
            h_scr[i * sub:(i + 1) * sub, :] = (_rms(x, ng) * (1.0 + sc) + sh).astype(BF16)
        rkv_ref[...] = lax.dot_general(h_scr[...], wrkv_ref[...], NN, preferred_element_type=F32)

    n_piece = C // LANES
    tn = rest_ref.shape[1] // n_piece
    for jj in range(1, 4):
        @pl.when(j == jj)
        def _(jj=jj):
            for piece in range(n_piece):
                cols = slice(piece * tn, (piece + 1) * tn)
                rest_ref[:, cols] = lax.dot_general(h_scr[...], wrest_ref[:, cols], NN,
                                                    preferred_element_type=F32)
                mix_in_place(jj - 1, piece)


def _in_projection(x2d, P, l, mod_row0, rows_per_mod, seq):
    n_tok = x2d.shape[0]
    tm = 1024
    tn = REST_W // 3
    assert tm % seq == 0
    rest_blk = lambda i, j: (i, jnp.maximum(j - 1, 0))
    return pl.pallas_call(
        functools.partial(_inproj_kernel, tm=tm, seq=seq),
        grid=(n_tok // tm, 4),
        in_specs=[
            pl.BlockSpec((tm, D_MODEL), lambda i, j: (i, 0)),
            _mod_spec(l, mod_row0, rows_per_mod, tm),
            _lspec(l, (4, D_MODEL)),
            _lspec(l, (D_MODEL, RKV_W), single=True),
            pl.BlockSpec((None, D_MODEL, tn), lambda i, j: (l, 0, jnp.maximum(j - 1, 0))),
            _lspec(l, (3, RWKV_WIDTH)),
            _lspec(l, (1, RWKV_WIDTH)),
        ],
        out_specs=[pl.BlockSpec((tm, RKV_W), lambda i, j: (i, 0)),
                   pl.BlockSpec((tm, tn), rest_blk),
                   pl.BlockSpec((tm, RWKV_WIDTH), lambda i, j: (i, 0))],
        out_shape=[jax.ShapeDtypeStruct((n_tok, RKV_W), F32),
                   jax.ShapeDtypeStruct((n_tok, REST_W), F32),
                   jax.ShapeDtypeStruct((n_tok, RWKV_WIDTH), F32)],
        scratch_shapes=[pltpu.VMEM((tm, D_MODEL), BF16)],
        compiler_params=pltpu.CompilerParams(
            dimension_semantics=("parallel", "arbitrary"), vmem_limit_bytes=VMEM_LIMIT_BYTES),
        name="in_projection",
    )(x2d, P['mod'], P['ng'], P['w_rkv'], P['w_rest'], P['mu'], P['k_k'])


def _wkv_masks(reverse):
    L = WKV_CHUNK
    row = lax.broadcasted_iota(jnp.int32, (L, PAIR), 0)
    col = _mod_pow2(lax.broadcasted_iota(jnp.int32, (L, PAIR), 1), L)
    blk = {bs: _div_pow2(row, bs) == _div_pow2(col, bs) for bs in (8, 16, 32, 64)}
    return dict(
        strict=(col > row) if reverse else (col < row),
        incl=(col >= row) if reverse else (col <= row),
        blk8=blk[8],
        off={bs: blk[2 * bs] & jnp.logical_not(blk[bs]) for bs in (8, 16, 32)},
        eye=row == col,
    )


def _bd(x, left):
    x = x.astype(BF16)
    z = jnp.zeros_like(x)
    return jnp.concatenate([jnp.where(left, x, z), jnp.where(left, z, x)], axis=0)


def _wkv_operand_scratch(nb):
    C, L = RWKV_WIDTH, WKV_CHUNK
    full = lambda dt: pltpu.VMEM((nb, 2, L, C), dt)
    pair = lambda rows: pltpu.VMEM((nb, 2, N_PAIRS, rows, PAIR), BF16)
    return dict(khd=full(F32), rd=full(F32), bonus=full(F32), etot=pltpu.VMEM((nb, 2, SUBLANES, C), F32),
                be=full(BF16), kte=full(BF16), vb=full(BF16),
                lhsA=pair(2 * L), rhsA=pair(4 * L), vbd=pair(2 * L), khdbd=pair(2 * L))


def _wkv_prepare(e, d, rkv, khr, lor, ka_ref, rk_ref, w0_ref, wup_ref, a0_ref, aup_ref, ops, ones_bd, tri, left):
    C, L = RWKV_WIDTH, WKV_CHUNK
    reverse = d == 1
    r = rkv[e, 0, :, 0:C]
    k = rkv[e, 0, :, C:2 * C]
    v = rkv[e, 0, :, 2 * C:3 * C]
    kh = khr[e, 0]
    lo_dec = lor[e, 0, :, 0:LANES]
    w_raw = _mm(jnp.tanh(lo_dec), wup_ref[d]) + w0_ref[d:d + 1, :]
    lw = -DECAY_SCALE * jax.nn.sigmoid(w_raw)
    a = jax.nn.sigmoid(_mm(lo_dec, aup_ref[d]) + a0_ref[d:d + 1, :])
    kt = k * (1.0 + (a - 1.0) * ka_ref[...])
    ops['bonus'][e, d] = _group_sum(r * kt * rk_ref[...], ones_bd) * v
    b = a * kh

    hi, lo2 = _split2(lw)
    cum = (lax.dot_general(tri[d], hi, NN, preferred_element_type=F32)
           + lax.dot_general(tri[d], lo2, NN, preferred_element_type=F32))
    tot = cum[0:1, :] if reverse else cum[L - 1:L, :]
    e_inv = jnp.exp(-cum)
    e_end = jnp.exp(tot - cum)
    khd = kh * jnp.exp(cum - lw)
    rd = r * jnp.exp(cum)
    bi = b * e_inv
    ki = kt * e_inv
    ops['khd'][e, d] = khd
    ops['rd'][e, d] = rd
    ops['etot'][e, d] = jnp.broadcast_to(jnp.exp(tot), (SUBLANES, C))
    ops['be'][e, d] = (b * e_end).astype(BF16)
    ops['kte'][e, d] = (kt * e_end).astype(BF16)
    ops['vb'][e, d] = v.astype(BF16)
    for p in range(N_PAIRS):
        sl = slice(p * PAIR, (p + 1) * PAIR)
        ops['lhsA'][e, d, p] = jnp.concatenate([khd[:, sl], rd[:, sl]], axis=0).astype(BF16)
        ops['rhsA'][e, d, p] = jnp.concatenate([_bd(bi[:, sl], left), _bd(ki[:, sl], left)], axis=0)
        ops['vbd'][e, d, p] = _bd(v[:, sl], left)
        ops['khdbd'][e, d, p] = _bd(khd[:, sl], left)


def _wkv_chains(chains, left, fillers):
    L = WKV_CHUNK
    i2 = lax.broadcasted_iota(jnp.int32, (PAIR, PAIR), 0)
    j2 = lax.broadcasted_iota(jnp.int32, (PAIR, PAIR), 1)
    same = _div_pow2(i2, HEAD_DIM) == _div_pow2(j2, HEAD_DIM)
    eye2 = i2 == j2
    bd = lambda x: _bd(x, left)
    lp = lambda g: jnp.where(left, g[:L], g[L:])
    n_stages = 13
    due = {((k + 1) * n_stages) // (len(fillers) + 1): f for k, f in enumerate(fillers)}
    emitted = [0]

    def each(f):
        for ch in chains:
            f(ch)
        emitted[0] += 1
        if emitted[0] in due:
            due[emitted[0]]()

    def a_blocks(ch):
        m = ch['masks']
        A = lax.dot_general(ch['lhsA'], ch['rhsA'], NT, preferred_element_type=F32)
        ch['A_ub'] = jnp.where(m['strict'], A[:L, :PAIR], 0.0)
        ch['A_uvrv'] = jnp.concatenate([jnp.where(m['strict'], A[:L, PAIR:], 0.0),
                                        jnp.where(m['incl'], A[L:, PAIR:], 0.0)], axis=0).astype(BF16)
        ch['A_rb'] = jnp.where(m['incl'], A[L:, :PAIR], 0.0).astype(BF16)
        a8 = jnp.where(m['blk8'], ch['A_ub'], 0.0)
        ch['A8'] = a8
        ch['X0'] = jnp.where(m['eye'], 1.0, -a8)
    each(a_blocks)

    def sq1(ch):
        a8_2 = _mm(ch['A8'], bd(ch['A8']))
        ch['A8_2lp'] = a8_2.astype(BF16)
        ch['A8_2'] = bd(a8_2)
    each(sq1)

    def sq2(ch):
        ch['A8_4'] = bd(_mm(ch['A8_2lp'], ch['A8_2']))
        ch['AV'] = _mm(ch['A_uvrv'], ch['vbd'])
    each(sq2)

    def neu1(ch):
        ch['X'] = ch['X0'] + _mm(ch['X0'], ch['A8_2'])
    each(neu1)

    def neu2(ch):
        ch['T'] = ch['X'] + _mm(ch['X'], ch['A8_4'])
    each(neu2)

    for bs in (8, 16, 32):
        def merge_a(ch, bs=bs):
            ch['TO'] = _mm(ch['T'], bd(jnp.where(ch['masks']['off'][bs], ch['A_ub'], 0.0)))
        each(merge_a)

        def merge_b(ch):
            ch['T'] = ch['T'] - _mm(ch['TO'], bd(ch['T']))
        each(merge_b)

    def q_stage(ch):
        rhs = jnp.concatenate([ch['khdbd'], bd(ch['AV'][:L])], axis=1)
        ch['Q'] = _mm(ch['T'], rhs).astype(BF16)
    each(q_stage)

    def mn_stage(ch):
        q = ch['Q']
        QB = _mm(q, ch['be'], TN)
        ch['M'] = (jnp.where(eye2, ch['etot'], 0.0) - jnp.where(same, QB[:PAIR], 0.0)).astype(BF16)
        ch['N'] = lp(_mm(ch['vb'], ch['kte'], TN)) - lp(QB[PAIR:])
        AQ = _mm(ch['A_rb'], jnp.concatenate([bd(q[:, :PAIR]), bd(q[:, PAIR:])], axis=1))
        ch['G1'] = ch['rd'] - AQ[:, :PAIR]
        ch['G2'] = ch['AV'][L:] - AQ[:, PAIR:]
    each(mn_stage)

    def out_stage(ch):
        ch['y'] = _mm(ch['G1'], bd(ch['S']), NT) + ch['G2']
        ch['Sn'] = _mm(ch['S'], ch['M']) + ch['N']
    each(out_stage)


def _wkv_kernel(rkv0, khr0, lor0, rkv1, khr1, lor1,
                ka_ref, rk_ref, w0_ref, wup_ref, a0_ref, aup_ref, s0_ref,
                *rest, zero_state, nb, nc, op_names):
    n_ops = len(op_names)
    y0_ref, y1_ref, bo0_ref, bo1_ref, sfin_ref, s_scr = rest[-(6 + 2 * n_ops):len(rest) - 2 * n_ops]
    op_sets = [dict(zip(op_names, rest[len(rest) - (2 - i) * n_ops:len(rest) - (1 - i) * n_ops])) for i in (0, 1)]
    t = pl.program_id(0)
    c = lax.rem(jnp.maximum(t - 1, 0), nc)

    @pl.when(t == 0)
    def _():
        for ref in op_sets[1].values():
            ref[...] = jnp.zeros(ref.shape, ref.dtype)

    @pl.when(c == 0)
    def _():
        if zero_state:
            s_scr[...] = jnp.zeros(s_scr.shape, F32)
        else:
            s_scr[...] = s0_ref[...]

    def step(ops_in, ops_out):
        L = WKV_CHUNK
        left = lax.broadcasted_iota(jnp.int32, (L, PAIR), 1) < HEAD_DIM
        masks = [_wkv_masks(False), _wkv_masks(True)]

        chains = []
        for e in range(nb):
            for d, (y_ref, bo_ref) in enumerate(((y0_ref, bo0_ref), (y1_ref, bo1_ref))):
                bo_ref[e, 0] = ops_in['bonus'][e, d]
                for p in range(N_PAIRS):
                    sl = slice(p * PAIR, (p + 1) * PAIR)
                    chains.append(dict(
                        e=e, d=d, p=p, sl=sl, y_ref=y_ref, masks=masks[d], S=s_scr[e, d, p],
                        lhsA=ops_in['lhsA'][e, d, p], rhsA=ops_in['rhsA'][e, d, p], vbd=ops_in['vbd'][e, d, p],
                        khdbd=ops_in['khdbd'][e, d, p], rd=ops_in['rd'][e, d, :, sl], be=ops_in['be'][e, d, :, sl],
                        kte=ops_in['kte'][e, d, :, sl], vb=ops_in['vb'][e, d, :, sl],
                        etot=ops_in['etot'][e, d, 0:1, sl]))
        ones_bd = _group_ones()
        ti = lax.broadcasted_iota(jnp.int32, (L, L), 0)
        tj = lax.broadcasted_iota(jnp.int32, (L, L), 1)
        tri = [jnp.where(tj <= ti, 1.0, 0.0).astype(BF16), jnp.where(tj >= ti, 1.0, 0.0).astype(BF16)]
        prepare = [
            functools.partial(_wkv_prepare, e, d, rkv, khr, lor, ka_ref, rk_ref, w0_ref, wup_ref, a0_ref, aup_ref,
                              ops_out, ones_bd, tri, left)
            for e in range(nb) for d, (rkv, khr, lor) in enumerate(((rkv0, khr0, lor0), (rkv1, khr1, lor1)))]
        _wkv_chains(chains, left, prepare)
        for ch in chains:
            ch['y_ref'][ch['e'], 0, :, ch['sl']] = ch['y']
            s_scr[ch['e'], ch['d'], ch['p']] = ch['Sn']

    parity = lax.rem(t, 2)

    @pl.when(parity == 0)
    def _():
        step(op_sets[1], op_sets[0])

    @pl.when(parity == 1)
    def _():
        step(op_sets[0], op_sets[1])

    @pl.when((c == nc - 1) & (t > 0))
    def _():
        for e in range(nb):
            for d in (0, 1):
                for p in range(N_PAIRS):
                    s2 = s_scr[e, d, p]
                    sfin_ref[e, d, 2 * p] = s2[:, :HEAD_DIM]
                    sfin_ref[e, d, 2 * p + 1] = pltpu.roll(s2, HEAD_DIM, 1)[:, :HEAD_DIM]


def _wkv(rkv, rest, kh, s0_pairs, n_batch, seq, P, l, s_all):
    L = WKV_CHUNK
    nc = seq // L
    nb = 2
    n_steps = (n_batch // nb) * nc
    n_tok = n_batch * seq
    zero_state = s0_pairs is None
    if zero_state:
        s0_pairs = jnp.zeros((nb, 2, N_PAIRS, HEAD_DIM, PAIR), F32)
    rkv4 = rkv.reshape(n_batch, nc, L, RKV_W)
    rest4 = rest.reshape(n_batch, nc, L, REST_W)
    kh4 = kh.reshape(n_batch, nc, L, RWKV_WIDTH)

    def in_pos(t, d):
        s = jnp.minimum(t, n_steps - 1)
        c = lax.rem(s, nc)
        return lax.div(s, nc), (c if d == 0 else nc - 1 - c)

    def out_pos(t, d):
        s = jnp.maximum(t - 1, 0)
        c = lax.rem(s, nc)
        return lax.div(s, nc), (c if d == 0 else nc - 1 - c)

    in_specs = []
    for d in (0, 1):
        in_specs += [
            pl.BlockSpec((nb, 1, L, RKV_W), lambda t, d=d: (*in_pos(t, d), 0, 0)),
            pl.BlockSpec((nb, 1, L, RWKV_WIDTH), lambda t, d=d: (*in_pos(t, d), 0, 0)),
            pl.BlockSpec((nb, 1, L, 2 * LANES), lambda t, d=d: (*in_pos(t, d), 0, COL_LORA // (2 * LANES))),
        ]
    state_spec = lambda idx: pl.BlockSpec((nb, 2, N_PAIRS, HEAD_DIM, PAIR), idx)
    in_specs += [
        _lspec(l, (1, RWKV_WIDTH)), _lspec(l, (1, RWKV_WIDTH)),
        _lspec(l, (2, RWKV_WIDTH)), _lspec(l, (2, LANES, RWKV_WIDTH)),
        _lspec(l, (2, RWKV_WIDTH)), _lspec(l, (2, LANES, RWKV_WIDTH)),
        state_spec((lambda t: (0, 0, 0, 0, 0)) if zero_state else (lambda t: (out_pos(t, 0)[0], 0, 0, 0, 0))),
    ]
    args = [rkv4, kh4, rest4, rkv4, kh4, rest4, P['k_a'], P['r_k'], P['w0'], P['wup'], P['a0'], P['aup'], s0_pairs]
    aliases = {}
    if s_all is not None:
        in_specs.append(pl.BlockSpec(memory_space=pl.ANY))
        aliases = {len(args): 4}
        args.append(s_all)
    tok_spec = lambda d: pl.BlockSpec((nb, 1, L, RWKV_WIDTH), lambda t, d=d: (*out_pos(t, d), 0, 0))
    sfin_spec = pl.BlockSpec((nb, None, 2, N_RWKV_HEADS, HEAD_DIM, HEAD_DIM),
                             lambda t: (out_pos(t, 0)[0], l, 0, 0, 0, 0))
    out_specs = [tok_spec(0), tok_spec(1), tok_spec(0), tok_spec(1), sfin_spec]
    tok = jax.ShapeDtypeStruct((n_batch, nc, L, RWKV_WIDTH), F32)
    op_scratch = _wkv_operand_scratch(nb)
    outs = pl.pallas_call(
        functools.partial(_wkv_kernel, zero_state=zero_state, nb=nb, nc=nc, op_names=tuple(op_scratch)),
        grid=(n_steps + 1,),
        in_specs=in_specs,
        out_specs=out_specs,
        out_shape=[tok, tok, tok, tok,
                   jax.ShapeDtypeStruct((n_batch, DEPTH, 2, N_RWKV_HEADS, HEAD_DIM, HEAD_DIM), F32)],
        input_output_aliases=aliases,
        scratch_shapes=([pltpu.VMEM((nb, 2, N_PAIRS, HEAD_DIM, PAIR), F32)]
                        + list(op_scratch.values()) + list(_wkv_operand_scratch(nb).values())),
        compiler_params=pltpu.CompilerParams(
            dimension_semantics=("arbitrary",), vmem_limit_bytes=VMEM_LIMIT_BYTES),
        name="wkv_scan",
    )(*args)
    y0, y1, bo0, bo1, s_fin = outs
    flat = lambda t: t.reshape(n_tok, RWKV_WIDTH)
    return flat(y0), flat(y1), flat(bo0), flat(bo1), s_fin


def _rope(x, cos, sin_signed):
    w = x.shape[1]
    lane = lax.broadcasted_iota(jnp.int32, x.shape, 1)
    partner = jnp.where(_mod_pow2(lane, 2 * ROPE_FREQS) < ROPE_FREQS,
                        pltpu.roll(x, w - ROPE_FREQS, 1), pltpu.roll(x, ROPE_FREQS, 1))
    return x * cos + partner * sin_signed


def _attn_kernel(*refs, tq, seq, past, use_rope):
    it = iter(refs)
    q_ref, kv_ref, qg_ref, kg_ref = next(it), next(it), next(it), next(it)
    if use_rope:
        cq_ref, sq_ref, ck_ref, sk_ref = next(it), next(it), next(it), next(it)
    if past:
        pk_ref, pv_ref = next(it), next(it)
    o_ref, ko_ref, vo_ref, kvar_scr, vvar_scr = refs[-5:]

    ones_bd = _group_ones()

    @pl.when(pl.program_id(1) == 0)
    def _():
        kv = kv_ref[...]
        k_raw = kv[:, :KV_WIDTH]
        kn = k_raw * lax.rsqrt(_group_sum(k_raw * k_raw, ones_bd) * (1.0 / HEAD_DIM) + NORM_EPS) * kg_ref[...]
        if use_rope:
            kn = _rope(kn, ck_ref[...], sk_ref[...])
        vn = kv[:, KV_WIDTH:]
        ko_ref[...] = kn
        vo_ref[...] = vn
        pieces = [(past, seq, kn, vn)]
        if past:
            pieces.append((0, past, pk_ref[...], pv_ref[...]))
        for start, n, kx, vx in pieces:
            left = lax.broadcasted_iota(jnp.int32, kx.shape, 1) < HEAD_DIM
            for x, scr in ((kx, kvar_scr), (vx, vvar_scr)):
                sw = pltpu.roll(x, HEAD_DIM, 1)
                scr[0, start:start + n, :] = jnp.where(left, x, 0.0).astype(BF16)
                scr[1, start:start + n, :] = jnp.where(left, 0.0, sw).astype(BF16)
                scr[2, start:start + n, :] = jnp.where(left, sw, 0.0).astype(BF16)
                scr[3, start:start + n, :] = jnp.where(left, 0.0, x).astype(BF16)

    scale = HEAD_DIM ** -0.5
    n_blk = ATTN_WIDTH // LANES
    qn = []
    for jb in range(n_blk):
        qb = q_ref[:, jb * LANES:(jb + 1) * LANES]
        x = qb * lax.rsqrt(_group_sum(qb * qb, ones_bd) * (1.0 / HEAD_DIM) + NORM_EPS) * qg_ref[...]
        if use_rope:
            x = _rope(x, cq_ref[...], sq_ref[...])
        qn.append((x * scale).astype(BF16))
    heads = [(jb, 2 * (jb // (n_blk // N_KV_HEADS)) + side) for jb in range(n_blk) for side in (0, 1)]
    scores = [lax.dot_general(qn[jb], kvar_scr[var], NT, preferred_element_type=F32) for jb, var in heads]
    exps = [jnp.exp(s - jnp.max(s, axis=-1, keepdims=True)) for s in scores]
    outs = [lax.dot_general(e.astype(BF16), vvar_scr[var], NN, preferred_element_type=F32)
            for e, (jb, var) in zip(exps, heads)]
    outs = [o / jnp.sum(e, axis=-1, keepdims=True) for o, e in zip(outs, exps)]
    for jb in range(n_blk):
        o_ref[:, jb * LANES:(jb + 1) * LANES] = (outs[2 * jb] + outs[2 * jb + 1]).astype(BF16)


def _attention(rest, n_batch, seq, P, l, rope128, past_kv, kv_all):
    tq = 256
    nq = seq // tq
    n_tok = n_batch * seq
    use_rope = rope128 is not None
    past = 0 if past_kv is None else past_kv[0].shape[2]
    in_specs = [
        pl.BlockSpec((tq, ATTN_WIDTH), lambda b, i: (b * nq + i, COL_Q // ATTN_WIDTH)),
        pl.BlockSpec((seq, 2 * KV_WIDTH), lambda b, i: (b, COL_KV // (2 * KV_WIDTH))),
        _lspec(l, (1, LANES)),
        _lspec(l, (1, LANES)),
    ]
    args = [rest, rest, P['q_gain'], P['k_gain']]
    if use_rope:
        cos, sin = rope128
        in_specs += [pl.BlockSpec((tq, LANES), lambda b, i: (i, 0)),
                     pl.BlockSpec((tq, LANES), lambda b, i: (i, 0)),
                     pl.BlockSpec((seq, LANES), lambda b, i: (0, 0)),
                     pl.BlockSpec((seq, LANES), lambda b, i: (0, 0))]
        args += [cos, sin, cos, sin]
    if past:
        in_specs += [pl.BlockSpec((None, None, past, KV_WIDTH), lambda b, i: (b, l, 0, 0)),
                     pl.BlockSpec((None, None, past, KV_WIDTH), lambda b, i: (b, l, 0, 0))]
        args += list(past_kv)
    aliases = {}
    if kv_all is not None:
        in_specs += [pl.BlockSpec(memory_space=pl.ANY), pl.BlockSpec(memory_space=pl.ANY)]
        aliases = {len(args): 1, len(args) + 1: 2}
        args += list(kv_all)
    cache_spec = pl.BlockSpec((None, None, seq, KV_WIDTH), lambda b, i: (b, l, 0, 0))
    out_specs = [pl.BlockSpec((tq, ATTN_WIDTH), lambda b, i: (b * nq + i, 0)), cache_spec, cache_spec]
    return pl.pallas_call(
        functools.partial(_attn_kernel, tq=tq, seq=seq, past=past, use_rope=use_rope),
        grid=(n_batch, nq),
        in_specs=in_specs,
        out_specs=out_specs,
        out_shape=[jax.ShapeDtypeStruct((n_tok, ATTN_WIDTH), BF16),
                   jax.ShapeDtypeStruct((n_batch, DEPTH, seq, KV_WIDTH), F32),
                   jax.ShapeDtypeStruct((n_batch, DEPTH, seq, KV_WIDTH), F32)],
        input_output_aliases=aliases,
        scratch_shapes=[pltpu.VMEM((2 * N_KV_HEADS, past + seq, KV_WIDTH), BF16),
                        pltpu.VMEM((2 * N_KV_HEADS, past + seq, KV_WIDTH), BF16)],
        compiler_params=pltpu.CompilerParams(
            dimension_semantics=("parallel", "arbitrary"), vmem_limit_bytes=VMEM_LIMIT_BYTES),
        name="attention",
    )(*args)


def _post_mlp_kernel(y0_ref, y1_ref, bo0_ref, bo1_ref, lor_ref, oa_ref, gr_ref, ga_ref, x_ref, mod_ref,
                     gup_ref, gnw_ref, gnb_ref, wbr_ref, wout_ref, ng_ref, up_ref, down_ref,
                     o_ref, x1_scr, h2_scr, f_scr):
    j = pl.program_id(1)
    last = pl.num_programs(1) - 1

    @pl.when(j == 0)
    def _():
        ones_bd = _group_ones()
        y = y0_ref[...] + y1_ref[...]
        mean = _group_sum(y, ones_bd) * (1.0 / HEAD_DIM)
        yc = y - mean
        var = _group_sum(yc * yc, ones_bd) * (1.0 / HEAD_DIM)
        yn = yc * lax.rsqrt(var + GN_EPS) * gnw_ref[...] + gnb_ref[...]
        lg = lor_ref[:, LANES:]
        g = _mm(jax.nn.sigmoid(lg), gup_ref[...])
        o_r = (yn + (bo0_ref[...] + bo1_ref[...])) * g
        merged = (jax.nn.sigmoid(gr_ref[...]) * _mm(o_r, wbr_ref[0])
                  + jax.nn.sigmoid(ga_ref[...]) * _mm(oa_ref[...], wbr_ref[1]))
        m = _mm(merged, wout_ref[...])
        g1 = mod_ref[0, 2:3, :]
        sh2 = mod_ref[0, 3:4, :]
        sc2 = mod_ref[0, 4:5, :]
        x1 = x_ref[...] + g1 * _rms(m, ng_ref[1:2, :])
        x1_scr[...] = x1
        h2_scr[...] = (_rms(x1, ng_ref[2:3, :]) * (1.0 + sc2) + sh2).astype(BF16)

    u = lax.dot_general(h2_scr[...], up_ref[...], NN, preferred_element_type=F32)
    u = jnp.square(jnp.maximum(u, 0.0)).astype(BF16)
    part = lax.dot_general(u, down_ref[...], NN, preferred_element_type=F32)

    @pl.when(j == 0)
    def _():
        f_scr[...] = part

    @pl.when((j > 0) & (j < last))
    def _():
        f_scr[...] += part

    @pl.when(j == last)
    def _():
        g2 = mod_ref[0, 5:6, :]
        o_ref[...] = x1_scr[...] + g2 * _rms(f_scr[...] + part, ng_ref[3:4, :])


def _post_mlp(y0, y1, bo0, bo1, rest, o_a, x2d, P, l, mod_row0, rows_per_mod):
    n_tok = x2d.shape[0]
    tm = 512
    ff = D_MODEL
    tok = lambda w: pl.BlockSpec((tm, w), lambda i, j: (i, 0))
    return pl.pallas_call(
        _post_mlp_kernel,
        grid=(n_tok // tm, D_FF // ff),
        in_specs=[
            tok(RWKV_WIDTH), tok(RWKV_WIDTH), tok(RWKV_WIDTH), tok(RWKV_WIDTH),
            pl.BlockSpec((tm, 2 * LANES), lambda i, j: (i, COL_LORA // (2 * LANES))),
            tok(ATTN_WIDTH),
            pl.BlockSpec((tm, D_MODEL), lambda i, j: (i, COL_GR // D_MODEL)),
            pl.BlockSpec((tm, D_MODEL), lambda i, j: (i, COL_GA // D_MODEL)),
            tok(D_MODEL),
            _mod_spec(l, mod_row0, rows_per_mod, tm),
            _lspec(l, (GATE_RANK, RWKV_WIDTH)), _lspec(l, (1, RWKV_WIDTH)), _lspec(l, (1, RWKV_WIDTH)),
            _lspec(l, (2, RWKV_WIDTH, D_MODEL), single=True), _lspec(l, (D_MODEL, D_MODEL), single=True),
            _lspec(l, (4, D_MODEL)),
            pl.BlockSpec((None, D_MODEL, ff), lambda i, j: (l, 0, j)),
            pl.BlockSpec((None, ff, D_MODEL), lambda i, j: (l, j, 0)),
        ],
        out_specs=tok(D_MODEL),
        out_shape=jax.ShapeDtypeStruct((n_tok, D_MODEL), F32),
        scratch_shapes=[pltpu.VMEM((tm, D_MODEL), F32), pltpu.VMEM((tm, D_MODEL), BF16),
                        pltpu.VMEM((tm, D_MODEL), F32)],
        compiler_params=pltpu.CompilerParams(
            dimension_semantics=("parallel", "arbitrary"), vmem_limit_bytes=VMEM_LIMIT_BYTES),
        name="merge_mlp",
    )(y0, y1, bo0, bo1, rest, o_a, rest, rest, x2d, P['mod'], P['gup'], P['gn_w'], P['gn_b'], P['w_br'], P['w_out'],
      P['ng'], P['up'], P['down'])


def _rope_tables(seq):
    n_rows = seq // GRID_W
    rows = np.repeat(np.arange(n_rows, dtype=np.float32), GRID_W)
    cols = np.tile(np.arange(GRID_W, dtype=np.float32), n_rows)
    half = HEAD_DIM // 2
    freqs = 1.0 / (jnp.asarray(ROPE_THETA, F32) ** (jnp.arange(0, half, 2, dtype=F32) / half))
    ang_r = jnp.asarray(rows)[:, None] * freqs
    ang_c = jnp.asarray(cols)[:, None] * freqs
    cr, sr, cc, sc = jnp.cos(ang_r), jnp.sin(ang_r), jnp.cos(ang_c), jnp.sin(ang_c)
    cos64 = jnp.concatenate([cr, cr, cc, cc], axis=1)
    sin64 = jnp.concatenate([-sr, sr, -sc, sc], axis=1)
    return jnp.tile(cos64, (1, LANES // HEAD_DIM)), jnp.tile(sin64, (1, LANES // HEAD_DIM))


def _pairs_from_state(s):
    b = s.shape[0]
    s = s.reshape(b, 2, N_PAIRS, 2, HEAD_DIM, HEAD_DIM)
    return jnp.concatenate([s[:, :, :, 0], s[:, :, :, 1]], axis=-1)


def _layer(x2d, n_batch, seq, P, l, mod_row0, rows_per_mod, rope128, past_kv, s0, carry):
    kv_all, s_all = (None, None) if carry is None else (carry[:2], carry[2])
    rkv, rest, kh = _in_projection(x2d, P, l, mod_row0, rows_per_mod, seq)
    s0_pairs = None if s0 is None else _pairs_from_state(s0)
    y0, y1, bo0, bo1, s_all = _wkv(rkv, rest, kh, s0_pairs, n_batch, seq, P, l, s_all)
    o_a, k_all, v_all = _attention(rest, n_batch, seq, P, l, rope128, past_kv, kv_all)
    x2 = _post_mlp(y0, y1, bo0, bo1, rest, o_a, x2d, P, l, mod_row0, rows_per_mod)
    return x2, (k_all, v_all, s_all)


def kernel(x_prompt, x_sample, cache_k, cache_v, state_wkv, c, c_ctx, w_in, w_br, w_out, w_mod, b_mod, norm_g, mlp_up, mlp_down, rwkv_mu, rwkv_k_k, rwkv_k_a, rwkv_r_k, decay_w0, decay_up, iclr_a0, iclr_up, gate_up, gn_w, gn_b, q_gain, k_gain):
    n_ctx, seq_ctx, _ = x_prompt.shape
    n_dec, seq_dec, _ = x_sample.shape
    past = cache_k.shape[2]

    cvec8 = jnp.zeros((SUBLANES, D_MODEL), F32).at[0].set(c_ctx).at[1:1 + n_dec].set(c)
    mod = _modulation(cvec8, w_mod, b_mod)
    mod = jnp.pad(mod.reshape(DEPTH, SUBLANES, N_MOD, D_MODEL), ((0, 0), (0, 0), (0, SUBLANES - N_MOD), (0, 0)))

    o = np.cumsum((0, RWKV_WIDTH, RWKV_WIDTH, RWKV_WIDTH, DECAY_RANK, ICLR_RANK, GATE_RANK,
                   ATTN_WIDTH, KV_WIDTH, KV_WIDTH, 2 * D_MODEL))
    w_rest = jnp.concatenate([w_in[:, :, o[9]:o[10]], w_in[:, :, o[6]:o[7]], w_in[:, :, o[3]:o[6]],
                              w_in[:, :, o[7]:o[9]]], axis=-1)
    zpad = jnp.zeros((DEPTH, 2, DECAY_RANK, RWKV_WIDTH), F32)
    row = lambda a: a.reshape(DEPTH, 1, -1)
    tile2 = lambda a: jnp.tile(a, (1, LANES // HEAD_DIM)).reshape(DEPTH, 1, LANES)
    P = dict(
        mod=mod, ng=norm_g, w_rkv=w_in[:, :, o[0]:o[3]].astype(BF16), w_rest=w_rest.astype(BF16),
        mu=rwkv_mu, k_k=row(rwkv_k_k), k_a=row(rwkv_k_a), r_k=row(rwkv_r_k),
        w0=decay_w0, wup=jnp.concatenate([decay_up, zpad], axis=2).astype(BF16),
        a0=iclr_a0, aup=jnp.concatenate([zpad, iclr_up], axis=2).astype(BF16),
        gup=gate_up.astype(BF16), gn_w=row(gn_w), gn_b=row(gn_b),
        w_br=w_br.astype(BF16), w_out=w_out.astype(BF16), up=mlp_up.astype(BF16), down=mlp_down.astype(BF16),
        q_gain=tile2(q_gain), k_gain=tile2(k_gain),
    )

    x = x_prompt.reshape(n_ctx * seq_ctx, D_MODEL)
    carry = None
    for l in range(DEPTH):
        x, carry = _layer(x, n_ctx, seq_ctx, P, l, 0, 0, None, None, None, carry)
    y_prompt = x.reshape(n_ctx, seq_ctx, D_MODEL)
    new_k, new_v, new_s = carry
    cache_shape = (n_ctx, DEPTH, seq_ctx, N_KV_HEADS, HEAD_DIM)

    rope128 = _rope_tables(seq_dec)
    past_kv = (cache_k.reshape(n_dec, DEPTH, past, KV_WIDTH), cache_v.reshape(n_dec, DEPTH, past, KV_WIDTH))
    x = x_sample.reshape(n_dec * seq_dec, D_MODEL)
    for l in range(DEPTH):
        x, _ = _layer(x, n_dec, seq_dec, P, l, 1, seq_dec, rope128, past_kv, state_wkv[:, l], None)
    y_sample = x.reshape(n_dec, seq_dec, D_MODEL)

    return (y_prompt, y_sample, new_k.reshape(cache_shape), new_v.reshape(cache_shape), new_s)
```

```python
import functools

import numpy as np
import jax
import jax.numpy as jnp
from jax import lax
from jax.experimental import pallas as pl
from jax.experimental.pallas import tpu as pltpu

F32 = jnp.float32
BF16 = jnp.bfloat16

D_MODEL = 1024
DEPTH = 2
GRID_W = 64
HEAD_DIM = 64
N_RWKV_HEADS = 8
RWKV_WIDTH = N_RWKV_HEADS * HEAD_DIM
N_Q_HEADS = 8
N_KV_HEADS = 2
ATTN_WIDTH = N_Q_HEADS * HEAD_DIM
KV_WIDTH = N_KV_HEADS * HEAD_DIM
DECAY_RANK = 64
ICLR_RANK = 64
GATE_RANK = 128
D_FF = 4 * D_MODEL
ROPE_THETA = 10000.0
ROPE_FREQS = HEAD_DIM // 4
N_MOD = 6
NORM_EPS = 1e-6
GN_EPS = 64e-5
DECAY_SCALE = 0.606531
D_IN = 3 * RWKV_WIDTH + DECAY_RANK + ICLR_RANK + GATE_RANK + ATTN_WIDTH + 2 * KV_WIDTH + 2 * D_MODEL

LANES = 128
SUBLANES = 8
VMEM_LIMIT_BYTES = 56 * 1024 * 1024

RKV_W = 3 * RWKV_WIDTH
GATES_W = 2 * D_MODEL
COL_Q = 0
COL_LORA = COL_Q + ATTN_WIDTH
COL_KV = COL_LORA + 2 * LANES
REST_W = COL_KV + 2 * KV_WIDTH

WKV_CHUNK = 64
PAIR = 2 * HEAD_DIM
N_PAIRS = N_RWKV_HEADS // 2

NN = (((1,), (0,)), ((), ()))
NT = (((1,), (1,)), ((), ()))
TN = (((0,), (0,)), ((), ()))


def _mm(a, b, dims=NN):
    return lax.dot_general(a.astype(BF16), b.astype(BF16), dims, preferred_element_type=F32)


def _split2(x):
    hi = x.astype(BF16)
    lo = (x - hi.astype(F32)).astype(BF16)
    return hi, lo


def _mm_exact_rhs(x, m_bf16):
    hi, lo = _split2(x)
    d = lambda p: lax.dot_general(p, m_bf16, NN, preferred_element_type=F32)
    return d(hi) + d(lo)


def _div_pow2(i, n):
    return lax.shift_right_logical(i, int(np.log2(n)))


def _mod_pow2(i, n):
    return lax.bitwise_and(i, n - 1)


def _group_ones():
    i = lax.broadcasted_iota(jnp.int32, (LANES, LANES), 0)
    j = lax.broadcasted_iota(jnp.int32, (LANES, LANES), 1)
    return jnp.where(_div_pow2(i, HEAD_DIM) == _div_pow2(j, HEAD_DIM), 1.0, 0.0).astype(BF16)


def _group_sum(x, ones_bd):
    blocks = [
        _mm_exact_rhs(x[:, i * LANES:(i + 1) * LANES], ones_bd)
        for i in range(x.shape[1] // LANES)
    ]
    return blocks[0] if len(blocks) == 1 else jnp.concatenate(blocks, axis=1)


def _lspec(l, shape, single=False):
    kw = dict(pipeline_mode=pl.Buffered(1)) if single else {}
    return pl.BlockSpec((None,) + tuple(shape), lambda *g: (l,) + (0,) * len(shape), **kw)


def _mod_spec(l, row0, rows_per_mod, tm):
    if rows_per_mod:
        return pl.BlockSpec((None, 1, SUBLANES, D_MODEL), lambda i, *g: (l, row0 + i * tm // rows_per_mod, 0, 0))
    return pl.BlockSpec((None, 1, SUBLANES, D_MODEL), lambda *g: (l, row0, 0, 0))


def _rms(x, g):
    return x * lax.rsqrt(jnp.mean(x * x, axis=-1, keepdims=True) + NORM_EPS) * g


def _mod_kernel(c_ref, w_ref, b_ref, o_ref):
    c = c_ref[...]
    s = c * jax.nn.sigmoid(c)
    o_ref[0] = _mm(s, w_ref[0]) + b_ref[0]


def _modulation(cvec8, w_mod, b_mod):
    tn = D_MODEL
    n = N_MOD * D_MODEL
    return pl.pallas_call(
        _mod_kernel,
        grid=(DEPTH, n // tn),
        in_specs=[
            pl.BlockSpec((SUBLANES, D_MODEL), lambda l, j: (0, 0)),
            pl.BlockSpec((1, D_MODEL, tn), lambda l, j: (l, 0, j)),
            pl.BlockSpec((1, 1, tn), lambda l, j: (l, 0, j)),
        ],
        out_specs=pl.BlockSpec((1, SUBLANES, tn), lambda l, j: (l, 0, j)),
        out_shape=jax.ShapeDtypeStruct((DEPTH, SUBLANES, n), F32),
        name="modulation",
    )(cvec8, w_mod, b_mod.reshape(DEPTH, 1, n))


def _shift_mix(x, mu):
    n = x.shape[0]
    row = lax.broadcasted_iota(jnp.int32, x.shape, 0)
    x_prev = jnp.where(row == 0, 0.0, pltpu.roll(x, 1, 0))
    x_next = jnp.where(row == n - 1, 0.0, pltpu.roll(x, n - 1, 0))
    return x * (1.0 - mu) + (0.5 * mu) * (x_prev + x_next)


def _inproj_kernel(x_ref, mod_ref, ng_ref, wrkv_ref, wrest_ref, mu_ref, kk_ref,
                   rkv_ref, gates_ref, rest_ref, kh_ref, h_scr, *, tm, seq):
    j = pl.program_id(1)
    C = RWKV_WIDTH

    def mix_in_place(i, piece):
        lanes = slice(piece * LANES, (piece + 1) * LANES)
        cols = slice(i * C + piece * LANES, i * C + (piece + 1) * LANES)
        for s in range(tm // seq):
            rows = slice(s * seq, (s + 1) * seq)
            mixed = _shift_mix(rkv_ref[rows, cols], mu_ref[i:i + 1, lanes])
            rkv_ref[rows, cols] = mixed
            if i == 1:
                kk = mixed * kk_ref[:, lanes]
                kh_ref[rows, lanes] = kk * lax.rsqrt(_group_sum(kk * kk, _group_ones()) + 1e-12)

    @pl.when(j == 0)
    def _():
        sh = mod_ref[0, 0:1, :]
        sc = mod_ref[0, 1:2, :]
        ng = ng_ref[0:1, :]
        sub = 256
        for i in range(tm // sub):
            x = x_ref[i * sub:(i + 1) * sub, :]
            h_scr[i * sub:(i + 1) * sub, :] = (_rms(x, ng) * (1.0 + sc) + sh).astype(BF16)
        rkv_ref[...] = lax.dot_general(h_scr[...], wrkv_ref[...], NN, preferred_element_type=F32)

    n_piece = C // LANES
    tn = wrest_ref.shape[1] // n_piece
    for jj in range(1, 4):
        @pl.when(j == jj)
        def _(jj=jj):
            for piece in range(n_piece):
                cols = slice(piece * tn, (piece + 1) * tn)
                out = lax.dot_general(h_scr[...], wrest_ref[:, cols], NN, preferred_element_type=F32)
                if jj < 3:
                    gates_ref[:, cols] = out.astype(BF16)
                else:
                    rest_ref[:, cols] = out
                mix_in_place(jj - 1, piece)


def _in_projection(x2d, P, l, mod_row0, rows_per_mod, seq):
    n_tok = x2d.shape[0]
    tm = 1024
    tn = D_MODEL
    assert tm % seq == 0 and GATES_W == 2 * tn and REST_W == tn
    return pl.pallas_call(
        functools.partial(_inproj_kernel, tm=tm, seq=seq),
        grid=(n_tok // tm, 4),
        in_specs=[
            pl.BlockSpec((tm, D_MODEL), lambda i, j: (i, 0)),
            _mod_spec(l, mod_row0, rows_per_mod, tm),
            _lspec(l, (4, D_MODEL)),
            _lspec(l, (D_MODEL, RKV_W), single=True),
            pl.BlockSpec((None, D_MODEL, tn), lambda i, j: (l, 0, jnp.maximum(j - 1, 0))),
            _lspec(l, (3, RWKV_WIDTH)),
            _lspec(l, (1, RWKV_WIDTH)),
        ],
        out_specs=[pl.BlockSpec((tm, RKV_W), lambda i, j: (i, 0)),
                   pl.BlockSpec((tm, tn), lambda i, j: (i, jnp.clip(j - 1, 0, 1))),
                   pl.BlockSpec((tm, tn), lambda i, j: (i, 0)),
                   pl.BlockSpec((tm, RWKV_WIDTH), lambda i, j: (i, 0))],
        out_shape=[jax.ShapeDtypeStruct((n_tok, RKV_W), F32),
                   jax.ShapeDtypeStruct((n_tok, GATES_W), BF16),
                   jax.ShapeDtypeStruct((n_tok, REST_W), F32),
                   jax.ShapeDtypeStruct((n_tok, RWKV_WIDTH), F32)],
        scratch_shapes=[pltpu.VMEM((tm, D_MODEL), BF16)],
        compiler_params=pltpu.CompilerParams(
            dimension_semantics=("parallel", "arbitrary"), vmem_limit_bytes=VMEM_LIMIT_BYTES),
        name="in_projection",
    )(x2d, P['mod'], P['ng'], P['w_rkv'], P['w_rest'], P['mu'], P['k_k'])


def _wkv_masks(reverse):
    L = WKV_CHUNK
    row = lax.broadcasted_iota(jnp.int32, (L, PAIR), 0)
    col = _mod_pow2(lax.broadcasted_iota(jnp.int32, (L, PAIR), 1), L)
    blk = {bs: _div_pow2(row, bs) == _div_pow2(col, bs) for bs in (8, 16, 32, 64)}
    return dict(
        strict=(col > row) if reverse else (col < row),
        incl=(col >= row) if reverse else (col <= row),
        blk8=blk[8],
        off={bs: blk[2 * bs] & jnp.logical_not(blk[bs]) for bs in (8, 16, 32)},
        eye=row == col,
    )


def _bd(x, left):
    x = x.astype(BF16)
    z = jnp.zeros_like(x)
    return jnp.concatenate([jnp.where(left, x, z), jnp.where(left, z, x)], axis=0)


def _wkv_operand_scratch(nb):
    C, L = RWKV_WIDTH, WKV_CHUNK
    full = lambda dt: pltpu.VMEM((nb, 2, L, C), dt)
    pair = lambda rows: pltpu.VMEM((nb, 2, N_PAIRS, rows, PAIR), BF16)
    return dict(khd=full(F32), rd=full(F32), bonus=full(F32), etot=pltpu.VMEM((nb, 2, SUBLANES, C), F32),
                be=full(BF16), kte=full(BF16), vb=full(BF16),
                lhsA=pair(2 * L), rhsA=pair(4 * L), vbd=pair(2 * L), khdbd=pair(2 * L))


def _wkv_prepare(e, d, rkv, khr, lor, ka_ref, rk_ref, w0_ref, wup_ref, a0_ref, aup_ref, ops, ones_bd, tri, left):
    C, L = RWKV_WIDTH, WKV_CHUNK
    reverse = d == 1
    r = rkv[e, 0, :, 0:C]
    k = rkv[e, 0, :, C:2 * C]
    v = rkv[e, 0, :, 2 * C:3 * C]
    kh = khr[e, 0]
    lo_dec = lor[e, 0, :, 0:LANES]
    w_raw = _mm(jnp.tanh(lo_dec), wup_ref[d]) + w0_ref[d:d + 1, :]
    lw = -DECAY_SCALE * jax.nn.sigmoid(w_raw)
    a = jax.nn.sigmoid(_mm(lo_dec, aup_ref[d]) + a0_ref[d:d + 1, :])
    kt = k * (1.0 + (a - 1.0) * ka_ref[...])
    ops['bonus'][e, d] = _group_sum(r * kt * rk_ref[...], ones_bd) * v
    b = a * kh

    hi, lo2 = _split2(lw)
    cum = (lax.dot_general(tri[d], hi, NN, preferred_element_type=F32)
           + lax.dot_general(tri[d], lo2, NN, preferred_element_type=F32))
    tot = cum[0:1, :] if reverse else cum[L - 1:L, :]
    e_inv = jnp.exp(-cum)
    e_end = jnp.exp(tot - cum)
    khd = kh * jnp.exp(cum - lw)
    rd = r * jnp.exp(cum)
    bi = b * e_inv
    ki = kt * e_inv
    ops['khd'][e, d] = khd
    ops['rd'][e, d] = rd
    ops['etot'][e, d] = jnp.broadcast_to(jnp.exp(tot), (SUBLANES, C))
    ops['be'][e, d] = (b * e_end).astype(BF16)
    ops['kte'][e, d] = (kt * e_end).astype(BF16)
    ops['vb'][e, d] = v.astype(BF16)
    for p in range(N_PAIRS):
        sl = slice(p * PAIR, (p + 1) * PAIR)
        ops['lhsA'][e, d, p] = jnp.concatenate([khd[:, sl], rd[:, sl]], axis=0).astype(BF16)
        ops['rhsA'][e, d, p] = jnp.concatenate([_bd(bi[:, sl], left), _bd(ki[:, sl], left)], axis=0)
        ops['vbd'][e, d, p] = _bd(v[:, sl], left)
        ops['khdbd'][e, d, p] = _bd(khd[:, sl], left)


def _wkv_chains(chains, left, fillers):
    L = WKV_CHUNK
    i2 = lax.broadcasted_iota(jnp.int32, (PAIR, PAIR), 0)
    j2 = lax.broadcasted_iota(jnp.int32, (PAIR, PAIR), 1)
    same = _div_pow2(i2, HEAD_DIM) == _div_pow2(j2, HEAD_DIM)
    eye2 = i2 == j2
    bd = lambda x: _bd(x, left)
    lp = lambda g: jnp.where(left, g[:L], g[L:])
    n_stages = 13
    due = {((k + 1) * n_stages) // (len(fillers) + 1): f for k, f in enumerate(fillers)}
    emitted = [0]

    def each(f):
        for ch in chains:
            f(ch)
        emitted[0] += 1
        if emitted[0] in due:
            due[emitted[0]]()

    def a_blocks(ch):
        m = ch['masks']
        A = lax.dot_general(ch['lhsA'], ch['rhsA'], NT, preferred_element_type=F32)
        ch['A_ub'] = jnp.where(m['strict'], A[:L, :PAIR], 0.0)
        ch['A_uvrv'] = jnp.concatenate([jnp.where(m['strict'], A[:L, PAIR:], 0.0),
                                        jnp.where(m['incl'], A[L:, PAIR:], 0.0)], axis=0).astype(BF16)
        ch['A_rb'] = jnp.where(m['incl'], A[L:, :PAIR], 0.0).astype(BF16)
        a8 = jnp.where(m['blk8'], ch['A_ub'], 0.0)
        ch['A8'] = a8
        ch['X0'] = jnp.where(m['eye'], 1.0, -a8)
    each(a_blocks)

    def sq1(ch):
        a8_2 = _mm(ch['A8'], bd(ch['A8']))
        ch['A8_2lp'] = a8_2.astype(BF16)
        ch['A8_2'] = bd(a8_2)
    each(sq1)

    def sq2(ch):
        ch['A8_4'] = bd(_mm(ch['A8_2lp'], ch['A8_2']))
        ch['AV'] = _mm(ch['A_uvrv'], ch['vbd'])
    each(sq2)

    def neu1(ch):
        ch['X'] = ch['X0'] + _mm(ch['X0'], ch['A8_2'])
    each(neu1)

    def neu2(ch):
        ch['T'] = ch['X'] + _mm(ch['X'], ch['A8_4'])
    each(neu2)

    for bs in (8, 16, 32):
        def merge_a(ch, bs=bs):
            ch['TO'] = _mm(ch['T'], bd(jnp.where(ch['masks']['off'][bs], ch['A_ub'], 0.0)))
        each(merge_a)

        def merge_b(ch):
            ch['T'] = ch['T'] - _mm(ch['TO'], bd(ch['T']))
        each(merge_b)

    def q_stage(ch):
        rhs = jnp.concatenate([ch['khdbd'], bd(ch['AV'][:L])], axis=1)
        ch['Q'] = _mm(ch['T'], rhs).astype(BF16)
    each(q_stage)

    def mn_stage(ch):
        q = ch['Q']
        QB = _mm(q, ch['be'], TN)
        ch['M'] = (jnp.where(eye2, ch['etot'], 0.0) - jnp.where(same, QB[:PAIR], 0.0)).astype(BF16)
        ch['N'] = lp(_mm(ch['vb'], ch['kte'], TN)) - lp(QB[PAIR:])
        AQ = _mm(ch['A_rb'], jnp.concatenate([bd(q[:, :PAIR]), bd(q[:, PAIR:])], axis=1))
        ch['G1'] = ch['rd'] - AQ[:, :PAIR]
        ch['G2'] = ch['AV'][L:] - AQ[:, PAIR:]
    each(mn_stage)

    def out_stage(ch):
        ch['y'] = _mm(ch['G1'], bd(ch['S']), NT) + ch['G2']
        ch['Sn'] = _mm(ch['S'], ch['M']) + ch['N']
    each(out_stage)


def _wkv_kernel(rkv0, khr0, lor0, rkv1, khr1, lor1,
                ka_ref, rk_ref, w0_ref, wup_ref, a0_ref, aup_ref, s0_ref,
                *rest, zero_state, nb, nc, op_names):
    n_ops = len(op_names)
    y0_ref, y1_ref, bo0_ref, bo1_ref, sfin_ref, s_scr = rest[-(6 + 2 * n_ops):len(rest) - 2 * n_ops]
    op_sets = [dict(zip(op_names, rest[len(rest) - (2 - i) * n_ops:len(rest) - (1 - i) * n_ops])) for i in (0, 1)]
    t = pl.program_id(0)
    c = lax.rem(jnp.maximum(t - 1, 0), nc)

    @pl.when(t == 0)
    def _():
        for ref in op_sets[1].values():
            ref[...] = jnp.zeros(ref.shape, ref.dtype)

    @pl.when(c == 0)
    def _():
        if zero_state:
            s_scr[...] = jnp.zeros(s_scr.shape, F32)
        else:
            s_scr[...] = s0_ref[...]

    def step(ops_in, ops_out):
        L = WKV_CHUNK
        left = lax.broadcasted_iota(jnp.int32, (L, PAIR), 1) < HEAD_DIM
        masks = [_wkv_masks(False), _wkv_masks(True)]

        chains = []
        for e in range(nb):
            for d, (y_ref, bo_ref) in enumerate(((y0_ref, bo0_ref), (y1_ref, bo1_ref))):
                bo_ref[e, 0] = ops_in['bonus'][e, d].astype(BF16)
                for p in range(N_PAIRS):
                    sl = slice(p * PAIR, (p + 1) * PAIR)
                    chains.append(dict(
                        e=e, d=d, p=p, sl=sl, y_ref=y_ref, masks=masks[d], S=s_scr[e, d, p],
                        lhsA=ops_in['lhsA'][e, d, p], rhsA=ops_in['rhsA'][e, d, p], vbd=ops_in['vbd'][e, d, p],
                        khdbd=ops_in['khdbd'][e, d, p], rd=ops_in['rd'][e, d, :, sl], be=ops_in['be'][e, d, :, sl],
                        kte=ops_in['kte'][e, d, :, sl], vb=ops_in['vb'][e, d, :, sl],
                        etot=ops_in['etot'][e, d, 0:1, sl]))
        ones_bd = _group_ones()
        ti = lax.broadcasted_iota(jnp.int32, (L, L), 0)
        tj = lax.broadcasted_iota(jnp.int32, (L, L), 1)
        tri = [jnp.where(tj <= ti, 1.0, 0.0).astype(BF16), jnp.where(tj >= ti, 1.0, 0.0).astype(BF16)]
        prepare = [
            functools.partial(_wkv_prepare, e, d, rkv, khr, lor, ka_ref, rk_ref, w0_ref, wup_ref, a0_ref, aup_ref,
                              ops_out, ones_bd, tri, left)
            for e in range(nb) for d, (rkv, khr, lor) in enumerate(((rkv0, khr0, lor0), (rkv1, khr1, lor1)))]
        _wkv_chains(chains, left, prepare)
        for ch in chains:
            ch['y_ref'][ch['e'], 0, :, ch['sl']] = ch['y'].astype(BF16)
            s_scr[ch['e'], ch['d'], ch['p']] = ch['Sn']

    parity = lax.rem(t, 2)

    @pl.when(parity == 0)
    def _():
        step(op_sets[1], op_sets[0])

    @pl.when(parity == 1)
    def _():
        step(op_sets[0], op_sets[1])

    @pl.when((c == nc - 1) & (t > 0))
    def _():
        for e in range(nb):
            for d in (0, 1):
                for p in range(N_PAIRS):
                    s2 = s_scr[e, d, p]
                    sfin_ref[e, d, 2 * p] = s2[:, :HEAD_DIM]
                    sfin_ref[e, d, 2 * p + 1] = pltpu.roll(s2, HEAD_DIM, 1)[:, :HEAD_DIM]


def _wkv(rkv, rest, kh, s0_pairs, n_batch, seq, P, l, s_all):
    L = WKV_CHUNK
    nc = seq // L
    nb = 2
    n_steps = (n_batch // nb) * nc
    n_tok = n_batch * seq
    zero_state = s0_pairs is None
    if zero_state:
        s0_pairs = jnp.zeros((nb, 2, N_PAIRS, HEAD_DIM, PAIR), F32)
    rkv4 = rkv.reshape(n_batch, nc, L, RKV_W)
    rest4 = rest.reshape(n_batch, nc, L, REST_W)
    kh4 = kh.reshape(n_batch, nc, L, RWKV_WIDTH)

    def in_pos(t, d):
        s = jnp.minimum(t, n_steps - 1)
        c = lax.rem(s, nc)
        return lax.div(s, nc), (c if d == 0 else nc - 1 - c)

    def out_pos(t, d):
        s = jnp.maximum(t - 1, 0)
        c = lax.rem(s, nc)
        return lax.div(s, nc), (c if d == 0 else nc - 1 - c)

    in_specs = []
    for d in (0, 1):
        in_specs += [
            pl.BlockSpec((nb, 1, L, RKV_W), lambda t, d=d: (*in_pos(t, d), 0, 0)),
            pl.BlockSpec((nb, 1, L, RWKV_WIDTH), lambda t, d=d: (*in_pos(t, d), 0, 0)),
            pl.BlockSpec((nb, 1, L, 2 * LANES), lambda t, d=d: (*in_pos(t, d), 0, COL_LORA // (2 * LANES))),
        ]
    state_spec = lambda idx: pl.BlockSpec((nb, 2, N_PAIRS, HEAD_DIM, PAIR), idx)
    in_specs += [
        _lspec(l, (1, RWKV_WIDTH)), _lspec(l, (1, RWKV_WIDTH)),
        _lspec(l, (2, RWKV_WIDTH)), _lspec(l, (2, LANES, RWKV_WIDTH)),
        _lspec(l, (2, RWKV_WIDTH)), _lspec(l, (2, LANES, RWKV_WIDTH)),
        state_spec((lambda t: (0, 0, 0, 0, 0)) if zero_state else (lambda t: (out_pos(t, 0)[0], 0, 0, 0, 0))),
    ]
    args = [rkv4, kh4, rest4, rkv4, kh4, rest4, P['k_a'], P['r_k'], P['w0'], P['wup'], P['a0'], P['aup'], s0_pairs]
    aliases = {}
    if s_all is not None:
        in_specs.append(pl.BlockSpec(memory_space=pl.ANY))
        aliases = {len(args): 4}
        args.append(s_all)
    tok_spec = lambda d: pl.BlockSpec((nb, 1, L, RWKV_WIDTH), lambda t, d=d: (*out_pos(t, d), 0, 0))
    sfin_spec = pl.BlockSpec((nb, None, 2, N_RWKV_HEADS, HEAD_DIM, HEAD_DIM),
                             lambda t: (out_pos(t, 0)[0], l, 0, 0, 0, 0))
    out_specs = [tok_spec(0), tok_spec(1), tok_spec(0), tok_spec(1), sfin_spec]
    tok = jax.ShapeDtypeStruct((n_batch, nc, L, RWKV_WIDTH), BF16)
    op_scratch = _wkv_operand_scratch(nb)
    outs = pl.pallas_call(
        functools.partial(_wkv_kernel, zero_state=zero_state, nb=nb, nc=nc, op_names=tuple(op_scratch)),
        grid=(n_steps + 1,),
        in_specs=in_specs,
        out_specs=out_specs,
        out_shape=[tok, tok, tok, tok,
                   jax.ShapeDtypeStruct((n_batch, DEPTH, 2, N_RWKV_HEADS, HEAD_DIM, HEAD_DIM), F32)],
        input_output_aliases=aliases,
        scratch_shapes=([pltpu.VMEM((nb, 2, N_PAIRS, HEAD_DIM, PAIR), F32)]
                        + list(op_scratch.values()) + list(_wkv_operand_scratch(nb).values())),
        compiler_params=pltpu.CompilerParams(
            dimension_semantics=("arbitrary",), vmem_limit_bytes=VMEM_LIMIT_BYTES),
        name="wkv_scan",
    )(*args)
    y0, y1, bo0, bo1, s_fin = outs
    flat = lambda t: t.reshape(n_tok, RWKV_WIDTH)
    return flat(y0), flat(y1), flat(bo0), flat(bo1), s_fin


def _rope(x, cos, sin_signed):
    w = x.shape[1]
    lane = lax.broadcasted_iota(jnp.int32, x.shape, 1)
    partner = jnp.where(_mod_pow2(lane, 2 * ROPE_FREQS) < ROPE_FREQS,
                        pltpu.roll(x, w - ROPE_FREQS, 1), pltpu.roll(x, ROPE_FREQS, 1))
    return x * cos + partner * sin_signed


def _attn_kernel(*refs, tq, seq, past, use_rope):
    it = iter(refs)
    q_ref, kv_ref, qg_ref, kg_ref = next(it), next(it), next(it), next(it)
    if use_rope:
        cq_ref, sq_ref, ck_ref, sk_ref = next(it), next(it), next(it), next(it)
    if past:
        pk_ref, pv_ref = next(it), next(it)
    o_ref, ko_ref, vo_ref, kvar_scr, vvar_scr = refs[-5:]

    ones_bd = _group_ones()

    @pl.when(pl.program_id(1) == 0)
    def _():
        kv = kv_ref[...]
        k_raw = kv[:, :KV_WIDTH]
        kn = k_raw * lax.rsqrt(_group_sum(k_raw * k_raw, ones_bd) * (1.0 / HEAD_DIM) + NORM_EPS) * kg_ref[...]
        if use_rope:
            kn = _rope(kn, ck_ref[...], sk_ref[...])
        vn = kv[:, KV_WIDTH:]
        ko_ref[...] = kn
        vo_ref[...] = vn
        pieces = [(past, seq, kn, vn)]
        if past:
            pieces.append((0, past, pk_ref[...], pv_ref[...]))
        for start, n, kx, vx in pieces:
            left = lax.broadcasted_iota(jnp.int32, kx.shape, 1) < HEAD_DIM
            for x, scr in ((kx, kvar_scr), (vx, vvar_scr)):
                sw = pltpu.roll(x, HEAD_DIM, 1)
                scr[0, start:start + n, :] = jnp.where(left, x, 0.0).astype(BF16)
                scr[1, start:start + n, :] = jnp.where(left, 0.0, sw).astype(BF16)
                scr[2, start:start + n, :] = jnp.where(left, sw, 0.0).astype(BF16)
                scr[3, start:start + n, :] = jnp.where(left, 0.0, x).astype(BF16)

    scale = HEAD_DIM ** -0.5
    n_blk = ATTN_WIDTH // LANES
    qn = []
    for jb in range(n_blk):
        qb = q_ref[:, jb * LANES:(jb + 1) * LANES]
        x = qb * lax.rsqrt(_group_sum(qb * qb, ones_bd) * (1.0 / HEAD_DIM) + NORM_EPS) * qg_ref[...]
        if use_rope:
            x = _rope(x, cq_ref[...], sq_ref[...])
        qn.append((x * scale).astype(BF16))
    heads = [(jb, 2 * (jb // (n_blk // N_KV_HEADS)) + side) for jb in range(n_blk) for side in (0, 1)]
    scores = [lax.dot_general(qn[jb], kvar_scr[var], NT, preferred_element_type=F32) for jb, var in heads]
    exps = [jnp.exp(s - jnp.max(s, axis=-1, keepdims=True)) for s in scores]
    outs = [lax.dot_general(e.astype(BF16), vvar_scr[var], NN, preferred_element_type=F32)
            for e, (jb, var) in zip(exps, heads)]
    outs = [o / jnp.sum(e, axis=-1, keepdims=True) for o, e in zip(outs, exps)]
    for jb in range(n_blk):
        o_ref[:, jb * LANES:(jb + 1) * LANES] = (outs[2 * jb] + outs[2 * jb + 1]).astype(BF16)


def _attention(rest, n_batch, seq, P, l, rope128, past_kv, kv_all):
    tq = 256
    nq = seq // tq
    n_tok = n_batch * seq
    use_rope = rope128 is not None
    past = 0 if past_kv is None else past_kv[0].shape[2]
    in_specs = [
        pl.BlockSpec((tq, ATTN_WIDTH), lambda b, i: (b * nq + i, COL_Q // ATTN_WIDTH)),
        pl.BlockSpec((seq, 2 * KV_WIDTH), lambda b, i: (b, COL_KV // (2 * KV_WIDTH))),
        _lspec(l, (1, LANES)),
        _lspec(l, (1, LANES)),
    ]
    args = [rest, rest, P['q_gain'], P['k_gain']]
    if use_rope:
        cos, sin = rope128
        in_specs += [pl.BlockSpec((tq, LANES), lambda b, i: (i, 0)),
                     pl.BlockSpec((tq, LANES), lambda b, i: (i, 0)),
                     pl.BlockSpec((seq, LANES), lambda b, i: (0, 0)),
                     pl.BlockSpec((seq, LANES), lambda b, i: (0, 0))]
        args += [cos, sin, cos, sin]
    if past:
        in_specs += [pl.BlockSpec((None, None, past, KV_WIDTH), lambda b, i: (b, l, 0, 0)),
                     pl.BlockSpec((None, None, past, KV_WIDTH), lambda b, i: (b, l, 0, 0))]
        args += list(past_kv)
    aliases = {}
    if kv_all is not None:
        in_specs += [pl.BlockSpec(memory_space=pl.ANY), pl.BlockSpec(memory_space=pl.ANY)]
        aliases = {len(args): 1, len(args) + 1: 2}
        args += list(kv_all)
    cache_spec = pl.BlockSpec((None, None, seq, KV_WIDTH), lambda b, i: (b, l, 0, 0))
    out_specs = [pl.BlockSpec((tq, ATTN_WIDTH), lambda b, i: (b * nq + i, 0)), cache_spec, cache_spec]
    return pl.pallas_call(
        functools.partial(_attn_kernel, tq=tq, seq=seq, past=past, use_rope=use_rope),
        grid=(n_batch, nq),
        in_specs=in_specs,
        out_specs=out_specs,
        out_shape=[jax.ShapeDtypeStruct((n_tok, ATTN_WIDTH), BF16),
                   jax.ShapeDtypeStruct((n_batch, DEPTH, seq, KV_WIDTH), F32),
                   jax.ShapeDtypeStruct((n_batch, DEPTH, seq, KV_WIDTH), F32)],
        input_output_aliases=aliases,
        scratch_shapes=[pltpu.VMEM((2 * N_KV_HEADS, past + seq, KV_WIDTH), BF16),
                        pltpu.VMEM((2 * N_KV_HEADS, past + seq, KV_WIDTH), BF16)],
        compiler_params=pltpu.CompilerParams(
            dimension_semantics=("parallel", "arbitrary"), vmem_limit_bytes=VMEM_LIMIT_BYTES),
        name="attention",
    )(*args)


def _post_kernel(y0_ref, y1_ref, bo0_ref, bo1_ref, lor_ref, oa_ref, gr_ref, ga_ref, x_ref, mod_ref,
                 gup_ref, gnw_ref, gnb_ref, wbr_ref, wout_ref, ng_ref, x1_ref, h2_ref):
    ones_bd = _group_ones()
    y = y0_ref[...].astype(F32) + y1_ref[...].astype(F32)
    mean = _group_sum(y, ones_bd) * (1.0 / HEAD_DIM)
    yc = y - mean
    var = _group_sum(yc * yc, ones_bd) * (1.0 / HEAD_DIM)
    yn = yc * lax.rsqrt(var + GN_EPS) * gnw_ref[...] + gnb_ref[...]
    lg = lor_ref[:, LANES:]
    g = _mm(jax.nn.sigmoid(lg), gup_ref[...])
    o_r = (yn + (bo0_ref[...].astype(F32) + bo1_ref[...].astype(F32))) * g
    merged = (jax.nn.sigmoid(gr_ref[...].astype(F32)) * _mm(o_r, wbr_ref[0])
              + jax.nn.sigmoid(ga_ref[...].astype(F32)) * _mm(oa_ref[...], wbr_ref[1]))
    m = _mm(merged, wout_ref[...])
    g1 = mod_ref[0, 2:3, :]
    sh2 = mod_ref[0, 3:4, :]
    sc2 = mod_ref[0, 4:5, :]
    x1 = x_ref[...] + g1 * _rms(m, ng_ref[1:2, :])
    x1_ref[...] = x1
    h2_ref[...] = (_rms(x1, ng_ref[2:3, :]) * (1.0 + sc2) + sh2).astype(BF16)


def _post(y0, y1, bo0, bo1, rest, gates, o_a, x2d, P, l, mod_row0, rows_per_mod):
    n_tok = x2d.shape[0]
    tm = 512
    tok = lambda w: pl.BlockSpec((tm, w), lambda i: (i, 0))
    return pl.pallas_call(
        _post_kernel,
        grid=(n_tok // tm,),
        in_specs=[
            tok(RWKV_WIDTH), tok(RWKV_WIDTH), tok(RWKV_WIDTH), tok(RWKV_WIDTH),
            pl.BlockSpec((tm, 2 * LANES), lambda i: (i, COL_LORA // (2 * LANES))),
            tok(ATTN_WIDTH),
            pl.BlockSpec((tm, D_MODEL), lambda i: (i, 0)),
            pl.BlockSpec((tm, D_MODEL), lambda i: (i, 1)),
            tok(D_MODEL),
            _mod_spec(l, mod_row0, rows_per_mod, tm),
            _lspec(l, (GATE_RANK, RWKV_WIDTH)), _lspec(l, (1, RWKV_WIDTH)), _lspec(l, (1, RWKV_WIDTH)),
            _lspec(l, (2, RWKV_WIDTH, D_MODEL)), _lspec(l, (D_MODEL, D_MODEL)), _lspec(l, (4, D_MODEL)),
        ],
        out_specs=[tok(D_MODEL), tok(D_MODEL)],
        out_shape=[jax.ShapeDtypeStruct((n_tok, D_MODEL), F32), jax.ShapeDtypeStruct((n_tok, D_MODEL), BF16)],
        compiler_params=pltpu.CompilerParams(
            dimension_semantics=("parallel",), vmem_limit_bytes=VMEM_LIMIT_BYTES),
        name="merge_outproj",
    )(y0, y1, bo0, bo1, rest, o_a, gates, gates, x2d, P['mod'], P['gup'], P['gn_w'], P['gn_b'], P['w_br'], P['w_out'],
      P['ng'])


def _mlp_kernel(h_ref, x1_ref, mod_ref, up_ref, down_ref, ng_ref, o_ref):
    h = h_ref[...]
    f = None
    ff = D_MODEL
    for j in range(D_FF // ff):
        u = lax.dot_general(h, up_ref[:, j * ff:(j + 1) * ff], NN, preferred_element_type=F32)
        u = jnp.square(jnp.maximum(u, 0.0)).astype(BF16)
        part = lax.dot_general(u, down_ref[j * ff:(j + 1) * ff, :], NN, preferred_element_type=F32)
        f = part if f is None else f + part
    g2 = mod_ref[0, 5:6, :]
    o_ref[...] = x1_ref[...] + g2 * _rms(f, ng_ref[3:4, :])


def _mlp(h2, x1, P, l, mod_row0, rows_per_mod):
    n_tok = x1.shape[0]
    tm = 512
    tok = lambda: pl.BlockSpec((tm, D_MODEL), lambda i: (i, 0))
    return pl.pallas_call(
        _mlp_kernel,
        grid=(n_tok // tm,),
        in_specs=[
            tok(), tok(),
            _mod_spec(l, mod_row0, rows_per_mod, tm),
            _lspec(l, (D_MODEL, D_FF), single=True),
            _lspec(l, (D_FF, D_MODEL), single=True),
            _lspec(l, (4, D_MODEL)),
        ],
        out_specs=tok(),
        out_shape=jax.ShapeDtypeStruct((n_tok, D_MODEL), F32),
        compiler_params=pltpu.CompilerParams(
            dimension_semantics=("parallel",), vmem_limit_bytes=VMEM_LIMIT_BYTES),
        name="mlp",
    )(h2, x1, P['mod'], P['up'], P['down'], P['ng'])


def _rope_tables(seq):
    n_rows = seq // GRID_W
    rows = np.repeat(np.arange(n_rows, dtype=np.float32), GRID_W)
    cols = np.tile(np.arange(GRID_W, dtype=np.float32), n_rows)
    half = HEAD_DIM // 2
    freqs = 1.0 / (jnp.asarray(ROPE_THETA, F32) ** (jnp.arange(0, half, 2, dtype=F32) / half))
    ang_r = jnp.asarray(rows)[:, None] * freqs
    ang_c = jnp.asarray(cols)[:, None] * freqs
    cr, sr, cc, sc = jnp.cos(ang_r), jnp.sin(ang_r), jnp.cos(ang_c), jnp.sin(ang_c)
    cos64 = jnp.concatenate([cr, cr, cc, cc], axis=1)
    sin64 = jnp.concatenate([-sr, sr, -sc, sc], axis=1)
    return jnp.tile(cos64, (1, LANES // HEAD_DIM)), jnp.tile(sin64, (1, LANES // HEAD_DIM))


def _pairs_from_state(s):
    b = s.shape[0]
    s = s.reshape(b, 2, N_PAIRS, 2, HEAD_DIM, HEAD_DIM)
    return jnp.concatenate([s[:, :, :, 0], s[:, :, :, 1]], axis=-1)


def _layer(x2d, n_batch, seq, P, l, mod_row0, rows_per_mod, rope128, past_kv, s0, carry):
    kv_all, s_all = (None, None) if carry is None else (carry[:2], carry[2])
    rkv, gates, rest, kh = _in_projection(x2d, P, l, mod_row0, rows_per_mod, seq)
    s0_pairs = None if s0 is None else _pairs_from_state(s0)
    y0, y1, bo0, bo1, s_all = _wkv(rkv, rest, kh, s0_pairs, n_batch, seq, P, l, s_all)
    o_a, k_all, v_all = _attention(rest, n_batch, seq, P, l, rope128, past_kv, kv_all)
    x1, h2 = _post(y0, y1, bo0, bo1, rest, gates, o_a, x2d, P, l, mod_row0, rows_per_mod)
    x2 = _mlp(h2, x1, P, l, mod_row0, rows_per_mod)
    return x2, (k_all, v_all, s_all)


def kernel(x_prompt, x_sample, cache_k, cache_v, state_wkv, c, c_ctx, w_in, w_br, w_out, w_mod, b_mod, norm_g, mlp_up, mlp_down, rwkv_mu, rwkv_k_k, rwkv_k_a, rwkv_r_k, decay_w0, decay_up, iclr_a0, iclr_up, gate_up, gn_w, gn_b, q_gain, k_gain):
    n_ctx, seq_ctx, _ = x_prompt.shape
    n_dec, seq_dec, _ = x_sample.shape
    past = cache_k.shape[2]

    cvec8 = jnp.zeros((SUBLANES, D_MODEL), F32).at[0].set(c_ctx).at[1:1 + n_dec].set(c)
    mod = _modulation(cvec8, w_mod, b_mod)
    mod = jnp.pad(mod.reshape(DEPTH, SUBLANES, N_MOD, D_MODEL), ((0, 0), (0, 0), (0, SUBLANES - N_MOD), (0, 0)))

    o = np.cumsum((0, RWKV_WIDTH, RWKV_WIDTH, RWKV_WIDTH, DECAY_RANK, ICLR_RANK, GATE_RANK,
                   ATTN_WIDTH, KV_WIDTH, KV_WIDTH, 2 * D_MODEL))
    w_rest = jnp.concatenate([w_in[:, :, o[9]:o[10]], w_in[:, :, o[6]:o[7]], w_in[:, :, o[3]:o[6]],
                              w_in[:, :, o[7]:o[9]]], axis=-1)
    zpad = jnp.zeros((DEPTH, 2, DECAY_RANK, RWKV_WIDTH), F32)
    row = lambda a: a.reshape(DEPTH, 1, -1)
    tile2 = lambda a: jnp.tile(a, (1, LANES // HEAD_DIM)).reshape(DEPTH, 1, LANES)
    P = dict(
        mod=mod, ng=norm_g, w_rkv=w_in[:, :, o[0]:o[3]].astype(BF16), w_rest=w_rest.astype(BF16),
        mu=rwkv_mu, k_k=row(rwkv_k_k), k_a=row(rwkv_k_a), r_k=row(rwkv_r_k),
        w0=decay_w0, wup=jnp.concatenate([decay_up, zpad], axis=2).astype(BF16),
        a0=iclr_a0, aup=jnp.concatenate([zpad, iclr_up], axis=2).astype(BF16),
        gup=gate_up.astype(BF16), gn_w=row(gn_w), gn_b=row(gn_b),
        w_br=w_br.astype(BF16), w_out=w_out.astype(BF16), up=mlp_up.astype(BF16), down=mlp_down.astype(BF16),
        q_gain=tile2(q_gain), k_gain=tile2(k_gain),
    )

    x = x_prompt.reshape(n_ctx * seq_ctx, D_MODEL)
    carry = None
    for l in range(DEPTH):
        x, carry = _layer(x, n_ctx, seq_ctx, P, l, 0, 0, None, None, None, carry)
    y_prompt = x.reshape(n_ctx, seq_ctx, D_MODEL)
    new_k, new_v, new_s = carry
    cache_shape = (n_ctx, DEPTH, seq_ctx, N_KV_HEADS, HEAD_DIM)

    rope128 = _rope_tables(seq_dec)
    past_kv = (cache_k.reshape(n_dec, DEPTH, past, KV_WIDTH), cache_v.reshape(n_dec, DEPTH, past, KV_WIDTH))
    x = x_sample.reshape(n_dec * seq_dec, D_MODEL)
    for l in range(DEPTH):
        x, _ = _layer(x, n_dec, seq_dec, P, l, 1, seq_dec, rope128, past_kv, state_wkv[:, l], None)
    y_sample = x.reshape(n_dec, seq_dec, D_MODEL)

    return (y_prompt, y_sample, new_k.reshape(cache_shape), new_v.reshape(cache_shape), new_s)
```

```python
import functools

import numpy as np
import jax
import jax.numpy as jnp
from jax import lax
from jax.experimental import pallas as pl
from jax.experimental.pallas import tpu as pltpu

F32 = jnp.float32
BF16 = jnp.bfloat16

D_MODEL = 1024
DEPTH = 2
GRID_W = 64
HEAD_DIM = 64
N_RWKV_HEADS = 8
RWKV_WIDTH = N_RWKV_HEADS * HEAD_DIM
N_Q_HEADS = 8
N_KV_HEADS = 2
ATTN_WIDTH = N_Q_HEADS * HEAD_DIM
KV_WIDTH = N_KV_HEADS * HEAD_DIM
DECAY_RANK = 64
ICLR_RANK = 64
GATE_RANK = 128
D_FF = 4 * D_MODEL
ROPE_THETA = 10000.0
ROPE_FREQS = HEAD_DIM // 4
N_MOD = 6
NORM_EPS = 1e-6
GN_EPS = 64e-5
DECAY_SCALE = 0.606531
D_IN = 3 * RWKV_WIDTH + DECAY_RANK + ICLR_RANK + GATE_RANK + ATTN_WIDTH + 2 * KV_WIDTH + 2 * D_MODEL

LANES = 128
SUBLANES = 8
VMEM_LIMIT_BYTES = 56 * 1024 * 1024

RKV_W = 3 * RWKV_WIDTH
GATES_W = 2 * D_MODEL
COL_Q = 0
COL_LORA = COL_Q + ATTN_WIDTH
COL_KV = COL_LORA + 2 * LANES
REST_W = COL_KV + 2 * KV_WIDTH

WKV_CHUNK = 64
PAIR = 2 * HEAD_DIM
N_PAIRS = N_RWKV_HEADS // 2

NN = (((1,), (0,)), ((), ()))
NT = (((1,), (1,)), ((), ()))
TN = (((0,), (0,)), ((), ()))


def _mm(a, b, dims=NN):
    return lax.dot_general(a.astype(BF16), b.astype(BF16), dims, preferred_element_type=F32)


def _split2(x):
    hi = x.astype(BF16)
    lo = (x - hi.astype(F32)).astype(BF16)
    return hi, lo


def _mm_exact_rhs(x, m_bf16):
    hi, lo = _split2(x)
    d = lambda p: lax.dot_general(p, m_bf16, NN, preferred_element_type=F32)
    return d(hi) + d(lo)


def _div_pow2(i, n):
    return lax.shift_right_logical(i, int(np.log2(n)))


def _mod_pow2(i, n):
    return lax.bitwise_and(i, n - 1)


def _group_ones():
    i = lax.broadcasted_iota(jnp.int32, (LANES, LANES), 0)
    j = lax.broadcasted_iota(jnp.int32, (LANES, LANES), 1)
    return jnp.where(_div_pow2(i, HEAD_DIM) == _div_pow2(j, HEAD_DIM), 1.0, 0.0).astype(BF16)


def _group_sum(x, ones_bd):
    blocks = [
        _mm_exact_rhs(x[:, i * LANES:(i + 1) * LANES], ones_bd)
        for i in range(x.shape[1] // LANES)
    ]
    return blocks[0] if len(blocks) == 1 else jnp.concatenate(blocks, axis=1)


def _lspec(l, shape, single=False):
    kw = dict(pipeline_mode=pl.Buffered(1)) if single else {}
    return pl.BlockSpec((None,) + tuple(shape), lambda *g: (l,) + (0,) * len(shape), **kw)


def _mod_spec(l, row0, rows_per_mod, tm):
    if rows_per_mod:
        return pl.BlockSpec((None, 1, SUBLANES, D_MODEL), lambda i, *g: (l, row0 + i * tm // rows_per_mod, 0, 0))
    return pl.BlockSpec((None, 1, SUBLANES, D_MODEL), lambda *g: (l, row0, 0, 0))


def _rms(x, g):
    return x * lax.rsqrt(jnp.mean(x * x, axis=-1, keepdims=True) + NORM_EPS) * g


def _mod_kernel(c_ref, w_ref, b_ref, o_ref):
    c = c_ref[...]
    s = c * jax.nn.sigmoid(c)
    o_ref[0] = _mm(s, w_ref[0]) + b_ref[0]


def _modulation(cvec8, w_mod, b_mod):
    tn = D_MODEL
    n = N_MOD * D_MODEL
    return pl.pallas_call(
        _mod_kernel,
        grid=(DEPTH, n // tn),
        in_specs=[
            pl.BlockSpec((SUBLANES, D_MODEL), lambda l, j: (0, 0)),
            pl.BlockSpec((1, D_MODEL, tn), lambda l, j: (l, 0, j)),
            pl.BlockSpec((1, 1, tn), lambda l, j: (l, 0, j)),
        ],
        out_specs=pl.BlockSpec((1, SUBLANES, tn), lambda l, j: (l, 0, j)),
        out_shape=jax.ShapeDtypeStruct((DEPTH, SUBLANES, n), F32),
        name="modulation",
    )(cvec8, w_mod, b_mod.reshape(DEPTH, 1, n))


def _shift_mix(x, mu):
    n = x.shape[0]
    row = lax.broadcasted_iota(jnp.int32, x.shape, 0)
    x_prev = jnp.where(row == 0, 0.0, pltpu.roll(x, 1, 0))
    x_next = jnp.where(row == n - 1, 0.0, pltpu.roll(x, n - 1, 0))
    return x * (1.0 - mu) + (0.5 * mu) * (x_prev + x_next)


def _inproj_kernel(x_ref, mod_ref, ng_ref, wrkv_ref, wrest_ref, mu_ref, kk_ref,
                   rkv_ref, gates_ref, rest_ref, kh_ref, h_scr, *, tm, seq):
    j = pl.program_id(1)
    C = RWKV_WIDTH

    def mix_in_place(i, piece):
        lanes = slice(piece * LANES, (piece + 1) * LANES)
        cols = slice(i * C + piece * LANES, i * C + (piece + 1) * LANES)
        for s in range(tm // seq):
            rows = slice(s * seq, (s + 1) * seq)
            mixed = _shift_mix(rkv_ref[rows, cols], mu_ref[i:i + 1, lanes])
            rkv_ref[rows, cols] = mixed
            if i == 1:
                kk = mixed * kk_ref[:, lanes]
                kh_ref[rows, lanes] = kk * lax.rsqrt(_group_sum(kk * kk, _group_ones()) + 1e-12)

    @pl.when(j == 0)
    def _():
        sh = mod_ref[0, 0:1, :]
        sc = mod_ref[0, 1:2, :]
        ng = ng_ref[0:1, :]
        sub = 256
        for i in range(tm // sub):
            x = x_ref[i * sub:(i + 1) * sub, :]
            h_scr[i * sub:(i + 1) * sub, :] = (_rms(x, ng) * (1.0 + sc) + sh).astype(BF16)
        rkv_ref[...] = lax.dot_general(h_scr[...], wrkv_ref[...], NN, preferred_element_type=F32)

    n_piece = C // LANES
    tn = wrest_ref.shape[1] // n_piece
    for jj in range(1, 4):
        @pl.when(j == jj)
        def _(jj=jj):
            for piece in range(n_piece):
                cols = slice(piece * tn, (piece + 1) * tn)
                out = lax.dot_general(h_scr[...], wrest_ref[:, cols], NN, preferred_element_type=F32)
                if jj < 3:
                    gates_ref[:, cols] = out.astype(BF16)
                else:
                    rest_ref[:, cols] = out
                mix_in_place(jj - 1, piece)


def _in_projection(x2d, P, l, mod_row0, rows_per_mod, seq):
    n_tok = x2d.shape[0]
    tm = 1024
    tn = D_MODEL
    assert tm % seq == 0 and GATES_W == 2 * tn and REST_W == tn
    return pl.pallas_call(
        functools.partial(_inproj_kernel, tm=tm, seq=seq),
        grid=(n_tok // tm, 4),
        in_specs=[
            pl.BlockSpec((tm, D_MODEL), lambda i, j: (i, 0)),
            _mod_spec(l, mod_row0, rows_per_mod, tm),
            _lspec(l, (4, D_MODEL)),
            _lspec(l, (D_MODEL, RKV_W), single=True),
            pl.BlockSpec((None, D_MODEL, tn), lambda i, j: (l, 0, jnp.maximum(j - 1, 0))),
            _lspec(l, (3, RWKV_WIDTH)),
            _lspec(l, (1, RWKV_WIDTH)),
        ],
        out_specs=[pl.BlockSpec((tm, RKV_W), lambda i, j: (i, 0)),
                   pl.BlockSpec((tm, tn), lambda i, j: (i, jnp.clip(j - 1, 0, 1))),
                   pl.BlockSpec((tm, tn), lambda i, j: (i, 0)),
                   pl.BlockSpec((tm, RWKV_WIDTH), lambda i, j: (i, 0))],
        out_shape=[jax.ShapeDtypeStruct((n_tok, RKV_W), F32),
                   jax.ShapeDtypeStruct((n_tok, GATES_W), BF16),
                   jax.ShapeDtypeStruct((n_tok, REST_W), F32),
                   jax.ShapeDtypeStruct((n_tok, RWKV_WIDTH), F32)],
        scratch_shapes=[pltpu.VMEM((tm, D_MODEL), BF16)],
        compiler_params=pltpu.CompilerParams(
            dimension_semantics=("parallel", "arbitrary"), vmem_limit_bytes=VMEM_LIMIT_BYTES),
        name="in_projection",
    )(x2d, P['mod'], P['ng'], P['w_rkv'], P['w_rest'], P['mu'], P['k_k'])


def _wkv_masks(reverse):
    L = WKV_CHUNK
    row = lax.broadcasted_iota(jnp.int32, (L, PAIR), 0)
    col = _mod_pow2(lax.broadcasted_iota(jnp.int32, (L, PAIR), 1), L)
    return dict(
        strict=(col > row) if reverse else (col < row),
        incl=(col >= row) if reverse else (col <= row),
        blk8=_div_pow2(row, 8) == _div_pow2(col, 8),
        eye=row == col,
    )


def _bd(x, left):
    x = x.astype(BF16)
    z = jnp.zeros_like(x)
    return jnp.concatenate([jnp.where(left, x, z), jnp.where(left, z, x)], axis=0)


def _wkv_operand_scratch(nb):
    C, L = RWKV_WIDTH, WKV_CHUNK
    full = lambda dt: pltpu.VMEM((nb, 2, L, C), dt)
    pair = lambda rows: pltpu.VMEM((nb, 2, N_PAIRS, rows, PAIR), BF16)
    return dict(khd=full(F32), rd=full(F32), bonus=full(F32), etot=pltpu.VMEM((nb, 2, SUBLANES, C), F32),
                be=full(BF16), kte=full(BF16), vb=full(BF16),
                lhsA=pair(2 * L), rhsA=pair(4 * L), vbd=pair(2 * L), khdbd=pair(2 * L))


def _wkv_prepare(e, d, rkv, khr, lor, ka_ref, rk_ref, w0_ref, wup_ref, a0_ref, aup_ref, ops, ones_bd, tri, left):
    C, L = RWKV_WIDTH, WKV_CHUNK
    reverse = d == 1
    r = rkv[e, 0, :, 0:C]
    k = rkv[e, 0, :, C:2 * C]
    v = rkv[e, 0, :, 2 * C:3 * C]
    kh = khr[e, 0]
    lo_dec = lor[e, 0, :, 0:LANES]
    w_raw = _mm(jnp.tanh(lo_dec), wup_ref[d]) + w0_ref[d:d + 1, :]
    lw = -DECAY_SCALE * jax.nn.sigmoid(w_raw)
    a = jax.nn.sigmoid(_mm(lo_dec, aup_ref[d]) + a0_ref[d:d + 1, :])
    kt = k * (1.0 + (a - 1.0) * ka_ref[...])
    ops['bonus'][e, d] = _group_sum(r * kt * rk_ref[...], ones_bd) * v
    b = a * kh

    hi, lo2 = _split2(lw)
    cum = (lax.dot_general(tri[d], hi, NN, preferred_element_type=F32)
           + lax.dot_general(tri[d], lo2, NN, preferred_element_type=F32))
    tot = cum[0:1, :] if reverse else cum[L - 1:L, :]
    e_inv = jnp.exp(-cum)
    e_end = jnp.exp(tot - cum)
    khd = kh * jnp.exp(cum - lw)
    rd = r * jnp.exp(cum)
    bi = b * e_inv
    ki = kt * e_inv
    ops['khd'][e, d] = khd
    ops['rd'][e, d] = rd
    ops['etot'][e, d] = jnp.broadcast_to(jnp.exp(tot), (SUBLANES, C))
    ops['be'][e, d] = (b * e_end).astype(BF16)
    ops['kte'][e, d] = (kt * e_end).astype(BF16)
    ops['vb'][e, d] = v.astype(BF16)
    for p in range(N_PAIRS):
        sl = slice(p * PAIR, (p + 1) * PAIR)
        ops['lhsA'][e, d, p] = jnp.concatenate([khd[:, sl], rd[:, sl]], axis=0).astype(BF16)
        ops['rhsA'][e, d, p] = jnp.concatenate([_bd(bi[:, sl], left), _bd(ki[:, sl], left)], axis=0)
        ops['vbd'][e, d, p] = _bd(v[:, sl], left)
        ops['khdbd'][e, d, p] = _bd(khd[:, sl], left)


def _wkv_chains(chains, left, fillers):
    L = WKV_CHUNK
    i2 = lax.broadcasted_iota(jnp.int32, (PAIR, PAIR), 0)
    j2 = lax.broadcasted_iota(jnp.int32, (PAIR, PAIR), 1)
    same = _div_pow2(i2, HEAD_DIM) == _div_pow2(j2, HEAD_DIM)
    eye2 = i2 == j2
    bd = lambda x: _bd(x, left)
    lp = lambda g: jnp.where(left, g[:L], g[L:])
    n_stages = 13
    due = {((k + 1) * n_stages) // (len(fillers) + 1): f for k, f in enumerate(fillers)}
    emitted = [0]

    def each(f):
        for ch in chains:
            f(ch)
        emitted[0] += 1
        if emitted[0] in due:
            due[emitted[0]]()

    def a_blocks(ch):
        m = ch['masks']
        A = lax.dot_general(ch['lhsA'], ch['rhsA'], NT, preferred_element_type=F32)
        ch['A_ub'] = jnp.where(m['strict'], A[:L, :PAIR], 0.0)
        ch['A_uvrv'] = jnp.concatenate([jnp.where(m['strict'], A[:L, PAIR:], 0.0),
                                        jnp.where(m['incl'], A[L:, PAIR:], 0.0)], axis=0).astype(BF16)
        ch['A_rb'] = jnp.where(m['incl'], A[L:, :PAIR], 0.0).astype(BF16)
        a8 = jnp.where(m['blk8'], ch['A_ub'], 0.0)
        ch['A8'] = a8
        ch['X0'] = jnp.where(m['eye'], 1.0, -a8)
        ch['Nlow'] = ch['A_ub'] - a8
    each(a_blocks)

    stack = lambda a, b: jnp.concatenate([a.astype(BF16), b.astype(BF16)], axis=0)

    def sq1(ch):
        ch['A8_2'] = _mm(ch['A8'], bd(ch['A8']))
    each(sq1)

    def sq2(ch):
        r = _mm(stack(ch['A8_2'], ch['X0']), bd(ch['A8_2']))
        ch['A8_4'] = r[:L]
        ch['X1'] = ch['X0'] + r[L:]
        ch['AV'] = _mm(ch['A_uvrv'], ch['vbd'])
    each(sq2)

    def dinv(ch):
        ch['T8'] = ch['X1'] + _mm(ch['X1'], bd(ch['A8_4']))
    each(dinv)

    def e1(ch):
        ch['E'] = _mm(ch['T8'], bd(ch['Nlow']))
    each(e1)

    def e2(ch):
        ch['E2'] = _mm(ch['E'], bd(ch['E']))
    each(e2)

    def e3(ch):
        ime = jnp.where(ch['masks']['eye'], 1.0, 0.0) - ch['E']
        r = _mm(stack(ch['E2'], ime), bd(ch['E2']))
        ch['E4'] = r[:L]
        ch['Y1'] = ime + r[L:]
    each(e3)

    def e4(ch):
        ch['Y2'] = ch['Y1'] + _mm(ch['Y1'], bd(ch['E4']))
    each(e4)

    def t_stage(ch):
        ch['T'] = _mm(ch['Y2'], bd(ch['T8']))
    each(t_stage)

    def p_stage(ch):
        ch['P'] = _mm(ch['A_rb'], bd(ch['T']))
    each(p_stage)

    def q_stage(ch):
        rhs = jnp.concatenate([ch['khdbd'], bd(ch['AV'][:L])], axis=1)
        r = _mm(stack(ch['T'], ch['P']), rhs)
        ch['Q'] = r[:L].astype(BF16)
        ch['G1'] = ch['rd'] - r[L:, :PAIR]
        ch['G2'] = ch['AV'][L:] - r[L:, PAIR:]
    each(q_stage)

    def mn_stage(ch):
        QB = _mm(ch['Q'], ch['be'], TN)
        ch['M'] = (jnp.where(eye2, ch['etot'], 0.0) - jnp.where(same, QB[:PAIR], 0.0)).astype(BF16)
        ch['N'] = lp(_mm(ch['vb'], ch['kte'], TN)) - lp(QB[PAIR:])
    each(mn_stage)

    def out_stage(ch):
        ch['y'] = _mm(ch['G1'], bd(ch['S']), NT) + ch['G2']
        ch['Sn'] = _mm(ch['S'], ch['M']) + ch['N']
    each(out_stage)


def _wkv_kernel(rkv0, khr0, lor0, rkv1, khr1, lor1,
                ka_ref, rk_ref, w0_ref, wup_ref, a0_ref, aup_ref, s0_ref,
                *rest, zero_state, nb, nc, op_names):
    n_ops = len(op_names)
    y0_ref, y1_ref, bo0_ref, bo1_ref, sfin_ref, s_scr = rest[-(6 + 2 * n_ops):len(rest) - 2 * n_ops]
    op_sets = [dict(zip(op_names, rest[len(rest) - (2 - i) * n_ops:len(rest) - (1 - i) * n_ops])) for i in (0, 1)]
    t = pl.program_id(0)
    c = lax.rem(jnp.maximum(t - 1, 0), nc)

    @pl.when(t == 0)
    def _():
        for ref in op_sets[1].values():
            ref[...] = jnp.zeros(ref.shape, ref.dtype)

    @pl.when(c == 0)
    def _():
        if zero_state:
            s_scr[...] = jnp.zeros(s_scr.shape, F32)
        else:
            s_scr[...] = s0_ref[...]

    def step(ops_in, ops_out):
        L = WKV_CHUNK
        left = lax.broadcasted_iota(jnp.int32, (L, PAIR), 1) < HEAD_DIM
        masks = [_wkv_masks(False), _wkv_masks(True)]

        chains = []
        for e in range(nb):
            for d, (y_ref, bo_ref) in enumerate(((y0_ref, bo0_ref), (y1_ref, bo1_ref))):
                bo_ref[e, 0] = ops_in['bonus'][e, d].astype(BF16)
                for p in range(N_PAIRS):
                    sl = slice(p * PAIR, (p + 1) * PAIR)
                    chains.append(dict(
                        e=e, d=d, p=p, sl=sl, y_ref=y_ref, masks=masks[d], S=s_scr[e, d, p],
                        lhsA=ops_in['lhsA'][e, d, p], rhsA=ops_in['rhsA'][e, d, p], vbd=ops_in['vbd'][e, d, p],
                        khdbd=ops_in['khdbd'][e, d, p], rd=ops_in['rd'][e, d, :, sl], be=ops_in['be'][e, d, :, sl],
                        kte=ops_in['kte'][e, d, :, sl], vb=ops_in['vb'][e, d, :, sl],
                        etot=ops_in['etot'][e, d, 0:1, sl]))
        ones_bd = _group_ones()
        ti = lax.broadcasted_iota(jnp.int32, (L, L), 0)
        tj = lax.broadcasted_iota(jnp.int32, (L, L), 1)
        tri = [jnp.where(tj <= ti, 1.0, 0.0).astype(BF16), jnp.where(tj >= ti, 1.0, 0.0).astype(BF16)]
        prepare = [
            functools.partial(_wkv_prepare, e, d, rkv, khr, lor, ka_ref, rk_ref, w0_ref, wup_ref, a0_ref, aup_ref,
                              ops_out, ones_bd, tri, left)
            for e in range(nb) for d, (rkv, khr, lor) in enumerate(((rkv0, khr0, lor0), (rkv1, khr1, lor1)))]
        _wkv_chains(chains, left, prepare)
        for ch in chains:
            ch['y_ref'][ch['e'], 0, :, ch['sl']] = ch['y'].astype(BF16)
            s_scr[ch['e'], ch['d'], ch['p']] = ch['Sn']

    parity = lax.rem(t, 2)

    @pl.when(parity == 0)
    def _():
        step(op_sets[1], op_sets[0])

    @pl.when(parity == 1)
    def _():
        step(op_sets[0], op_sets[1])

    @pl.when((c == nc - 1) & (t > 0))
    def _():
        for e in range(nb):
            for d in (0, 1):
                for p in range(N_PAIRS):
                    s2 = s_scr[e, d, p]
                    sfin_ref[e, d, 2 * p] = s2[:, :HEAD_DIM]
                    sfin_ref[e, d, 2 * p + 1] = pltpu.roll(s2, HEAD_DIM, 1)[:, :HEAD_DIM]


def _wkv(rkv, rest, kh, s0_pairs, n_batch, seq, P, l, s_all):
    L = WKV_CHUNK
    nc = seq // L
    nb = 2
    n_steps = (n_batch // nb) * nc
    n_tok = n_batch * seq
    zero_state = s0_pairs is None
    if zero_state:
        s0_pairs = jnp.zeros((nb, 2, N_PAIRS, HEAD_DIM, PAIR), F32)
    rkv4 = rkv.reshape(n_batch, nc, L, RKV_W)
    rest4 = rest.reshape(n_batch, nc, L, REST_W)
    kh4 = kh.reshape(n_batch, nc, L, RWKV_WIDTH)

    def in_pos(t, d):
        s = jnp.minimum(t, n_steps - 1)
        c = lax.rem(s, nc)
        return lax.div(s, nc), (c if d == 0 else nc - 1 - c)

    def out_pos(t, d):
        s = jnp.maximum(t - 1, 0)
        c = lax.rem(s, nc)
        return lax.div(s, nc), (c if d == 0 else nc - 1 - c)

    in_specs = []
    for d in (0, 1):
        in_specs += [
            pl.BlockSpec((nb, 1, L, RKV_W), lambda t, d=d: (*in_pos(t, d), 0, 0)),
            pl.BlockSpec((nb, 1, L, RWKV_WIDTH), lambda t, d=d: (*in_pos(t, d), 0, 0)),
            pl.BlockSpec((nb, 1, L, 2 * LANES), lambda t, d=d: (*in_pos(t, d), 0, COL_LORA // (2 * LANES))),
        ]
    state_spec = lambda idx: pl.BlockSpec((nb, 2, N_PAIRS, HEAD_DIM, PAIR), idx)
    in_specs += [
        _lspec(l, (1, RWKV_WIDTH)), _lspec(l, (1, RWKV_WIDTH)),
        _lspec(l, (2, RWKV_WIDTH)), _lspec(l, (2, LANES, RWKV_WIDTH)),
        _lspec(l, (2, RWKV_WIDTH)), _lspec(l, (2, LANES, RWKV_WIDTH)),
        state_spec((lambda t: (0, 0, 0, 0, 0)) if zero_state else (lambda t: (out_pos(t, 0)[0], 0, 0, 0, 0))),
    ]
    args = [rkv4, kh4, rest4, rkv4, kh4, rest4, P['k_a'], P['r_k'], P['w0'], P['wup'], P['a0'], P['aup'], s0_pairs]
    aliases = {}
    if s_all is not None:
        in_specs.append(pl.BlockSpec(memory_space=pl.ANY))
        aliases = {len(args): 4}
        args.append(s_all)
    tok_spec = lambda d: pl.BlockSpec((nb, 1, L, RWKV_WIDTH), lambda t, d=d: (*out_pos(t, d), 0, 0))
    sfin_spec = pl.BlockSpec((nb, None, 2, N_RWKV_HEADS, HEAD_DIM, HEAD_DIM),
                             lambda t: (out_pos(t, 0)[0], l, 0, 0, 0, 0))
    out_specs = [tok_spec(0), tok_spec(1), tok_spec(0), tok_spec(1), sfin_spec]
    tok = jax.ShapeDtypeStruct((n_batch, nc, L, RWKV_WIDTH), BF16)
    op_scratch = _wkv_operand_scratch(nb)
    outs = pl.pallas_call(
        functools.partial(_wkv_kernel, zero_state=zero_state, nb=nb, nc=nc, op_names=tuple(op_scratch)),
        grid=(n_steps + 1,),
        in_specs=in_specs,
        out_specs=out_specs,
        out_shape=[tok, tok, tok, tok,
                   jax.ShapeDtypeStruct((n_batch, DEPTH, 2, N_RWKV_HEADS, HEAD_DIM, HEAD_DIM), F32)],
        input_output_aliases=aliases,
        scratch_shapes=([pltpu.VMEM((nb, 2, N_PAIRS, HEAD_DIM, PAIR), F32)]
                        + list(op_scratch.values()) + list(_wkv_operand_scratch(nb).values())),
        compiler_params=pltpu.CompilerParams(
            dimension_semantics=("arbitrary",), vmem_limit_bytes=VMEM_LIMIT_BYTES),
        name="wkv_scan",
    )(*args)
    y0, y1, bo0, bo1, s_fin = outs
    flat = lambda t: t.reshape(n_tok, RWKV_WIDTH)
    return flat(y0), flat(y1), flat(bo0), flat(bo1), s_fin


def _rope(x, cos, sin_signed):
    w = x.shape[1]
    lane = lax.broadcasted_iota(jnp.int32, x.shape, 1)
    partner = jnp.where(_mod_pow2(lane, 2 * ROPE_FREQS) < ROPE_FREQS,
                        pltpu.roll(x, w - ROPE_FREQS, 1), pltpu.roll(x, ROPE_FREQS, 1))
    return x * cos + partner * sin_signed


def _attn_kernel(*refs, tq, seq, past, use_rope):
    it = iter(refs)
    q_ref, kv_ref, qg_ref, kg_ref = next(it), next(it), next(it), next(it)
    if use_rope:
        cq_ref, sq_ref, ck_ref, sk_ref = next(it), next(it), next(it), next(it)
    if past:
        pk_ref, pv_ref = next(it), next(it)
    o_ref, ko_ref, vo_ref, kvar_scr, vvar_scr = refs[-5:]

    ones_bd = _group_ones()

    @pl.when(pl.program_id(1) == 0)
    def _():
        kv = kv_ref[...]
        k_raw = kv[:, :KV_WIDTH]
        kn = k_raw * lax.rsqrt(_group_sum(k_raw * k_raw, ones_bd) * (1.0 / HEAD_DIM) + NORM_EPS) * kg_ref[...]
        if use_rope:
            kn = _rope(kn, ck_ref[...], sk_ref[...])
        vn = kv[:, KV_WIDTH:]
        ko_ref[...] = kn
        vo_ref[...] = vn
        pieces = [(past, seq, kn, vn)]
        if past:
            pieces.append((0, past, pk_ref[...], pv_ref[...]))
        for start, n, kx, vx in pieces:
            left = lax.broadcasted_iota(jnp.int32, kx.shape, 1) < HEAD_DIM
            for x, scr in ((kx, kvar_scr), (vx, vvar_scr)):
                sw = pltpu.roll(x, HEAD_DIM, 1)
                scr[0, start:start + n, :] = jnp.where(left, x, 0.0).astype(BF16)
                scr[1, start:start + n, :] = jnp.where(left, 0.0, sw).astype(BF16)
                scr[2, start:start + n, :] = jnp.where(left, sw, 0.0).astype(BF16)
                scr[3, start:start + n, :] = jnp.where(left, 0.0, x).astype(BF16)

    scale = HEAD_DIM ** -0.5
    n_blk = ATTN_WIDTH // LANES
    qn = []
    for jb in range(n_blk):
        qb = q_ref[:, jb * LANES:(jb + 1) * LANES]
        x = qb * lax.rsqrt(_group_sum(qb * qb, ones_bd) * (1.0 / HEAD_DIM) + NORM_EPS) * qg_ref[...]
        if use_rope:
            x = _rope(x, cq_ref[...], sq_ref[...])
        qn.append((x * scale).astype(BF16))
    heads = [(jb, 2 * (jb // (n_blk // N_KV_HEADS)) + side) for jb in range(n_blk) for side in (0, 1)]
    scores = [lax.dot_general(qn[jb], kvar_scr[var], NT, preferred_element_type=F32) for jb, var in heads]
    exps = [jnp.exp(s - jnp.max(s, axis=-1, keepdims=True)) for s in scores]
    outs = [lax.dot_general(e.astype(BF16), vvar_scr[var], NN, preferred_element_type=F32)
            for e, (jb, var) in zip(exps, heads)]
    outs = [o / jnp.sum(e, axis=-1, keepdims=True) for o, e in zip(outs, exps)]
    for jb in range(n_blk):
        o_ref[:, jb * LANES:(jb + 1) * LANES] = (outs[2 * jb] + outs[2 * jb + 1]).astype(BF16)


def _attention(rest, n_batch, seq, P, l, rope128, past_kv, kv_all):
    tq = min(seq, 512)
    nq = seq // tq
    n_tok = n_batch * seq
    use_rope = rope128 is not None
    past = 0 if past_kv is None else past_kv[0].shape[2]
    in_specs = [
        pl.BlockSpec((tq, ATTN_WIDTH), lambda b, i: (b * nq + i, COL_Q // ATTN_WIDTH)),
        pl.BlockSpec((seq, 2 * KV_WIDTH), lambda b, i: (b, COL_KV // (2 * KV_WIDTH))),
        _lspec(l, (1, LANES)),
        _lspec(l, (1, LANES)),
    ]
    args = [rest, rest, P['q_gain'], P['k_gain']]
    if use_rope:
        cos, sin = rope128
        in_specs += [pl.BlockSpec((tq, LANES), lambda b, i: (i, 0)),
                     pl.BlockSpec((tq, LANES), lambda b, i: (i, 0)),
                     pl.BlockSpec((seq, LANES), lambda b, i: (0, 0)),
                     pl.BlockSpec((seq, LANES), lambda b, i: (0, 0))]
        args += [cos, sin, cos, sin]
    if past:
        in_specs += [pl.BlockSpec((None, None, past, KV_WIDTH), lambda b, i: (b, l, 0, 0)),
                     pl.BlockSpec((None, None, past, KV_WIDTH), lambda b, i: (b, l, 0, 0))]
        args += list(past_kv)
    aliases = {}
    if kv_all is not None:
        in_specs += [pl.BlockSpec(memory_space=pl.ANY), pl.BlockSpec(memory_space=pl.ANY)]
        aliases = {len(args): 1, len(args) + 1: 2}
        args += list(kv_all)
    cache_spec = pl.BlockSpec((None, None, seq, KV_WIDTH), lambda b, i: (b, l, 0, 0))
    out_specs = [pl.BlockSpec((tq, ATTN_WIDTH), lambda b, i: (b * nq + i, 0)), cache_spec, cache_spec]
    return pl.pallas_call(
        functools.partial(_attn_kernel, tq=tq, seq=seq, past=past, use_rope=use_rope),
        grid=(n_batch, nq),
        in_specs=in_specs,
        out_specs=out_specs,
        out_shape=[jax.ShapeDtypeStruct((n_tok, ATTN_WIDTH), BF16),
                   jax.ShapeDtypeStruct((n_batch, DEPTH, seq, KV_WIDTH), F32),
                   jax.ShapeDtypeStruct((n_batch, DEPTH, seq, KV_WIDTH), F32)],
        input_output_aliases=aliases,
        scratch_shapes=[pltpu.VMEM((2 * N_KV_HEADS, past + seq, KV_WIDTH), BF16),
                        pltpu.VMEM((2 * N_KV_HEADS, past + seq, KV_WIDTH), BF16)],
        compiler_params=pltpu.CompilerParams(
            dimension_semantics=("parallel", "arbitrary"), vmem_limit_bytes=VMEM_LIMIT_BYTES),
        name="attention",
    )(*args)


def _post_kernel(y0_ref, y1_ref, bo0_ref, bo1_ref, lor_ref, oa_ref, gr_ref, ga_ref, x_ref, mod_ref,
                 gup_ref, gnw_ref, gnb_ref, wbr_ref, wout_ref, ng_ref, x1_ref, h2_ref):
    ones_bd = _group_ones()
    y = y0_ref[...].astype(F32) + y1_ref[...].astype(F32)
    mean = _group_sum(y, ones_bd) * (1.0 / HEAD_DIM)
    yc = y - mean
    var = _group_sum(yc * yc, ones_bd) * (1.0 / HEAD_DIM)
    yn = yc * lax.rsqrt(var + GN_EPS) * gnw_ref[...] + gnb_ref[...]
    lg = lor_ref[:, LANES:]
    g = _mm(jax.nn.sigmoid(lg), gup_ref[...])
    o_r = (yn + (bo0_ref[...].astype(F32) + bo1_ref[...].astype(F32))) * g
    merged = (jax.nn.sigmoid(gr_ref[...].astype(F32)) * _mm(o_r, wbr_ref[0])
              + jax.nn.sigmoid(ga_ref[...].astype(F32)) * _mm(oa_ref[...], wbr_ref[1]))
    m = _mm(merged, wout_ref[...])
    g1 = mod_ref[0, 2:3, :]
    sh2 = mod_ref[0, 3:4, :]
    sc2 = mod_ref[0, 4:5, :]
    x1 = x_ref[...] + g1 * _rms(m, ng_ref[1:2, :])
    x1_ref[...] = x1
    h2_ref[...] = (_rms(x1, ng_ref[2:3, :]) * (1.0 + sc2) + sh2).astype(BF16)


def _post(y0, y1, bo0, bo1, rest, gates, o_a, x2d, P, l, mod_row0, rows_per_mod):
    n_tok = x2d.shape[0]
    tm = 512
    tok = lambda w: pl.BlockSpec((tm, w), lambda i: (i, 0))
    return pl.pallas_call(
        _post_kernel,
        grid=(n_tok // tm,),
        in_specs=[
            tok(RWKV_WIDTH), tok(RWKV_WIDTH), tok(RWKV_WIDTH), tok(RWKV_WIDTH),
            pl.BlockSpec((tm, 2 * LANES), lambda i: (i, COL_LORA // (2 * LANES))),
            tok(ATTN_WIDTH),
            pl.BlockSpec((tm, D_MODEL), lambda i: (i, 0)),
            pl.BlockSpec((tm, D_MODEL), lambda i: (i, 1)),
            tok(D_MODEL),
            _mod_spec(l, mod_row0, rows_per_mod, tm),
            _lspec(l, (GATE_RANK, RWKV_WIDTH)), _lspec(l, (1, RWKV_WIDTH)), _lspec(l, (1, RWKV_WIDTH)),
            _lspec(l, (2, RWKV_WIDTH, D_MODEL)), _lspec(l, (D_MODEL, D_MODEL)), _lspec(l, (4, D_MODEL)),
        ],
        out_specs=[tok(D_MODEL), tok(D_MODEL)],
        out_shape=[jax.ShapeDtypeStruct((n_tok, D_MODEL), F32), jax.ShapeDtypeStruct((n_tok, D_MODEL), BF16)],
        compiler_params=pltpu.CompilerParams(
            dimension_semantics=("parallel",), vmem_limit_bytes=VMEM_LIMIT_BYTES),
        name="merge_outproj",
    )(y0, y1, bo0, bo1, rest, o_a, gates, gates, x2d, P['mod'], P['gup'], P['gn_w'], P['gn_b'], P['w_br'], P['w_out'],
      P['ng'])


def _mlp_kernel(h_ref, x1_ref, mod_ref, up_ref, down_ref, ng_ref, o_ref):
    h = h_ref[...]
    f = None
    ff = D_MODEL
    for j in range(D_FF // ff):
        u = lax.dot_general(h, up_ref[:, j * ff:(j + 1) * ff], NN, preferred_element_type=F32)
        u = jnp.square(jnp.maximum(u, 0.0)).astype(BF16)
        part = lax.dot_general(u, down_ref[j * ff:(j + 1) * ff, :], NN, preferred_element_type=F32)
        f = part if f is None else f + part
    g2 = mod_ref[0, 5:6, :]
    o_ref[...] = x1_ref[...] + g2 * _rms(f, ng_ref[3:4, :])


def _mlp(h2, x1, P, l, mod_row0, rows_per_mod):
    n_tok = x1.shape[0]
    tm = 512
    tok = lambda: pl.BlockSpec((tm, D_MODEL), lambda i: (i, 0))
    return pl.pallas_call(
        _mlp_kernel,
        grid=(n_tok // tm,),
        in_specs=[
            tok(), tok(),
            _mod_spec(l, mod_row0, rows_per_mod, tm),
            _lspec(l, (D_MODEL, D_FF), single=True),
            _lspec(l, (D_FF, D_MODEL), single=True),
            _lspec(l, (4, D_MODEL)),
        ],
        out_specs=tok(),
        out_shape=jax.ShapeDtypeStruct((n_tok, D_MODEL), F32),
        compiler_params=pltpu.CompilerParams(
            dimension_semantics=("parallel",), vmem_limit_bytes=VMEM_LIMIT_BYTES),
        name="mlp",
    )(h2, x1, P['mod'], P['up'], P['down'], P['ng'])


def _rope_tables(seq):
    n_rows = seq // GRID_W
    rows = np.repeat(np.arange(n_rows, dtype=np.float32), GRID_W)
    cols = np.tile(np.arange(GRID_W, dtype=np.float32), n_rows)
    half = HEAD_DIM // 2
    freqs = 1.0 / (jnp.asarray(ROPE_THETA, F32) ** (jnp.arange(0, half, 2, dtype=F32) / half))
    ang_r = jnp.asarray(rows)[:, None] * freqs
    ang_c = jnp.asarray(cols)[:, None] * freqs
    cr, sr, cc, sc = jnp.cos(ang_r), jnp.sin(ang_r), jnp.cos(ang_c), jnp.sin(ang_c)
    cos64 = jnp.concatenate([cr, cr, cc, cc], axis=1)
    sin64 = jnp.concatenate([-sr, sr, -sc, sc], axis=1)
    return jnp.tile(cos64, (1, LANES // HEAD_DIM)), jnp.tile(sin64, (1, LANES // HEAD_DIM))


def _pairs_from_state(s):
    b = s.shape[0]
    s = s.reshape(b, 2, N_PAIRS, 2, HEAD_DIM, HEAD_DIM)
    return jnp.concatenate([s[:, :, :, 0], s[:, :, :, 1]], axis=-1)


def _layer(x2d, n_batch, seq, P, l, mod_row0, rows_per_mod, rope128, past_kv, s0, carry):
    kv_all, s_all = (None, None) if carry is None else (carry[:2], carry[2])
    rkv, gates, rest, kh = _in_projection(x2d, P, l, mod_row0, rows_per_mod, seq)
    s0_pairs = None if s0 is None else _pairs_from_state(s0)
    y0, y1, bo0, bo1, s_all = _wkv(rkv, rest, kh, s0_pairs, n_batch, seq, P, l, s_all)
    o_a, k_all, v_all = _attention(rest, n_batch, seq, P, l, rope128, past_kv, kv_all)
    x1, h2 = _post(y0, y1, bo0, bo1, rest, gates, o_a, x2d, P, l, mod_row0, rows_per_mod)
    x2 = _mlp(h2, x1, P, l, mod_row0, rows_per_mod)
    return x2, (k_all, v_all, s_all)


def kernel(x_prompt, x_sample, cache_k, cache_v, state_wkv, c, c_ctx, w_in, w_br, w_out, w_mod, b_mod, norm_g, mlp_up, mlp_down, rwkv_mu, rwkv_k_k, rwkv_k_a, rwkv_r_k, decay_w0, decay_up, iclr_a0, iclr_up, gate_up, gn_w, gn_b, q_gain, k_gain):
    n_ctx, seq_ctx, _ = x_prompt.shape
    n_dec, seq_dec, _ = x_sample.shape
    past = cache_k.shape[2]

    cvec8 = jnp.zeros((SUBLANES, D_MODEL), F32).at[0].set(c_ctx).at[1:1 + n_dec].set(c)
    mod = _modulation(cvec8, w_mod, b_mod)
    mod = jnp.pad(mod.reshape(DEPTH, SUBLANES, N_MOD, D_MODEL), ((0, 0), (0, 0), (0, SUBLANES - N_MOD), (0, 0)))

    o = np.cumsum((0, RWKV_WIDTH, RWKV_WIDTH, RWKV_WIDTH, DECAY_RANK, ICLR_RANK, GATE_RANK,
                   ATTN_WIDTH, KV_WIDTH, KV_WIDTH, 2 * D_MODEL))
    w_rest = jnp.concatenate([w_in[:, :, o[9]:o[10]], w_in[:, :, o[6]:o[7]], w_in[:, :, o[3]:o[6]],
                              w_in[:, :, o[7]:o[9]]], axis=-1)
    zpad = jnp.zeros((DEPTH, 2, DECAY_RANK, RWKV_WIDTH), F32)
    row = lambda a: a.reshape(DEPTH, 1, -1)
    tile2 = lambda a: jnp.tile(a, (1, LANES // HEAD_DIM)).reshape(DEPTH, 1, LANES)
    P = dict(
        mod=mod, ng=norm_g, w_rkv=w_in[:, :, o[0]:o[3]].astype(BF16), w_rest=w_rest.astype(BF16),
        mu=rwkv_mu, k_k=row(rwkv_k_k), k_a=row(rwkv_k_a), r_k=row(rwkv_r_k),
        w0=decay_w0, wup=jnp.concatenate([decay_up, zpad], axis=2).astype(BF16),
        a0=iclr_a0, aup=jnp.concatenate([zpad, iclr_up], axis=2).astype(BF16),
        gup=gate_up.astype(BF16), gn_w=row(gn_w), gn_b=row(gn_b),
        w_br=w_br.astype(BF16), w_out=w_out.astype(BF16), up=mlp_up.astype(BF16), down=mlp_down.astype(BF16),
        q_gain=tile2(q_gain), k_gain=tile2(k_gain),
    )

    x = x_prompt.reshape(n_ctx * seq_ctx, D_MODEL)
    carry = None
    for l in range(DEPTH):
        x, carry = _layer(x, n_ctx, seq_ctx, P, l, 0, 0, None, None, None, carry)
    y_prompt = x.reshape(n_ctx, seq_ctx, D_MODEL)
    new_k, new_v, new_s = carry
    cache_shape = (n_ctx, DEPTH, seq_ctx, N_KV_HEADS, HEAD_DIM)

    rope128 = _rope_tables(seq_dec)
    past_kv = (cache_k.reshape(n_dec, DEPTH, past, KV_WIDTH), cache_v.reshape(n_dec, DEPTH, past, KV_WIDTH))
    x = x_sample.reshape(n_dec * seq_dec, D_MODEL)
    for l in range(DEPTH):
        x, _ = _layer(x, n_dec, seq_dec, P, l, 1, seq_dec, rope128, past_kv, state_wkv[:, l], None)
    y_sample = x.reshape(n_dec, seq_dec, D_MODEL)

    return (y_prompt, y_sample, new_k.reshape(cache_shape), new_v.reshape(cache_shape), new_s)
```

```python
import functools

import numpy as np
import jax
import jax.numpy as jnp
from jax import lax
from jax.experimental import pallas as pl
from jax.experimental.pallas import tpu as pltpu

F32 = jnp.float32
BF16 = jnp.bfloat16

D_MODEL = 1024
DEPTH = 2
GRID_W = 64
HEAD_DIM = 64
N_RWKV_HEADS = 8
RWKV_WIDTH = N_RWKV_HEADS * HEAD_DIM
N_Q_HEADS = 8
N_KV_HEADS = 2
ATTN_WIDTH = N_Q_HEADS * HEAD_DIM
KV_WIDTH = N_KV_HEADS * HEAD_DIM
DECAY_RANK = 64
ICLR_RANK = 64
GATE_RANK = 128
D_FF = 4 * D_MODEL
ROPE_THETA = 10000.0
ROPE_FREQS = HEAD_DIM // 4
N_MOD = 6
NORM_EPS = 1e-6
GN_EPS = 64e-5
DECAY_SCALE = 0.606531
D_IN = 3 * RWKV_WIDTH + DECAY_RANK + ICLR_RANK + GATE_RANK + ATTN_WIDTH + 2 * KV_WIDTH + 2 * D_MODEL

LANES = 128
SUBLANES = 8
VMEM_LIMIT_BYTES = 56 * 1024 * 1024

RKV_W = 3 * RWKV_WIDTH
GATES_W = 2 * D_MODEL
COL_Q = 0
COL_LORA = COL_Q + ATTN_WIDTH
COL_KV = COL_LORA + 2 * LANES
REST_W = COL_KV + 2 * KV_WIDTH

WKV_CHUNK = 64
PAIR = 2 * HEAD_DIM
N_PAIRS = N_RWKV_HEADS // 2

NN = (((1,), (0,)), ((), ()))
NT = (((1,), (1,)), ((), ()))
TN = (((0,), (0,)), ((), ()))


def _mm(a, b, dims=NN):
    return lax.dot_general(a.astype(BF16), b.astype(BF16), dims, preferred_element_type=F32)


def _split2(x):
    hi = x.astype(BF16)
    lo = (x - hi.astype(F32)).astype(BF16)
    return hi, lo


def _mm_exact_rhs(x, m_bf16):
    hi, lo = _split2(x)
    d = lambda p: lax.dot_general(p, m_bf16, NN, preferred_element_type=F32)
    return d(hi) + d(lo)


def _div_pow2(i, n):
    return lax.shift_right_logical(i, int(np.log2(n)))


def _mod_pow2(i, n):
    return lax.bitwise_and(i, n - 1)


def _group_ones():
    i = lax.broadcasted_iota(jnp.int32, (LANES, LANES), 0)
    j = lax.broadcasted_iota(jnp.int32, (LANES, LANES), 1)
    return jnp.where(_div_pow2(i, HEAD_DIM) == _div_pow2(j, HEAD_DIM), 1.0, 0.0).astype(BF16)


def _group_sum(x, ones_bd):
    blocks = [
        _mm_exact_rhs(x[:, i * LANES:(i + 1) * LANES], ones_bd)
        for i in range(x.shape[1] // LANES)
    ]
    return blocks[0] if len(blocks) == 1 else jnp.concatenate(blocks, axis=1)


def _lspec(l, shape, single=False):
    kw = dict(pipeline_mode=pl.Buffered(1)) if single else {}
    return pl.BlockSpec((None,) + tuple(shape), lambda *g: (l,) + (0,) * len(shape), **kw)


def _mod_spec(l, row0, rows_per_mod, tm):
    if rows_per_mod:
        return pl.BlockSpec((None, 1, SUBLANES, D_MODEL), lambda i, *g: (l, row0 + i * tm // rows_per_mod, 0, 0))
    return pl.BlockSpec((None, 1, SUBLANES, D_MODEL), lambda *g: (l, row0, 0, 0))


def _rms(x, g):
    return x * lax.rsqrt(jnp.mean(x * x, axis=-1, keepdims=True) + NORM_EPS) * g


def _mod_kernel(c_ref, w_ref, b_ref, o_ref):
    c = c_ref[...]
    s = c * jax.nn.sigmoid(c)
    o_ref[0] = _mm(s, w_ref[0]) + b_ref[0]


def _modulation(cvec8, w_mod, b_mod):
    tn = D_MODEL
    n = N_MOD * D_MODEL
    return pl.pallas_call(
        _mod_kernel,
        grid=(DEPTH, n // tn),
        in_specs=[
            pl.BlockSpec((SUBLANES, D_MODEL), lambda l, j: (0, 0)),
            pl.BlockSpec((1, D_MODEL, tn), lambda l, j: (l, 0, j)),
            pl.BlockSpec((1, 1, tn), lambda l, j: (l, 0, j)),
        ],
        out_specs=pl.BlockSpec((1, SUBLANES, tn), lambda l, j: (l, 0, j)),
        out_shape=jax.ShapeDtypeStruct((DEPTH, SUBLANES, n), F32),
        name="modulation",
    )(cvec8, w_mod, b_mod.reshape(DEPTH, 1, n))


def _shift_mix(x, mu):
    n = x.shape[0]
    row = lax.broadcasted_iota(jnp.int32, x.shape, 0)
    x_prev = jnp.where(row == 0, 0.0, pltpu.roll(x, 1, 0))
    x_next = jnp.where(row == n - 1, 0.0, pltpu.roll(x, n - 1, 0))
    return x * (1.0 - mu) + (0.5 * mu) * (x_prev + x_next)


def _inproj_kernel(x_ref, mod_ref, ng_ref, wrkv_ref, wrest_ref, mu_ref, kk_ref,
                   rkv_ref, gates_ref, rest_ref, kh_ref, h_scr, *, tm, seq):
    j = pl.program_id(1)
    C = RWKV_WIDTH

    def mix_in_place(i, piece):
        lanes = slice(piece * LANES, (piece + 1) * LANES)
        cols = slice(i * C + piece * LANES, i * C + (piece + 1) * LANES)
        for s in range(tm // seq):
            rows = slice(s * seq, (s + 1) * seq)
            mixed = _shift_mix(rkv_ref[rows, cols], mu_ref[i:i + 1, lanes])
            rkv_ref[rows, cols] = mixed
            if i == 1:
                kk = mixed * kk_ref[:, lanes]
                kh_ref[rows, lanes] = kk * lax.rsqrt(_group_sum(kk * kk, _group_ones()) + 1e-12)

    @pl.when(j == 0)
    def _():
        sh = mod_ref[0, 0:1, :]
        sc = mod_ref[0, 1:2, :]
        ng = ng_ref[0:1, :]
        sub = 256
        for i in range(tm // sub):
            x = x_ref[i * sub:(i + 1) * sub, :]
            h_scr[i * sub:(i + 1) * sub, :] = (_rms(x, ng) * (1.0 + sc) + sh).astype(BF16)
        rkv_ref[...] = lax.dot_general(h_scr[...], wrkv_ref[...], NN, preferred_element_type=F32)

    n_piece = C // LANES
    tn = wrest_ref.shape[1] // n_piece
    for jj in range(1, 4):
        @pl.when(j == jj)
        def _(jj=jj):
            for piece in range(n_piece):
                cols = slice(piece * tn, (piece + 1) * tn)
                out = lax.dot_general(h_scr[...], wrest_ref[:, cols], NN, preferred_element_type=F32)
                if jj < 3:
                    gates_ref[:, cols] = out.astype(BF16)
                else:
                    rest_ref[:, cols] = out
                mix_in_place(jj - 1, piece)


def _in_projection(x2d, P, l, mod_row0, rows_per_mod, seq):
    n_tok = x2d.shape[0]
    tm = 1024
    tn = D_MODEL
    assert tm % seq == 0 and GATES_W == 2 * tn and REST_W == tn
    return pl.pallas_call(
        functools.partial(_inproj_kernel, tm=tm, seq=seq),
        grid=(n_tok // tm, 4),
        in_specs=[
            pl.BlockSpec((tm, D_MODEL), lambda i, j: (i, 0)),
            _mod_spec(l, mod_row0, rows_per_mod, tm),
            _lspec(l, (4, D_MODEL)),
            _lspec(l, (D_MODEL, RKV_W), single=True),
            pl.BlockSpec((None, D_MODEL, tn), lambda i, j: (l, 0, jnp.maximum(j - 1, 0))),
            _lspec(l, (3, RWKV_WIDTH)),
            _lspec(l, (1, RWKV_WIDTH)),
        ],
        out_specs=[pl.BlockSpec((tm, RKV_W), lambda i, j: (i, 0)),
                   pl.BlockSpec((tm, tn), lambda i, j: (i, jnp.clip(j - 1, 0, 1))),
                   pl.BlockSpec((tm, tn), lambda i, j: (i, 0)),
                   pl.BlockSpec((tm, RWKV_WIDTH), lambda i, j: (i, 0))],
        out_shape=[jax.ShapeDtypeStruct((n_tok, RKV_W), F32),
                   jax.ShapeDtypeStruct((n_tok, GATES_W), BF16),
                   jax.ShapeDtypeStruct((n_tok, REST_W), F32),
                   jax.ShapeDtypeStruct((n_tok, RWKV_WIDTH), F32)],
        scratch_shapes=[pltpu.VMEM((tm, D_MODEL), BF16)],
        compiler_params=pltpu.CompilerParams(
            dimension_semantics=("parallel", "arbitrary"), vmem_limit_bytes=VMEM_LIMIT_BYTES),
        name="in_projection",
    )(x2d, P['mod'], P['ng'], P['w_rkv'], P['w_rest'], P['mu'], P['k_k'])


def _wkv_masks(reverse):
    L = WKV_CHUNK
    row = lax.broadcasted_iota(jnp.int32, (L, PAIR), 0)
    col = _mod_pow2(lax.broadcasted_iota(jnp.int32, (L, PAIR), 1), L)
    return dict(
        strict=(col > row) if reverse else (col < row),
        incl=(col >= row) if reverse else (col <= row),
        blk8=_div_pow2(row, 8) == _div_pow2(col, 8),
        eye=row == col,
    )


def _bd(x, left):
    x = x.astype(BF16)
    z = jnp.zeros_like(x)
    return jnp.concatenate([jnp.where(left, x, z), jnp.where(left, z, x)], axis=0)


def _wkv_operand_scratch(nb):
    C, L = RWKV_WIDTH, WKV_CHUNK
    full = lambda dt: pltpu.VMEM((nb, 2, L, C), dt)
    pair = lambda rows: pltpu.VMEM((nb, 2, N_PAIRS, rows, PAIR), BF16)
    return dict(khd=full(F32), rd=full(F32), bonus=full(F32), etot=pltpu.VMEM((nb, 2, SUBLANES, C), F32),
                be=full(BF16), kte=full(BF16), vb=full(BF16),
                lhsA=pair(2 * L), rhsA=pair(4 * L), vbd=pair(2 * L), khdbd=pair(2 * L))


def _wkv_prepare(e, d, rkv, khr, lor, ka_ref, rk_ref, w0_ref, wup_ref, a0_ref, aup_ref, ops, ones_bd, tri, left):
    C, L = RWKV_WIDTH, WKV_CHUNK
    reverse = d == 1
    r = rkv[e, 0, :, 0:C]
    k = rkv[e, 0, :, C:2 * C]
    v = rkv[e, 0, :, 2 * C:3 * C]
    kh = khr[e, 0]
    lo_dec = lor[e, 0, :, 0:LANES]
    w_raw = _mm(jnp.tanh(lo_dec), wup_ref[d]) + w0_ref[d:d + 1, :]
    lw = -DECAY_SCALE * jax.nn.sigmoid(w_raw)
    a = jax.nn.sigmoid(_mm(lo_dec, aup_ref[d]) + a0_ref[d:d + 1, :])
    kt = k * (1.0 + (a - 1.0) * ka_ref[...])
    ops['bonus'][e, d] = _group_sum(r * kt * rk_ref[...], ones_bd) * v
    b = a * kh

    hi, lo2 = _split2(lw)
    cum = (lax.dot_general(tri[d], hi, NN, preferred_element_type=F32)
           + lax.dot_general(tri[d], lo2, NN, preferred_element_type=F32))
    tot = cum[0:1, :] if reverse else cum[L - 1:L, :]
    e_inv = jnp.exp(-cum)
    e_end = jnp.exp(tot - cum)
    khd = kh * jnp.exp(cum - lw)
    rd = r * jnp.exp(cum)
    bi = b * e_inv
    ki = kt * e_inv
    ops['khd'][e, d] = khd
    ops['rd'][e, d] = rd
    ops['etot'][e, d] = jnp.broadcast_to(jnp.exp(tot), (SUBLANES, C))
    ops['be'][e, d] = (b * e_end).astype(BF16)
    ops['kte'][e, d] = (kt * e_end).astype(BF16)
    ops['vb'][e, d] = v.astype(BF16)
    for p in range(N_PAIRS):
        sl = slice(p * PAIR, (p + 1) * PAIR)
        ops['lhsA'][e, d, p] = jnp.concatenate([khd[:, sl], rd[:, sl]], axis=0).astype(BF16)
        ops['rhsA'][e, d, p] = jnp.concatenate([_bd(bi[:, sl], left), _bd(ki[:, sl], left)], axis=0)
        ops['vbd'][e, d, p] = _bd(v[:, sl], left)
        ops['khdbd'][e, d, p] = _bd(khd[:, sl], left)


def _wkv_chains(chains, left, fillers):
    L = WKV_CHUNK
    i2 = lax.broadcasted_iota(jnp.int32, (PAIR, PAIR), 0)
    j2 = lax.broadcasted_iota(jnp.int32, (PAIR, PAIR), 1)
    same = _div_pow2(i2, HEAD_DIM) == _div_pow2(j2, HEAD_DIM)
    eye2 = i2 == j2
    bd = lambda x: _bd(x, left)
    lp = lambda g: jnp.where(left, g[:L], g[L:])
    n_stages = 13
    due = {((k + 1) * n_stages) // (len(fillers) + 1): f for k, f in enumerate(fillers)}
    emitted = [0]

    def each(f):
        for ch in chains:
            f(ch)
        emitted[0] += 1
        if emitted[0] in due:
            due[emitted[0]]()

    def a_blocks(ch):
        m = ch['masks']
        A = lax.dot_general(ch['lhsA'], ch['rhsA'], NT, preferred_element_type=F32)
        ch['A_ub'] = jnp.where(m['strict'], A[:L, :PAIR], 0.0)
        ch['A_uvrv'] = jnp.concatenate([jnp.where(m['strict'], A[:L, PAIR:], 0.0),
                                        jnp.where(m['incl'], A[L:, PAIR:], 0.0)], axis=0).astype(BF16)
        ch['A_rb'] = jnp.where(m['incl'], A[L:, :PAIR], 0.0).astype(BF16)
        a8 = jnp.where(m['blk8'], ch['A_ub'], 0.0)
        ch['A8'] = a8
        ch['X0'] = jnp.where(m['eye'], 1.0, -a8)
        ch['Nlow'] = ch['A_ub'] - a8
    each(a_blocks)

    stack = lambda a, b: jnp.concatenate([a.astype(BF16), b.astype(BF16)], axis=0)

    def sq1(ch):
        ch['A8_2'] = _mm(ch['A8'], bd(ch['A8']))
    each(sq1)

    def sq2(ch):
        r = _mm(stack(ch['A8_2'], ch['X0']), bd(ch['A8_2']))
        ch['A8_4'] = r[:L]
        ch['X1'] = ch['X0'] + r[L:]
        ch['AV'] = _mm(ch['A_uvrv'], ch['vbd'])
    each(sq2)

    def dinv(ch):
        ch['T8'] = ch['X1'] + _mm(ch['X1'], bd(ch['A8_4']))
    each(dinv)

    def e1(ch):
        ch['E'] = _mm(ch['T8'], bd(ch['Nlow']))
    each(e1)

    def e2(ch):
        ch['E2'] = _mm(ch['E'], bd(ch['E']))
    each(e2)

    def e3(ch):
        ime = jnp.where(ch['masks']['eye'], 1.0, 0.0) - ch['E']
        r = _mm(stack(ch['E2'], ime), bd(ch['E2']))
        ch['E4'] = r[:L]
        ch['Y1'] = ime + r[L:]
    each(e3)

    def e4(ch):
        ch['Y2'] = ch['Y1'] + _mm(ch['Y1'], bd(ch['E4']))
    each(e4)

    def t_stage(ch):
        ch['T'] = _mm(ch['Y2'], bd(ch['T8']))
    each(t_stage)

    def p_stage(ch):
        ch['P'] = _mm(ch['A_rb'], bd(ch['T']))
    each(p_stage)

    def q_stage(ch):
        rhs = jnp.concatenate([ch['khdbd'], bd(ch['AV'][:L])], axis=1)
        r = _mm(stack(ch['T'], ch['P']), rhs)
        ch['Q'] = r[:L].astype(BF16)
        ch['G1'] = ch['rd'] - r[L:, :PAIR]
        ch['G2'] = ch['AV'][L:] - r[L:, PAIR:]
    each(q_stage)

    def mn_stage(ch):
        QB = _mm(ch['Q'], ch['be'], TN)
        ch['M'] = (jnp.where(eye2, ch['etot'], 0.0) - jnp.where(same, QB[:PAIR], 0.0)).astype(BF16)
        ch['N'] = lp(_mm(ch['vb'], ch['kte'], TN)) - lp(QB[PAIR:])
    each(mn_stage)

    def out_stage(ch):
        ch['y'] = _mm(ch['G1'], bd(ch['S']), NT) + ch['G2']
        ch['Sn'] = _mm(ch['S'], ch['M']) + ch['N']
    each(out_stage)


def _wkv_kernel(rkv0, khr0, lor0, rkv1, khr1, lor1,
                ka_ref, rk_ref, w0_ref, wup_ref, a0_ref, aup_ref, s0_ref,
                *rest, zero_state, nb, nc, op_names):
    n_ops = len(op_names)
    y0_ref, y1_ref, bo0_ref, bo1_ref, sfin_ref, s_scr = rest[-(6 + 2 * n_ops):len(rest) - 2 * n_ops]
    op_sets = [dict(zip(op_names, rest[len(rest) - (2 - i) * n_ops:len(rest) - (1 - i) * n_ops])) for i in (0, 1)]
    t = pl.program_id(0)
    c = lax.rem(jnp.maximum(t - 1, 0), nc)

    @pl.when(t == 0)
    def _():
        for ref in op_sets[1].values():
            ref[...] = jnp.zeros(ref.shape, ref.dtype)

    @pl.when(c == 0)
    def _():
        if zero_state:
            s_scr[...] = jnp.zeros(s_scr.shape, F32)
        else:
            s_scr[...] = s0_ref[...]

    def step(ops_in, ops_out):
        L = WKV_CHUNK
        left = lax.broadcasted_iota(jnp.int32, (L, PAIR), 1) < HEAD_DIM
        masks = [_wkv_masks(False), _wkv_masks(True)]

        chains = []
        for e in range(nb):
            for d, (y_ref, bo_ref) in enumerate(((y0_ref, bo0_ref), (y1_ref, bo1_ref))):
                bo_ref[e, 0] = ops_in['bonus'][e, d].astype(BF16)
                for p in range(N_PAIRS):
                    sl = slice(p * PAIR, (p + 1) * PAIR)
                    chains.append(dict(
                        e=e, d=d, p=p, sl=sl, y_ref=y_ref, masks=masks[d], S=s_scr[e, d, p],
                        lhsA=ops_in['lhsA'][e, d, p], rhsA=ops_in['rhsA'][e, d, p], vbd=ops_in['vbd'][e, d, p],
                        khdbd=ops_in['khdbd'][e, d, p], rd=ops_in['rd'][e, d, :, sl], be=ops_in['be'][e, d, :, sl],
                        kte=ops_in['kte'][e, d, :, sl], vb=ops_in['vb'][e, d, :, sl],
                        etot=ops_in['etot'][e, d, 0:1, sl]))
        ones_bd = _group_ones()
        ti = lax.broadcasted_iota(jnp.int32, (L, L), 0)
        tj = lax.broadcasted_iota(jnp.int32, (L, L), 1)
        tri = [jnp.where(tj <= ti, 1.0, 0.0).astype(BF16), jnp.where(tj >= ti, 1.0, 0.0).astype(BF16)]
        prepare = [
            functools.partial(_wkv_prepare, e, d, rkv, khr, lor, ka_ref, rk_ref, w0_ref, wup_ref, a0_ref, aup_ref,
                              ops_out, ones_bd, tri, left)
            for e in range(nb) for d, (rkv, khr, lor) in enumerate(((rkv0, khr0, lor0), (rkv1, khr1, lor1)))]
        _wkv_chains(chains, left, prepare)
        for ch in chains:
            ch['y_ref'][ch['e'], 0, :, ch['sl']] = ch['y'].astype(BF16)
            s_scr[ch['e'], ch['d'], ch['p']] = ch['Sn']

    parity = lax.rem(t, 2)

    @pl.when(parity == 0)
    def _():
        step(op_sets[1], op_sets[0])

    @pl.when(parity == 1)
    def _():
        step(op_sets[0], op_sets[1])

    @pl.when((c == nc - 1) & (t > 0))
    def _():
        for e in range(nb):
            for d in (0, 1):
                for p in range(N_PAIRS):
                    s2 = s_scr[e, d, p]
                    sfin_ref[e, d, 2 * p] = s2[:, :HEAD_DIM]
                    sfin_ref[e, d, 2 * p + 1] = pltpu.roll(s2, HEAD_DIM, 1)[:, :HEAD_DIM]


def _wkv(rkv, rest, kh, s0_pairs, n_batch, seq, P, l, s_all):
    L = WKV_CHUNK
    nc = seq // L
    nb = 4
    n_steps = (n_batch // nb) * nc
    n_tok = n_batch * seq
    zero_state = s0_pairs is None
    if zero_state:
        s0_pairs = jnp.zeros((nb, 2, N_PAIRS, HEAD_DIM, PAIR), F32)
    rkv4 = rkv.reshape(n_batch, nc, L, RKV_W)
    rest4 = rest.reshape(n_batch, nc, L, REST_W)
    kh4 = kh.reshape(n_batch, nc, L, RWKV_WIDTH)

    def in_pos(t, d):
        s = jnp.minimum(t, n_steps - 1)
        c = lax.rem(s, nc)
        return lax.div(s, nc), (c if d == 0 else nc - 1 - c)

    def out_pos(t, d):
        s = jnp.maximum(t - 1, 0)
        c = lax.rem(s, nc)
        return lax.div(s, nc), (c if d == 0 else nc - 1 - c)

    in_specs = []
    for d in (0, 1):
        in_specs += [
            pl.BlockSpec((nb, 1, L, RKV_W), lambda t, d=d: (*in_pos(t, d), 0, 0)),
            pl.BlockSpec((nb, 1, L, RWKV_WIDTH), lambda t, d=d: (*in_pos(t, d), 0, 0)),
            pl.BlockSpec((nb, 1, L, 2 * LANES), lambda t, d=d: (*in_pos(t, d), 0, COL_LORA // (2 * LANES))),
        ]
    state_spec = lambda idx: pl.BlockSpec((nb, 2, N_PAIRS, HEAD_DIM, PAIR), idx)
    in_specs += [
        _lspec(l, (1, RWKV_WIDTH)), _lspec(l, (1, RWKV_WIDTH)),
        _lspec(l, (2, RWKV_WIDTH)), _lspec(l, (2, LANES, RWKV_WIDTH)),
        _lspec(l, (2, RWKV_WIDTH)), _lspec(l, (2, LANES, RWKV_WIDTH)),
        state_spec((lambda t: (0, 0, 0, 0, 0)) if zero_state else (lambda t: (out_pos(t, 0)[0], 0, 0, 0, 0))),
    ]
    args = [rkv4, kh4, rest4, rkv4, kh4, rest4, P['k_a'], P['r_k'], P['w0'], P['wup'], P['a0'], P['aup'], s0_pairs]
    aliases = {}
    if s_all is not None:
        in_specs.append(pl.BlockSpec(memory_space=pl.ANY))
        aliases = {len(args): 4}
        args.append(s_all)
    tok_spec = lambda d: pl.BlockSpec((nb, 1, L, RWKV_WIDTH), lambda t, d=d: (*out_pos(t, d), 0, 0))
    sfin_spec = pl.BlockSpec((nb, None, 2, N_RWKV_HEADS, HEAD_DIM, HEAD_DIM),
                             lambda t: (out_pos(t, 0)[0], l, 0, 0, 0, 0))
    out_specs = [tok_spec(0), tok_spec(1), tok_spec(0), tok_spec(1), sfin_spec]
    tok = jax.ShapeDtypeStruct((n_batch, nc, L, RWKV_WIDTH), BF16)
    op_scratch = _wkv_operand_scratch(nb)
    outs = pl.pallas_call(
        functools.partial(_wkv_kernel, zero_state=zero_state, nb=nb, nc=nc, op_names=tuple(op_scratch)),
        grid=(n_steps + 1,),
        in_specs=in_specs,
        out_specs=out_specs,
        out_shape=[tok, tok, tok, tok,
                   jax.ShapeDtypeStruct((n_batch, DEPTH, 2, N_RWKV_HEADS, HEAD_DIM, HEAD_DIM), F32)],
        input_output_aliases=aliases,
        scratch_shapes=([pltpu.VMEM((nb, 2, N_PAIRS, HEAD_DIM, PAIR), F32)]
                        + list(op_scratch.values()) + list(_wkv_operand_scratch(nb).values())),
        compiler_params=pltpu.CompilerParams(
            dimension_semantics=("arbitrary",), vmem_limit_bytes=VMEM_LIMIT_BYTES),
        name="wkv_scan",
    )(*args)
    y0, y1, bo0, bo1, s_fin = outs
    flat = lambda t: t.reshape(n_tok, RWKV_WIDTH)
    return flat(y0), flat(y1), flat(bo0), flat(bo1), s_fin


def _rope(x, cos, sin_signed):
    w = x.shape[1]
    lane = lax.broadcasted_iota(jnp.int32, x.shape, 1)
    partner = jnp.where(_mod_pow2(lane, 2 * ROPE_FREQS) < ROPE_FREQS,
                        pltpu.roll(x, w - ROPE_FREQS, 1), pltpu.roll(x, ROPE_FREQS, 1))
    return x * cos + partner * sin_signed


def _attn_kernel(*refs, tq, seq, past, use_rope):
    it = iter(refs)
    q_ref, kv_ref, qg_ref, kg_ref = next(it), next(it), next(it), next(it)
    if use_rope:
        cq_ref, sq_ref, ck_ref, sk_ref = next(it), next(it), next(it), next(it)
    if past:
        pk_ref, pv_ref = next(it), next(it)
    o_ref, ko_ref, vo_ref, kvar_scr, vvar_scr = refs[-5:]

    ones_bd = _group_ones()

    @pl.when(pl.program_id(1) == 0)
    def _():
        kv = kv_ref[...]
        k_raw = kv[:, :KV_WIDTH]
        kn = k_raw * lax.rsqrt(_group_sum(k_raw * k_raw, ones_bd) * (1.0 / HEAD_DIM) + NORM_EPS) * kg_ref[...]
        if use_rope:
            kn = _rope(kn, ck_ref[...], sk_ref[...])
        vn = kv[:, KV_WIDTH:]
        ko_ref[...] = kn
        vo_ref[...] = vn
        pieces = [(past, seq, kn, vn)]
        if past:
            pieces.append((0, past, pk_ref[...], pv_ref[...]))
        for start, n, kx, vx in pieces:
            left = lax.broadcasted_iota(jnp.int32, kx.shape, 1) < HEAD_DIM
            for x, scr in ((kx, kvar_scr), (vx, vvar_scr)):
                sw = pltpu.roll(x, HEAD_DIM, 1)
                scr[0, start:start + n, :] = jnp.where(left, x, 0.0).astype(BF16)
                scr[1, start:start + n, :] = jnp.where(left, 0.0, sw).astype(BF16)
                scr[2, start:start + n, :] = jnp.where(left, sw, 0.0).astype(BF16)
                scr[3, start:start + n, :] = jnp.where(left, 0.0, x).astype(BF16)

    scale = HEAD_DIM ** -0.5
    n_blk = ATTN_WIDTH // LANES
    qn = []
    for jb in range(n_blk):
        qb = q_ref[:, jb * LANES:(jb + 1) * LANES]
        x = qb * lax.rsqrt(_group_sum(qb * qb, ones_bd) * (1.0 / HEAD_DIM) + NORM_EPS) * qg_ref[...]
        if use_rope:
            x = _rope(x, cq_ref[...], sq_ref[...])
        qn.append((x * scale).astype(BF16))
    heads = [(jb, 2 * (jb // (n_blk // N_KV_HEADS)) + side) for jb in range(n_blk) for side in (0, 1)]
    scores = [lax.dot_general(qn[jb], kvar_scr[var], NT, preferred_element_type=F32) for jb, var in heads]
    exps = [jnp.exp(s - jnp.max(s, axis=-1, keepdims=True)) for s in scores]
    outs = [lax.dot_general(e.astype(BF16), vvar_scr[var], NN, preferred_element_type=F32)
            for e, (jb, var) in zip(exps, heads)]
    outs = [o / jnp.sum(e, axis=-1, keepdims=True) for o, e in zip(outs, exps)]
    for jb in range(n_blk):
        o_ref[:, jb * LANES:(jb + 1) * LANES] = (outs[2 * jb] + outs[2 * jb + 1]).astype(BF16)


def _attention(rest, n_batch, seq, P, l, rope128, past_kv, kv_all):
    tq = min(seq, 512)
    nq = seq // tq
    n_tok = n_batch * seq
    use_rope = rope128 is not None
    past = 0 if past_kv is None else past_kv[0].shape[2]
    in_specs = [
        pl.BlockSpec((tq, ATTN_WIDTH), lambda b, i: (b * nq + i, COL_Q // ATTN_WIDTH)),
        pl.BlockSpec((seq, 2 * KV_WIDTH), lambda b, i: (b, COL_KV // (2 * KV_WIDTH))),
        _lspec(l, (1, LANES)),
        _lspec(l, (1, LANES)),
    ]
    args = [rest, rest, P['q_gain'], P['k_gain']]
    if use_rope:
        cos, sin = rope128
        in_specs += [pl.BlockSpec((tq, LANES), lambda b, i: (i, 0)),
                     pl.BlockSpec((tq, LANES), lambda b, i: (i, 0)),
                     pl.BlockSpec((seq, LANES), lambda b, i: (0, 0)),
                     pl.BlockSpec((seq, LANES), lambda b, i: (0, 0))]
        args += [cos, sin, cos, sin]
    if past:
        in_specs += [pl.BlockSpec((None, None, past, KV_WIDTH), lambda b, i: (b, l, 0, 0)),
                     pl.BlockSpec((None, None, past, KV_WIDTH), lambda b, i: (b, l, 0, 0))]
        args += list(past_kv)
    aliases = {}
    if kv_all is not None:
        in_specs += [pl.BlockSpec(memory_space=pl.ANY), pl.BlockSpec(memory_space=pl.ANY)]
        aliases = {len(args): 1, len(args) + 1: 2}
        args += list(kv_all)
    cache_spec = pl.BlockSpec((None, None, seq, KV_WIDTH), lambda b, i: (b, l, 0, 0))
    out_specs = [pl.BlockSpec((tq, ATTN_WIDTH), lambda b, i: (b * nq + i, 0)), cache_spec, cache_spec]
    return pl.pallas_call(
        functools.partial(_attn_kernel, tq=tq, seq=seq, past=past, use_rope=use_rope),
        grid=(n_batch, nq),
        in_specs=in_specs,
        out_specs=out_specs,
        out_shape=[jax.ShapeDtypeStruct((n_tok, ATTN_WIDTH), BF16),
                   jax.ShapeDtypeStruct((n_batch, DEPTH, seq, KV_WIDTH), F32),
                   jax.ShapeDtypeStruct((n_batch, DEPTH, seq, KV_WIDTH), F32)],
        input_output_aliases=aliases,
        scratch_shapes=[pltpu.VMEM((2 * N_KV_HEADS, past + seq, KV_WIDTH), BF16),
                        pltpu.VMEM((2 * N_KV_HEADS, past + seq, KV_WIDTH), BF16)],
        compiler_params=pltpu.CompilerParams(
            dimension_semantics=("parallel", "arbitrary"), vmem_limit_bytes=VMEM_LIMIT_BYTES),
        name="attention",
    )(*args)


def _post_kernel(y0_ref, y1_ref, bo0_ref, bo1_ref, lor_ref, oa_ref, gr_ref, ga_ref, x_ref, mod_ref,
                 gup_ref, gnw_ref, gnb_ref, wbr_ref, wout_ref, ng_ref, x1_ref, h2_ref):
    ones_bd = _group_ones()
    y = y0_ref[...].astype(F32) + y1_ref[...].astype(F32)
    mean = _group_sum(y, ones_bd) * (1.0 / HEAD_DIM)
    yc = y - mean
    var = _group_sum(yc * yc, ones_bd) * (1.0 / HEAD_DIM)
    yn = yc * lax.rsqrt(var + GN_EPS) * gnw_ref[...] + gnb_ref[...]
    lg = lor_ref[:, LANES:]
    g = _mm(jax.nn.sigmoid(lg), gup_ref[...])
    o_r = (yn + (bo0_ref[...].astype(F32) + bo1_ref[...].astype(F32))) * g
    merged = (jax.nn.sigmoid(gr_ref[...].astype(F32)) * _mm(o_r, wbr_ref[0])
              + jax.nn.sigmoid(ga_ref[...].astype(F32)) * _mm(oa_ref[...], wbr_ref[1]))
    m = _mm(merged, wout_ref[...])
    g1 = mod_ref[0, 2:3, :]
    sh2 = mod_ref[0, 3:4, :]
    sc2 = mod_ref[0, 4:5, :]
    x1 = x_ref[...] + g1 * _rms(m, ng_ref[1:2, :])
    x1_ref[...] = x1
    h2_ref[...] = (_rms(x1, ng_ref[2:3, :]) * (1.0 + sc2) + sh2).astype(BF16)


def _post(y0, y1, bo0, bo1, rest, gates, o_a, x2d, P, l, mod_row0, rows_per_mod):
    n_tok = x2d.shape[0]
    tm = 512
    tok = lambda w: pl.BlockSpec((tm, w), lambda i: (i, 0))
    return pl.pallas_call(
        _post_kernel,
        grid=(n_tok // tm,),
        in_specs=[
            tok(RWKV_WIDTH), tok(RWKV_WIDTH), tok(RWKV_WIDTH), tok(RWKV_WIDTH),
            pl.BlockSpec((tm, 2 * LANES), lambda i: (i, COL_LORA // (2 * LANES))),
            tok(ATTN_WIDTH),
            pl.BlockSpec((tm, D_MODEL), lambda i: (i, 0)),
            pl.BlockSpec((tm, D_MODEL), lambda i: (i, 1)),
            tok(D_MODEL),
            _mod_spec(l, mod_row0, rows_per_mod, tm),
            _lspec(l, (GATE_RANK, RWKV_WIDTH)), _lspec(l, (1, RWKV_WIDTH)), _lspec(l, (1, RWKV_WIDTH)),
            _lspec(l, (2, RWKV_WIDTH, D_MODEL)), _lspec(l, (D_MODEL, D_MODEL)), _lspec(l, (4, D_MODEL)),
        ],
        out_specs=[tok(D_MODEL), tok(D_MODEL)],
        out_shape=[jax.ShapeDtypeStruct((n_tok, D_MODEL), F32), jax.ShapeDtypeStruct((n_tok, D_MODEL), BF16)],
        compiler_params=pltpu.CompilerParams(
            dimension_semantics=("parallel",), vmem_limit_bytes=VMEM_LIMIT_BYTES),
        name="merge_outproj",
    )(y0, y1, bo0, bo1, rest, o_a, gates, gates, x2d, P['mod'], P['gup'], P['gn_w'], P['gn_b'], P['w_br'], P['w_out'],
      P['ng'])


def _mlp_kernel(h_ref, x1_ref, mod_ref, up_ref, down_ref, ng_ref, o_ref):
    h = h_ref[...]
    f = None
    ff = D_MODEL
    for j in range(D_FF // ff):
        u = lax.dot_general(h, up_ref[:, j * ff:(j + 1) * ff], NN, preferred_element_type=F32)
        u = jnp.square(jnp.maximum(u, 0.0)).astype(BF16)
        part = lax.dot_general(u, down_ref[j * ff:(j + 1) * ff, :], NN, preferred_element_type=F32)
        f = part if f is None else f + part
    g2 = mod_ref[0, 5:6, :]
    o_ref[...] = x1_ref[...] + g2 * _rms(f, ng_ref[3:4, :])


def _mlp(h2, x1, P, l, mod_row0, rows_per_mod):
    n_tok = x1.shape[0]
    tm = 512
    tok = lambda: pl.BlockSpec((tm, D_MODEL), lambda i: (i, 0))
    return pl.pallas_call(
        _mlp_kernel,
        grid=(n_tok // tm,),
        in_specs=[
            tok(), tok(),
            _mod_spec(l, mod_row0, rows_per_mod, tm),
            _lspec(l, (D_MODEL, D_FF), single=True),
            _lspec(l, (D_FF, D_MODEL), single=True),
            _lspec(l, (4, D_MODEL)),
        ],
        out_specs=tok(),
        out_shape=jax.ShapeDtypeStruct((n_tok, D_MODEL), F32),
        compiler_params=pltpu.CompilerParams(
            dimension_semantics=("parallel",), vmem_limit_bytes=VMEM_LIMIT_BYTES),
        name="mlp",
    )(h2, x1, P['mod'], P['up'], P['down'], P['ng'])


def _rope_tables(seq):
    n_rows = seq // GRID_W
    rows = np.repeat(np.arange(n_rows, dtype=np.float32), GRID_W)
    cols = np.tile(np.arange(GRID_W, dtype=np.float32), n_rows)
    half = HEAD_DIM // 2
    freqs = 1.0 / (jnp.asarray(ROPE_THETA, F32) ** (jnp.arange(0, half, 2, dtype=F32) / half))
    ang_r = jnp.asarray(rows)[:, None] * freqs
    ang_c = jnp.asarray(cols)[:, None] * freqs
    cr, sr, cc, sc = jnp.cos(ang_r), jnp.sin(ang_r), jnp.cos(ang_c), jnp.sin(ang_c)
    cos64 = jnp.concatenate([cr, cr, cc, cc], axis=1)
    sin64 = jnp.concatenate([-sr, sr, -sc, sc], axis=1)
    return jnp.tile(cos64, (1, LANES // HEAD_DIM)), jnp.tile(sin64, (1, LANES // HEAD_DIM))


def _pairs_from_state(s):
    b = s.shape[0]
    s = s.reshape(b, 2, N_PAIRS, 2, HEAD_DIM, HEAD_DIM)
    return jnp.concatenate([s[:, :, :, 0], s[:, :, :, 1]], axis=-1)


def _layer(x2d, n_batch, seq, P, l, mod_row0, rows_per_mod, rope128, past_kv, s0, carry):
    kv_all, s_all = (None, None) if carry is None else (carry[:2], carry[2])
    rkv, gates, rest, kh = _in_projection(x2d, P, l, mod_row0, rows_per_mod, seq)
    s0_pairs = None if s0 is None else _pairs_from_state(s0)
    y0, y1, bo0, bo1, s_all = _wkv(rkv, rest, kh, s0_pairs, n_batch, seq, P, l, s_all)
    o_a, k_all, v_all = _attention(rest, n_batch, seq, P, l, rope128, past_kv, kv_all)
    x1, h2 = _post(y0, y1, bo0, bo1, rest, gates, o_a, x2d, P, l, mod_row0, rows_per_mod)
    x2 = _mlp(h2, x1, P, l, mod_row0, rows_per_mod)
    return x2, (k_all, v_all, s_all)


def kernel(x_prompt, x_sample, cache_k, cache_v, state_wkv, c, c_ctx, w_in, w_br, w_out, w_mod, b_mod, norm_g, mlp_up, mlp_down, rwkv_mu, rwkv_k_k, rwkv_k_a, rwkv_r_k, decay_w0, decay_up, iclr_a0, iclr_up, gate_up, gn_w, gn_b, q_gain, k_gain):
    n_ctx, seq_ctx, _ = x_prompt.shape
    n_dec, seq_dec, _ = x_sample.shape
    past = cache_k.shape[2]

    cvec8 = jnp.zeros((SUBLANES, D_MODEL), F32).at[0].set(c_ctx).at[1:1 + n_dec].set(c)
    mod = _modulation(cvec8, w_mod, b_mod)
    mod = jnp.pad(mod.reshape(DEPTH, SUBLANES, N_MOD, D_MODEL), ((0, 0), (0, 0), (0, SUBLANES - N_MOD), (0, 0)))

    o = np.cumsum((0, RWKV_WIDTH, RWKV_WIDTH, RWKV_WIDTH, DECAY_RANK, ICLR_RANK, GATE_RANK,
                   ATTN_WIDTH, KV_WIDTH, KV_WIDTH, 2 * D_MODEL))
    w_rest = jnp.concatenate([w_in[:, :, o[9]:o[10]], w_in[:, :, o[6]:o[7]], w_in[:, :, o[3]:o[6]],
                              w_in[:, :, o[7]:o[9]]], axis=-1)
    zpad = jnp.zeros((DEPTH, 2, DECAY_RANK, RWKV_WIDTH), F32)
    row = lambda a: a.reshape(DEPTH, 1, -1)
    tile2 = lambda a: jnp.tile(a, (1, LANES // HEAD_DIM)).reshape(DEPTH, 1, LANES)
    P = dict(
        mod=mod, ng=norm_g, w_rkv=w_in[:, :, o[0]:o[3]].astype(BF16), w_rest=w_rest.astype(BF16),
        mu=rwkv_mu, k_k=row(rwkv_k_k), k_a=row(rwkv_k_a), r_k=row(rwkv_r_k),
        w0=decay_w0, wup=jnp.concatenate([decay_up, zpad], axis=2).astype(BF16),
        a0=iclr_a0, aup=jnp.concatenate([zpad, iclr_up], axis=2).astype(BF16),
        gup=gate_up.astype(BF16), gn_w=row(gn_w), gn_b=row(gn_b),
        w_br=w_br.astype(BF16), w_out=w_out.astype(BF16), up=mlp_up.astype(BF16), down=mlp_down.astype(BF16),
        q_gain=tile2(q_gain), k_gain=tile2(k_gain),
    )

    x = x_prompt.reshape(n_ctx * seq_ctx, D_MODEL)
    carry = None
    for l in range(DEPTH):
        x, carry = _layer(x, n_ctx, seq_ctx, P, l, 0, 0, None, None, None, carry)
    y_prompt = x.reshape(n_ctx, seq_ctx, D_MODEL)
    new_k, new_v, new_s = carry
    cache_shape = (n_ctx, DEPTH, seq_ctx, N_KV_HEADS, HEAD_DIM)

    rope128 = _rope_tables(seq_dec)
    past_kv = (cache_k.reshape(n_dec, DEPTH, past, KV_WIDTH), cache_v.reshape(n_dec, DEPTH, past, KV_WIDTH))
    x = x_sample.reshape(n_dec * seq_dec, D_MODEL)
    for l in range(DEPTH):
        x, _ = _layer(x, n_dec, seq_dec, P, l, 1, seq_dec, rope128, past_kv, state_wkv[:, l], None)
    y_sample = x.reshape(n_dec, seq_dec, D_MODEL)

    return (y_prompt, y_sample, new_k.reshape(cache_shape), new_v.reshape(cache_shape), new_s)
```

```python
import functools

import numpy as np
import jax
import jax.numpy as jnp
from jax import lax
from jax.experimental import pallas as pl
from jax.experimental.pallas import tpu as pltpu

F32 = jnp.float32
BF16 = jnp.bfloat16

D_MODEL = 1024
DEPTH = 2
GRID_W = 64
HEAD_DIM = 64
N_RWKV_HEADS = 8
RWKV_WIDTH = N_RWKV_HEADS * HEAD_DIM
N_Q_HEADS = 8
N_KV_HEADS = 2
ATTN_WIDTH = N_Q_HEADS * HEAD_DIM
KV_WIDTH = N_KV_HEADS * HEAD_DIM
DECAY_RANK = 64
ICLR_RANK = 64
GATE_RANK = 128
D_FF = 4 * D_MODEL
ROPE_THETA = 10000.0
ROPE_FREQS = HEAD_DIM // 4
N_MOD = 6
NORM_EPS = 1e-6
GN_EPS = 64e-5
DECAY_SCALE = 0.606531
D_IN = 3 * RWKV_WIDTH + DECAY_RANK + ICLR_RANK + GATE_RANK + ATTN_WIDTH + 2 * KV_WIDTH + 2 * D_MODEL

LANES = 128
SUBLANES = 8
VMEM_LIMIT_BYTES = 56 * 1024 * 1024

RKV_W = 3 * RWKV_WIDTH
GATES_W = 2 * D_MODEL
COL_Q = 0
COL_LORA = COL_Q + ATTN_WIDTH
COL_KV = COL_LORA + 2 * LANES
REST_W = COL_KV + 2 * KV_WIDTH

WKV_CHUNK = 64
PAIR = 2 * HEAD_DIM
N_PAIRS = N_RWKV_HEADS // 2

NN = (((1,), (0,)), ((), ()))
NT = (((1,), (1,)), ((), ()))
TN = (((0,), (0,)), ((), ()))


def _mm(a, b, dims=NN):
    return lax.dot_general(a.astype(BF16), b.astype(BF16), dims, preferred_element_type=F32)


def _split2(x):
    hi = x.astype(BF16)
    lo = (x - hi.astype(F32)).astype(BF16)
    return hi, lo


def _mm_exact_rhs(x, m_bf16):
    hi, lo = _split2(x)
    d = lambda p: lax.dot_general(p, m_bf16, NN, preferred_element_type=F32)
    return d(hi) + d(lo)


def _div_pow2(i, n):
    return lax.shift_right_logical(i, int(np.log2(n)))


def _mod_pow2(i, n):
    return lax.bitwise_and(i, n - 1)


def _group_ones():
    i = lax.broadcasted_iota(jnp.int32, (LANES, LANES), 0)
    j = lax.broadcasted_iota(jnp.int32, (LANES, LANES), 1)
    return jnp.where(_div_pow2(i, HEAD_DIM) == _div_pow2(j, HEAD_DIM), 1.0, 0.0).astype(BF16)


def _group_sum(x, ones_bd):
    blocks = [
        _mm_exact_rhs(x[:, i * LANES:(i + 1) * LANES], ones_bd)
        for i in range(x.shape[1] // LANES)
    ]
    return blocks[0] if len(blocks) == 1 else jnp.concatenate(blocks, axis=1)


def _lspec(l, shape, single=False):
    kw = dict(pipeline_mode=pl.Buffered(1)) if single else {}
    return pl.BlockSpec((None,) + tuple(shape), lambda *g: (l,) + (0,) * len(shape), **kw)


def _mod_spec(l, row0, rows_per_mod, tm):
    if rows_per_mod:
        return pl.BlockSpec((None, 1, SUBLANES, D_MODEL), lambda i, *g: (l, row0 + i * tm // rows_per_mod, 0, 0))
    return pl.BlockSpec((None, 1, SUBLANES, D_MODEL), lambda *g: (l, row0, 0, 0))


def _rms(x, g):
    return x * lax.rsqrt(jnp.mean(x * x, axis=-1, keepdims=True) + NORM_EPS) * g


def _mod_kernel(c_ref, w_ref, b_ref, o_ref):
    c = c_ref[...]
    s = c * jax.nn.sigmoid(c)
    o_ref[0] = _mm(s, w_ref[0]) + b_ref[0]


def _modulation(cvec8, w_mod, b_mod):
    tn = D_MODEL
    n = N_MOD * D_MODEL
    return pl.pallas_call(
        _mod_kernel,
        grid=(DEPTH, n // tn),
        in_specs=[
            pl.BlockSpec((SUBLANES, D_MODEL), lambda l, j: (0, 0)),
            pl.BlockSpec((1, D_MODEL, tn), lambda l, j: (l, 0, j)),
            pl.BlockSpec((1, 1, tn), lambda l, j: (l, 0, j)),
        ],
        out_specs=pl.BlockSpec((1, SUBLANES, tn), lambda l, j: (l, 0, j)),
        out_shape=jax.ShapeDtypeStruct((DEPTH, SUBLANES, n), F32),
        name="modulation",
    )(cvec8, w_mod, b_mod.reshape(DEPTH, 1, n))


def _shift_mix(x, mu):
    n = x.shape[0]
    row = lax.broadcasted_iota(jnp.int32, x.shape, 0)
    x_prev = jnp.where(row == 0, 0.0, pltpu.roll(x, 1, 0))
    x_next = jnp.where(row == n - 1, 0.0, pltpu.roll(x, n - 1, 0))
    return x * (1.0 - mu) + (0.5 * mu) * (x_prev + x_next)


def _inproj_kernel(x_ref, mod_ref, ng_ref, wrkv_ref, wrest_ref, mu_ref, kk_ref,
                   rkv_ref, gates_ref, rest_ref, kh_ref, h_scr, *, tm, seq):
    j = pl.program_id(1)
    C = RWKV_WIDTH

    def mix_in_place(i, piece):
        lanes = slice(piece * LANES, (piece + 1) * LANES)
        cols = slice(i * C + piece * LANES, i * C + (piece + 1) * LANES)
        for s in range(tm // seq):
            rows = slice(s * seq, (s + 1) * seq)
            mixed = _shift_mix(rkv_ref[rows, cols], mu_ref[i:i + 1, lanes])
            rkv_ref[rows, cols] = mixed
            if i == 1:
                kk = mixed * kk_ref[:, lanes]
                kh_ref[rows, lanes] = kk * lax.rsqrt(_group_sum(kk * kk, _group_ones()) + 1e-12)

    @pl.when(j == 0)
    def _():
        sh = mod_ref[0, 0:1, :]
        sc = mod_ref[0, 1:2, :]
        ng = ng_ref[0:1, :]
        sub = 256
        for i in range(tm // sub):
            x = x_ref[i * sub:(i + 1) * sub, :]
            h_scr[i * sub:(i + 1) * sub, :] = (_rms(x, ng) * (1.0 + sc) + sh).astype(BF16)
        rkv_ref[...] = lax.dot_general(h_scr[...], wrkv_ref[...], NN, preferred_element_type=F32)

    n_piece = C // LANES
    tn = wrest_ref.shape[1] // n_piece
    for jj in range(1, 4):
        @pl.when(j == jj)
        def _(jj=jj):
            for piece in range(n_piece):
                cols = slice(piece * tn, (piece + 1) * tn)
                out = lax.dot_general(h_scr[...], wrest_ref[:, cols], NN, preferred_element_type=F32)
                if jj < 3:
                    gates_ref[:, cols] = out.astype(BF16)
                else:
                    rest_ref[:, cols] = out
                mix_in_place(jj - 1, piece)


def _in_projection(x2d, P, l, mod_row0, rows_per_mod, seq):
    n_tok = x2d.shape[0]
    tm = 1024
    tn = D_MODEL
    assert tm % seq == 0 and GATES_W == 2 * tn and REST_W == tn
    return pl.pallas_call(
        functools.partial(_inproj_kernel, tm=tm, seq=seq),
        grid=(n_tok // tm, 4),
        in_specs=[
            pl.BlockSpec((tm, D_MODEL), lambda i, j: (i, 0)),
            _mod_spec(l, mod_row0, rows_per_mod, tm),
            _lspec(l, (4, D_MODEL)),
            _lspec(l, (D_MODEL, RKV_W), single=True),
            pl.BlockSpec((None, D_MODEL, tn), lambda i, j: (l, 0, jnp.maximum(j - 1, 0))),
            _lspec(l, (3, RWKV_WIDTH)),
            _lspec(l, (1, RWKV_WIDTH)),
        ],
        out_specs=[pl.BlockSpec((tm, RKV_W), lambda i, j: (i, 0)),
                   pl.BlockSpec((tm, tn), lambda i, j: (i, jnp.clip(j - 1, 0, 1))),
                   pl.BlockSpec((tm, tn), lambda i, j: (i, 0)),
                   pl.BlockSpec((tm, RWKV_WIDTH), lambda i, j: (i, 0))],
        out_shape=[jax.ShapeDtypeStruct((n_tok, RKV_W), F32),
                   jax.ShapeDtypeStruct((n_tok, GATES_W), BF16),
                   jax.ShapeDtypeStruct((n_tok, REST_W), F32),
                   jax.ShapeDtypeStruct((n_tok, RWKV_WIDTH), F32)],
        scratch_shapes=[pltpu.VMEM((tm, D_MODEL), BF16)],
        compiler_params=pltpu.CompilerParams(
            dimension_semantics=("parallel", "arbitrary"), vmem_limit_bytes=VMEM_LIMIT_BYTES),
        name="in_projection",
    )(x2d, P['mod'], P['ng'], P['w_rkv'], P['w_rest'], P['mu'], P['k_k'])


def _wkv_masks(reverse):
    L = WKV_CHUNK
    row = lax.broadcasted_iota(jnp.int32, (L, PAIR), 0)
    col = _mod_pow2(lax.broadcasted_iota(jnp.int32, (L, PAIR), 1), L)
    return dict(
        strict=(col > row) if reverse else (col < row),
        incl=(col >= row) if reverse else (col <= row),
        blk8=_div_pow2(row, 8) == _div_pow2(col, 8),
        eye=row == col,
    )


def _bd(x, left):
    x = x.astype(BF16)
    z = jnp.zeros_like(x)
    return jnp.concatenate([jnp.where(left, x, z), jnp.where(left, z, x)], axis=0)


def _wkv_operand_scratch(nb):
    C, L = RWKV_WIDTH, WKV_CHUNK
    full = lambda dt: pltpu.VMEM((nb, 2, L, C), dt)
    pair = lambda rows: pltpu.VMEM((nb, 2, N_PAIRS, rows, PAIR), BF16)
    return dict(khd=full(F32), rd=full(F32), bonus=full(F32), etot=pltpu.VMEM((nb, 2, SUBLANES, C), F32),
                be=full(BF16), kte=full(BF16), vb=full(BF16),
                lhsA=pair(2 * L), rhsA=pair(4 * L), vbd=pair(2 * L), khdbd=pair(2 * L))


def _wkv_prepare(e, d, rkv, khr, lor, ka_ref, rk_ref, w0_ref, wup_ref, a0_ref, aup_ref, ops, ones_bd, tri, left):
    C, L = RWKV_WIDTH, WKV_CHUNK
    reverse = d == 1
    r = rkv[e, 0, :, 0:C]
    k = rkv[e, 0, :, C:2 * C]
    v = rkv[e, 0, :, 2 * C:3 * C]
    kh = khr[e, 0]
    lo_dec = lor[e, 0, :, 0:LANES]
    w_raw = _mm(jnp.tanh(lo_dec), wup_ref[d]) + w0_ref[d:d + 1, :]
    lw = -DECAY_SCALE * jax.nn.sigmoid(w_raw)
    a = jax.nn.sigmoid(_mm(lo_dec, aup_ref[d]) + a0_ref[d:d + 1, :])
    kt = k * (1.0 + (a - 1.0) * ka_ref[...])
    ops['bonus'][e, d] = _group_sum(r * kt * rk_ref[...], ones_bd) * v
    b = a * kh

    hi, lo2 = _split2(lw)
    cum = (lax.dot_general(tri[d], hi, NN, preferred_element_type=F32)
           + lax.dot_general(tri[d], lo2, NN, preferred_element_type=F32))
    tot = cum[0:1, :] if reverse else cum[L - 1:L, :]
    e_inv = jnp.exp(-cum)
    e_end = jnp.exp(tot - cum)
    khd = kh * jnp.exp(cum - lw)
    rd = r * jnp.exp(cum)
    bi = b * e_inv
    ki = kt * e_inv
    ops['khd'][e, d] = khd
    ops['rd'][e, d] = rd
    ops['etot'][e, d] = jnp.broadcast_to(jnp.exp(tot), (SUBLANES, C))
    ops['be'][e, d] = (b * e_end).astype(BF16)
    ops['kte'][e, d] = (kt * e_end).astype(BF16)
    ops['vb'][e, d] = v.astype(BF16)
    for p in range(N_PAIRS):
        sl = slice(p * PAIR, (p + 1) * PAIR)
        ops['lhsA'][e, d, p] = jnp.concatenate([khd[:, sl], rd[:, sl]], axis=0).astype(BF16)
        ops['rhsA'][e, d, p] = jnp.concatenate([_bd(bi[:, sl], left), _bd(ki[:, sl], left)], axis=0)
        ops['vbd'][e, d, p] = _bd(v[:, sl], left)
        ops['khdbd'][e, d, p] = _bd(khd[:, sl], left)


def _wkv_chains(chains, left, fillers):
    L = WKV_CHUNK
    i2 = lax.broadcasted_iota(jnp.int32, (PAIR, PAIR), 0)
    j2 = lax.broadcasted_iota(jnp.int32, (PAIR, PAIR), 1)
    same = _div_pow2(i2, HEAD_DIM) == _div_pow2(j2, HEAD_DIM)
    eye2 = i2 == j2
    bd = lambda x: _bd(x, left)
    lp = lambda g: jnp.where(left, g[:L], g[L:])
    n_stages = 13
    due = {((k + 1) * n_stages) // (len(fillers) + 1): f for k, f in enumerate(fillers)}
    emitted = [0]

    def each(f):
        for ch in chains:
            f(ch)
        emitted[0] += 1
        if emitted[0] in due:
            due[emitted[0]]()

    def a_blocks(ch):
        m = ch['masks']
        A = lax.dot_general(ch['lhsA'], ch['rhsA'], NT, preferred_element_type=F32)
        ch['A_ub'] = jnp.where(m['strict'], A[:L, :PAIR], 0.0)
        ch['A_uvrv'] = jnp.concatenate([jnp.where(m['strict'], A[:L, PAIR:], 0.0),
                                        jnp.where(m['incl'], A[L:, PAIR:], 0.0)], axis=0).astype(BF16)
        ch['A_rb'] = jnp.where(m['incl'], A[L:, :PAIR], 0.0).astype(BF16)
        a8 = jnp.where(m['blk8'], ch['A_ub'], 0.0)
        ch['A8'] = a8
        ch['X0'] = jnp.where(m['eye'], 1.0, -a8)
        ch['Nlow'] = ch['A_ub'] - a8
    each(a_blocks)

    stack = lambda a, b: jnp.concatenate([a.astype(BF16), b.astype(BF16)], axis=0)

    def sq1(ch):
        ch['A8_2'] = _mm(ch['A8'], bd(ch['A8']))
    each(sq1)

    def sq2(ch):
        r = _mm(stack(ch['A8_2'], ch['X0']), bd(ch['A8_2']))
        ch['A8_4'] = r[:L]
        ch['X1'] = ch['X0'] + r[L:]
        ch['AV'] = _mm(ch['A_uvrv'], ch['vbd'])
    each(sq2)

    def dinv(ch):
        ch['T8'] = ch['X1'] + _mm(ch['X1'], bd(ch['A8_4']))
    each(dinv)

    def e1(ch):
        ch['E'] = _mm(ch['T8'], bd(ch['Nlow']))
    each(e1)

    def e2(ch):
        ch['E2'] = _mm(ch['E'], bd(ch['E']))
    each(e2)

    def e3(ch):
        ime = jnp.where(ch['masks']['eye'], 1.0, 0.0) - ch['E']
        r = _mm(stack(ch['E2'], ime), bd(ch['E2']))
        ch['E4'] = r[:L]
        ch['Y1'] = ime + r[L:]
    each(e3)

    def e4(ch):
        ch['Y2'] = ch['Y1'] + _mm(ch['Y1'], bd(ch['E4']))
    each(e4)

    def t_stage(ch):
        ch['T'] = _mm(ch['Y2'], bd(ch['T8']))
    each(t_stage)

    def p_stage(ch):
        ch['P'] = _mm(ch['A_rb'], bd(ch['T']))
    each(p_stage)

    def q_stage(ch):
        rhs = jnp.concatenate([ch['khdbd'], bd(ch['AV'][:L])], axis=1)
        r = _mm(stack(ch['T'], ch['P']), rhs)
        ch['Q'] = r[:L].astype(BF16)
        ch['G1'] = ch['rd'] - r[L:, :PAIR]
        ch['G2'] = ch['AV'][L:] - r[L:, PAIR:]
    each(q_stage)

    def mn_stage(ch):
        QB = _mm(ch['Q'], ch['be'], TN)
        ch['M'] = (jnp.where(eye2, ch['etot'], 0.0) - jnp.where(same, QB[:PAIR], 0.0)).astype(BF16)
        ch['N'] = lp(_mm(ch['vb'], ch['kte'], TN)) - lp(QB[PAIR:])
    each(mn_stage)

    def out_stage(ch):
        ch['y'] = _mm(ch['G1'], bd(ch['S']), NT) + ch['G2']
        ch['Sn'] = _mm(ch['S'], ch['M']) + ch['N']
    each(out_stage)


def _wkv_kernel(rkv0, khr0, lor0, rkv1, khr1, lor1,
                ka_ref, rk_ref, w0_ref, wup_ref, a0_ref, aup_ref, s0_ref,
                *rest, zero_state, nb, nc, op_names):
    n_ops = len(op_names)
    y0_ref, y1_ref, bo0_ref, bo1_ref, sfin_ref, s_scr = rest[-(6 + 2 * n_ops):len(rest) - 2 * n_ops]
    op_sets = [dict(zip(op_names, rest[len(rest) - (2 - i) * n_ops:len(rest) - (1 - i) * n_ops])) for i in (0, 1)]
    t = pl.program_id(0)
    c = lax.rem(jnp.maximum(t - 1, 0), nc)

    @pl.when(t == 0)
    def _():
        for ref in op_sets[1].values():
            ref[...] = jnp.zeros(ref.shape, ref.dtype)

    @pl.when(c == 0)
    def _():
        if zero_state:
            s_scr[...] = jnp.zeros(s_scr.shape, F32)
        else:
            s_scr[...] = s0_ref[...]

    def step(ops_in, ops_out):
        L = WKV_CHUNK
        left = lax.broadcasted_iota(jnp.int32, (L, PAIR), 1) < HEAD_DIM
        masks = [_wkv_masks(False), _wkv_masks(True)]

        chains = []
        for e in range(nb):
            for d, (y_ref, bo_ref) in enumerate(((y0_ref, bo0_ref), (y1_ref, bo1_ref))):
                bo_ref[e, 0] = ops_in['bonus'][e, d].astype(BF16)
                for p in range(N_PAIRS):
                    sl = slice(p * PAIR, (p + 1) * PAIR)
                    chains.append(dict(
                        e=e, d=d, p=p, sl=sl, y_ref=y_ref, masks=masks[d], S=s_scr[e, d, p],
                        lhsA=ops_in['lhsA'][e, d, p], rhsA=ops_in['rhsA'][e, d, p], vbd=ops_in['vbd'][e, d, p],
                        khdbd=ops_in['khdbd'][e, d, p], rd=ops_in['rd'][e, d, :, sl], be=ops_in['be'][e, d, :, sl],
                        kte=ops_in['kte'][e, d, :, sl], vb=ops_in['vb'][e, d, :, sl],
                        etot=ops_in['etot'][e, d, 0:1, sl]))
        ones_bd = _group_ones()
        ti = lax.broadcasted_iota(jnp.int32, (L, L), 0)
        tj = lax.broadcasted_iota(jnp.int32, (L, L), 1)
        tri = [jnp.where(tj <= ti, 1.0, 0.0).astype(BF16), jnp.where(tj >= ti, 1.0, 0.0).astype(BF16)]
        prepare = [
            functools.partial(_wkv_prepare, e, d, rkv, khr, lor, ka_ref, rk_ref, w0_ref, wup_ref, a0_ref, aup_ref,
                              ops_out, ones_bd, tri, left)
            for e in range(nb) for d, (rkv, khr, lor) in enumerate(((rkv0, khr0, lor0), (rkv1, khr1, lor1)))]
        _wkv_chains(chains, left, prepare)
        for ch in chains:
            ch['y_ref'][ch['e'], 0, :, ch['sl']] = ch['y'].astype(BF16)
            s_scr[ch['e'], ch['d'], ch['p']] = ch['Sn']

    parity = lax.rem(t, 2)

    @pl.when(parity == 0)
    def _():
        step(op_sets[1], op_sets[0])

    @pl.when(parity == 1)
    def _():
        step(op_sets[0], op_sets[1])

    @pl.when((c == nc - 1) & (t > 0))
    def _():
        for e in range(nb):
            for d in (0, 1):
                for p in range(N_PAIRS):
                    s2 = s_scr[e, d, p]
                    sfin_ref[e, d, 2 * p] = s2[:, :HEAD_DIM]
                    sfin_ref[e, d, 2 * p + 1] = pltpu.roll(s2, HEAD_DIM, 1)[:, :HEAD_DIM]


def _wkv(rkv, rest, kh, s0_pairs, n_batch, seq, P, l, s_all):
    L = WKV_CHUNK
    nc = seq // L
    nb = 4
    n_steps = (n_batch // nb) * nc
    n_tok = n_batch * seq
    zero_state = s0_pairs is None
    if zero_state:
        s0_pairs = jnp.zeros((nb, 2, N_PAIRS, HEAD_DIM, PAIR), F32)
    rkv4 = rkv.reshape(n_batch, nc, L, RKV_W)
    rest4 = rest.reshape(n_batch, nc, L, REST_W)
    kh4 = kh.reshape(n_batch, nc, L, RWKV_WIDTH)

    def in_pos(t, d):
        s = jnp.minimum(t, n_steps - 1)
        c = lax.rem(s, nc)
        return lax.div(s, nc), (c if d == 0 else nc - 1 - c)

    def out_pos(t, d):
        s = jnp.maximum(t - 1, 0)
        c = lax.rem(s, nc)
        return lax.div(s, nc), (c if d == 0 else nc - 1 - c)

    in_specs = []
    for d in (0, 1):
        in_specs += [
            pl.BlockSpec((nb, 1, L, RKV_W), lambda t, d=d: (*in_pos(t, d), 0, 0)),
            pl.BlockSpec((nb, 1, L, RWKV_WIDTH), lambda t, d=d: (*in_pos(t, d), 0, 0)),
            pl.BlockSpec((nb, 1, L, 2 * LANES), lambda t, d=d: (*in_pos(t, d), 0, COL_LORA // (2 * LANES))),
        ]
    state_spec = lambda idx: pl.BlockSpec((nb, 2, N_PAIRS, HEAD_DIM, PAIR), idx)
    in_specs += [
        _lspec(l, (1, RWKV_WIDTH)), _lspec(l, (1, RWKV_WIDTH)),
        _lspec(l, (2, RWKV_WIDTH)), _lspec(l, (2, LANES, RWKV_WIDTH)),
        _lspec(l, (2, RWKV_WIDTH)), _lspec(l, (2, LANES, RWKV_WIDTH)),
        state_spec((lambda t: (0, 0, 0, 0, 0)) if zero_state else (lambda t: (out_pos(t, 0)[0], 0, 0, 0, 0))),
    ]
    args = [rkv4, kh4, rest4, rkv4, kh4, rest4, P['k_a'], P['r_k'], P['w0'], P['wup'], P['a0'], P['aup'], s0_pairs]
    aliases = {}
    if s_all is not None:
        in_specs.append(pl.BlockSpec(memory_space=pl.ANY))
        aliases = {len(args): 4}
        args.append(s_all)
    tok_spec = lambda d: pl.BlockSpec((nb, 1, L, RWKV_WIDTH), lambda t, d=d: (*out_pos(t, d), 0, 0))
    sfin_spec = pl.BlockSpec((nb, None, 2, N_RWKV_HEADS, HEAD_DIM, HEAD_DIM),
                             lambda t: (out_pos(t, 0)[0], l, 0, 0, 0, 0))
    out_specs = [tok_spec(0), tok_spec(1), tok_spec(0), tok_spec(1), sfin_spec]
    tok = jax.ShapeDtypeStruct((n_batch, nc, L, RWKV_WIDTH), BF16)
    op_scratch = _wkv_operand_scratch(nb)
    outs = pl.pallas_call(
        functools.partial(_wkv_kernel, zero_state=zero_state, nb=nb, nc=nc, op_names=tuple(op_scratch)),
        grid=(n_steps + 1,),
        in_specs=in_specs,
        out_specs=out_specs,
        out_shape=[tok, tok, tok, tok,
                   jax.ShapeDtypeStruct((n_batch, DEPTH, 2, N_RWKV_HEADS, HEAD_DIM, HEAD_DIM), F32)],
        input_output_aliases=aliases,
        scratch_shapes=([pltpu.VMEM((nb, 2, N_PAIRS, HEAD_DIM, PAIR), F32)]
                        + list(op_scratch.values()) + list(_wkv_operand_scratch(nb).values())),
        compiler_params=pltpu.CompilerParams(
            dimension_semantics=("arbitrary",), vmem_limit_bytes=VMEM_LIMIT_BYTES),
        name="wkv_scan",
    )(*args)
    y0, y1, bo0, bo1, s_fin = outs
    flat = lambda t: t.reshape(n_tok, RWKV_WIDTH)
    return flat(y0), flat(y1), flat(bo0), flat(bo1), s_fin


def _rope(x, cos, sin_signed):
    w = x.shape[1]
    lane = lax.broadcasted_iota(jnp.int32, x.shape, 1)
    partner = jnp.where(_mod_pow2(lane, 2 * ROPE_FREQS) < ROPE_FREQS,
                        pltpu.roll(x, w - ROPE_FREQS, 1), pltpu.roll(x, ROPE_FREQS, 1))
    return x * cos + partner * sin_signed


def _attn_kernel(*refs, tq, seq, past, use_rope, ns):
    it = iter(refs)
    q_ref, kv_ref, qg_ref, kg_ref = next(it), next(it), next(it), next(it)
    if use_rope:
        cq_ref, sq_ref, ck_ref, sk_ref = next(it), next(it), next(it), next(it)
    if past:
        pk_ref, pv_ref = next(it), next(it)
    o_ref, ko_ref, vo_ref, kvar_scr, vvar_scr = refs[-5:]

    ones_bd = _group_ones()

    @pl.when(pl.program_id(1) == 0)
    def _():
        for sq in range(ns):
            kv = kv_ref[sq * seq:(sq + 1) * seq, :]
            k_raw = kv[:, :KV_WIDTH]
            kn = k_raw * lax.rsqrt(_group_sum(k_raw * k_raw, ones_bd) * (1.0 / HEAD_DIM) + NORM_EPS) * kg_ref[...]
            if use_rope:
                kn = _rope(kn, ck_ref[...], sk_ref[...])
            vn = kv[:, KV_WIDTH:]
            ko_ref[sq] = kn
            vo_ref[sq] = vn
            pieces = [(past, seq, kn, vn)]
            if past:
                pieces.append((0, past, pk_ref[sq], pv_ref[sq]))
            for start, n, kx, vx in pieces:
                left = lax.broadcasted_iota(jnp.int32, kx.shape, 1) < HEAD_DIM
                for x, scr in ((kx, kvar_scr), (vx, vvar_scr)):
                    sw = pltpu.roll(x, HEAD_DIM, 1)
                    scr[4 * sq + 0, start:start + n, :] = jnp.where(left, x, 0.0).astype(BF16)
                    scr[4 * sq + 1, start:start + n, :] = jnp.where(left, 0.0, sw).astype(BF16)
                    scr[4 * sq + 2, start:start + n, :] = jnp.where(left, sw, 0.0).astype(BF16)
                    scr[4 * sq + 3, start:start + n, :] = jnp.where(left, 0.0, x).astype(BF16)

    scale = HEAD_DIM ** -0.5
    n_blk = ATTN_WIDTH // LANES
    blocks = [(sq, jb) for sq in range(ns) for jb in range(n_blk)]
    qn = []
    for sq, jb in blocks:
        qb = q_ref[sq * tq:(sq + 1) * tq, jb * LANES:(jb + 1) * LANES]
        x = qb * lax.rsqrt(_group_sum(qb * qb, ones_bd) * (1.0 / HEAD_DIM) + NORM_EPS) * qg_ref[...]
        if use_rope:
            x = _rope(x, cq_ref[...], sq_ref[...])
        qn.append((x * scale).astype(BF16))
    heads = [(i, 4 * sq + 2 * (jb // (n_blk // N_KV_HEADS)) + side)
             for i, (sq, jb) in enumerate(blocks) for side in (0, 1)]
    scores = [lax.dot_general(qn[i], kvar_scr[var], NT, preferred_element_type=F32) for i, var in heads]
    exps = [jnp.exp(s - jnp.max(s, axis=-1, keepdims=True)) for s in scores]
    outs = [lax.dot_general(e.astype(BF16), vvar_scr[var], NN, preferred_element_type=F32)
            for e, (i, var) in zip(exps, heads)]
    outs = [o / jnp.sum(e, axis=-1, keepdims=True) for o, e in zip(outs, exps)]
    for i, (sq, jb) in enumerate(blocks):
        o_ref[sq * tq:(sq + 1) * tq, jb * LANES:(jb + 1) * LANES] = (outs[2 * i] + outs[2 * i + 1]).astype(BF16)


def _attention(rest, n_batch, seq, P, l, rope128, past_kv, kv_all):
    tq = min(seq, 512)
    nq = seq // tq
    ns = 4 if nq == 1 else 1
    n_tok = n_batch * seq
    use_rope = rope128 is not None
    past = 0 if past_kv is None else past_kv[0].shape[2]
    in_specs = [
        pl.BlockSpec((ns * tq, ATTN_WIDTH), lambda b, i: (b * nq + i, COL_Q // ATTN_WIDTH)),
        pl.BlockSpec((ns * seq, 2 * KV_WIDTH), lambda b, i: (b, COL_KV // (2 * KV_WIDTH))),
        _lspec(l, (1, LANES)),
        _lspec(l, (1, LANES)),
    ]
    args = [rest, rest, P['q_gain'], P['k_gain']]
    if use_rope:
        cos, sin = rope128
        in_specs += [pl.BlockSpec((tq, LANES), lambda b, i: (i, 0)),
                     pl.BlockSpec((tq, LANES), lambda b, i: (i, 0)),
                     pl.BlockSpec((seq, LANES), lambda b, i: (0, 0)),
                     pl.BlockSpec((seq, LANES), lambda b, i: (0, 0))]
        args += [cos, sin, cos, sin]
    if past:
        in_specs += [pl.BlockSpec((ns, None, past, KV_WIDTH), lambda b, i: (b, l, 0, 0)),
                     pl.BlockSpec((ns, None, past, KV_WIDTH), lambda b, i: (b, l, 0, 0))]
        args += list(past_kv)
    aliases = {}
    if kv_all is not None:
        in_specs += [pl.BlockSpec(memory_space=pl.ANY), pl.BlockSpec(memory_space=pl.ANY)]
        aliases = {len(args): 1, len(args) + 1: 2}
        args += list(kv_all)
    cache_spec = pl.BlockSpec((ns, None, seq, KV_WIDTH), lambda b, i: (b, l, 0, 0))
    out_specs = [pl.BlockSpec((ns * tq, ATTN_WIDTH), lambda b, i: (b * nq + i, 0)), cache_spec, cache_spec]
    return pl.pallas_call(
        functools.partial(_attn_kernel, tq=tq, seq=seq, past=past, use_rope=use_rope, ns=ns),
        grid=(n_batch // ns, nq),
        in_specs=in_specs,
        out_specs=out_specs,
        out_shape=[jax.ShapeDtypeStruct((n_tok, ATTN_WIDTH), BF16),
                   jax.ShapeDtypeStruct((n_batch, DEPTH, seq, KV_WIDTH), F32),
                   jax.ShapeDtypeStruct((n_batch, DEPTH, seq, KV_WIDTH), F32)],
        input_output_aliases=aliases,
        scratch_shapes=[pltpu.VMEM((ns * 2 * N_KV_HEADS, past + seq, KV_WIDTH), BF16),
                        pltpu.VMEM((ns * 2 * N_KV_HEADS, past + seq, KV_WIDTH), BF16)],
        compiler_params=pltpu.CompilerParams(
            dimension_semantics=("parallel", "arbitrary"), vmem_limit_bytes=VMEM_LIMIT_BYTES),
        name="attention",
    )(*args)


def _post_kernel(y0_ref, y1_ref, bo0_ref, bo1_ref, lor_ref, oa_ref, gr_ref, ga_ref, x_ref, mod_ref,
                 gup_ref, gnw_ref, gnb_ref, wbr_ref, wout_ref, ng_ref, x1_ref, h2_ref):
    ones_bd = _group_ones()
    y = y0_ref[...].astype(F32) + y1_ref[...].astype(F32)
    mean = _group_sum(y, ones_bd) * (1.0 / HEAD_DIM)
    yc = y - mean
    var = _group_sum(yc * yc, ones_bd) * (1.0 / HEAD_DIM)
    yn = yc * lax.rsqrt(var + GN_EPS) * gnw_ref[...] + gnb_ref[...]
    lg = lor_ref[:, LANES:]
    g = _mm(jax.nn.sigmoid(lg), gup_ref[...])
    o_r = (yn + (bo0_ref[...].astype(F32) + bo1_ref[...].astype(F32))) * g
    merged = (jax.nn.sigmoid(gr_ref[...].astype(F32)) * _mm(o_r, wbr_ref[0])
              + jax.nn.sigmoid(ga_ref[...].astype(F32)) * _mm(oa_ref[...], wbr_ref[1]))
    m = _mm(merged, wout_ref[...])
    g1 = mod_ref[0, 2:3, :]
    sh2 = mod_ref[0, 3:4, :]
    sc2 = mod_ref[0, 4:5, :]
    x1 = x_ref[...] + g1 * _rms(m, ng_ref[1:2, :])
    x1_ref[...] = x1
    h2_ref[...] = (_rms(x1, ng_ref[2:3, :]) * (1.0 + sc2) + sh2).astype(BF16)


def _post(y0, y1, bo0, bo1, rest, gates, o_a, x2d, P, l, mod_row0, rows_per_mod):
    n_tok = x2d.shape[0]
    tm = 512
    tok = lambda w: pl.BlockSpec((tm, w), lambda i: (i, 0))
    return pl.pallas_call(
        _post_kernel,
        grid=(n_tok // tm,),
        in_specs=[
            tok(RWKV_WIDTH), tok(RWKV_WIDTH), tok(RWKV_WIDTH), tok(RWKV_WIDTH),
            pl.BlockSpec((tm, 2 * LANES), lambda i: (i, COL_LORA // (2 * LANES))),
            tok(ATTN_WIDTH),
            pl.BlockSpec((tm, D_MODEL), lambda i: (i, 0)),
            pl.BlockSpec((tm, D_MODEL), lambda i: (i, 1)),
            tok(D_MODEL),
            _mod_spec(l, mod_row0, rows_per_mod, tm),
            _lspec(l, (GATE_RANK, RWKV_WIDTH)), _lspec(l, (1, RWKV_WIDTH)), _lspec(l, (1, RWKV_WIDTH)),
            _lspec(l, (2, RWKV_WIDTH, D_MODEL)), _lspec(l, (D_MODEL, D_MODEL)), _lspec(l, (4, D_MODEL)),
        ],
        out_specs=[tok(D_MODEL), tok(D_MODEL)],
        out_shape=[jax.ShapeDtypeStruct((n_tok, D_MODEL), F32), jax.ShapeDtypeStruct((n_tok, D_MODEL), BF16)],
        compiler_params=pltpu.CompilerParams(
            dimension_semantics=("parallel",), vmem_limit_bytes=VMEM_LIMIT_BYTES),
        name="merge_outproj",
    )(y0, y1, bo0, bo1, rest, o_a, gates, gates, x2d, P['mod'], P['gup'], P['gn_w'], P['gn_b'], P['w_br'], P['w_out'],
      P['ng'])


def _mlp_kernel(h_ref, x1_ref, mod_ref, up_ref, down_ref, ng_ref, o_ref):
    h = h_ref[...]
    f = None
    ff = D_MODEL
    for j in range(D_FF // ff):
        u = lax.dot_general(h, up_ref[:, j * ff:(j + 1) * ff], NN, preferred_element_type=F32)
        u = jnp.square(jnp.maximum(u, 0.0)).astype(BF16)
        part = lax.dot_general(u, down_ref[j * ff:(j + 1) * ff, :], NN, preferred_element_type=F32)
        f = part if f is None else f + part
    g2 = mod_ref[0, 5:6, :]
    o_ref[...] = x1_ref[...] + g2 * _rms(f, ng_ref[3:4, :])


def _mlp(h2, x1, P, l, mod_row0, rows_per_mod):
    n_tok = x1.shape[0]
    tm = 1024
    tok = lambda: pl.BlockSpec((tm, D_MODEL), lambda i: (i, 0))
    return pl.pallas_call(
        _mlp_kernel,
        grid=(n_tok // tm,),
        in_specs=[
            tok(), tok(),
            _mod_spec(l, mod_row0, rows_per_mod, tm),
            _lspec(l, (D_MODEL, D_FF), single=True),
            _lspec(l, (D_FF, D_MODEL), single=True),
            _lspec(l, (4, D_MODEL)),
        ],
        out_specs=tok(),
        out_shape=jax.ShapeDtypeStruct((n_tok, D_MODEL), F32),
        compiler_params=pltpu.CompilerParams(
            dimension_semantics=("parallel",), vmem_limit_bytes=VMEM_LIMIT_BYTES),
        name="mlp",
    )(h2, x1, P['mod'], P['up'], P['down'], P['ng'])


def _rope_tables(seq):
    n_rows = seq // GRID_W
    rows = np.repeat(np.arange(n_rows, dtype=np.float32), GRID_W)
    cols = np.tile(np.arange(GRID_W, dtype=np.float32), n_rows)
    half = HEAD_DIM // 2
    freqs = 1.0 / (jnp.asarray(ROPE_THETA, F32) ** (jnp.arange(0, half, 2, dtype=F32) / half))
    ang_r = jnp.asarray(rows)[:, None] * freqs
    ang_c = jnp.asarray(cols)[:, None] * freqs
    cr, sr, cc, sc = jnp.cos(ang_r), jnp.sin(ang_r), jnp.cos(ang_c), jnp.sin(ang_c)
    cos64 = jnp.concatenate([cr, cr, cc, cc], axis=1)
    sin64 = jnp.concatenate([-sr, sr, -sc, sc], axis=1)
    return jnp.tile(cos64, (1, LANES // HEAD_DIM)), jnp.tile(sin64, (1, LANES // HEAD_DIM))


def _pairs_from_state(s):
    b = s.shape[0]
    s = s.reshape(b, 2, N_PAIRS, 2, HEAD_DIM, HEAD_DIM)
    return jnp.concatenate([s[:, :, :, 0], s[:, :, :, 1]], axis=-1)


def _layer(x2d, n_batch, seq, P, l, mod_row0, rows_per_mod, rope128, past_kv, s0, carry):
    kv_all, s_all = (None, None) if carry is None else (carry[:2], carry[2])
    rkv, gates, rest, kh = _in_projection(x2d, P, l, mod_row0, rows_per_mod, seq)
    s0_pairs = None if s0 is None else _pairs_from_state(s0)
    y0, y1, bo0, bo1, s_all = _wkv(rkv, rest, kh, s0_pairs, n_batch, seq, P, l, s_all)
    o_a, k_all, v_all = _attention(rest, n_batch, seq, P, l, rope128, past_kv, kv_all)
    x1, h2 = _post(y0, y1, bo0, bo1, rest, gates, o_a, x2d, P, l, mod_row0, rows_per_mod)
    x2 = _mlp(h2, x1, P, l, mod_row0, rows_per_mod)
    return x2, (k_all, v_all, s_all)


def kernel(x_prompt, x_sample, cache_k, cache_v, state_wkv, c, c_ctx, w_in, w_br, w_out, w_mod, b_mod, norm_g, mlp_up, mlp_down, rwkv_mu, rwkv_k_k, rwkv_k_a, rwkv_r_k, decay_w0, decay_up, iclr_a0, iclr_up, gate_up, gn_w, gn_b, q_gain, k_gain):
    n_ctx, seq_ctx, _ = x_prompt.shape
    n_dec, seq_dec, _ = x_sample.shape
    past = cache_k.shape[2]

    cvec8 = jnp.zeros((SUBLANES, D_MODEL), F32).at[0].set(c_ctx).at[1:1 + n_dec].set(c)
    mod = _modulation(cvec8, w_mod, b_mod)
    mod = jnp.pad(mod.reshape(DEPTH, SUBLANES, N_MOD, D_MODEL), ((0, 0), (0, 0), (0, SUBLANES - N_MOD), (0, 0)))

    o = np.cumsum((0, RWKV_WIDTH, RWKV_WIDTH, RWKV_WIDTH, DECAY_RANK, ICLR_RANK, GATE_RANK,
                   ATTN_WIDTH, KV_WIDTH, KV_WIDTH, 2 * D_MODEL))
    w_rest = jnp.concatenate([w_in[:, :, o[9]:o[10]], w_in[:, :, o[6]:o[7]], w_in[:, :, o[3]:o[6]],
                              w_in[:, :, o[7]:o[9]]], axis=-1)
    zpad = jnp.zeros((DEPTH, 2, DECAY_RANK, RWKV_WIDTH), F32)
    row = lambda a: a.reshape(DEPTH, 1, -1)
    tile2 = lambda a: jnp.tile(a, (1, LANES // HEAD_DIM)).reshape(DEPTH, 1, LANES)
    P = dict(
        mod=mod, ng=norm_g, w_rkv=w_in[:, :, o[0]:o[3]].astype(BF16), w_rest=w_rest.astype(BF16),
        mu=rwkv_mu, k_k=row(rwkv_k_k), k_a=row(rwkv_k_a), r_k=row(rwkv_r_k),
        w0=decay_w0, wup=jnp.concatenate([decay_up, zpad], axis=2).astype(BF16),
        a0=iclr_a0, aup=jnp.concatenate([zpad, iclr_up], axis=2).astype(BF16),
        gup=gate_up.astype(BF16), gn_w=row(gn_w), gn_b=row(gn_b),
        w_br=w_br.astype(BF16), w_out=w_out.astype(BF16), up=mlp_up.astype(BF16), down=mlp_down.astype(BF16),
        q_gain=tile2(q_gain), k_gain=tile2(k_gain),
    )

    x = x_prompt.reshape(n_ctx * seq_ctx, D_MODEL)
    carry = None
    for l in range(DEPTH):
        x, carry = _layer(x, n_ctx, seq_ctx, P, l, 0, 0, None, None, None, carry)
    y_prompt = x.reshape(n_ctx, seq_ctx, D_MODEL)
    new_k, new_v, new_s = carry
    cache_shape = (n_ctx, DEPTH, seq_ctx, N_KV_HEADS, HEAD_DIM)

    rope128 = _rope_tables(seq_dec)
    past_kv = (cache_k.reshape(n_dec, DEPTH, past, KV_WIDTH), cache_v.reshape(n_dec, DEPTH, past, KV_WIDTH))
    x = x_sample.reshape(n_dec * seq_dec, D_MODEL)
    for l in range(DEPTH):
        x, _ = _layer(x, n_dec, seq_dec, P, l, 1, seq_dec, rope128, past_kv, state_wkv[:, l], None)
    y_sample = x.reshape(n_dec, seq_dec, D_MODEL)

    return (y_prompt, y_sample, new_k.reshape(cache_shape), new_v.reshape(cache_shape), new_s)
```

```python
import functools

import numpy as np
import jax
import jax.numpy as jnp
from jax import lax
from jax.experimental import pallas as pl
from jax.experimental.pallas import tpu as pltpu

F32 = jnp.float32
BF16 = jnp.bfloat16

D_MODEL = 1024
DEPTH = 2
GRID_W = 64
HEAD_DIM = 64
N_RWKV_HEADS = 8
RWKV_WIDTH = N_RWKV_HEADS * HEAD_DIM
N_Q_HEADS = 8
N_KV_HEADS = 2
ATTN_WIDTH = N_Q_HEADS * HEAD_DIM
KV_WIDTH = N_KV_HEADS * HEAD_DIM
DECAY_RANK = 64
ICLR_RANK = 64
GATE_RANK = 128
D_FF = 4 * D_MODEL
ROPE_THETA = 10000.0
ROPE_FREQS = HEAD_DIM // 4
N_MOD = 6
NORM_EPS = 1e-6
GN_EPS = 64e-5
DECAY_SCALE = 0.606531
D_IN = 3 * RWKV_WIDTH + DECAY_RANK + ICLR_RANK + GATE_RANK + ATTN_WIDTH + 2 * KV_WIDTH + 2 * D_MODEL

LANES = 128
SUBLANES = 8
VMEM_LIMIT_BYTES = 56 * 1024 * 1024

RKV_W = 3 * RWKV_WIDTH
GATES_W = 2 * D_MODEL
COL_Q = 0
COL_LORA = COL_Q + ATTN_WIDTH
COL_KV = COL_LORA + 2 * LANES
REST_W = COL_KV + 2 * KV_WIDTH

WKV_CHUNK = 64
PAIR = 2 * HEAD_DIM
N_PAIRS = N_RWKV_HEADS // 2

NN = (((1,), (0,)), ((), ()))
NT = (((1,), (1,)), ((), ()))
TN = (((0,), (0,)), ((), ()))


def _mm(a, b, dims=NN):
    return lax.dot_general(a.astype(BF16), b.astype(BF16), dims, preferred_element_type=F32)


def _split2(x):
    hi = x.astype(BF16)
    lo = (x - hi.astype(F32)).astype(BF16)
    return hi, lo


def _mm_exact_rhs(x, m_bf16):
    hi, lo = _split2(x)
    d = lambda p: lax.dot_general(p, m_bf16, NN, preferred_element_type=F32)
    return d(hi) + d(lo)


def _div_pow2(i, n):
    return lax.shift_right_logical(i, int(np.log2(n)))


def _mod_pow2(i, n):
    return lax.bitwise_and(i, n - 1)


def _group_ones():
    i = lax.broadcasted_iota(jnp.int32, (LANES, LANES), 0)
    j = lax.broadcasted_iota(jnp.int32, (LANES, LANES), 1)
    return jnp.where(_div_pow2(i, HEAD_DIM) == _div_pow2(j, HEAD_DIM), 1.0, 0.0).astype(BF16)


def _group_sum(x, ones_bd):
    blocks = [
        _mm_exact_rhs(x[:, i * LANES:(i + 1) * LANES], ones_bd)
        for i in range(x.shape[1] // LANES)
    ]
    return blocks[0] if len(blocks) == 1 else jnp.concatenate(blocks, axis=1)


def _lspec(l, shape, single=False):
    kw = dict(pipeline_mode=pl.Buffered(1)) if single else {}
    return pl.BlockSpec((None,) + tuple(shape), lambda *g: (l,) + (0,) * len(shape), **kw)


def _mod_spec(l, row0, rows_per_mod, tm):
    if rows_per_mod:
        return pl.BlockSpec((None, 1, SUBLANES, D_MODEL), lambda i, *g: (l, row0 + i * tm // rows_per_mod, 0, 0))
    return pl.BlockSpec((None, 1, SUBLANES, D_MODEL), lambda *g: (l, row0, 0, 0))


def _rms(x, g):
    return x * lax.rsqrt(jnp.mean(x * x, axis=-1, keepdims=True) + NORM_EPS) * g


def _mod_kernel(c_ref, w_ref, b_ref, o_ref):
    c = c_ref[...]
    s = c * jax.nn.sigmoid(c)
    o_ref[0] = _mm(s, w_ref[0]) + b_ref[0]


def _modulation(cvec8, w_mod, b_mod):
    tn = D_MODEL
    n = N_MOD * D_MODEL
    return pl.pallas_call(
        _mod_kernel,
        grid=(DEPTH, n // tn),
        in_specs=[
            pl.BlockSpec((SUBLANES, D_MODEL), lambda l, j: (0, 0)),
            pl.BlockSpec((1, D_MODEL, tn), lambda l, j: (l, 0, j)),
            pl.BlockSpec((1, 1, tn), lambda l, j: (l, 0, j)),
        ],
        out_specs=pl.BlockSpec((1, SUBLANES, tn), lambda l, j: (l, 0, j)),
        out_shape=jax.ShapeDtypeStruct((DEPTH, SUBLANES, n), F32),
        name="modulation",
    )(cvec8, w_mod, b_mod.reshape(DEPTH, 1, n))


def _shift_mix(x, mu):
    n = x.shape[0]
    row = lax.broadcasted_iota(jnp.int32, x.shape, 0)
    x_prev = jnp.where(row == 0, 0.0, pltpu.roll(x, 1, 0))
    x_next = jnp.where(row == n - 1, 0.0, pltpu.roll(x, n - 1, 0))
    return x * (1.0 - mu) + (0.5 * mu) * (x_prev + x_next)


def _inproj_kernel(x_ref, mod_ref, ng_ref, wrkv_ref, wrest_ref, mu_ref, kk_ref,
                   rkv_ref, gates_ref, rest_ref, kh_ref, h_scr, *, tm, seq):
    C = RWKV_WIDTH

    def mix_in_place(i, piece):
        lanes = slice(piece * LANES, (piece + 1) * LANES)
        cols = slice(i * C + piece * LANES, i * C + (piece + 1) * LANES)
        for s in range(tm // seq):
            rows = slice(s * seq, (s + 1) * seq)
            mixed = _shift_mix(rkv_ref[rows, cols], mu_ref[i:i + 1, lanes])
            rkv_ref[rows, cols] = mixed
            if i == 1:
                kk = mixed * kk_ref[:, lanes]
                kh_ref[rows, lanes] = kk * lax.rsqrt(_group_sum(kk * kk, _group_ones()) + 1e-12)

    sh = mod_ref[0, 0:1, :]
    sc = mod_ref[0, 1:2, :]
    ng = ng_ref[0:1, :]
    sub = 256
    for i in range(tm // sub):
        x = x_ref[i * sub:(i + 1) * sub, :]
        h_scr[i * sub:(i + 1) * sub, :] = (_rms(x, ng) * (1.0 + sc) + sh).astype(BF16)
    rkv_ref[...] = lax.dot_general(h_scr[...], wrkv_ref[...], NN, preferred_element_type=F32)

    tn = 2 * LANES
    n_piece = C // LANES
    for chunk in range((GATES_W + REST_W) // tn):
        cols = slice(chunk * tn, (chunk + 1) * tn)
        out = lax.dot_general(h_scr[...], wrest_ref[:, cols], NN, preferred_element_type=F32)
        if chunk * tn < GATES_W:
            gates_ref[:, cols] = out.astype(BF16)
        else:
            rest_ref[:, chunk * tn - GATES_W:(chunk + 1) * tn - GATES_W] = out
        mix_in_place(chunk // n_piece, chunk % n_piece)


def _in_projection(x2d, P, l, mod_row0, rows_per_mod, seq):
    n_tok = x2d.shape[0]
    tm = 1024
    assert tm % seq == 0 and (GATES_W + REST_W) // (2 * LANES) == 3 * (RWKV_WIDTH // LANES)
    tok = lambda w: pl.BlockSpec((tm, w), lambda i: (i, 0))
    return pl.pallas_call(
        functools.partial(_inproj_kernel, tm=tm, seq=seq),
        grid=(n_tok // tm,),
        in_specs=[
            tok(D_MODEL),
            _mod_spec(l, mod_row0, rows_per_mod, tm),
            _lspec(l, (4, D_MODEL)),
            _lspec(l, (D_MODEL, RKV_W), single=True),
            _lspec(l, (D_MODEL, GATES_W + REST_W), single=True),
            _lspec(l, (3, RWKV_WIDTH)),
            _lspec(l, (1, RWKV_WIDTH)),
        ],
        out_specs=[tok(RKV_W), tok(GATES_W), tok(REST_W), tok(RWKV_WIDTH)],
        out_shape=[jax.ShapeDtypeStruct((n_tok, RKV_W), F32),
                   jax.ShapeDtypeStruct((n_tok, GATES_W), BF16),
                   jax.ShapeDtypeStruct((n_tok, REST_W), F32),
                   jax.ShapeDtypeStruct((n_tok, RWKV_WIDTH), F32)],
        scratch_shapes=[pltpu.VMEM((tm, D_MODEL), BF16)],
        compiler_params=pltpu.CompilerParams(
            dimension_semantics=("parallel",), vmem_limit_bytes=VMEM_LIMIT_BYTES),
        name="in_projection",
    )(x2d, P['mod'], P['ng'], P['w_rkv'], P['w_rest'], P['mu'], P['k_k'])


def _wkv_masks(reverse):
    L = WKV_CHUNK
    row = lax.broadcasted_iota(jnp.int32, (L, PAIR), 0)
    col = _mod_pow2(lax.broadcasted_iota(jnp.int32, (L, PAIR), 1), L)
    return dict(
        strict=(col > row) if reverse else (col < row),
        incl=(col >= row) if reverse else (col <= row),
        blk8=_div_pow2(row, 8) == _div_pow2(col, 8),
        eye=row == col,
    )


def _bd(x, left):
    x = x.astype(BF16)
    z = jnp.zeros_like(x)
    return jnp.concatenate([jnp.where(left, x, z), jnp.where(left, z, x)], axis=0)


def _wkv_operand_scratch(nb):
    C, L = RWKV_WIDTH, WKV_CHUNK
    full = lambda dt: pltpu.VMEM((nb, 2, L, C), dt)
    pair = lambda rows: pltpu.VMEM((nb, 2, N_PAIRS, rows, PAIR), BF16)
    return dict(khd=full(F32), rd=full(F32), bonus=full(F32), etot=pltpu.VMEM((nb, 2, SUBLANES, C), F32),
                be=full(BF16), kte=full(BF16), vb=full(BF16),
                lhsA=pair(2 * L), rhsA=pair(4 * L), vbd=pair(2 * L), khdbd=pair(2 * L))


def _wkv_prepare(e, d, rkv, khr, lor, ka_ref, rk_ref, w0_ref, wup_ref, a0_ref, aup_ref, ops, ones_bd, tri, left):
    C, L = RWKV_WIDTH, WKV_CHUNK
    reverse = d == 1
    r = rkv[e, 0, :, 0:C]
    k = rkv[e, 0, :, C:2 * C]
    v = rkv[e, 0, :, 2 * C:3 * C]
    kh = khr[e, 0]
    lo_dec = lor[e, 0, :, 0:LANES]
    w_raw = _mm(jnp.tanh(lo_dec), wup_ref[d]) + w0_ref[d:d + 1, :]
    lw = -DECAY_SCALE * jax.nn.sigmoid(w_raw)
    a = jax.nn.sigmoid(_mm(lo_dec, aup_ref[d]) + a0_ref[d:d + 1, :])
    kt = k * (1.0 + (a - 1.0) * ka_ref[...])
    ops['bonus'][e, d] = _group_sum(r * kt * rk_ref[...], ones_bd) * v
    b = a * kh

    hi, lo2 = _split2(lw)
    cum = (lax.dot_general(tri[d], hi, NN, preferred_element_type=F32)
           + lax.dot_general(tri[d], lo2, NN, preferred_element_type=F32))
    tot = cum[0:1, :] if reverse else cum[L - 1:L, :]
    e_inv = jnp.exp(-cum)
    e_end = jnp.exp(tot - cum)
    khd = kh * jnp.exp(cum - lw)
    rd = r * jnp.exp(cum)
    bi = b * e_inv
    ki = kt * e_inv
    ops['khd'][e, d] = khd
    ops['rd'][e, d] = rd
    ops['etot'][e, d] = jnp.broadcast_to(jnp.exp(tot), (SUBLANES, C))
    ops['be'][e, d] = (b * e_end).astype(BF16)
    ops['kte'][e, d] = (kt * e_end).astype(BF16)
    ops['vb'][e, d] = v.astype(BF16)
    for p in range(N_PAIRS):
        sl = slice(p * PAIR, (p + 1) * PAIR)
        ops['lhsA'][e, d, p] = jnp.concatenate([khd[:, sl], rd[:, sl]], axis=0).astype(BF16)
        ops['rhsA'][e, d, p] = jnp.concatenate([_bd(bi[:, sl], left), _bd(ki[:, sl], left)], axis=0)
        ops['vbd'][e, d, p] = _bd(v[:, sl], left)
        ops['khdbd'][e, d, p] = _bd(khd[:, sl], left)


def _wkv_chains(chains, left, fillers):
    L = WKV_CHUNK
    i2 = lax.broadcasted_iota(jnp.int32, (PAIR, PAIR), 0)
    j2 = lax.broadcasted_iota(jnp.int32, (PAIR, PAIR), 1)
    same = _div_pow2(i2, HEAD_DIM) == _div_pow2(j2, HEAD_DIM)
    eye2 = i2 == j2
    bd = lambda x: _bd(x, left)
    lp = lambda g: jnp.where(left, g[:L], g[L:])
    n_stages = 13
    due = {((k + 1) * n_stages) // (len(fillers) + 1): f for k, f in enumerate(fillers)}
    emitted = [0]

    def each(f):
        for ch in chains:
            f(ch)
        emitted[0] += 1
        if emitted[0] in due:
            due[emitted[0]]()

    def a_blocks(ch):
        m = ch['masks']
        A = lax.dot_general(ch['lhsA'], ch['rhsA'], NT, preferred_element_type=F32)
        ch['A_ub'] = jnp.where(m['strict'], A[:L, :PAIR], 0.0)
        ch['A_uvrv'] = jnp.concatenate([jnp.where(m['strict'], A[:L, PAIR:], 0.0),
                                        jnp.where(m['incl'], A[L:, PAIR:], 0.0)], axis=0).astype(BF16)
        ch['A_rb'] = jnp.where(m['incl'], A[L:, :PAIR], 0.0).astype(BF16)
        a8 = jnp.where(m['blk8'], ch['A_ub'], 0.0)
        ch['A8'] = a8
        ch['X0'] = jnp.where(m['eye'], 1.0, -a8)
        ch['Nlow'] = ch['A_ub'] - a8
    each(a_blocks)

    stack = lambda a, b: jnp.concatenate([a.astype(BF16), b.astype(BF16)], axis=0)

    def sq1(ch):
        ch['A8_2'] = _mm(ch['A8'], bd(ch['A8']))
    each(sq1)

    def sq2(ch):
        r = _mm(stack(ch['A8_2'], ch['X0']), bd(ch['A8_2']))
        ch['A8_4'] = r[:L]
        ch['X1'] = ch['X0'] + r[L:]
        ch['AV'] = _mm(ch['A_uvrv'], ch['vbd'])
    each(sq2)

    def dinv(ch):
        ch['T8'] = ch['X1'] + _mm(ch['X1'], bd(ch['A8_4']))
    each(dinv)

    def e1(ch):
        ch['E'] = _mm(ch['T8'], bd(ch['Nlow']))
    each(e1)

    def e2(ch):
        ch['E2'] = _mm(ch['E'], bd(ch['E']))
    each(e2)

    def e3(ch):
        ime = jnp.where(ch['masks']['eye'], 1.0, 0.0) - ch['E']
        r = _mm(stack(ch['E2'], ime), bd(ch['E2']))
        ch['E4'] = r[:L]
        ch['Y1'] = ime + r[L:]
    each(e3)

    def e4(ch):
        ch['Y2'] = ch['Y1'] + _mm(ch['Y1'], bd(ch['E4']))
    each(e4)

    def t_stage(ch):
        ch['T'] = _mm(ch['Y2'], bd(ch['T8']))
    each(t_stage)

    def p_stage(ch):
        ch['P'] = _mm(ch['A_rb'], bd(ch['T']))
    each(p_stage)

    def q_stage(ch):
        rhs = jnp.concatenate([ch['khdbd'], bd(ch['AV'][:L])], axis=1)
        r = _mm(stack(ch['T'], ch['P']), rhs)
        ch['Q'] = r[:L].astype(BF16)
        ch['G1'] = ch['rd'] - r[L:, :PAIR]
        ch['G2'] = ch['AV'][L:] - r[L:, PAIR:]
    each(q_stage)

    def mn_stage(ch):
        QB = _mm(ch['Q'], ch['be'], TN)
        ch['M'] = (jnp.where(eye2, ch['etot'], 0.0) - jnp.where(same, QB[:PAIR], 0.0)).astype(BF16)
        ch['N'] = lp(_mm(ch['vb'], ch['kte'], TN)) - lp(QB[PAIR:])
    each(mn_stage)

    def out_stage(ch):
        ch['y'] = _mm(ch['G1'], bd(ch['S']), NT) + ch['G2']
        ch['Sn'] = _mm(ch['S'], ch['M']) + ch['N']
    each(out_stage)


def _wkv_kernel(rkv0, khr0, lor0, rkv1, khr1, lor1,
                ka_ref, rk_ref, w0_ref, wup_ref, a0_ref, aup_ref, s0_ref,
                *rest, zero_state, nb, nc, op_names):
    n_ops = len(op_names)
    y0_ref, y1_ref, bo0_ref, bo1_ref, sfin_ref, s_scr = rest[-(6 + 2 * n_ops):len(rest) - 2 * n_ops]
    op_sets = [dict(zip(op_names, rest[len(rest) - (2 - i) * n_ops:len(rest) - (1 - i) * n_ops])) for i in (0, 1)]
    t = pl.program_id(0)
    c = lax.rem(jnp.maximum(t - 1, 0), nc)

    @pl.when(t == 0)
    def _():
        for ref in op_sets[1].values():
            ref[...] = jnp.zeros(ref.shape, ref.dtype)

    @pl.when(c == 0)
    def _():
        if zero_state:
            s_scr[...] = jnp.zeros(s_scr.shape, F32)
        else:
            s_scr[...] = s0_ref[...]

    def step(ops_in, ops_out):
        L = WKV_CHUNK
        left = lax.broadcasted_iota(jnp.int32, (L, PAIR), 1) < HEAD_DIM
        masks = [_wkv_masks(False), _wkv_masks(True)]

        chains = []
        for e in range(nb):
            for d, (y_ref, bo_ref) in enumerate(((y0_ref, bo0_ref), (y1_ref, bo1_ref))):
                bo_ref[e, 0] = ops_in['bonus'][e, d].astype(BF16)
                for p in range(N_PAIRS):
                    sl = slice(p * PAIR, (p + 1) * PAIR)
                    chains.append(dict(
                        e=e, d=d, p=p, sl=sl, y_ref=y_ref, masks=masks[d], S=s_scr[e, d, p],
                        lhsA=ops_in['lhsA'][e, d, p], rhsA=ops_in['rhsA'][e, d, p], vbd=ops_in['vbd'][e, d, p],
                        khdbd=ops_in['khdbd'][e, d, p], rd=ops_in['rd'][e, d, :, sl], be=ops_in['be'][e, d, :, sl],
                        kte=ops_in['kte'][e, d, :, sl], vb=ops_in['vb'][e, d, :, sl],
                        etot=ops_in['etot'][e, d, 0:1, sl]))
        ones_bd = _group_ones()
        ti = lax.broadcasted_iota(jnp.int32, (L, L), 0)
        tj = lax.broadcasted_iota(jnp.int32, (L, L), 1)
        tri = [jnp.where(tj <= ti, 1.0, 0.0).astype(BF16), jnp.where(tj >= ti, 1.0, 0.0).astype(BF16)]
        prepare = [
            functools.partial(_wkv_prepare, e, d, rkv, khr, lor, ka_ref, rk_ref, w0_ref, wup_ref, a0_ref, aup_ref,
                              ops_out, ones_bd, tri, left)
            for e in range(nb) for d, (rkv, khr, lor) in enumerate(((rkv0, khr0, lor0), (rkv1, khr1, lor1)))]
        _wkv_chains(chains, left, prepare)
        for ch in chains:
            ch['y_ref'][ch['e'], 0, :, ch['sl']] = ch['y'].astype(BF16)
            s_scr[ch['e'], ch['d'], ch['p']] = ch['Sn']

    parity = lax.rem(t, 2)

    @pl.when(parity == 0)
    def _():
        step(op_sets[1], op_sets[0])

    @pl.when(parity == 1)
    def _():
        step(op_sets[0], op_sets[1])

    @pl.when((c == nc - 1) & (t > 0))
    def _():
        for e in range(nb):
            for d in (0, 1):
                for p in range(N_PAIRS):
                    s2 = s_scr[e, d, p]
                    sfin_ref[e, d, 2 * p] = s2[:, :HEAD_DIM]
                    sfin_ref[e, d, 2 * p + 1] = pltpu.roll(s2, HEAD_DIM, 1)[:, :HEAD_DIM]


def _wkv(rkv, rest, kh, s0_pairs, n_batch, seq, P, l, s_all):
    L = WKV_CHUNK
    nc = seq // L
    nb = 4
    n_steps = (n_batch // nb) * nc
    n_tok = n_batch * seq
    zero_state = s0_pairs is None
    if zero_state:
        s0_pairs = jnp.zeros((nb, 2, N_PAIRS, HEAD_DIM, PAIR), F32)
    rkv4 = rkv.reshape(n_batch, nc, L, RKV_W)
    rest4 = rest.reshape(n_batch, nc, L, REST_W)
    kh4 = kh.reshape(n_batch, nc, L, RWKV_WIDTH)

    def in_pos(t, d):
        s = jnp.minimum(t, n_steps - 1)
        c = lax.rem(s, nc)
        return lax.div(s, nc), (c if d == 0 else nc - 1 - c)

    def out_pos(t, d):
        s = jnp.maximum(t - 1, 0)
        c = lax.rem(s, nc)
        return lax.div(s, nc), (c if d == 0 else nc - 1 - c)

    in_specs = []
    for d in (0, 1):
        in_specs += [
            pl.BlockSpec((nb, 1, L, RKV_W), lambda t, d=d: (*in_pos(t, d), 0, 0)),
            pl.BlockSpec((nb, 1, L, RWKV_WIDTH), lambda t, d=d: (*in_pos(t, d), 0, 0)),
            pl.BlockSpec((nb, 1, L, 2 * LANES), lambda t, d=d: (*in_pos(t, d), 0, COL_LORA // (2 * LANES))),
        ]
    state_spec = lambda idx: pl.BlockSpec((nb, 2, N_PAIRS, HEAD_DIM, PAIR), idx)
    in_specs += [
        _lspec(l, (1, RWKV_WIDTH)), _lspec(l, (1, RWKV_WIDTH)),
        _lspec(l, (2, RWKV_WIDTH)), _lspec(l, (2, LANES, RWKV_WIDTH)),
        _lspec(l, (2, RWKV_WIDTH)), _lspec(l, (2, LANES, RWKV_WIDTH)),
        state_spec((lambda t: (0, 0, 0, 0, 0)) if zero_state else (lambda t: (out_pos(t, 0)[0], 0, 0, 0, 0))),
    ]
    args = [rkv4, kh4, rest4, rkv4, kh4, rest4, P['k_a'], P['r_k'], P['w0'], P['wup'], P['a0'], P['aup'], s0_pairs]
    aliases = {}
    if s_all is not None:
        in_specs.append(pl.BlockSpec(memory_space=pl.ANY))
        aliases = {len(args): 4}
        args.append(s_all)
    tok_spec = lambda d: pl.BlockSpec((nb, 1, L, RWKV_WIDTH), lambda t, d=d: (*out_pos(t, d), 0, 0))
    sfin_spec = pl.BlockSpec((nb, None, 2, N_RWKV_HEADS, HEAD_DIM, HEAD_DIM),
                             lambda t: (out_pos(t, 0)[0], l, 0, 0, 0, 0))
    out_specs = [tok_spec(0), tok_spec(1), tok_spec(0), tok_spec(1), sfin_spec]
    tok = jax.ShapeDtypeStruct((n_batch, nc, L, RWKV_WIDTH), BF16)
    op_scratch = _wkv_operand_scratch(nb)
    outs = pl.pallas_call(
        functools.partial(_wkv_kernel, zero_state=zero_state, nb=nb, nc=nc, op_names=tuple(op_scratch)),
        grid=(n_steps + 1,),
        in_specs=in_specs,
        out_specs=out_specs,
        out_shape=[tok, tok, tok, tok,
                   jax.ShapeDtypeStruct((n_batch, DEPTH, 2, N_RWKV_HEADS, HEAD_DIM, HEAD_DIM), F32)],
        input_output_aliases=aliases,
        scratch_shapes=([pltpu.VMEM((nb, 2, N_PAIRS, HEAD_DIM, PAIR), F32)]
                        + list(op_scratch.values()) + list(_wkv_operand_scratch(nb).values())),
        compiler_params=pltpu.CompilerParams(
            dimension_semantics=("arbitrary",), vmem_limit_bytes=VMEM_LIMIT_BYTES),
        name="wkv_scan",
    )(*args)
    y0, y1, bo0, bo1, s_fin = outs
    flat = lambda t: t.reshape(n_tok, RWKV_WIDTH)
    return flat(y0), flat(y1), flat(bo0), flat(bo1), s_fin


def _rope(x, cos, sin_signed):
    w = x.shape[1]
    lane = lax.broadcasted_iota(jnp.int32, x.shape, 1)
    partner = jnp.where(_mod_pow2(lane, 2 * ROPE_FREQS) < ROPE_FREQS,
                        pltpu.roll(x, w - ROPE_FREQS, 1), pltpu.roll(x, ROPE_FREQS, 1))
    return x * cos + partner * sin_signed


def _attn_kernel(*refs, tq, seq, past, use_rope, ns):
    it = iter(refs)
    q_ref, kv_ref, qg_ref, kg_ref = next(it), next(it), next(it), next(it)
    if use_rope:
        cq_ref, sq_ref, ck_ref, sk_ref = next(it), next(it), next(it), next(it)
    if past:
        pk_ref, pv_ref = next(it), next(it)
    o_ref, ko_ref, vo_ref, kvar_scr, vvar_scr = refs[-5:]

    ones_bd = _group_ones()

    @pl.when(pl.program_id(1) == 0)
    def _():
        for sq in range(ns):
            kv = kv_ref[sq * seq:(sq + 1) * seq, :]
            k_raw = kv[:, :KV_WIDTH]
            kn = k_raw * lax.rsqrt(_group_sum(k_raw * k_raw, ones_bd) * (1.0 / HEAD_DIM) + NORM_EPS) * kg_ref[...]
            if use_rope:
                kn = _rope(kn, ck_ref[...], sk_ref[...])
            vn = kv[:, KV_WIDTH:]
            ko_ref[sq] = kn
            vo_ref[sq] = vn
            pieces = [(past, seq, kn, vn)]
            if past:
                pieces.append((0, past, pk_ref[sq], pv_ref[sq]))
            for start, n, kx, vx in pieces:
                left = lax.broadcasted_iota(jnp.int32, kx.shape, 1) < HEAD_DIM
                for x, scr in ((kx, kvar_scr), (vx, vvar_scr)):
                    sw = pltpu.roll(x, HEAD_DIM, 1)
                    scr[4 * sq + 0, start:start + n, :] = jnp.where(left, x, 0.0).astype(BF16)
                    scr[4 * sq + 1, start:start + n, :] = jnp.where(left, 0.0, sw).astype(BF16)
                    scr[4 * sq + 2, start:start + n, :] = jnp.where(left, sw, 0.0).astype(BF16)
                    scr[4 * sq + 3, start:start + n, :] = jnp.where(left, 0.0, x).astype(BF16)

    scale = HEAD_DIM ** -0.5
    n_blk = ATTN_WIDTH // LANES
    blocks = [(sq, jb) for sq in range(ns) for jb in range(n_blk)]
    qn = []
    for sq, jb in blocks:
        qb = q_ref[sq * tq:(sq + 1) * tq, jb * LANES:(jb + 1) * LANES]
        x = qb * lax.rsqrt(_group_sum(qb * qb, ones_bd) * (1.0 / HEAD_DIM) + NORM_EPS) * qg_ref[...]
        if use_rope:
            x = _rope(x, cq_ref[...], sq_ref[...])
        qn.append((x * scale).astype(BF16))
    heads = [(i, 4 * sq + 2 * (jb // (n_blk // N_KV_HEADS)) + side)
             for i, (sq, jb) in enumerate(blocks) for side in (0, 1)]
    scores = [lax.dot_general(qn[i], kvar_scr[var], NT, preferred_element_type=F32) for i, var in heads]
    exps = [jnp.exp(s - jnp.max(s, axis=-1, keepdims=True)) for s in scores]
    outs = [lax.dot_general(e.astype(BF16), vvar_scr[var], NN, preferred_element_type=F32)
            for e, (i, var) in zip(exps, heads)]
    outs = [o / jnp.sum(e, axis=-1, keepdims=True) for o, e in zip(outs, exps)]
    for i, (sq, jb) in enumerate(blocks):
        o_ref[sq * tq:(sq + 1) * tq, jb * LANES:(jb + 1) * LANES] = (outs[2 * i] + outs[2 * i + 1]).astype(BF16)


def _attention(rest, n_batch, seq, P, l, rope128, past_kv, kv_all):
    tq = min(seq, 512)
    nq = seq // tq
    ns = 4 if nq == 1 else 1
    n_tok = n_batch * seq
    use_rope = rope128 is not None
    past = 0 if past_kv is None else past_kv[0].shape[2]
    in_specs = [
        pl.BlockSpec((ns * tq, ATTN_WIDTH), lambda b, i: (b * nq + i, COL_Q // ATTN_WIDTH)),
        pl.BlockSpec((ns * seq, 2 * KV_WIDTH), lambda b, i: (b, COL_KV // (2 * KV_WIDTH))),
        _lspec(l, (1, LANES)),
        _lspec(l, (1, LANES)),
    ]
    args = [rest, rest, P['q_gain'], P['k_gain']]
    if use_rope:
        cos, sin = rope128
        in_specs += [pl.BlockSpec((tq, LANES), lambda b, i: (i, 0)),
                     pl.BlockSpec((tq, LANES), lambda b, i: (i, 0)),
                     pl.BlockSpec((seq, LANES), lambda b, i: (0, 0)),
                     pl.BlockSpec((seq, LANES), lambda b, i: (0, 0))]
        args += [cos, sin, cos, sin]
    if past:
        in_specs += [pl.BlockSpec((ns, None, past, KV_WIDTH), lambda b, i: (b, l, 0, 0)),
                     pl.BlockSpec((ns, None, past, KV_WIDTH), lambda b, i: (b, l, 0, 0))]
        args += list(past_kv)
    aliases = {}
    if kv_all is not None:
        in_specs += [pl.BlockSpec(memory_space=pl.ANY), pl.BlockSpec(memory_space=pl.ANY)]
        aliases = {len(args): 1, len(args) + 1: 2}
        args += list(kv_all)
    cache_spec = pl.BlockSpec((ns, None, seq, KV_WIDTH), lambda b, i: (b, l, 0, 0))
    out_specs = [pl.BlockSpec((ns * tq, ATTN_WIDTH), lambda b, i: (b * nq + i, 0)), cache_spec, cache_spec]
    return pl.pallas_call(
        functools.partial(_attn_kernel, tq=tq, seq=seq, past=past, use_rope=use_rope, ns=ns),
        grid=(n_batch // ns, nq),
        in_specs=in_specs,
        out_specs=out_specs,
        out_shape=[jax.ShapeDtypeStruct((n_tok, ATTN_WIDTH), BF16),
                   jax.ShapeDtypeStruct((n_batch, DEPTH, seq, KV_WIDTH), F32),
                   jax.ShapeDtypeStruct((n_batch, DEPTH, seq, KV_WIDTH), F32)],
        input_output_aliases=aliases,
        scratch_shapes=[pltpu.VMEM((ns * 2 * N_KV_HEADS, past + seq, KV_WIDTH), BF16),
                        pltpu.VMEM((ns * 2 * N_KV_HEADS, past + seq, KV_WIDTH), BF16)],
        compiler_params=pltpu.CompilerParams(
            dimension_semantics=("parallel", "arbitrary"), vmem_limit_bytes=VMEM_LIMIT_BYTES),
        name="attention",
    )(*args)


def _post_kernel(y0_ref, y1_ref, bo0_ref, bo1_ref, lor_ref, oa_ref, gr_ref, ga_ref, x_ref, mod_ref,
                 gup_ref, gnw_ref, gnb_ref, wbr_ref, wout_ref, ng_ref, x1_ref, h2_ref):
    ones_bd = _group_ones()
    y = y0_ref[...].astype(F32) + y1_ref[...].astype(F32)
    mean = _group_sum(y, ones_bd) * (1.0 / HEAD_DIM)
    yc = y - mean
    var = _group_sum(yc * yc, ones_bd) * (1.0 / HEAD_DIM)
    yn = yc * lax.rsqrt(var + GN_EPS) * gnw_ref[...] + gnb_ref[...]
    lg = lor_ref[:, LANES:]
    g = _mm(jax.nn.sigmoid(lg), gup_ref[...])
    o_r = (yn + (bo0_ref[...].astype(F32) + bo1_ref[...].astype(F32))) * g
    merged = (jax.nn.sigmoid(gr_ref[...].astype(F32)) * _mm(o_r, wbr_ref[0])
              + jax.nn.sigmoid(ga_ref[...].astype(F32)) * _mm(oa_ref[...], wbr_ref[1]))
    m = _mm(merged, wout_ref[...])
    g1 = mod_ref[0, 2:3, :]
    sh2 = mod_ref[0, 3:4, :]
    sc2 = mod_ref[0, 4:5, :]
    x1 = x_ref[...] + g1 * _rms(m, ng_ref[1:2, :])
    x1_ref[...] = x1
    h2_ref[...] = (_rms(x1, ng_ref[2:3, :]) * (1.0 + sc2) + sh2).astype(BF16)


def _post(y0, y1, bo0, bo1, rest, gates, o_a, x2d, P, l, mod_row0, rows_per_mod):
    n_tok = x2d.shape[0]
    tm = 512
    tok = lambda w: pl.BlockSpec((tm, w), lambda i: (i, 0))
    return pl.pallas_call(
        _post_kernel,
        grid=(n_tok // tm,),
        in_specs=[
            tok(RWKV_WIDTH), tok(RWKV_WIDTH), tok(RWKV_WIDTH), tok(RWKV_WIDTH),
            pl.BlockSpec((tm, 2 * LANES), lambda i: (i, COL_LORA // (2 * LANES))),
            tok(ATTN_WIDTH),
            pl.BlockSpec((tm, D_MODEL), lambda i: (i, 0)),
            pl.BlockSpec((tm, D_MODEL), lambda i: (i, 1)),
            tok(D_MODEL),
            _mod_spec(l, mod_row0, rows_per_mod, tm),
            _lspec(l, (GATE_RANK, RWKV_WIDTH)), _lspec(l, (1, RWKV_WIDTH)), _lspec(l, (1, RWKV_WIDTH)),
            _lspec(l, (2, RWKV_WIDTH, D_MODEL)), _lspec(l, (D_MODEL, D_MODEL)), _lspec(l, (4, D_MODEL)),
        ],
        out_specs=[tok(D_MODEL), tok(D_MODEL)],
        out_shape=[jax.ShapeDtypeStruct((n_tok, D_MODEL), F32), jax.ShapeDtypeStruct((n_tok, D_MODEL), BF16)],
        compiler_params=pltpu.CompilerParams(
            dimension_semantics=("parallel",), vmem_limit_bytes=VMEM_LIMIT_BYTES),
        name="merge_outproj",
    )(y0, y1, bo0, bo1, rest, o_a, gates, gates, x2d, P['mod'], P['gup'], P['gn_w'], P['gn_b'], P['w_br'], P['w_out'],
      P['ng'])


def _mlp_kernel(h_ref, x1_ref, mod_ref, up_ref, down_ref, ng_ref, o_ref):
    h = h_ref[...]
    f = None
    ff = D_MODEL
    for j in range(D_FF // ff):
        u = lax.dot_general(h, up_ref[:, j * ff:(j + 1) * ff], NN, preferred_element_type=F32)
        u = jnp.square(jnp.maximum(u, 0.0)).astype(BF16)
        part = lax.dot_general(u, down_ref[j * ff:(j + 1) * ff, :], NN, preferred_element_type=F32)
        f = part if f is None else f + part
    g2 = mod_ref[0, 5:6, :]
    o_ref[...] = x1_ref[...] + g2 * _rms(f, ng_ref[3:4, :])


def _mlp(h2, x1, P, l, mod_row0, rows_per_mod):
    n_tok = x1.shape[0]
    tm = 1024
    tok = lambda: pl.BlockSpec((tm, D_MODEL), lambda i: (i, 0))
    return pl.pallas_call(
        _mlp_kernel,
        grid=(n_tok // tm,),
        in_specs=[
            tok(), tok(),
            _mod_spec(l, mod_row0, rows_per_mod, tm),
            _lspec(l, (D_MODEL, D_FF), single=True),
            _lspec(l, (D_FF, D_MODEL), single=True),
            _lspec(l, (4, D_MODEL)),
        ],
        out_specs=tok(),
        out_shape=jax.ShapeDtypeStruct((n_tok, D_MODEL), F32),
        compiler_params=pltpu.CompilerParams(
            dimension_semantics=("parallel",), vmem_limit_bytes=VMEM_LIMIT_BYTES),
        name="mlp",
    )(h2, x1, P['mod'], P['up'], P['down'], P['ng'])


def _rope_tables(seq):
    n_rows = seq // GRID_W
    rows = np.repeat(np.arange(n_rows, dtype=np.float32), GRID_W)
    cols = np.tile(np.arange(GRID_W, dtype=np.float32), n_rows)
    half = HEAD_DIM // 2
    freqs = 1.0 / (jnp.asarray(ROPE_THETA, F32) ** (jnp.arange(0, half, 2, dtype=F32) / half))
    ang_r = jnp.asarray(rows)[:, None] * freqs
    ang_c = jnp.asarray(cols)[:, None] * freqs
    cr, sr, cc, sc = jnp.cos(ang_r), jnp.sin(ang_r), jnp.cos(ang_c), jnp.sin(ang_c)
    cos64 = jnp.concatenate([cr, cr, cc, cc], axis=1)
    sin64 = jnp.concatenate([-sr, sr, -sc, sc], axis=1)
    return jnp.tile(cos64, (1, LANES // HEAD_DIM)), jnp.tile(sin64, (1, LANES // HEAD_DIM))


def _pairs_from_state(s):
    b = s.shape[0]
    s = s.reshape(b, 2, N_PAIRS, 2, HEAD_DIM, HEAD_DIM)
    return jnp.concatenate([s[:, :, :, 0], s[:, :, :, 1]], axis=-1)


def _layer(x2d, n_batch, seq, P, l, mod_row0, rows_per_mod, rope128, past_kv, s0, carry):
    kv_all, s_all = (None, None) if carry is None else (carry[:2], carry[2])
    rkv, gates, rest, kh = _in_projection(x2d, P, l, mod_row0, rows_per_mod, seq)
    s0_pairs = None if s0 is None else _pairs_from_state(s0)
    y0, y1, bo0, bo1, s_all = _wkv(rkv, rest, kh, s0_pairs, n_batch, seq, P, l, s_all)
    o_a, k_all, v_all = _attention(rest, n_batch, seq, P, l, rope128, past_kv, kv_all)
    x1, h2 = _post(y0, y1, bo0, bo1, rest, gates, o_a, x2d, P, l, mod_row0, rows_per_mod)
    x2 = _mlp(h2, x1, P, l, mod_row0, rows_per_mod)
    return x2, (k_all, v_all, s_all)


def kernel(x_prompt, x_sample, cache_k, cache_v, state_wkv, c, c_ctx, w_in, w_br, w_out, w_mod, b_mod, norm_g, mlp_up, mlp_down, rwkv_mu, rwkv_k_k, rwkv_k_a, rwkv_r_k, decay_w0, decay_up, iclr_a0, iclr_up, gate_up, gn_w, gn_b, q_gain, k_gain):
    n_ctx, seq_ctx, _ = x_prompt.shape
    n_dec, seq_dec, _ = x_sample.shape
    past = cache_k.shape[2]

    cvec8 = jnp.zeros((SUBLANES, D_MODEL), F32).at[0].set(c_ctx).at[1:1 + n_dec].set(c)
    mod = _modulation(cvec8, w_mod, b_mod)
    mod = jnp.pad(mod.reshape(DEPTH, SUBLANES, N_MOD, D_MODEL), ((0, 0), (0, 0), (0, SUBLANES - N_MOD), (0, 0)))

    o = np.cumsum((0, RWKV_WIDTH, RWKV_WIDTH, RWKV_WIDTH, DECAY_RANK, ICLR_RANK, GATE_RANK,
                   ATTN_WIDTH, KV_WIDTH, KV_WIDTH, 2 * D_MODEL))
    w_rest = jnp.concatenate([w_in[:, :, o[9]:o[10]], w_in[:, :, o[6]:o[7]], w_in[:, :, o[3]:o[6]],
                              w_in[:, :, o[7]:o[9]]], axis=-1)
    zpad = jnp.zeros((DEPTH, 2, DECAY_RANK, RWKV_WIDTH), F32)
    row = lambda a: a.reshape(DEPTH, 1, -1)
    tile2 = lambda a: jnp.tile(a, (1, LANES // HEAD_DIM)).reshape(DEPTH, 1, LANES)
    P = dict(
        mod=mod, ng=norm_g, w_rkv=w_in[:, :, o[0]:o[3]].astype(BF16), w_rest=w_rest.astype(BF16),
        mu=rwkv_mu, k_k=row(rwkv_k_k), k_a=row(rwkv_k_a), r_k=row(rwkv_r_k),
        w0=decay_w0, wup=jnp.concatenate([decay_up, zpad], axis=2).astype(BF16),
        a0=iclr_a0, aup=jnp.concatenate([zpad, iclr_up], axis=2).astype(BF16),
        gup=gate_up.astype(BF16), gn_w=row(gn_w), gn_b=row(gn_b),
        w_br=w_br.astype(BF16), w_out=w_out.astype(BF16), up=mlp_up.astype(BF16), down=mlp_down.astype(BF16),
        q_gain=tile2(q_gain), k_gain=tile2(k_gain),
    )

    x = x_prompt.reshape(n_ctx * seq_ctx, D_MODEL)
    carry = None
    for l in range(DEPTH):
        x, carry = _layer(x, n_ctx, seq_ctx, P, l, 0, 0, None, None, None, carry)
    y_prompt = x.reshape(n_ctx, seq_ctx, D_MODEL)
    new_k, new_v, new_s = carry
    cache_shape = (n_ctx, DEPTH, seq_ctx, N_KV_HEADS, HEAD_DIM)

    rope128 = _rope_tables(seq_dec)
    past_kv = (cache_k.reshape(n_dec, DEPTH, past, KV_WIDTH), cache_v.reshape(n_dec, DEPTH, past, KV_WIDTH))
    x = x_sample.reshape(n_dec * seq_dec, D_MODEL)
    for l in range(DEPTH):
        x, _ = _layer(x, n_dec, seq_dec, P, l, 1, seq_dec, rope128, past_kv, state_wkv[:, l], None)
    y_sample = x.reshape(n_dec, seq_dec, D_MODEL)

    return (y_prompt, y_sample, new_k.reshape(cache_shape), new_v.reshape(cache_shape), new_s)
```

```python
import functools

import numpy as np
import jax
import jax.numpy as jnp
from jax import lax
from jax.experimental import pallas as pl
from jax.experimental.pallas import tpu as pltpu

F32 = jnp.float32
BF16 = jnp.bfloat16

D_MODEL = 1024
DEPTH = 2
GRID_W = 64
HEAD_DIM = 64
N_RWKV_HEADS = 8
RWKV_WIDTH = N_RWKV_HEADS * HEAD_DIM
N_Q_HEADS = 8
N_KV_HEADS = 2
ATTN_WIDTH = N_Q_HEADS * HEAD_DIM
KV_WIDTH = N_KV_HEADS * HEAD_DIM
DECAY_RANK = 64
ICLR_RANK = 64
GATE_RANK = 128
D_FF = 4 * D_MODEL
ROPE_THETA = 10000.0
ROPE_FREQS = HEAD_DIM // 4
N_MOD = 6
NORM_EPS = 1e-6
GN_EPS = 64e-5
DECAY_SCALE = 0.606531
D_IN = 3 * RWKV_WIDTH + DECAY_RANK + ICLR_RANK + GATE_RANK + ATTN_WIDTH + 2 * KV_WIDTH + 2 * D_MODEL

LANES = 128
SUBLANES = 8
VMEM_LIMIT_BYTES = 56 * 1024 * 1024

RKV_W = 3 * RWKV_WIDTH
GATES_W = 2 * D_MODEL
COL_Q = 0
COL_LORA = COL_Q + ATTN_WIDTH
COL_KV = COL_LORA + 2 * LANES
REST_W = COL_KV + 2 * KV_WIDTH

WKV_CHUNK = 64
PAIR = 2 * HEAD_DIM
N_PAIRS = N_RWKV_HEADS // 2

NN = (((1,), (0,)), ((), ()))
NT = (((1,), (1,)), ((), ()))
TN = (((0,), (0,)), ((), ()))


def _mm(a, b, dims=NN):
    return lax.dot_general(a.astype(BF16), b.astype(BF16), dims, preferred_element_type=F32)


def _split2(x):
    hi = x.astype(BF16)
    lo = (x - hi.astype(F32)).astype(BF16)
    return hi, lo


def _mm_exact_rhs(x, m_bf16):
    hi, lo = _split2(x)
    d = lambda p: lax.dot_general(p, m_bf16, NN, preferred_element_type=F32)
    return d(hi) + d(lo)


def _div_pow2(i, n):
    return lax.shift_right_logical(i, int(np.log2(n)))


def _mod_pow2(i, n):
    return lax.bitwise_and(i, n - 1)


def _group_ones():
    i = lax.broadcasted_iota(jnp.int32, (LANES, LANES), 0)
    j = lax.broadcasted_iota(jnp.int32, (LANES, LANES), 1)
    return jnp.where(_div_pow2(i, HEAD_DIM) == _div_pow2(j, HEAD_DIM), 1.0, 0.0).astype(BF16)


def _group_sum(x, ones_bd):
    blocks = [
        _mm_exact_rhs(x[:, i * LANES:(i + 1) * LANES], ones_bd)
        for i in range(x.shape[1] // LANES)
    ]
    return blocks[0] if len(blocks) == 1 else jnp.concatenate(blocks, axis=1)


def _lspec(l, shape, single=False):
    kw = dict(pipeline_mode=pl.Buffered(1)) if single else {}
    return pl.BlockSpec((None,) + tuple(shape), lambda *g: (l,) + (0,) * len(shape), **kw)


def _mod_spec(l, row0, rows_per_mod, tm):
    if rows_per_mod:
        return pl.BlockSpec((None, 1, SUBLANES, D_MODEL), lambda i, *g: (l, row0 + i * tm // rows_per_mod, 0, 0))
    return pl.BlockSpec((None, 1, SUBLANES, D_MODEL), lambda *g: (l, row0, 0, 0))


def _rms(x, g):
    return x * lax.rsqrt(jnp.mean(x * x, axis=-1, keepdims=True) + NORM_EPS) * g


def _mod_kernel(c_ref, w_ref, b_ref, o_ref):
    c = c_ref[...]
    s = c * jax.nn.sigmoid(c)
    o_ref[0] = _mm(s, w_ref[0]) + b_ref[0]


def _modulation(cvec8, w_mod, b_mod):
    tn = D_MODEL
    n = N_MOD * D_MODEL
    return pl.pallas_call(
        _mod_kernel,
        grid=(DEPTH, n // tn),
        in_specs=[
            pl.BlockSpec((SUBLANES, D_MODEL), lambda l, j: (0, 0)),
            pl.BlockSpec((1, D_MODEL, tn), lambda l, j: (l, 0, j)),
            pl.BlockSpec((1, 1, tn), lambda l, j: (l, 0, j)),
        ],
        out_specs=pl.BlockSpec((1, SUBLANES, tn), lambda l, j: (l, 0, j)),
        out_shape=jax.ShapeDtypeStruct((DEPTH, SUBLANES, n), F32),
        name="modulation",
    )(cvec8, w_mod, b_mod.reshape(DEPTH, 1, n))


def _shift_mix(x, mu):
    n = x.shape[0]
    row = lax.broadcasted_iota(jnp.int32, x.shape, 0)
    x_prev = jnp.where(row == 0, 0.0, pltpu.roll(x, 1, 0))
    x_next = jnp.where(row == n - 1, 0.0, pltpu.roll(x, n - 1, 0))
    return x * (1.0 - mu) + (0.5 * mu) * (x_prev + x_next)


def _inproj_kernel(x_ref, mod_ref, ng_ref, wrkv_ref, wrest_ref, mu_ref, kk_ref,
                   rkv_ref, gates_ref, rest_ref, kh_ref, h_scr, *, tm, seq):
    C = RWKV_WIDTH

    def mix_in_place(i, piece):
        lanes = slice(piece * LANES, (piece + 1) * LANES)
        cols = slice(i * C + piece * LANES, i * C + (piece + 1) * LANES)
        for s in range(tm // seq):
            rows = slice(s * seq, (s + 1) * seq)
            mixed = _shift_mix(rkv_ref[rows, cols], mu_ref[i:i + 1, lanes])
            rkv_ref[rows, cols] = mixed
            if i == 1:
                kk = mixed * kk_ref[:, lanes]
                kh_ref[rows, lanes] = kk * lax.rsqrt(_group_sum(kk * kk, _group_ones()) + 1e-12)

    sh = mod_ref[0, 0:1, :]
    sc = mod_ref[0, 1:2, :]
    ng = ng_ref[0:1, :]
    sub = 256
    for i in range(tm // sub):
        x = x_ref[i * sub:(i + 1) * sub, :]
        h_scr[i * sub:(i + 1) * sub, :] = (_rms(x, ng) * (1.0 + sc) + sh).astype(BF16)
    rkv_ref[...] = lax.dot_general(h_scr[...], wrkv_ref[...], NN, preferred_element_type=F32)

    tn = 2 * LANES
    n_piece = C // LANES
    for chunk in range((GATES_W + REST_W) // tn):
        cols = slice(chunk * tn, (chunk + 1) * tn)
        out = lax.dot_general(h_scr[...], wrest_ref[:, cols], NN, preferred_element_type=F32)
        if chunk * tn < GATES_W:
            gates_ref[:, cols] = out.astype(BF16)
        else:
            rest_ref[:, chunk * tn - GATES_W:(chunk + 1) * tn - GATES_W] = out
        mix_in_place(chunk // n_piece, chunk % n_piece)


def _in_projection(x2d, P, l, mod_row0, rows_per_mod, seq):
    n_tok = x2d.shape[0]
    tm = 1024
    assert tm % seq == 0 and (GATES_W + REST_W) // (2 * LANES) == 3 * (RWKV_WIDTH // LANES)
    tok = lambda w: pl.BlockSpec((tm, w), lambda i: (i, 0))
    return pl.pallas_call(
        functools.partial(_inproj_kernel, tm=tm, seq=seq),
        grid=(n_tok // tm,),
        in_specs=[
            tok(D_MODEL),
            _mod_spec(l, mod_row0, rows_per_mod, tm),
            _lspec(l, (4, D_MODEL)),
            _lspec(l, (D_MODEL, RKV_W), single=True),
            _lspec(l, (D_MODEL, GATES_W + REST_W), single=True),
            _lspec(l, (3, RWKV_WIDTH)),
            _lspec(l, (1, RWKV_WIDTH)),
        ],
        out_specs=[tok(RKV_W), tok(GATES_W), tok(REST_W), tok(RWKV_WIDTH)],
        out_shape=[jax.ShapeDtypeStruct((n_tok, RKV_W), F32),
                   jax.ShapeDtypeStruct((n_tok, GATES_W), BF16),
                   jax.ShapeDtypeStruct((n_tok, REST_W), F32),
                   jax.ShapeDtypeStruct((n_tok, RWKV_WIDTH), F32)],
        scratch_shapes=[pltpu.VMEM((tm, D_MODEL), BF16)],
        compiler_params=pltpu.CompilerParams(
            dimension_semantics=("parallel",), vmem_limit_bytes=VMEM_LIMIT_BYTES),
        name="in_projection",
    )(x2d, P['mod'], P['ng'], P['w_rkv'], P['w_rest'], P['mu'], P['k_k'])


def _wkv_masks(reverse):
    L = WKV_CHUNK
    row = lax.broadcasted_iota(jnp.int32, (L, PAIR), 0)
    col = _mod_pow2(lax.broadcasted_iota(jnp.int32, (L, PAIR), 1), L)
    return dict(
        strict=(col > row) if reverse else (col < row),
        incl=(col >= row) if reverse else (col <= row),
        blk8=_div_pow2(row, 8) == _div_pow2(col, 8),
        eye=row == col,
    )


def _bd(x, left):
    x = x.astype(BF16)
    z = jnp.zeros_like(x)
    return jnp.concatenate([jnp.where(left, x, z), jnp.where(left, z, x)], axis=0)


def _wkv_operand_scratch(nb):
    C, L = RWKV_WIDTH, WKV_CHUNK
    full = lambda dt: pltpu.VMEM((nb, 2, L, C), dt)
    pair = lambda rows: pltpu.VMEM((nb, 2, N_PAIRS, rows, PAIR), BF16)
    return dict(khd=full(F32), rd=full(F32), bonus=full(F32), etot=pltpu.VMEM((nb, 2, SUBLANES, C), F32),
                be=full(BF16), kte=full(BF16), vb=full(BF16),
                lhsA=pair(2 * L), rhsA=pair(4 * L), vbd=pair(2 * L), khdbd=pair(2 * L))


def _wkv_prepare(e, d, rkv, khr, lor, ka_ref, rk_ref, w0_ref, wup_ref, a0_ref, aup_ref, ops, ones_bd, tri, left):
    C, L = RWKV_WIDTH, WKV_CHUNK
    reverse = d == 1
    r = rkv[e, 0, :, 0:C]
    k = rkv[e, 0, :, C:2 * C]
    v = rkv[e, 0, :, 2 * C:3 * C]
    kh = khr[e, 0]
    lo_dec = lor[e, 0, :, 0:LANES]
    w_raw = _mm(jnp.tanh(lo_dec), wup_ref[d]) + w0_ref[d:d + 1, :]
    lw = -DECAY_SCALE * jax.nn.sigmoid(w_raw)
    a = jax.nn.sigmoid(_mm(lo_dec, aup_ref[d]) + a0_ref[d:d + 1, :])
    kt = k * (1.0 + (a - 1.0) * ka_ref[...])
    ops['bonus'][e, d] = _group_sum(r * kt * rk_ref[...], ones_bd) * v
    b = a * kh

    hi, lo2 = _split2(lw)
    cum = (lax.dot_general(tri[d], hi, NN, preferred_element_type=F32)
           + lax.dot_general(tri[d], lo2, NN, preferred_element_type=F32))
    tot = cum[0:1, :] if reverse else cum[L - 1:L, :]
    e_inv = jnp.exp(-cum)
    e_end = jnp.exp(tot - cum)
    khd = kh * jnp.exp(cum - lw)
    rd = r * jnp.exp(cum)
    bi = b * e_inv
    ki = kt * e_inv
    ops['khd'][e, d] = khd
    ops['rd'][e, d] = rd
    ops['etot'][e, d] = jnp.broadcast_to(jnp.exp(tot), (SUBLANES, C))
    ops['be'][e, d] = (b * e_end).astype(BF16)
    ops['kte'][e, d] = (kt * e_end).astype(BF16)
    ops['vb'][e, d] = v.astype(BF16)
    for p in range(N_PAIRS):
        sl = slice(p * PAIR, (p + 1) * PAIR)
        ops['lhsA'][e, d, p] = jnp.concatenate([khd[:, sl], rd[:, sl]], axis=0).astype(BF16)
        ops['rhsA'][e, d, p] = jnp.concatenate([_bd(bi[:, sl], left), _bd(ki[:, sl], left)], axis=0)
        ops['vbd'][e, d, p] = _bd(v[:, sl], left)
        ops['khdbd'][e, d, p] = _bd(khd[:, sl], left)


def _wkv_chains(chains, left, fillers):
    L = WKV_CHUNK
    i2 = lax.broadcasted_iota(jnp.int32, (PAIR, PAIR), 0)
    j2 = lax.broadcasted_iota(jnp.int32, (PAIR, PAIR), 1)
    same = _div_pow2(i2, HEAD_DIM) == _div_pow2(j2, HEAD_DIM)
    eye2 = i2 == j2
    bd = lambda x: _bd(x, left)
    lp = lambda g: jnp.where(left, g[:L], g[L:])
    n_stages = 13
    due = {((k + 1) * n_stages) // (len(fillers) + 1): f for k, f in enumerate(fillers)}
    emitted = [0]

    def each(f):
        for ch in chains:
            f(ch)
        emitted[0] += 1
        if emitted[0] in due:
            due[emitted[0]]()

    def a_blocks(ch):
        m = ch['masks']
        A = lax.dot_general(ch['lhsA'], ch['rhsA'], NT, preferred_element_type=F32)
        ch['A_ub'] = jnp.where(m['strict'], A[:L, :PAIR], 0.0)
        ch['A_uvrv'] = jnp.concatenate([jnp.where(m['strict'], A[:L, PAIR:], 0.0),
                                        jnp.where(m['incl'], A[L:, PAIR:], 0.0)], axis=0).astype(BF16)
        ch['A_rb'] = jnp.where(m['incl'], A[L:, :PAIR], 0.0).astype(BF16)
        a8 = jnp.where(m['blk8'], ch['A_ub'], 0.0)
        ch['A8'] = a8
        ch['X0'] = jnp.where(m['eye'], 1.0, -a8)
        ch['Nlow'] = ch['A_ub'] - a8
    each(a_blocks)

    stack = lambda a, b: jnp.concatenate([a.astype(BF16), b.astype(BF16)], axis=0)

    def sq1(ch):
        ch['A8_2'] = _mm(ch['A8'], bd(ch['A8']))
    each(sq1)

    def sq2(ch):
        r = _mm(stack(ch['A8_2'], ch['X0']), bd(ch['A8_2']))
        ch['A8_4'] = r[:L]
        ch['X1'] = ch['X0'] + r[L:]
        ch['AV'] = _mm(ch['A_uvrv'], ch['vbd'])
    each(sq2)

    def dinv(ch):
        ch['T8'] = ch['X1'] + _mm(ch['X1'], bd(ch['A8_4']))
    each(dinv)

    def e1(ch):
        ch['E'] = _mm(ch['T8'], bd(ch['Nlow']))
    each(e1)

    def e2(ch):
        ch['E2'] = _mm(ch['E'], bd(ch['E']))
    each(e2)

    def e3(ch):
        ime = jnp.where(ch['masks']['eye'], 1.0, 0.0) - ch['E']
        r = _mm(stack(ch['E2'], ime), bd(ch['E2']))
        ch['E4'] = r[:L]
        ch['Y1'] = ime + r[L:]
    each(e3)

    def e4(ch):
        ch['Y2'] = ch['Y1'] + _mm(ch['Y1'], bd(ch['E4']))
    each(e4)

    def t_stage(ch):
        ch['T'] = _mm(ch['Y2'], bd(ch['T8']))
    each(t_stage)

    def p_stage(ch):
        ch['P'] = _mm(ch['A_rb'], bd(ch['T']))
    each(p_stage)

    def q_stage(ch):
        rhs = jnp.concatenate([ch['khdbd'], bd(ch['AV'][:L])], axis=1)
        r = _mm(stack(ch['T'], ch['P']), rhs)
        ch['Q'] = r[:L].astype(BF16)
        ch['G1'] = ch['rd'] - r[L:, :PAIR]
        ch['G2'] = ch['AV'][L:] - r[L:, PAIR:]
    each(q_stage)

    def mn_stage(ch):
        QB = _mm(ch['Q'], ch['be'], TN)
        ch['M'] = (jnp.where(eye2, ch['etot'], 0.0) - jnp.where(same, QB[:PAIR], 0.0)).astype(BF16)
        ch['N'] = lp(_mm(ch['vb'], ch['kte'], TN)) - lp(QB[PAIR:])
    each(mn_stage)

    def out_stage(ch):
        ch['y'] = _mm(ch['G1'], bd(ch['S']), NT) + ch['G2']
        ch['Sn'] = _mm(ch['S'], ch['M']) + ch['N']
    each(out_stage)


def _wkv_kernel(rkv0, khr0, lor0, rkv1, khr1, lor1,
                ka_ref, rk_ref, w0_ref, wup_ref, a0_ref, aup_ref, s0_ref,
                *rest, zero_state, nb, nc, op_names):
    n_ops = len(op_names)
    y0_ref, y1_ref, bo0_ref, bo1_ref, sfin_ref, s_scr = rest[-(6 + 2 * n_ops):len(rest) - 2 * n_ops]
    op_sets = [dict(zip(op_names, rest[len(rest) - (2 - i) * n_ops:len(rest) - (1 - i) * n_ops])) for i in (0, 1)]
    t = pl.program_id(0)
    c = lax.rem(jnp.maximum(t - 1, 0), nc)

    @pl.when(t == 0)
    def _():
        for ref in op_sets[1].values():
            ref[...] = jnp.zeros(ref.shape, ref.dtype)

    @pl.when(c == 0)
    def _():
        if zero_state:
            s_scr[...] = jnp.zeros(s_scr.shape, F32)
        else:
            s_scr[...] = s0_ref[...]

    def step(ops_in, ops_out):
        L = WKV_CHUNK
        left = lax.broadcasted_iota(jnp.int32, (L, PAIR), 1) < HEAD_DIM
        masks = [_wkv_masks(False), _wkv_masks(True)]

        chains = []
        for e in range(nb):
            for d, (y_ref, bo_ref) in enumerate(((y0_ref, bo0_ref), (y1_ref, bo1_ref))):
                bo_ref[e, 0] = ops_in['bonus'][e, d].astype(BF16)
                for p in range(N_PAIRS):
                    sl = slice(p * PAIR, (p + 1) * PAIR)
                    chains.append(dict(
                        e=e, d=d, p=p, sl=sl, y_ref=y_ref, masks=masks[d], S=s_scr[e, d, p],
                        lhsA=ops_in['lhsA'][e, d, p], rhsA=ops_in['rhsA'][e, d, p], vbd=ops_in['vbd'][e, d, p],
                        khdbd=ops_in['khdbd'][e, d, p], rd=ops_in['rd'][e, d, :, sl], be=ops_in['be'][e, d, :, sl],
                        kte=ops_in['kte'][e, d, :, sl], vb=ops_in['vb'][e, d, :, sl],
                        etot=ops_in['etot'][e, d, 0:1, sl]))
        ones_bd = _group_ones()
        ti = lax.broadcasted_iota(jnp.int32, (L, L), 0)
        tj = lax.broadcasted_iota(jnp.int32, (L, L), 1)
        tri = [jnp.where(tj <= ti, 1.0, 0.0).astype(BF16), jnp.where(tj >= ti, 1.0, 0.0).astype(BF16)]
        prepare = [
            functools.partial(_wkv_prepare, e, d, rkv, khr, lor, ka_ref, rk_ref, w0_ref, wup_ref, a0_ref, aup_ref,
                              ops_out, ones_bd, tri, left)
            for e in range(nb) for d, (rkv, khr, lor) in enumerate(((rkv0, khr0, lor0), (rkv1, khr1, lor1)))]
        _wkv_chains(chains, left, prepare)
        for ch in chains:
            ch['y_ref'][ch['e'], 0, :, ch['sl']] = ch['y'].astype(BF16)
            s_scr[ch['e'], ch['d'], ch['p']] = ch['Sn']

    parity = lax.rem(t, 2)

    @pl.when(parity == 0)
    def _():
        step(op_sets[1], op_sets[0])

    @pl.when(parity == 1)
    def _():
        step(op_sets[0], op_sets[1])

    @pl.when((c == nc - 1) & (t > 0))
    def _():
        for e in range(nb):
            for d in (0, 1):
                for p in range(N_PAIRS):
                    s2 = s_scr[e, d, p]
                    sfin_ref[e, d, 2 * p] = s2[:, :HEAD_DIM]
                    sfin_ref[e, d, 2 * p + 1] = pltpu.roll(s2, HEAD_DIM, 1)[:, :HEAD_DIM]


def _wkv(rkv, rest, kh, s0_pairs, n_batch, seq, P, l, s_all):
    L = WKV_CHUNK
    nc = seq // L
    nb = 4
    n_steps = (n_batch // nb) * nc
    n_tok = n_batch * seq
    zero_state = s0_pairs is None
    if zero_state:
        s0_pairs = jnp.zeros((nb, 2, N_PAIRS, HEAD_DIM, PAIR), F32)
    rkv4 = rkv.reshape(n_batch, nc, L, RKV_W)
    rest4 = rest.reshape(n_batch, nc, L, REST_W)
    kh4 = kh.reshape(n_batch, nc, L, RWKV_WIDTH)

    def in_pos(t, d):
        s = jnp.minimum(t, n_steps - 1)
        c = lax.rem(s, nc)
        return lax.div(s, nc), (c if d == 0 else nc - 1 - c)

    def out_pos(t, d):
        s = jnp.maximum(t - 1, 0)
        c = lax.rem(s, nc)
        return lax.div(s, nc), (c if d == 0 else nc - 1 - c)

    in_specs = []
    for d in (0, 1):
        in_specs += [
            pl.BlockSpec((nb, 1, L, RKV_W), lambda t, d=d: (*in_pos(t, d), 0, 0)),
            pl.BlockSpec((nb, 1, L, RWKV_WIDTH), lambda t, d=d: (*in_pos(t, d), 0, 0)),
            pl.BlockSpec((nb, 1, L, 2 * LANES), lambda t, d=d: (*in_pos(t, d), 0, COL_LORA // (2 * LANES))),
        ]
    state_spec = lambda idx: pl.BlockSpec((nb, 2, N_PAIRS, HEAD_DIM, PAIR), idx)
    in_specs += [
        _lspec(l, (1, RWKV_WIDTH)), _lspec(l, (1, RWKV_WIDTH)),
        _lspec(l, (2, RWKV_WIDTH)), _lspec(l, (2, LANES, RWKV_WIDTH)),
        _lspec(l, (2, RWKV_WIDTH)), _lspec(l, (2, LANES, RWKV_WIDTH)),
        state_spec((lambda t: (0, 0, 0, 0, 0)) if zero_state else (lambda t: (out_pos(t, 0)[0], 0, 0, 0, 0))),
    ]
    args = [rkv4, kh4, rest4, rkv4, kh4, rest4, P['k_a'], P['r_k'], P['w0'], P['wup'], P['a0'], P['aup'], s0_pairs]
    aliases = {}
    if s_all is not None:
        in_specs.append(pl.BlockSpec(memory_space=pl.ANY))
        aliases = {len(args): 4}
        args.append(s_all)
    tok_spec = lambda d: pl.BlockSpec((nb, 1, L, RWKV_WIDTH), lambda t, d=d: (*out_pos(t, d), 0, 0))
    sfin_spec = pl.BlockSpec((nb, None, 2, N_RWKV_HEADS, HEAD_DIM, HEAD_DIM),
                             lambda t: (out_pos(t, 0)[0], l, 0, 0, 0, 0))
    out_specs = [tok_spec(0), tok_spec(1), tok_spec(0), tok_spec(1), sfin_spec]
    tok = jax.ShapeDtypeStruct((n_batch, nc, L, RWKV_WIDTH), BF16)
    op_scratch = _wkv_operand_scratch(nb)
    outs = pl.pallas_call(
        functools.partial(_wkv_kernel, zero_state=zero_state, nb=nb, nc=nc, op_names=tuple(op_scratch)),
        grid=(n_steps + 1,),
        in_specs=in_specs,
        out_specs=out_specs,
        out_shape=[tok, tok, tok, tok,
                   jax.ShapeDtypeStruct((n_batch, DEPTH, 2, N_RWKV_HEADS, HEAD_DIM, HEAD_DIM), F32)],
        input_output_aliases=aliases,
        scratch_shapes=([pltpu.VMEM((nb, 2, N_PAIRS, HEAD_DIM, PAIR), F32)]
                        + list(op_scratch.values()) + list(_wkv_operand_scratch(nb).values())),
        compiler_params=pltpu.CompilerParams(
            dimension_semantics=("arbitrary",), vmem_limit_bytes=VMEM_LIMIT_BYTES),
        name="wkv_scan",
    )(*args)
    y0, y1, bo0, bo1, s_fin = outs
    flat = lambda t: t.reshape(n_tok, RWKV_WIDTH)
    return flat(y0), flat(y1), flat(bo0), flat(bo1), s_fin


def _rope(x, cos, sin_signed):
    w = x.shape[1]
    lane = lax.broadcasted_iota(jnp.int32, x.shape, 1)
    partner = jnp.where(_mod_pow2(lane, 2 * ROPE_FREQS) < ROPE_FREQS,
                        pltpu.roll(x, w - ROPE_FREQS, 1), pltpu.roll(x, ROPE_FREQS, 1))
    return x * cos + partner * sin_signed


def _attn_kernel(*refs, tq, seq, past, use_rope, ns):
    it = iter(refs)
    q_ref, kv_ref, qg_ref, kg_ref = next(it), next(it), next(it), next(it)
    if use_rope:
        cq_ref, sq_ref, ck_ref, sk_ref = next(it), next(it), next(it), next(it)
    if past:
        pk_ref, pv_ref = next(it), next(it)
    o_ref, ko_ref, vo_ref, kvar_scr, vvar_scr = refs[-5:]

    ones_bd = _group_ones()

    @pl.when(pl.program_id(1) == 0)
    def _():
        for sq in range(ns):
            kv = kv_ref[sq * seq:(sq + 1) * seq, :]
            k_raw = kv[:, :KV_WIDTH]
            kn = k_raw * lax.rsqrt(_group_sum(k_raw * k_raw, ones_bd) * (1.0 / HEAD_DIM) + NORM_EPS) * kg_ref[...]
            if use_rope:
                kn = _rope(kn, ck_ref[...], sk_ref[...])
            vn = kv[:, KV_WIDTH:]
            ko_ref[sq] = kn
            vo_ref[sq] = vn
            pieces = [(past, seq, kn, vn)]
            if past:
                pieces.append((0, past, pk_ref[sq], pv_ref[sq]))
            for start, n, kx, vx in pieces:
                left = lax.broadcasted_iota(jnp.int32, kx.shape, 1) < HEAD_DIM
                for x, scr in ((kx, kvar_scr), (vx, vvar_scr)):
                    sw = pltpu.roll(x, HEAD_DIM, 1)
                    scr[4 * sq + 0, start:start + n, :] = jnp.where(left, x, 0.0).astype(BF16)
                    scr[4 * sq + 1, start:start + n, :] = jnp.where(left, 0.0, sw).astype(BF16)
                    scr[4 * sq + 2, start:start + n, :] = jnp.where(left, sw, 0.0).astype(BF16)
                    scr[4 * sq + 3, start:start + n, :] = jnp.where(left, 0.0, x).astype(BF16)

    scale = HEAD_DIM ** -0.5
    n_blk = ATTN_WIDTH // LANES
    blocks = [(sq, jb) for sq in range(ns) for jb in range(n_blk)]
    qn = []
    for sq, jb in blocks:
        qb = q_ref[sq * tq:(sq + 1) * tq, jb * LANES:(jb + 1) * LANES]
        x = qb * lax.rsqrt(_group_sum(qb * qb, ones_bd) * (1.0 / HEAD_DIM) + NORM_EPS) * qg_ref[...]
        if use_rope:
            x = _rope(x, cq_ref[...], sq_ref[...])
        qn.append((x * scale).astype(BF16))
    heads = [(i, 4 * sq + 2 * (jb // (n_blk // N_KV_HEADS)) + side)
             for i, (sq, jb) in enumerate(blocks) for side in (0, 1)]
    scores = [lax.dot_general(qn[i], kvar_scr[var], NT, preferred_element_type=F32) for i, var in heads]
    exps = [jnp.exp(s - jnp.max(s, axis=-1, keepdims=True)) for s in scores]
    outs = [lax.dot_general(e.astype(BF16), vvar_scr[var], NN, preferred_element_type=F32)
            for e, (i, var) in zip(exps, heads)]
    outs = [o / jnp.sum(e, axis=-1, keepdims=True) for o, e in zip(outs, exps)]
    for i, (sq, jb) in enumerate(blocks):
        o_ref[sq * tq:(sq + 1) * tq, jb * LANES:(jb + 1) * LANES] = (outs[2 * i] + outs[2 * i + 1]).astype(BF16)


def _attention(rest, n_batch, seq, P, l, rope128, past_kv, kv_all):
    tq = min(seq, 512)
    nq = seq // tq
    ns = 4 if nq == 1 else 1
    n_tok = n_batch * seq
    use_rope = rope128 is not None
    past = 0 if past_kv is None else past_kv[0].shape[2]
    in_specs = [
        pl.BlockSpec((ns * tq, ATTN_WIDTH), lambda b, i: (b * nq + i, COL_Q // ATTN_WIDTH)),
        pl.BlockSpec((ns * seq, 2 * KV_WIDTH), lambda b, i: (b, COL_KV // (2 * KV_WIDTH))),
        _lspec(l, (1, LANES)),
        _lspec(l, (1, LANES)),
    ]
    args = [rest, rest, P['q_gain'], P['k_gain']]
    if use_rope:
        cos, sin = rope128
        in_specs += [pl.BlockSpec((tq, LANES), lambda b, i: (i, 0)),
                     pl.BlockSpec((tq, LANES), lambda b, i: (i, 0)),
                     pl.BlockSpec((seq, LANES), lambda b, i: (0, 0)),
                     pl.BlockSpec((seq, LANES), lambda b, i: (0, 0))]
        args += [cos, sin, cos, sin]
    if past:
        in_specs += [pl.BlockSpec((ns, None, past, KV_WIDTH), lambda b, i: (b, l, 0, 0)),
                     pl.BlockSpec((ns, None, past, KV_WIDTH), lambda b, i: (b, l, 0, 0))]
        args += list(past_kv)
    aliases = {}
    if kv_all is not None:
        in_specs += [pl.BlockSpec(memory_space=pl.ANY), pl.BlockSpec(memory_space=pl.ANY)]
        aliases = {len(args): 1, len(args) + 1: 2}
        args += list(kv_all)
    cache_spec = pl.BlockSpec((ns, None, seq, KV_WIDTH), lambda b, i: (b, l, 0, 0))
    out_specs = [pl.BlockSpec((ns * tq, ATTN_WIDTH), lambda b, i: (b * nq + i, 0)), cache_spec, cache_spec]
    return pl.pallas_call(
        functools.partial(_attn_kernel, tq=tq, seq=seq, past=past, use_rope=use_rope, ns=ns),
        grid=(n_batch // ns, nq),
        in_specs=in_specs,
        out_specs=out_specs,
        out_shape=[jax.ShapeDtypeStruct((n_tok, ATTN_WIDTH), BF16),
                   jax.ShapeDtypeStruct((n_batch, DEPTH, seq, KV_WIDTH), F32),
                   jax.ShapeDtypeStruct((n_batch, DEPTH, seq, KV_WIDTH), F32)],
        input_output_aliases=aliases,
        scratch_shapes=[pltpu.VMEM((ns * 2 * N_KV_HEADS, past + seq, KV_WIDTH), BF16),
                        pltpu.VMEM((ns * 2 * N_KV_HEADS, past + seq, KV_WIDTH), BF16)],
        compiler_params=pltpu.CompilerParams(
            dimension_semantics=("parallel", "arbitrary"), vmem_limit_bytes=VMEM_LIMIT_BYTES),
        name="attention",
    )(*args)


def _post_mlp_kernel(y0_ref, y1_ref, bo0_ref, bo1_ref, lor_ref, oa_ref, gr_ref, ga_ref, x_ref, mod_ref,
                     gup_ref, gnw_ref, gnb_ref, wbr_ref, wout_ref, ng_ref, up_ref, down_ref, o_ref):
    ones_bd = _group_ones()
    y = y0_ref[...].astype(F32) + y1_ref[...].astype(F32)
    mean = _group_sum(y, ones_bd) * (1.0 / HEAD_DIM)
    yc = y - mean
    var = _group_sum(yc * yc, ones_bd) * (1.0 / HEAD_DIM)
    yn = yc * lax.rsqrt(var + GN_EPS) * gnw_ref[...] + gnb_ref[...]
    lg = lor_ref[:, LANES:]
    g = _mm(jax.nn.sigmoid(lg), gup_ref[...])
    o_r = (yn + (bo0_ref[...].astype(F32) + bo1_ref[...].astype(F32))) * g
    merged = (jax.nn.sigmoid(gr_ref[...].astype(F32)) * _mm(o_r, wbr_ref[0])
              + jax.nn.sigmoid(ga_ref[...].astype(F32)) * _mm(oa_ref[...], wbr_ref[1]))
    m = _mm(merged, wout_ref[...])
    g1 = mod_ref[0, 2:3, :]
    sh2 = mod_ref[0, 3:4, :]
    sc2 = mod_ref[0, 4:5, :]
    x1 = x_ref[...] + g1 * _rms(m, ng_ref[1:2, :])
    h = (_rms(x1, ng_ref[2:3, :]) * (1.0 + sc2) + sh2).astype(BF16)

    f = None
    ff = D_MODEL
    for j in range(D_FF // ff):
        u = lax.dot_general(h, up_ref[:, j * ff:(j + 1) * ff], NN, preferred_element_type=F32)
        u = jnp.square(jnp.maximum(u, 0.0)).astype(BF16)
        part = lax.dot_general(u, down_ref[j * ff:(j + 1) * ff, :], NN, preferred_element_type=F32)
        f = part if f is None else f + part
    g2 = mod_ref[0, 5:6, :]
    o_ref[...] = x1 + g2 * _rms(f, ng_ref[3:4, :])


def _post_mlp(y0, y1, bo0, bo1, rest, gates, o_a, x2d, P, l, mod_row0, rows_per_mod):
    n_tok = x2d.shape[0]
    tm = 512
    tok = lambda w: pl.BlockSpec((tm, w), lambda i: (i, 0))
    return pl.pallas_call(
        _post_mlp_kernel,
        grid=(n_tok // tm,),
        in_specs=[
            tok(RWKV_WIDTH), tok(RWKV_WIDTH), tok(RWKV_WIDTH), tok(RWKV_WIDTH),
            pl.BlockSpec((tm, 2 * LANES), lambda i: (i, COL_LORA // (2 * LANES))),
            tok(ATTN_WIDTH),
            pl.BlockSpec((tm, D_MODEL), lambda i: (i, 0)),
            pl.BlockSpec((tm, D_MODEL), lambda i: (i, 1)),
            tok(D_MODEL),
            _mod_spec(l, mod_row0, rows_per_mod, tm),
            _lspec(l, (GATE_RANK, RWKV_WIDTH)), _lspec(l, (1, RWKV_WIDTH)), _lspec(l, (1, RWKV_WIDTH)),
            _lspec(l, (2, RWKV_WIDTH, D_MODEL), single=True), _lspec(l, (D_MODEL, D_MODEL), single=True),
            _lspec(l, (4, D_MODEL)),
            _lspec(l, (D_MODEL, D_FF), single=True), _lspec(l, (D_FF, D_MODEL), single=True),
        ],
        out_specs=tok(D_MODEL),
        out_shape=jax.ShapeDtypeStruct((n_tok, D_MODEL), F32),
        compiler_params=pltpu.CompilerParams(
            dimension_semantics=("parallel",), vmem_limit_bytes=VMEM_LIMIT_BYTES),
        name="merge_mlp",
    )(y0, y1, bo0, bo1, rest, o_a, gates, gates, x2d, P['mod'], P['gup'], P['gn_w'], P['gn_b'], P['w_br'], P['w_out'],
      P['ng'], P['up'], P['down'])


def _rope_tables(seq):
    n_rows = seq // GRID_W
    rows = np.repeat(np.arange(n_rows, dtype=np.float32), GRID_W)
    cols = np.tile(np.arange(GRID_W, dtype=np.float32), n_rows)
    half = HEAD_DIM // 2
    freqs = 1.0 / (jnp.asarray(ROPE_THETA, F32) ** (jnp.arange(0, half, 2, dtype=F32) / half))
    ang_r = jnp.asarray(rows)[:, None] * freqs
    ang_c = jnp.asarray(cols)[:, None] * freqs
    cr, sr, cc, sc = jnp.cos(ang_r), jnp.sin(ang_r), jnp.cos(ang_c), jnp.sin(ang_c)
    cos64 = jnp.concatenate([cr, cr, cc, cc], axis=1)
    sin64 = jnp.concatenate([-sr, sr, -sc, sc], axis=1)
    return jnp.tile(cos64, (1, LANES // HEAD_DIM)), jnp.tile(sin64, (1, LANES // HEAD_DIM))


def _pairs_from_state(s):
    b = s.shape[0]
    s = s.reshape(b, 2, N_PAIRS, 2, HEAD_DIM, HEAD_DIM)
    return jnp.concatenate([s[:, :, :, 0], s[:, :, :, 1]], axis=-1)


def _layer(x2d, n_batch, seq, P, l, mod_row0, rows_per_mod, rope128, past_kv, s0, carry):
    kv_all, s_all = (None, None) if carry is None else (carry[:2], carry[2])
    rkv, gates, rest, kh = _in_projection(x2d, P, l, mod_row0, rows_per_mod, seq)
    s0_pairs = None if s0 is None else _pairs_from_state(s0)
    y0, y1, bo0, bo1, s_all = _wkv(rkv, rest, kh, s0_pairs, n_batch, seq, P, l, s_all)
    o_a, k_all, v_all = _attention(rest, n_batch, seq, P, l, rope128, past_kv, kv_all)
    x2 = _post_mlp(y0, y1, bo0, bo1, rest, gates, o_a, x2d, P, l, mod_row0, rows_per_mod)
    return x2, (k_all, v_all, s_all)


def kernel(x_prompt, x_sample, cache_k, cache_v, state_wkv, c, c_ctx, w_in, w_br, w_out, w_mod, b_mod, norm_g, mlp_up, mlp_down, rwkv_mu, rwkv_k_k, rwkv_k_a, rwkv_r_k, decay_w0, decay_up, iclr_a0, iclr_up, gate_up, gn_w, gn_b, q_gain, k_gain):
    n_ctx, seq_ctx, _ = x_prompt.shape
    n_dec, seq_dec, _ = x_sample.shape
    past = cache_k.shape[2]

    cvec8 = jnp.zeros((SUBLANES, D_MODEL), F32).at[0].set(c_ctx).at[1:1 + n_dec].set(c)
    mod = _modulation(cvec8, w_mod, b_mod)
    mod = jnp.pad(mod.reshape(DEPTH, SUBLANES, N_MOD, D_MODEL), ((0, 0), (0, 0), (0, SUBLANES - N_MOD), (0, 0)))

    o = np.cumsum((0, RWKV_WIDTH, RWKV_WIDTH, RWKV_WIDTH, DECAY_RANK, ICLR_RANK, GATE_RANK,
                   ATTN_WIDTH, KV_WIDTH, KV_WIDTH, 2 * D_MODEL))
    w_rest = jnp.concatenate([w_in[:, :, o[9]:o[10]], w_in[:, :, o[6]:o[7]], w_in[:, :, o[3]:o[6]],
                              w_in[:, :, o[7]:o[9]]], axis=-1)
    zpad = jnp.zeros((DEPTH, 2, DECAY_RANK, RWKV_WIDTH), F32)
    row = lambda a: a.reshape(DEPTH, 1, -1)
    tile2 = lambda a: jnp.tile(a, (1, LANES // HEAD_DIM)).reshape(DEPTH, 1, LANES)
    P = dict(
        mod=mod, ng=norm_g, w_rkv=w_in[:, :, o[0]:o[3]].astype(BF16), w_rest=w_rest.astype(BF16),
        mu=rwkv_mu, k_k=row(rwkv_k_k), k_a=row(rwkv_k_a), r_k=row(rwkv_r_k),
        w0=decay_w0, wup=jnp.concatenate([decay_up, zpad], axis=2).astype(BF16),
        a0=iclr_a0, aup=jnp.concatenate([zpad, iclr_up], axis=2).astype(BF16),
        gup=gate_up.astype(BF16), gn_w=row(gn_w), gn_b=row(gn_b),
        w_br=w_br.astype(BF16), w_out=w_out.astype(BF16), up=mlp_up.astype(BF16), down=mlp_down.astype(BF16),
        q_gain=tile2(q_gain), k_gain=tile2(k_gain),
    )

    x = x_prompt.reshape(n_ctx * seq_ctx, D_MODEL)
    carry = None
    for l in range(DEPTH):
        x, carry = _layer(x, n_ctx, seq_ctx, P, l, 0, 0, None, None, None, carry)
    y_prompt = x.reshape(n_ctx, seq_ctx, D_MODEL)
    new_k, new_v, new_s = carry
    cache_shape = (n_ctx, DEPTH, seq_ctx, N_KV_HEADS, HEAD_DIM)

    rope128 = _rope_tables(seq_dec)
    past_kv = (cache_k.reshape(n_dec, DEPTH, past, KV_WIDTH), cache_v.reshape(n_dec, DEPTH, past, KV_WIDTH))
    x = x_sample.reshape(n_dec * seq_dec, D_MODEL)
    for l in range(DEPTH):
        x, _ = _layer(x, n_dec, seq_dec, P, l, 1, seq_dec, rope128, past_kv, state_wkv[:, l], None)
    y_sample = x.reshape(n_dec, seq_dec, D_MODEL)

    return (y_prompt, y_sample, new_k.reshape(cache_shape), new_v.reshape(cache_shape), new_s)
```

```python
import functools

import numpy as np
import jax
import jax.numpy as jnp
from jax import lax
from jax.experimental import pallas as pl
from jax.experimental.pallas import tpu as pltpu

F32 = jnp.float32
BF16 = jnp.bfloat16

D_MODEL = 1024
DEPTH = 2
GRID_W = 64
HEAD_DIM = 64
N_RWKV_HEADS = 8
RWKV_WIDTH = N_RWKV_HEADS * HEAD_DIM
N_Q_HEADS = 8
N_KV_HEADS = 2
ATTN_WIDTH = N_Q_HEADS * HEAD_DIM
KV_WIDTH = N_KV_HEADS * HEAD_DIM
DECAY_RANK = 64
ICLR_RANK = 64
GATE_RANK = 128
D_FF = 4 * D_MODEL
ROPE_THETA = 10000.0
ROPE_FREQS = HEAD_DIM // 4
N_MOD = 6
NORM_EPS = 1e-6
GN_EPS = 64e-5
DECAY_SCALE = 0.606531
D_IN = 3 * RWKV_WIDTH + DECAY_RANK + ICLR_RANK + GATE_RANK + ATTN_WIDTH + 2 * KV_WIDTH + 2 * D_MODEL

LANES = 128
SUBLANES = 8
VMEM_LIMIT_BYTES = 56 * 1024 * 1024

RKV_W = 3 * RWKV_WIDTH
GATES_W = 2 * D_MODEL
COL_Q = 0
COL_LORA = COL_Q + ATTN_WIDTH
COL_KV = COL_LORA + 2 * LANES
REST_W = COL_KV + 2 * KV_WIDTH

WKV_CHUNK = 64
PAIR = 2 * HEAD_DIM
N_PAIRS = N_RWKV_HEADS // 2

NN = (((1,), (0,)), ((), ()))
NT = (((1,), (1,)), ((), ()))
TN = (((0,), (0,)), ((), ()))


def _mm(a, b, dims=NN):
    return lax.dot_general(a.astype(BF16), b.astype(BF16), dims, preferred_element_type=F32)


def _split2(x):
    hi = x.astype(BF16)
    lo = (x - hi.astype(F32)).astype(BF16)
    return hi, lo


def _div_pow2(i, n):
    return lax.shift_right_logical(i, int(np.log2(n)))


def _mod_pow2(i, n):
    return lax.bitwise_and(i, n - 1)


def _group_ones():
    i = lax.broadcasted_iota(jnp.int32, (LANES, LANES), 0)
    j = lax.broadcasted_iota(jnp.int32, (LANES, LANES), 1)
    return jnp.where(_div_pow2(i, HEAD_DIM) == _div_pow2(j, HEAD_DIM), 1.0, 0.0).astype(BF16)


def _group_sum(x, ones_bd):
    rows, n = x.shape[0], x.shape[1] // LANES
    parts = []
    for i in range(n):
        parts += list(_split2(x[:, i * LANES:(i + 1) * LANES]))
    r = lax.dot_general(jnp.concatenate(parts, axis=0), ones_bd, NN, preferred_element_type=F32)
    blocks = [r[2 * i * rows:(2 * i + 1) * rows] + r[(2 * i + 1) * rows:(2 * i + 2) * rows] for i in range(n)]
    return blocks[0] if n == 1 else jnp.concatenate(blocks, axis=1)


def _lspec(l, shape, single=False):
    kw = dict(pipeline_mode=pl.Buffered(1)) if single else {}
    return pl.BlockSpec((None,) + tuple(shape), lambda *g: (l,) + (0,) * len(shape), **kw)


def _mod_spec(l, row0, rows_per_mod, tm):
    if rows_per_mod:
        return pl.BlockSpec((None, 1, SUBLANES, D_MODEL), lambda i, *g: (l, row0 + i * tm // rows_per_mod, 0, 0))
    return pl.BlockSpec((None, 1, SUBLANES, D_MODEL), lambda *g: (l, row0, 0, 0))


def _rms(x, g):
    return x * lax.rsqrt(jnp.mean(x * x, axis=-1, keepdims=True) + NORM_EPS) * g


def _mod_kernel(c_ref, w_ref, b_ref, o_ref):
    c = c_ref[...]
    s = c * jax.nn.sigmoid(c)
    o_ref[0] = _mm(s, w_ref[0]) + b_ref[0]


def _modulation(cvec8, w_mod, b_mod):
    tn = D_MODEL
    n = N_MOD * D_MODEL
    return pl.pallas_call(
        _mod_kernel,
        grid=(DEPTH, n // tn),
        in_specs=[
            pl.BlockSpec((SUBLANES, D_MODEL), lambda l, j: (0, 0)),
            pl.BlockSpec((1, D_MODEL, tn), lambda l, j: (l, 0, j)),
            pl.BlockSpec((1, 1, tn), lambda l, j: (l, 0, j)),
        ],
        out_specs=pl.BlockSpec((1, SUBLANES, tn), lambda l, j: (l, 0, j)),
        out_shape=jax.ShapeDtypeStruct((DEPTH, SUBLANES, n), F32),
        name="modulation",
    )(cvec8, w_mod, b_mod.reshape(DEPTH, 1, n))


def _shift_mix(x, mu):
    n = x.shape[0]
    row = lax.broadcasted_iota(jnp.int32, x.shape, 0)
    x_prev = jnp.where(row == 0, 0.0, pltpu.roll(x, 1, 0))
    x_next = jnp.where(row == n - 1, 0.0, pltpu.roll(x, n - 1, 0))
    return x * (1.0 - mu) + (0.5 * mu) * (x_prev + x_next)


def _inproj_kernel(x_ref, mod_ref, ng_ref, wrkv_ref, wrest_ref, mu_ref, kk_ref,
                   rkv_ref, gates_ref, rest_ref, kh_ref, h_scr, *, tm, seq):
    C = RWKV_WIDTH

    def mix_in_place(i, piece):
        lanes = slice(piece * LANES, (piece + 1) * LANES)
        cols = slice(i * C + piece * LANES, i * C + (piece + 1) * LANES)
        for s in range(tm // seq):
            rows = slice(s * seq, (s + 1) * seq)
            mixed = _shift_mix(rkv_ref[rows, cols], mu_ref[i:i + 1, lanes])
            rkv_ref[rows, cols] = mixed
            if i == 1:
                kk = mixed * kk_ref[:, lanes]
                kh_ref[rows, lanes] = kk * lax.rsqrt(_group_sum(kk * kk, _group_ones()) + 1e-12)

    sh = mod_ref[0, 0:1, :]
    sc = mod_ref[0, 1:2, :]
    ng = ng_ref[0:1, :]
    sub = 256
    for i in range(tm // sub):
        x = x_ref[i * sub:(i + 1) * sub, :]
        h_scr[i * sub:(i + 1) * sub, :] = (_rms(x, ng) * (1.0 + sc) + sh).astype(BF16)
    rkv_ref[...] = lax.dot_general(h_scr[...], wrkv_ref[...], NN, preferred_element_type=F32)

    tn = 2 * LANES
    n_piece = C // LANES
    for chunk in range((GATES_W + REST_W) // tn):
        cols = slice(chunk * tn, (chunk + 1) * tn)
        out = lax.dot_general(h_scr[...], wrest_ref[:, cols], NN, preferred_element_type=F32)
        if chunk * tn < GATES_W:
            gates_ref[:, cols] = out.astype(BF16)
        else:
            rest_ref[:, chunk * tn - GATES_W:(chunk + 1) * tn - GATES_W] = out
        mix_in_place(chunk // n_piece, chunk % n_piece)


def _in_projection(x2d, P, l, mod_row0, rows_per_mod, seq):
    n_tok = x2d.shape[0]
    tm = 1024
    assert tm % seq == 0 and (GATES_W + REST_W) // (2 * LANES) == 3 * (RWKV_WIDTH // LANES)
    tok = lambda w: pl.BlockSpec((tm, w), lambda i: (i, 0))
    return pl.pallas_call(
        functools.partial(_inproj_kernel, tm=tm, seq=seq),
        grid=(n_tok // tm,),
        in_specs=[
            tok(D_MODEL),
            _mod_spec(l, mod_row0, rows_per_mod, tm),
            _lspec(l, (4, D_MODEL)),
            _lspec(l, (D_MODEL, RKV_W), single=True),
            _lspec(l, (D_MODEL, GATES_W + REST_W), single=True),
            _lspec(l, (3, RWKV_WIDTH)),
            _lspec(l, (1, RWKV_WIDTH)),
        ],
        out_specs=[tok(RKV_W), tok(GATES_W), tok(REST_W), tok(RWKV_WIDTH)],
        out_shape=[jax.ShapeDtypeStruct((n_tok, RKV_W), F32),
                   jax.ShapeDtypeStruct((n_tok, GATES_W), BF16),
                   jax.ShapeDtypeStruct((n_tok, REST_W), F32),
                   jax.ShapeDtypeStruct((n_tok, RWKV_WIDTH), F32)],
        scratch_shapes=[pltpu.VMEM((tm, D_MODEL), BF16)],
        compiler_params=pltpu.CompilerParams(
            dimension_semantics=("parallel",), vmem_limit_bytes=VMEM_LIMIT_BYTES),
        name="in_projection",
    )(x2d, P['mod'], P['ng'], P['w_rkv'], P['w_rest'], P['mu'], P['k_k'])


def _wkv_masks(reverse):
    L = WKV_CHUNK
    row = lax.broadcasted_iota(jnp.int32, (L, PAIR), 0)
    col = _mod_pow2(lax.broadcasted_iota(jnp.int32, (L, PAIR), 1), L)
    return dict(
        strict=(col > row) if reverse else (col < row),
        incl=(col >= row) if reverse else (col <= row),
        blk8=_div_pow2(row, 8) == _div_pow2(col, 8),
        eye=row == col,
    )


def _bd(x, left):
    x = x.astype(BF16)
    z = jnp.zeros_like(x)
    return jnp.concatenate([jnp.where(left, x, z), jnp.where(left, z, x)], axis=0)


def _wkv_operand_scratch(nb):
    C, L = RWKV_WIDTH, WKV_CHUNK
    full = lambda dt: pltpu.VMEM((nb, 2, L, C), dt)
    pair = lambda rows: pltpu.VMEM((nb, 2, N_PAIRS, rows, PAIR), BF16)
    return dict(khd=full(F32), rd=full(F32), bonus=full(F32), etot=pltpu.VMEM((nb, 2, SUBLANES, C), F32),
                be=full(BF16), kte=full(BF16), vb=full(BF16),
                lhsA=pair(2 * L), rhsA=pair(4 * L), vbd=pair(2 * L), khdbd=pair(2 * L))


def _wkv_lora(nb, lors, w0_ref, wup_ref, a0_ref, aup_ref, lora_out):
    L = WKV_CHUNK
    for d, lor in enumerate(lors):
        lo = jnp.concatenate([lor[e, 0, :, 0:LANES] for e in range(nb)], axis=0)
        w_raw = _mm(jnp.tanh(lo), wup_ref[d]) + w0_ref[d:d + 1, :]
        a_raw = _mm(lo, aup_ref[d]) + a0_ref[d:d + 1, :]
        for e in range(nb):
            lora_out[e, d] = (w_raw[e * L:(e + 1) * L], a_raw[e * L:(e + 1) * L])


def _wkv_prepare(e, d, rkv, khr, lora, ka_ref, rk_ref, ops, ones_bd, tri, left):
    C, L = RWKV_WIDTH, WKV_CHUNK
    reverse = d == 1
    r = rkv[e, 0, :, 0:C]
    k = rkv[e, 0, :, C:2 * C]
    v = rkv[e, 0, :, 2 * C:3 * C]
    kh = khr[e, 0]
    w_raw, a_raw = lora[e, d]
    lw = -DECAY_SCALE * jax.nn.sigmoid(w_raw)
    a = jax.nn.sigmoid(a_raw)
    kt = k * (1.0 + (a - 1.0) * ka_ref[...])
    ops['bonus'][e, d] = _group_sum(r * kt * rk_ref[...], ones_bd) * v
    b = a * kh

    hi, lo2 = _split2(lw)
    cum = (lax.dot_general(tri[d], hi, NN, preferred_element_type=F32)
           + lax.dot_general(tri[d], lo2, NN, preferred_element_type=F32))
    tot = cum[0:1, :] if reverse else cum[L - 1:L, :]
    e_inv = jnp.exp(-cum)
    e_end = jnp.exp(tot - cum)
    khd = kh * jnp.exp(cum - lw)
    rd = r * jnp.exp(cum)
    bi = b * e_inv
    ki = kt * e_inv
    ops['khd'][e, d] = khd
    ops['rd'][e, d] = rd
    ops['etot'][e, d] = jnp.broadcast_to(jnp.exp(tot), (SUBLANES, C))
    ops['be'][e, d] = (b * e_end).astype(BF16)
    ops['kte'][e, d] = (kt * e_end).astype(BF16)
    ops['vb'][e, d] = v.astype(BF16)
    for p in range(N_PAIRS):
        sl = slice(p * PAIR, (p + 1) * PAIR)
        ops['lhsA'][e, d, p] = jnp.concatenate([khd[:, sl], rd[:, sl]], axis=0).astype(BF16)
        ops['rhsA'][e, d, p] = jnp.concatenate([_bd(bi[:, sl], left), _bd(ki[:, sl], left)], axis=0)
        ops['vbd'][e, d, p] = _bd(v[:, sl], left)
        ops['khdbd'][e, d, p] = _bd(khd[:, sl], left)


def _wkv_chains(chains, left, fillers):
    L = WKV_CHUNK
    i2 = lax.broadcasted_iota(jnp.int32, (PAIR, PAIR), 0)
    j2 = lax.broadcasted_iota(jnp.int32, (PAIR, PAIR), 1)
    same = _div_pow2(i2, HEAD_DIM) == _div_pow2(j2, HEAD_DIM)
    eye2 = i2 == j2
    bd = lambda x: _bd(x, left)
    lp = lambda g: jnp.where(left, g[:L], g[L:])
    n_stages = 13
    due = {((k + 1) * n_stages) // (len(fillers) + 1): f for k, f in enumerate(fillers)}
    assert len(due) == len(fillers) and min(due) >= 1 and max(due) <= n_stages, "every filler needs its own stage"
    emitted = [0]

    def each(f):
        for ch in chains:
            f(ch)
        emitted[0] += 1
        if emitted[0] in due:
            due[emitted[0]]()

    def a_blocks(ch):
        m = ch['masks']
        A = lax.dot_general(ch['lhsA'], ch['rhsA'], NT, preferred_element_type=F32)
        ch['A_ub'] = jnp.where(m['strict'], A[:L, :PAIR], 0.0)
        ch['A_uvrv'] = jnp.concatenate([jnp.where(m['strict'], A[:L, PAIR:], 0.0),
                                        jnp.where(m['incl'], A[L:, PAIR:], 0.0)], axis=0).astype(BF16)
        ch['A_rb'] = jnp.where(m['incl'], A[L:, :PAIR], 0.0).astype(BF16)
        a8 = jnp.where(m['blk8'], ch['A_ub'], 0.0)
        ch['A8'] = a8
        ch['X0'] = jnp.where(m['eye'], 1.0, -a8)
        ch['Nlow'] = ch['A_ub'] - a8
    each(a_blocks)

    stack = lambda a, b: jnp.concatenate([a.astype(BF16), b.astype(BF16)], axis=0)

    def sq1(ch):
        ch['A8_2'] = _mm(ch['A8'], bd(ch['A8']))
    each(sq1)

    def sq2(ch):
        r = _mm(stack(ch['A8_2'], ch['X0']), bd(ch['A8_2']))
        ch['A8_4'] = r[:L]
        ch['X1'] = ch['X0'] + r[L:]
        ch['AV'] = _mm(ch['A_uvrv'], ch['vbd'])
    each(sq2)

    def dinv(ch):
        ch['T8'] = ch['X1'] + _mm(ch['X1'], bd(ch['A8_4']))
    each(dinv)

    def e1(ch):
        ch['E'] = _mm(ch['T8'], bd(ch['Nlow']))
    each(e1)

    def e2(ch):
        ch['E2'] = _mm(ch['E'], bd(ch['E']))
    each(e2)

    def e3(ch):
        ime = jnp.where(ch['masks']['eye'], 1.0, 0.0) - ch['E']
        r = _mm(stack(ch['E2'], ime), bd(ch['E2']))
        ch['E4'] = r[:L]
        ch['Y1'] = ime + r[L:]
    each(e3)

    def e4(ch):
        ch['Y2'] = ch['Y1'] + _mm(ch['Y1'], bd(ch['E4']))
    each(e4)

    def t_stage(ch):
        ch['T'] = _mm(ch['Y2'], bd(ch['T8']))
    each(t_stage)

    def p_stage(ch):
        ch['P'] = _mm(ch['A_rb'], bd(ch['T']))
    each(p_stage)

    def q_stage(ch):
        rhs = jnp.concatenate([ch['khdbd'], bd(ch['AV'][:L])], axis=1)
        r = _mm(stack(ch['T'], ch['P']), rhs)
        ch['Q'] = r[:L].astype(BF16)
        ch['G1'] = ch['rd'] - r[L:, :PAIR]
        ch['G2'] = ch['AV'][L:] - r[L:, PAIR:]
    each(q_stage)

    def mn_stage(ch):
        QB = _mm(ch['Q'], ch['be'], TN)
        ch['M'] = (jnp.where(eye2, ch['etot'], 0.0) - jnp.where(same, QB[:PAIR], 0.0)).astype(BF16)
        ch['N'] = lp(_mm(ch['vb'], ch['kte'], TN)) - lp(QB[PAIR:])
    each(mn_stage)

    def out_stage(ch):
        ch['y'] = _mm(ch['G1'], bd(ch['S']), NT) + ch['G2']
        ch['Sn'] = _mm(ch['S'], ch['M']) + ch['N']
    each(out_stage)


def _wkv_kernel(rkv0, khr0, lor0, rkv1, khr1, lor1,
                ka_ref, rk_ref, w0_ref, wup_ref, a0_ref, aup_ref, s0_ref,
                *rest, zero_state, nb, nc, op_names):
    n_ops = len(op_names)
    y0_ref, y1_ref, bo0_ref, bo1_ref, sfin_ref, s_scr = rest[-(6 + 2 * n_ops):len(rest) - 2 * n_ops]
    op_sets = [dict(zip(op_names, rest[len(rest) - (2 - i) * n_ops:len(rest) - (1 - i) * n_ops])) for i in (0, 1)]
    t = pl.program_id(0)
    c = lax.rem(jnp.maximum(t - 1, 0), nc)

    @pl.when(t == 0)
    def _():
        for ref in op_sets[1].values():
            ref[...] = jnp.zeros(ref.shape, ref.dtype)

    @pl.when(c == 0)
    def _():
        if zero_state:
            s_scr[...] = jnp.zeros(s_scr.shape, F32)
        else:
            s_scr[...] = s0_ref[...]

    def step(ops_in, ops_out):
        L = WKV_CHUNK
        left = lax.broadcasted_iota(jnp.int32, (L, PAIR), 1) < HEAD_DIM
        masks = [_wkv_masks(False), _wkv_masks(True)]

        chains = []
        for e in range(nb):
            for d, (y_ref, bo_ref) in enumerate(((y0_ref, bo0_ref), (y1_ref, bo1_ref))):
                bo_ref[e, 0] = ops_in['bonus'][e, d].astype(BF16)
                for p in range(N_PAIRS):
                    sl = slice(p * PAIR, (p + 1) * PAIR)
                    chains.append(dict(
                        e=e, d=d, p=p, sl=sl, y_ref=y_ref, masks=masks[d], S=s_scr[e, d, p],
                        lhsA=ops_in['lhsA'][e, d, p], rhsA=ops_in['rhsA'][e, d, p], vbd=ops_in['vbd'][e, d, p],
                        khdbd=ops_in['khdbd'][e, d, p], rd=ops_in['rd'][e, d, :, sl], be=ops_in['be'][e, d, :, sl],
                        kte=ops_in['kte'][e, d, :, sl], vb=ops_in['vb'][e, d, :, sl],
                        etot=ops_in['etot'][e, d, 0:1, sl]))
        ones_bd = _group_ones()
        ti = lax.broadcasted_iota(jnp.int32, (L, L), 0)
        tj = lax.broadcasted_iota(jnp.int32, (L, L), 1)
        tri = [jnp.where(tj <= ti, 1.0, 0.0).astype(BF16), jnp.where(tj >= ti, 1.0, 0.0).astype(BF16)]
        lora = {}
        prepare = [functools.partial(_wkv_lora, nb, (lor0, lor1), w0_ref, wup_ref, a0_ref, aup_ref, lora)] + [
            functools.partial(_wkv_prepare, e, d, rkv, khr, lora, ka_ref, rk_ref, ops_out, ones_bd, tri, left)
            for e in range(nb) for d, (rkv, khr) in enumerate(((rkv0, khr0), (rkv1, khr1)))]
        _wkv_chains(chains, left, prepare)
        for ch in chains:
            ch['y_ref'][ch['e'], 0, :, ch['sl']] = ch['y'].astype(BF16)
            s_scr[ch['e'], ch['d'], ch['p']] = ch['Sn']

    parity = lax.rem(t, 2)

    @pl.when(parity == 0)
    def _():
        step(op_sets[1], op_sets[0])

    @pl.when(parity == 1)
    def _():
        step(op_sets[0], op_sets[1])

    @pl.when((c == nc - 1) & (t > 0))
    def _():
        for e in range(nb):
            for d in (0, 1):
                for p in range(N_PAIRS):
                    s2 = s_scr[e, d, p]
                    sfin_ref[e, d, 2 * p] = s2[:, :HEAD_DIM]
                    sfin_ref[e, d, 2 * p + 1] = pltpu.roll(s2, HEAD_DIM, 1)[:, :HEAD_DIM]


def _wkv(rkv, rest, kh, s0_pairs, n_batch, seq, P, l, s_all):
    L = WKV_CHUNK
    nc = seq // L
    nb = 4
    n_steps = (n_batch // nb) * nc
    n_tok = n_batch * seq
    zero_state = s0_pairs is None
    if zero_state:
        s0_pairs = jnp.zeros((nb, 2, N_PAIRS, HEAD_DIM, PAIR), F32)
    rkv4 = rkv.reshape(n_batch, nc, L, RKV_W)
    rest4 = rest.reshape(n_batch, nc, L, REST_W)
    kh4 = kh.reshape(n_batch, nc, L, RWKV_WIDTH)

    def in_pos(t, d):
        s = jnp.minimum(t, n_steps - 1)
        c = lax.rem(s, nc)
        return lax.div(s, nc), (c if d == 0 else nc - 1 - c)

    def out_pos(t, d):
        s = jnp.maximum(t - 1, 0)
        c = lax.rem(s, nc)
        return lax.div(s, nc), (c if d == 0 else nc - 1 - c)

    in_specs = []
    for d in (0, 1):
        in_specs += [
            pl.BlockSpec((nb, 1, L, RKV_W), lambda t, d=d: (*in_pos(t, d), 0, 0)),
            pl.BlockSpec((nb, 1, L, RWKV_WIDTH), lambda t, d=d: (*in_pos(t, d), 0, 0)),
            pl.BlockSpec((nb, 1, L, 2 * LANES), lambda t, d=d: (*in_pos(t, d), 0, COL_LORA // (2 * LANES))),
        ]
    state_spec = lambda idx: pl.BlockSpec((nb, 2, N_PAIRS, HEAD_DIM, PAIR), idx)
    in_specs += [
        _lspec(l, (1, RWKV_WIDTH)), _lspec(l, (1, RWKV_WIDTH)),
        _lspec(l, (2, RWKV_WIDTH)), _lspec(l, (2, LANES, RWKV_WIDTH)),
        _lspec(l, (2, RWKV_WIDTH)), _lspec(l, (2, LANES, RWKV_WIDTH)),
        state_spec((lambda t: (0, 0, 0, 0, 0)) if zero_state else (lambda t: (out_pos(t, 0)[0], 0, 0, 0, 0))),
    ]
    args = [rkv4, kh4, rest4, rkv4, kh4, rest4, P['k_a'], P['r_k'], P['w0'], P['wup'], P['a0'], P['aup'], s0_pairs]
    aliases = {}
    if s_all is not None:
        in_specs.append(pl.BlockSpec(memory_space=pl.ANY))
        aliases = {len(args): 4}
        args.append(s_all)
    tok_spec = lambda d: pl.BlockSpec((nb, 1, L, RWKV_WIDTH), lambda t, d=d: (*out_pos(t, d), 0, 0))
    sfin_spec = pl.BlockSpec((nb, None, 2, N_RWKV_HEADS, HEAD_DIM, HEAD_DIM),
                             lambda t: (out_pos(t, 0)[0], l, 0, 0, 0, 0))
    out_specs = [tok_spec(0), tok_spec(1), tok_spec(0), tok_spec(1), sfin_spec]
    tok = jax.ShapeDtypeStruct((n_batch, nc, L, RWKV_WIDTH), BF16)
    op_scratch = _wkv_operand_scratch(nb)
    outs = pl.pallas_call(
        functools.partial(_wkv_kernel, zero_state=zero_state, nb=nb, nc=nc, op_names=tuple(op_scratch)),
        grid=(n_steps + 1,),
        in_specs=in_specs,
        out_specs=out_specs,
        out_shape=[tok, tok, tok, tok,
                   jax.ShapeDtypeStruct((n_batch, DEPTH, 2, N_RWKV_HEADS, HEAD_DIM, HEAD_DIM), F32)],
        input_output_aliases=aliases,
        scratch_shapes=([pltpu.VMEM((nb, 2, N_PAIRS, HEAD_DIM, PAIR), F32)]
                        + list(op_scratch.values()) + list(_wkv_operand_scratch(nb).values())),
        compiler_params=pltpu.CompilerParams(
            dimension_semantics=("arbitrary",), vmem_limit_bytes=VMEM_LIMIT_BYTES),
        name="wkv_scan",
    )(*args)
    y0, y1, bo0, bo1, s_fin = outs
    flat = lambda t: t.reshape(n_tok, RWKV_WIDTH)
    return flat(y0), flat(y1), flat(bo0), flat(bo1), s_fin


def _rope(x, cos, sin_signed):
    w = x.shape[1]
    lane = lax.broadcasted_iota(jnp.int32, x.shape, 1)
    partner = jnp.where(_mod_pow2(lane, 2 * ROPE_FREQS) < ROPE_FREQS,
                        pltpu.roll(x, w - ROPE_FREQS, 1), pltpu.roll(x, ROPE_FREQS, 1))
    return x * cos + partner * sin_signed


def _attn_kernel(*refs, tq, seq, past, use_rope, ns):
    it = iter(refs)
    q_ref, kv_ref, qg_ref, kg_ref = next(it), next(it), next(it), next(it)
    if use_rope:
        cq_ref, sq_ref, ck_ref, sk_ref = next(it), next(it), next(it), next(it)
    if past:
        pk_ref, pv_ref = next(it), next(it)
    o_ref, ko_ref, vo_ref, kvar_scr, vvar_scr = refs[-5:]

    ones_bd = _group_ones()

    @pl.when(pl.program_id(1) == 0)
    def _():
        for sq in range(ns):
            kv = kv_ref[sq * seq:(sq + 1) * seq, :]
            k_raw = kv[:, :KV_WIDTH]
            kn = k_raw * lax.rsqrt(_group_sum(k_raw * k_raw, ones_bd) * (1.0 / HEAD_DIM) + NORM_EPS) * kg_ref[...]
            if use_rope:
                kn = _rope(kn, ck_ref[...], sk_ref[...])
            vn = kv[:, KV_WIDTH:]
            ko_ref[sq] = kn
            vo_ref[sq] = vn
            pieces = [(past, seq, kn, vn)]
            if past:
                pieces.append((0, past, pk_ref[sq], pv_ref[sq]))
            for start, n, kx, vx in pieces:
                left = lax.broadcasted_iota(jnp.int32, kx.shape, 1) < HEAD_DIM
                for x, scr in ((kx, kvar_scr), (vx, vvar_scr)):
                    sw = pltpu.roll(x, HEAD_DIM, 1)
                    scr[4 * sq + 0, start:start + n, :] = jnp.where(left, x, 0.0).astype(BF16)
                    scr[4 * sq + 1, start:start + n, :] = jnp.where(left, 0.0, sw).astype(BF16)
                    scr[4 * sq + 2, start:start + n, :] = jnp.where(left, sw, 0.0).astype(BF16)
                    scr[4 * sq + 3, start:start + n, :] = jnp.where(left, 0.0, x).astype(BF16)

    scale = HEAD_DIM ** -0.5
    n_blk = ATTN_WIDTH // LANES
    blocks = [(sq, jb) for sq in range(ns) for jb in range(n_blk)]
    qn = []
    for sq, jb in blocks:
        qb = q_ref[sq * tq:(sq + 1) * tq, jb * LANES:(jb + 1) * LANES]
        x = qb * lax.rsqrt(_group_sum(qb * qb, ones_bd) * (1.0 / HEAD_DIM) + NORM_EPS) * qg_ref[...]
        if use_rope:
            x = _rope(x, cq_ref[...], sq_ref[...])
        qn.append((x * scale).astype(BF16))
    heads = [(i, 4 * sq + 2 * (jb // (n_blk // N_KV_HEADS)) + side)
             for i, (sq, jb) in enumerate(blocks) for side in (0, 1)]
    scores = [lax.dot_general(qn[i], kvar_scr[var], NT, preferred_element_type=F32) for i, var in heads]
    exps = [jnp.exp(s - jnp.max(s, axis=-1, keepdims=True)) for s in scores]
    outs = [lax.dot_general(e.astype(BF16), vvar_scr[var], NN, preferred_element_type=F32)
            for e, (i, var) in zip(exps, heads)]
    outs = [o / jnp.sum(e, axis=-1, keepdims=True) for o, e in zip(outs, exps)]
    for i, (sq, jb) in enumerate(blocks):
        o_ref[sq * tq:(sq + 1) * tq, jb * LANES:(jb + 1) * LANES] = (outs[2 * i] + outs[2 * i + 1]).astype(BF16)


def _attention(rest, n_batch, seq, P, l, rope128, past_kv, kv_all):
    tq = min(seq, 512)
    nq = seq // tq
    ns = 4 if nq == 1 else 1
    n_tok = n_batch * seq
    use_rope = rope128 is not None
    past = 0 if past_kv is None else past_kv[0].shape[2]
    in_specs = [
        pl.BlockSpec((ns * tq, ATTN_WIDTH), lambda b, i: (b * nq + i, COL_Q // ATTN_WIDTH)),
        pl.BlockSpec((ns * seq, 2 * KV_WIDTH), lambda b, i: (b, COL_KV // (2 * KV_WIDTH))),
        _lspec(l, (1, LANES)),
        _lspec(l, (1, LANES)),
    ]
    args = [rest, rest, P['q_gain'], P['k_gain']]
    if use_rope:
        cos, sin = rope128
        in_specs += [pl.BlockSpec((tq, LANES), lambda b, i: (i, 0)),
                     pl.BlockSpec((tq, LANES), lambda b, i: (i, 0)),
                     pl.BlockSpec((seq, LANES), lambda b, i: (0, 0)),
                     pl.BlockSpec((seq, LANES), lambda b, i: (0, 0))]
        args += [cos, sin, cos, sin]
    if past:
        in_specs += [pl.BlockSpec((ns, None, past, KV_WIDTH), lambda b, i: (b, l, 0, 0)),
                     pl.BlockSpec((ns, None, past, KV_WIDTH), lambda b, i: (b, l, 0, 0))]
        args += list(past_kv)
    aliases = {}
    if kv_all is not None:
        in_specs += [pl.BlockSpec(memory_space=pl.ANY), pl.BlockSpec(memory_space=pl.ANY)]
        aliases = {len(args): 1, len(args) + 1: 2}
        args += list(kv_all)
    cache_spec = pl.BlockSpec((ns, None, seq, KV_WIDTH), lambda b, i: (b, l, 0, 0))
    out_specs = [pl.BlockSpec((ns * tq, ATTN_WIDTH), lambda b, i: (b * nq + i, 0)), cache_spec, cache_spec]
    return pl.pallas_call(
        functools.partial(_attn_kernel, tq=tq, seq=seq, past=past, use_rope=use_rope, ns=ns),
        grid=(n_batch // ns, nq),
        in_specs=in_specs,
        out_specs=out_specs,
        out_shape=[jax.ShapeDtypeStruct((n_tok, ATTN_WIDTH), BF16),
                   jax.ShapeDtypeStruct((n_batch, DEPTH, seq, KV_WIDTH), F32),
                   jax.ShapeDtypeStruct((n_batch, DEPTH, seq, KV_WIDTH), F32)],
        input_output_aliases=aliases,
        scratch_shapes=[pltpu.VMEM((ns * 2 * N_KV_HEADS, past + seq, KV_WIDTH), BF16),
                        pltpu.VMEM((ns * 2 * N_KV_HEADS, past + seq, KV_WIDTH), BF16)],
        compiler_params=pltpu.CompilerParams(
            dimension_semantics=("parallel", "arbitrary"), vmem_limit_bytes=VMEM_LIMIT_BYTES),
        name="attention",
    )(*args)


def _post_mlp_kernel(y0_ref, y1_ref, bo0_ref, bo1_ref, lor_ref, oa_ref, gr_ref, ga_ref, x_ref, mod_ref,
                     gup_ref, gnw_ref, gnb_ref, wbr_ref, wout_ref, ng_ref, up_ref, down_ref, o_ref):
    ones_bd = _group_ones()
    y = y0_ref[...].astype(F32) + y1_ref[...].astype(F32)
    mean = _group_sum(y, ones_bd) * (1.0 / HEAD_DIM)
    yc = y - mean
    var = _group_sum(yc * yc, ones_bd) * (1.0 / HEAD_DIM)
    yn = yc * lax.rsqrt(var + GN_EPS) * gnw_ref[...] + gnb_ref[...]
    lg = lor_ref[:, LANES:]
    g = _mm(jax.nn.sigmoid(lg), gup_ref[...])
    o_r = (yn + (bo0_ref[...].astype(F32) + bo1_ref[...].astype(F32))) * g
    merged = (jax.nn.sigmoid(gr_ref[...].astype(F32)) * _mm(o_r, wbr_ref[0])
              + jax.nn.sigmoid(ga_ref[...].astype(F32)) * _mm(oa_ref[...], wbr_ref[1]))
    m = _mm(merged, wout_ref[...])
    g1 = mod_ref[0, 2:3, :]
    sh2 = mod_ref[0, 3:4, :]
    sc2 = mod_ref[0, 4:5, :]
    x1 = x_ref[...] + g1 * _rms(m, ng_ref[1:2, :])
    h = (_rms(x1, ng_ref[2:3, :]) * (1.0 + sc2) + sh2).astype(BF16)

    f = None
    ff = D_MODEL
    for j in range(D_FF // ff):
        u = lax.dot_general(h, up_ref[:, j * ff:(j + 1) * ff], NN, preferred_element_type=F32)
        u = jnp.square(jnp.maximum(u, 0.0)).astype(BF16)
        part = lax.dot_general(u, down_ref[j * ff:(j + 1) * ff, :], NN, preferred_element_type=F32)
        f = part if f is None else f + part
    g2 = mod_ref[0, 5:6, :]
    o_ref[...] = x1 + g2 * _rms(f, ng_ref[3:4, :])


def _post_mlp(y0, y1, bo0, bo1, rest, gates, o_a, x2d, P, l, mod_row0, rows_per_mod):
    n_tok = x2d.shape[0]
    tm = 512
    tok = lambda w: pl.BlockSpec((tm, w), lambda i: (i, 0))
    return pl.pallas_call(
        _post_mlp_kernel,
        grid=(n_tok // tm,),
        in_specs=[
            tok(RWKV_WIDTH), tok(RWKV_WIDTH), tok(RWKV_WIDTH), tok(RWKV_WIDTH),
            pl.BlockSpec((tm, 2 * LANES), lambda i: (i, COL_LORA // (2 * LANES))),
            tok(ATTN_WIDTH),
            pl.BlockSpec((tm, D_MODEL), lambda i: (i, 0)),
            pl.BlockSpec((tm, D_MODEL), lambda i: (i, 1)),
            tok(D_MODEL),
            _mod_spec(l, mod_row0, rows_per_mod, tm),
            _lspec(l, (GATE_RANK, RWKV_WIDTH)), _lspec(l, (1, RWKV_WIDTH)), _lspec(l, (1, RWKV_WIDTH)),
            _lspec(l, (2, RWKV_WIDTH, D_MODEL), single=True), _lspec(l, (D_MODEL, D_MODEL), single=True),
            _lspec(l, (4, D_MODEL)),
            _lspec(l, (D_MODEL, D_FF), single=True), _lspec(l, (D_FF, D_MODEL), single=True),
        ],
        out_specs=tok(D_MODEL),
        out_shape=jax.ShapeDtypeStruct((n_tok, D_MODEL), F32),
        compiler_params=pltpu.CompilerParams(
            dimension_semantics=("parallel",), vmem_limit_bytes=VMEM_LIMIT_BYTES),
        name="merge_mlp",
    )(y0, y1, bo0, bo1, rest, o_a, gates, gates, x2d, P['mod'], P['gup'], P['gn_w'], P['gn_b'], P['w_br'], P['w_out'],
      P['ng'], P['up'], P['down'])


def _rope_tables(seq):
    n_rows = seq // GRID_W
    rows = np.repeat(np.arange(n_rows, dtype=np.float32), GRID_W)
    cols = np.tile(np.arange(GRID_W, dtype=np.float32), n_rows)
    half = HEAD_DIM // 2
    freqs = 1.0 / (jnp.asarray(ROPE_THETA, F32) ** (jnp.arange(0, half, 2, dtype=F32) / half))
    ang_r = jnp.asarray(rows)[:, None] * freqs
    ang_c = jnp.asarray(cols)[:, None] * freqs
    cr, sr, cc, sc = jnp.cos(ang_r), jnp.sin(ang_r), jnp.cos(ang_c), jnp.sin(ang_c)
    cos64 = jnp.concatenate([cr, cr, cc, cc], axis=1)
    sin64 = jnp.concatenate([-sr, sr, -sc, sc], axis=1)
    return jnp.tile(cos64, (1, LANES // HEAD_DIM)), jnp.tile(sin64, (1, LANES // HEAD_DIM))


def _pairs_from_state(s):
    b = s.shape[0]
    s = s.reshape(b, 2, N_PAIRS, 2, HEAD_DIM, HEAD_DIM)
    return jnp.concatenate([s[:, :, :, 0], s[:, :, :, 1]], axis=-1)


def _layer(x2d, n_batch, seq, P, l, mod_row0, rows_per_mod, rope128, past_kv, s0, carry):
    kv_all, s_all = (None, None) if carry is None else (carry[:2], carry[2])
    rkv, gates, rest, kh = _in_projection(x2d, P, l, mod_row0, rows_per_mod, seq)
    s0_pairs = None if s0 is None else _pairs_from_state(s0)
    y0, y1, bo0, bo1, s_all = _wkv(rkv, rest, kh, s0_pairs, n_batch, seq, P, l, s_all)
    o_a, k_all, v_all = _attention(rest, n_batch, seq, P, l, rope128, past_kv, kv_all)
    x2 = _post_mlp(y0, y1, bo0, bo1, rest, gates, o_a, x2d, P, l, mod_row0, rows_per_mod)
    return x2, (k_all, v_all, s_all)


def kernel(x_prompt, x_sample, cache_k, cache_v, state_wkv, c, c_ctx, w_in, w_br, w_out, w_mod, b_mod, norm_g, mlp_up, mlp_down, rwkv_mu, rwkv_k_k, rwkv_k_a, rwkv_r_k, decay_w0, decay_up, iclr_a0, iclr_up, gate_up, gn_w, gn_b, q_gain, k_gain):
    n_ctx, seq_ctx, _ = x_prompt.shape
    n_dec, seq_dec, _ = x_sample.shape
    past = cache_k.shape[2]

    cvec8 = jnp.zeros((SUBLANES, D_MODEL), F32).at[0].set(c_ctx).at[1:1 + n_dec].set(c)
    mod = _modulation(cvec8, w_mod, b_mod)
    mod = jnp.pad(mod.reshape(DEPTH, SUBLANES, N_MOD, D_MODEL), ((0, 0), (0, 0), (0, SUBLANES - N_MOD), (0, 0)))

    o = np.cumsum((0, RWKV_WIDTH, RWKV_WIDTH, RWKV_WIDTH, DECAY_RANK, ICLR_RANK, GATE_RANK,
                   ATTN_WIDTH, KV_WIDTH, KV_WIDTH, 2 * D_MODEL))
    w_rest = jnp.concatenate([w_in[:, :, o[9]:o[10]], w_in[:, :, o[6]:o[7]], w_in[:, :, o[3]:o[6]],
                              w_in[:, :, o[7]:o[9]]], axis=-1)
    zpad = jnp.zeros((DEPTH, 2, DECAY_RANK, RWKV_WIDTH), F32)
    row = lambda a: a.reshape(DEPTH, 1, -1)
    tile2 = lambda a: jnp.tile(a, (1, LANES // HEAD_DIM)).reshape(DEPTH, 1, LANES)
    P = dict(
        mod=mod, ng=norm_g, w_rkv=w_in[:, :, o[0]:o[3]].astype(BF16), w_rest=w_rest.astype(BF16),
        mu=rwkv_mu, k_k=row(rwkv_k_k), k_a=row(rwkv_k_a), r_k=row(rwkv_r_k),
        w0=decay_w0, wup=jnp.concatenate([decay_up, zpad], axis=2).astype(BF16),
        a0=iclr_a0, aup=jnp.concatenate([zpad, iclr_up], axis=2).astype(BF16),
        gup=gate_up.astype(BF16), gn_w=row(gn_w), gn_b=row(gn_b),
        w_br=w_br.astype(BF16), w_out=w_out.astype(BF16), up=mlp_up.astype(BF16), down=mlp_down.astype(BF16),
        q_gain=tile2(q_gain), k_gain=tile2(k_gain),
    )

    x = x_prompt.reshape(n_ctx * seq_ctx, D_MODEL)
    carry = None
    for l in range(DEPTH):
        x, carry = _layer(x, n_ctx, seq_ctx, P, l, 0, 0, None, None, None, carry)
    y_prompt = x.reshape(n_ctx, seq_ctx, D_MODEL)
    new_k, new_v, new_s = carry
    cache_shape = (n_ctx, DEPTH, seq_ctx, N_KV_HEADS, HEAD_DIM)

    rope128 = _rope_tables(seq_dec)
    past_kv = (cache_k.reshape(n_dec, DEPTH, past, KV_WIDTH), cache_v.reshape(n_dec, DEPTH, past, KV_WIDTH))
    x = x_sample.reshape(n_dec * seq_dec, D_MODEL)
    for l in range(DEPTH):
        x, _ = _layer(x, n_dec, seq_dec, P, l, 1, seq_dec, rope128, past_kv, state_wkv[:, l], None)
    y_sample = x.reshape(n_dec, seq_dec, D_MODEL)

    return (y_prompt, y_sample, new_k.reshape(cache_shape), new_v.reshape(cache_shape), new_s)
```

```python
import functools

import numpy as np
import jax
import jax.numpy as jnp
from jax import lax
from jax.experimental import pallas as pl
from jax.experimental.pallas import tpu as pltpu

F32 = jnp.float32
BF16 = jnp.bfloat16

D_MODEL = 1024
DEPTH = 2
GRID_W = 64
HEAD_DIM = 64
N_RWKV_HEADS = 8
RWKV_WIDTH = N_RWKV_HEADS * HEAD_DIM
N_Q_HEADS = 8
N_KV_HEADS = 2
ATTN_WIDTH = N_Q_HEADS * HEAD_DIM
KV_WIDTH = N_KV_HEADS * HEAD_DIM
DECAY_RANK = 64
ICLR_RANK = 64
GATE_RANK = 128
D_FF = 4 * D_MODEL
ROPE_THETA = 10000.0
ROPE_FREQS = HEAD_DIM // 4
N_MOD = 6
NORM_EPS = 1e-6
GN_EPS = 64e-5
DECAY_SCALE = 0.606531
D_IN = 3 * RWKV_WIDTH + DECAY_RANK + ICLR_RANK + GATE_RANK + ATTN_WIDTH + 2 * KV_WIDTH + 2 * D_MODEL

LANES = 128
SUBLANES = 8
VMEM_LIMIT_BYTES = 56 * 1024 * 1024

RKV_W = 3 * RWKV_WIDTH
GATES_W = 2 * D_MODEL
COL_Q = 0
COL_LORA = COL_Q + ATTN_WIDTH
COL_KV = COL_LORA + 2 * LANES
REST_W = COL_KV + 2 * KV_WIDTH

WKV_CHUNK = 64
PAIR = 2 * HEAD_DIM
N_PAIRS = N_RWKV_HEADS // 2

NN = (((1,), (0,)), ((), ()))
NT = (((1,), (1,)), ((), ()))
TN = (((0,), (0,)), ((), ()))


def _mm(a, b, dims=NN):
    return lax.dot_general(a.astype(BF16), b.astype(BF16), dims, preferred_element_type=F32)


def _split2(x):
    hi = x.astype(BF16)
    lo = (x - hi.astype(F32)).astype(BF16)
    return hi, lo


def _div_pow2(i, n):
    return lax.shift_right_logical(i, int(np.log2(n)))


def _mod_pow2(i, n):
    return lax.bitwise_and(i, n - 1)


def _group_ones():
    i = lax.broadcasted_iota(jnp.int32, (LANES, LANES), 0)
    j = lax.broadcasted_iota(jnp.int32, (LANES, LANES), 1)
    return jnp.where(_div_pow2(i, HEAD_DIM) == _div_pow2(j, HEAD_DIM), 1.0, 0.0).astype(BF16)


def _group_sum(x, ones_bd):
    rows, n = x.shape[0], x.shape[1] // LANES
    parts = []
    for i in range(n):
        parts += list(_split2(x[:, i * LANES:(i + 1) * LANES]))
    r = lax.dot_general(jnp.concatenate(parts, axis=0), ones_bd, NN, preferred_element_type=F32)
    blocks = [r[2 * i * rows:(2 * i + 1) * rows] + r[(2 * i + 1) * rows:(2 * i + 2) * rows] for i in range(n)]
    return blocks[0] if n == 1 else jnp.concatenate(blocks, axis=1)


def _lspec(l, shape, single=False):
    kw = dict(pipeline_mode=pl.Buffered(1)) if single else {}
    return pl.BlockSpec((None,) + tuple(shape), lambda *g: (l,) + (0,) * len(shape), **kw)


def _mod_spec(l, row0, rows_per_mod, tm):
    if rows_per_mod:
        return pl.BlockSpec((None, 1, SUBLANES, D_MODEL), lambda i, *g: (l, row0 + i * tm // rows_per_mod, 0, 0))
    return pl.BlockSpec((None, 1, SUBLANES, D_MODEL), lambda *g: (l, row0, 0, 0))


def _rms(x, g):
    return x * lax.rsqrt(jnp.mean(x * x, axis=-1, keepdims=True) + NORM_EPS) * g


def _mod_kernel(c_ref, w_ref, b_ref, o_ref):
    c = c_ref[...]
    s = c * jax.nn.sigmoid(c)
    o_ref[0] = _mm(s, w_ref[0]) + b_ref[0]


def _modulation(cvec8, w_mod, b_mod):
    tn = D_MODEL
    n = N_MOD * D_MODEL
    return pl.pallas_call(
        _mod_kernel,
        grid=(DEPTH, n // tn),
        in_specs=[
            pl.BlockSpec((SUBLANES, D_MODEL), lambda l, j: (0, 0)),
            pl.BlockSpec((1, D_MODEL, tn), lambda l, j: (l, 0, j)),
            pl.BlockSpec((1, 1, tn), lambda l, j: (l, 0, j)),
        ],
        out_specs=pl.BlockSpec((1, SUBLANES, tn), lambda l, j: (l, 0, j)),
        out_shape=jax.ShapeDtypeStruct((DEPTH, SUBLANES, n), F32),
        name="modulation",
    )(cvec8, w_mod, b_mod.reshape(DEPTH, 1, n))


def _shift_mix(x, mu):
    n = x.shape[0]
    row = lax.broadcasted_iota(jnp.int32, x.shape, 0)
    x_prev = jnp.where(row == 0, 0.0, pltpu.roll(x, 1, 0))
    x_next = jnp.where(row == n - 1, 0.0, pltpu.roll(x, n - 1, 0))
    return x * (1.0 - mu) + (0.5 * mu) * (x_prev + x_next)


def _inproj_kernel(x_ref, mod_ref, ng_ref, wrkv_ref, wrest_ref, mu_ref, kk_ref,
                   rkv_ref, gates_ref, rest_ref, kh_ref, h_scr, *, tm, seq):
    C = RWKV_WIDTH

    def mix_in_place(i, piece):
        lanes = slice(piece * LANES, (piece + 1) * LANES)
        cols = slice(i * C + piece * LANES, i * C + (piece + 1) * LANES)
        for s in range(tm // seq):
            rows = slice(s * seq, (s + 1) * seq)
            mixed = _shift_mix(rkv_ref[rows, cols], mu_ref[i:i + 1, lanes])
            rkv_ref[rows, cols] = mixed
            if i == 1:
                kk = mixed * kk_ref[:, lanes]
                kh_ref[rows, lanes] = kk * lax.rsqrt(_group_sum(kk * kk, _group_ones()) + 1e-12)

    sh = mod_ref[0, 0:1, :]
    sc = mod_ref[0, 1:2, :]
    ng = ng_ref[0:1, :]
    sub = 256
    for i in range(tm // sub):
        x = x_ref[i * sub:(i + 1) * sub, :]
        h_scr[i * sub:(i + 1) * sub, :] = (_rms(x, ng) * (1.0 + sc) + sh).astype(BF16)
    rkv_ref[...] = lax.dot_general(h_scr[...], wrkv_ref[...], NN, preferred_element_type=F32)

    tn = 2 * LANES
    n_piece = C // LANES
    for chunk in range((GATES_W + REST_W) // tn):
        cols = slice(chunk * tn, (chunk + 1) * tn)
        out = lax.dot_general(h_scr[...], wrest_ref[:, cols], NN, preferred_element_type=F32)
        if chunk * tn < GATES_W:
            gates_ref[:, cols] = out.astype(BF16)
        else:
            rest_ref[:, chunk * tn - GATES_W:(chunk + 1) * tn - GATES_W] = out
        mix_in_place(chunk // n_piece, chunk % n_piece)


def _in_projection(x2d, P, l, mod_row0, rows_per_mod, seq):
    n_tok = x2d.shape[0]
    tm = 1024
    assert tm % seq == 0 and (GATES_W + REST_W) // (2 * LANES) == 3 * (RWKV_WIDTH // LANES)
    tok = lambda w: pl.BlockSpec((tm, w), lambda i: (i, 0))
    return pl.pallas_call(
        functools.partial(_inproj_kernel, tm=tm, seq=seq),
        grid=(n_tok // tm,),
        in_specs=[
            tok(D_MODEL),
            _mod_spec(l, mod_row0, rows_per_mod, tm),
            _lspec(l, (4, D_MODEL)),
            _lspec(l, (D_MODEL, RKV_W), single=True),
            _lspec(l, (D_MODEL, GATES_W + REST_W), single=True),
            _lspec(l, (3, RWKV_WIDTH)),
            _lspec(l, (1, RWKV_WIDTH)),
        ],
        out_specs=[tok(RKV_W), tok(GATES_W), tok(REST_W), tok(RWKV_WIDTH)],
        out_shape=[jax.ShapeDtypeStruct((n_tok, RKV_W), F32),
                   jax.ShapeDtypeStruct((n_tok, GATES_W), BF16),
                   jax.ShapeDtypeStruct((n_tok, REST_W), F32),
                   jax.ShapeDtypeStruct((n_tok, RWKV_WIDTH), F32)],
        scratch_shapes=[pltpu.VMEM((tm, D_MODEL), BF16)],
        compiler_params=pltpu.CompilerParams(
            dimension_semantics=("parallel",), vmem_limit_bytes=VMEM_LIMIT_BYTES),
        name="in_projection",
    )(x2d, P['mod'], P['ng'], P['w_rkv'], P['w_rest'], P['mu'], P['k_k'])


def _wkv_masks(reverse):
    L = WKV_CHUNK
    row = lax.broadcasted_iota(jnp.int32, (L, PAIR), 0)
    col = _mod_pow2(lax.broadcasted_iota(jnp.int32, (L, PAIR), 1), L)
    return dict(
        strict=(col > row) if reverse else (col < row),
        incl=(col >= row) if reverse else (col <= row),
        blk8=_div_pow2(row, 8) == _div_pow2(col, 8),
        eye=row == col,
    )


def _bd(x, left):
    x = x.astype(BF16)
    z = jnp.zeros_like(x)
    return jnp.concatenate([jnp.where(left, x, z), jnp.where(left, z, x)], axis=0)


def _wkv_operand_scratch(nb):
    C, L = RWKV_WIDTH, WKV_CHUNK
    full = lambda dt: pltpu.VMEM((nb, 2, L, C), dt)
    pair = lambda rows: pltpu.VMEM((nb, 2, N_PAIRS, rows, PAIR), BF16)
    return dict(khd=full(F32), rd=full(F32), bonus=full(F32), etot=pltpu.VMEM((nb, 2, SUBLANES, C), F32),
                be=full(BF16), kte=full(BF16), vb=full(BF16),
                lhsA=pair(2 * L), rhsA=pair(4 * L), vbd=pair(2 * L), khdbd=pair(2 * L))


def _wkv_lora(nb, lors, rkvs, w0_ref, wup_ref, a0_ref, aup_ref, ka_ref, rk_ref, ops, ones_bd, lora_out):
    C, L = RWKV_WIDTH, WKV_CHUNK
    for d, (lor, rkv) in enumerate(zip(lors, rkvs)):
        rows = lambda lo_col, hi_col: jnp.concatenate([rkv[e, 0, :, lo_col:hi_col] for e in range(nb)], axis=0)
        lo = jnp.concatenate([lor[e, 0, :, 0:LANES] for e in range(nb)], axis=0)
        w_raw = _mm(jnp.tanh(lo), wup_ref[d]) + w0_ref[d:d + 1, :]
        a = jax.nn.sigmoid(_mm(lo, aup_ref[d]) + a0_ref[d:d + 1, :])
        kt = rows(C, 2 * C) * (1.0 + (a - 1.0) * ka_ref[...])
        bonus = _group_sum(rows(0, C) * kt * rk_ref[...], ones_bd) * rows(2 * C, 3 * C)
        for e in range(nb):
            sl = slice(e * L, (e + 1) * L)
            ops['bonus'][e, d] = bonus[sl]
            lora_out[e, d] = (w_raw[sl], a[sl], kt[sl])


def _wkv_prepare(e, d, rkv, khr, lora, ops, tri, left):
    C, L = RWKV_WIDTH, WKV_CHUNK
    reverse = d == 1
    r = rkv[e, 0, :, 0:C]
    v = rkv[e, 0, :, 2 * C:3 * C]
    kh = khr[e, 0]
    w_raw, a, kt = lora[e, d]
    lw = -DECAY_SCALE * jax.nn.sigmoid(w_raw)
    b = a * kh

    hi, lo2 = _split2(lw)
    cum = (lax.dot_general(tri[d], hi, NN, preferred_element_type=F32)
           + lax.dot_general(tri[d], lo2, NN, preferred_element_type=F32))
    tot = cum[0:1, :] if reverse else cum[L - 1:L, :]
    e_inv = jnp.exp(-cum)
    e_end = jnp.exp(tot - cum)
    khd = kh * jnp.exp(cum - lw)
    rd = r * jnp.exp(cum)
    bi = b * e_inv
    ki = kt * e_inv
    ops['khd'][e, d] = khd
    ops['rd'][e, d] = rd
    ops['etot'][e, d] = jnp.broadcast_to(jnp.exp(tot), (SUBLANES, C))
    ops['be'][e, d] = (b * e_end).astype(BF16)
    ops['kte'][e, d] = (kt * e_end).astype(BF16)
    ops['vb'][e, d] = v.astype(BF16)
    for p in range(N_PAIRS):
        sl = slice(p * PAIR, (p + 1) * PAIR)
        ops['lhsA'][e, d, p] = jnp.concatenate([khd[:, sl], rd[:, sl]], axis=0).astype(BF16)
        ops['rhsA'][e, d, p] = jnp.concatenate([_bd(bi[:, sl], left), _bd(ki[:, sl], left)], axis=0)
        ops['vbd'][e, d, p] = _bd(v[:, sl], left)
        ops['khdbd'][e, d, p] = _bd(khd[:, sl], left)


def _wkv_chains(chains, left, fillers):
    L = WKV_CHUNK
    i2 = lax.broadcasted_iota(jnp.int32, (PAIR, PAIR), 0)
    j2 = lax.broadcasted_iota(jnp.int32, (PAIR, PAIR), 1)
    same = _div_pow2(i2, HEAD_DIM) == _div_pow2(j2, HEAD_DIM)
    eye2 = i2 == j2
    bd = lambda x: _bd(x, left)
    lp = lambda g: jnp.where(left, g[:L], g[L:])
    n_stages = 13
    due = {((k + 1) * n_stages) // (len(fillers) + 1): f for k, f in enumerate(fillers)}
    assert len(due) == len(fillers) and min(due) >= 1 and max(due) <= n_stages, "every filler needs its own stage"
    emitted = [0]

    def each(f):
        for ch in chains:
            f(ch)
        emitted[0] += 1
        if emitted[0] in due:
            due[emitted[0]]()

    def a_blocks(ch):
        m = ch['masks']
        A = lax.dot_general(ch['lhsA'], ch['rhsA'], NT, preferred_element_type=F32)
        ch['A_ub'] = jnp.where(m['strict'], A[:L, :PAIR], 0.0)
        ch['A_uvrv'] = jnp.concatenate([jnp.where(m['strict'], A[:L, PAIR:], 0.0),
                                        jnp.where(m['incl'], A[L:, PAIR:], 0.0)], axis=0).astype(BF16)
        ch['A_rb'] = jnp.where(m['incl'], A[L:, :PAIR], 0.0).astype(BF16)
        a8 = jnp.where(m['blk8'], ch['A_ub'], 0.0)
        ch['A8'] = a8
        ch['X0'] = jnp.where(m['eye'], 1.0, -a8)
        ch['Nlow'] = ch['A_ub'] - a8
    each(a_blocks)

    stack = lambda a, b: jnp.concatenate([a.astype(BF16), b.astype(BF16)], axis=0)

    def sq1(ch):
        ch['A8_2'] = _mm(ch['A8'], bd(ch['A8']))
    each(sq1)

    def sq2(ch):
        r = _mm(stack(ch['A8_2'], ch['X0']), bd(ch['A8_2']))
        ch['A8_4'] = r[:L]
        ch['X1'] = ch['X0'] + r[L:]
        ch['AV'] = _mm(ch['A_uvrv'], ch['vbd'])
    each(sq2)

    def dinv(ch):
        ch['T8'] = ch['X1'] + _mm(ch['X1'], bd(ch['A8_4']))
    each(dinv)

    def e1(ch):
        ch['E'] = _mm(ch['T8'], bd(ch['Nlow']))
    each(e1)

    def e2(ch):
        ch['E2'] = _mm(ch['E'], bd(ch['E']))
    each(e2)

    def e3(ch):
        ime = jnp.where(ch['masks']['eye'], 1.0, 0.0) - ch['E']
        r = _mm(stack(ch['E2'], ime), bd(ch['E2']))
        ch['E4'] = r[:L]
        ch['Y1'] = ime + r[L:]
    each(e3)

    def e4(ch):
        ch['Y2'] = ch['Y1'] + _mm(ch['Y1'], bd(ch['E4']))
    each(e4)

    def t_stage(ch):
        ch['T'] = _mm(ch['Y2'], bd(ch['T8']))
    each(t_stage)

    def p_stage(ch):
        ch['P'] = _mm(ch['A_rb'], bd(ch['T']))
    each(p_stage)

    def q_stage(ch):
        rhs = jnp.concatenate([ch['khdbd'], bd(ch['AV'][:L])], axis=1)
        r = _mm(stack(ch['T'], ch['P']), rhs)
        ch['Q'] = r[:L].astype(BF16)
        ch['G1'] = ch['rd'] - r[L:, :PAIR]
        ch['G2'] = ch['AV'][L:] - r[L:, PAIR:]
    each(q_stage)

    def mn_stage(ch):
        QB = _mm(ch['Q'], ch['be'], TN)
        ch['M'] = (jnp.where(eye2, ch['etot'], 0.0) - jnp.where(same, QB[:PAIR], 0.0)).astype(BF16)
        ch['N'] = lp(_mm(ch['vb'], ch['kte'], TN)) - lp(QB[PAIR:])
    each(mn_stage)

    def out_stage(ch):
        ch['y'] = _mm(ch['G1'], bd(ch['S']), NT) + ch['G2']
        ch['Sn'] = _mm(ch['S'], ch['M']) + ch['N']
    each(out_stage)


def _wkv_kernel(rkv0, khr0, lor0, rkv1, khr1, lor1,
                ka_ref, rk_ref, w0_ref, wup_ref, a0_ref, aup_ref, s0_ref,
                *rest, zero_state, nb, nc, op_names):
    n_ops = len(op_names)
    y0_ref, y1_ref, bo0_ref, bo1_ref, sfin_ref, s_scr = rest[-(6 + 2 * n_ops):len(rest) - 2 * n_ops]
    op_sets = [dict(zip(op_names, rest[len(rest) - (2 - i) * n_ops:len(rest) - (1 - i) * n_ops])) for i in (0, 1)]
    t = pl.program_id(0)
    c = lax.rem(jnp.maximum(t - 1, 0), nc)

    @pl.when(t == 0)
    def _():
        for ref in op_sets[1].values():
            ref[...] = jnp.zeros(ref.shape, ref.dtype)

    @pl.when(c == 0)
    def _():
        if zero_state:
            s_scr[...] = jnp.zeros(s_scr.shape, F32)
        else:
            s_scr[...] = s0_ref[...]

    def step(ops_in, ops_out):
        L = WKV_CHUNK
        left = lax.broadcasted_iota(jnp.int32, (L, PAIR), 1) < HEAD_DIM
        masks = [_wkv_masks(False), _wkv_masks(True)]

        chains = []
        for e in range(nb):
            for d, (y_ref, bo_ref) in enumerate(((y0_ref, bo0_ref), (y1_ref, bo1_ref))):
                bo_ref[e, 0] = ops_in['bonus'][e, d].astype(BF16)
                for p in range(N_PAIRS):
                    sl = slice(p * PAIR, (p + 1) * PAIR)
                    chains.append(dict(
                        e=e, d=d, p=p, sl=sl, y_ref=y_ref, masks=masks[d], S=s_scr[e, d, p],
                        lhsA=ops_in['lhsA'][e, d, p], rhsA=ops_in['rhsA'][e, d, p], vbd=ops_in['vbd'][e, d, p],
                        khdbd=ops_in['khdbd'][e, d, p], rd=ops_in['rd'][e, d, :, sl], be=ops_in['be'][e, d, :, sl],
                        kte=ops_in['kte'][e, d, :, sl], vb=ops_in['vb'][e, d, :, sl],
                        etot=ops_in['etot'][e, d, 0:1, sl]))
        ones_bd = _group_ones()
        ti = lax.broadcasted_iota(jnp.int32, (L, L), 0)
        tj = lax.broadcasted_iota(jnp.int32, (L, L), 1)
        tri = [jnp.where(tj <= ti, 1.0, 0.0).astype(BF16), jnp.where(tj >= ti, 1.0, 0.0).astype(BF16)]
        lora = {}
        prepare = [functools.partial(_wkv_lora, nb, (lor0, lor1), (rkv0, rkv1), w0_ref, wup_ref, a0_ref, aup_ref,
                                     ka_ref, rk_ref, ops_out, ones_bd, lora)] + [
            functools.partial(_wkv_prepare, e, d, rkv, khr, lora, ops_out, tri, left)
            for e in range(nb) for d, (rkv, khr) in enumerate(((rkv0, khr0), (rkv1, khr1)))]
        _wkv_chains(chains, left, prepare)
        for ch in chains:
            ch['y_ref'][ch['e'], 0, :, ch['sl']] = ch['y'].astype(BF16)
            s_scr[ch['e'], ch['d'], ch['p']] = ch['Sn']

    parity = lax.rem(t, 2)

    @pl.when(parity == 0)
    def _():
        step(op_sets[1], op_sets[0])

    @pl.when(parity == 1)
    def _():
        step(op_sets[0], op_sets[1])

    @pl.when((c == nc - 1) & (t > 0))
    def _():
        for e in range(nb):
            for d in (0, 1):
                for p in range(N_PAIRS):
                    s2 = s_scr[e, d, p]
                    sfin_ref[e, d, 2 * p] = s2[:, :HEAD_DIM]
                    sfin_ref[e, d, 2 * p + 1] = pltpu.roll(s2, HEAD_DIM, 1)[:, :HEAD_DIM]


def _wkv(rkv, rest, kh, s0_pairs, n_batch, seq, P, l, s_all):
    L = WKV_CHUNK
    nc = seq // L
    nb = 4
    n_steps = (n_batch // nb) * nc
    n_tok = n_batch * seq
    zero_state = s0_pairs is None
    if zero_state:
        s0_pairs = jnp.zeros((nb, 2, N_PAIRS, HEAD_DIM, PAIR), F32)
    rkv4 = rkv.reshape(n_batch, nc, L, RKV_W)
    rest4 = rest.reshape(n_batch, nc, L, REST_W)
    kh4 = kh.reshape(n_batch, nc, L, RWKV_WIDTH)

    def in_pos(t, d):
        s = jnp.minimum(t, n_steps - 1)
        c = lax.rem(s, nc)
        return lax.div(s, nc), (c if d == 0 else nc - 1 - c)

    def out_pos(t, d):
        s = jnp.maximum(t - 1, 0)
        c = lax.rem(s, nc)
        return lax.div(s, nc), (c if d == 0 else nc - 1 - c)

    in_specs = []
    for d in (0, 1):
        in_specs += [
            pl.BlockSpec((nb, 1, L, RKV_W), lambda t, d=d: (*in_pos(t, d), 0, 0)),
            pl.BlockSpec((nb, 1, L, RWKV_WIDTH), lambda t, d=d: (*in_pos(t, d), 0, 0)),
            pl.BlockSpec((nb, 1, L, 2 * LANES), lambda t, d=d: (*in_pos(t, d), 0, COL_LORA // (2 * LANES))),
        ]
    state_spec = lambda idx: pl.BlockSpec((nb, 2, N_PAIRS, HEAD_DIM, PAIR), idx)
    in_specs += [
        _lspec(l, (1, RWKV_WIDTH)), _lspec(l, (1, RWKV_WIDTH)),
        _lspec(l, (2, RWKV_WIDTH)), _lspec(l, (2, LANES, RWKV_WIDTH)),
        _lspec(l, (2, RWKV_WIDTH)), _lspec(l, (2, LANES, RWKV_WIDTH)),
        state_spec((lambda t: (0, 0, 0, 0, 0)) if zero_state else (lambda t: (out_pos(t, 0)[0], 0, 0, 0, 0))),
    ]
    args = [rkv4, kh4, rest4, rkv4, kh4, rest4, P['k_a'], P['r_k'], P['w0'], P['wup'], P['a0'], P['aup'], s0_pairs]
    aliases = {}
    if s_all is not None:
        in_specs.append(pl.BlockSpec(memory_space=pl.ANY))
        aliases = {len(args): 4}
        args.append(s_all)
    tok_spec = lambda d: pl.BlockSpec((nb, 1, L, RWKV_WIDTH), lambda t, d=d: (*out_pos(t, d), 0, 0))
    sfin_spec = pl.BlockSpec((nb, None, 2, N_RWKV_HEADS, HEAD_DIM, HEAD_DIM),
                             lambda t: (out_pos(t, 0)[0], l, 0, 0, 0, 0))
    out_specs = [tok_spec(0), tok_spec(1), tok_spec(0), tok_spec(1), sfin_spec]
    tok = jax.ShapeDtypeStruct((n_batch, nc, L, RWKV_WIDTH), BF16)
    op_scratch = _wkv_operand_scratch(nb)
    outs = pl.pallas_call(
        functools.partial(_wkv_kernel, zero_state=zero_state, nb=nb, nc=nc, op_names=tuple(op_scratch)),
        grid=(n_steps + 1,),
        in_specs=in_specs,
        out_specs=out_specs,
        out_shape=[tok, tok, tok, tok,
                   jax.ShapeDtypeStruct((n_batch, DEPTH, 2, N_RWKV_HEADS, HEAD_DIM, HEAD_DIM), F32)],
        input_output_aliases=aliases,
        scratch_shapes=([pltpu.VMEM((nb, 2, N_PAIRS, HEAD_DIM, PAIR), F32)]
                        + list(op_scratch.values()) + list(_wkv_operand_scratch(nb).values())),
        compiler_params=pltpu.CompilerParams(
            dimension_semantics=("arbitrary",), vmem_limit_bytes=VMEM_LIMIT_BYTES),
        name="wkv_scan",
    )(*args)
    y0, y1, bo0, bo1, s_fin = outs
    flat = lambda t: t.reshape(n_tok, RWKV_WIDTH)
    return flat(y0), flat(y1), flat(bo0), flat(bo1), s_fin


def _rope(x, cos, sin_signed):
    w = x.shape[1]
    lane = lax.broadcasted_iota(jnp.int32, x.shape, 1)
    partner = jnp.where(_mod_pow2(lane, 2 * ROPE_FREQS) < ROPE_FREQS,
                        pltpu.roll(x, w - ROPE_FREQS, 1), pltpu.roll(x, ROPE_FREQS, 1))
    return x * cos + partner * sin_signed


def _attn_kernel(*refs, tq, seq, past, use_rope, ns):
    it = iter(refs)
    q_ref, kv_ref, qg_ref, kg_ref = next(it), next(it), next(it), next(it)
    if use_rope:
        cq_ref, sq_ref, ck_ref, sk_ref = next(it), next(it), next(it), next(it)
    if past:
        pk_ref, pv_ref = next(it), next(it)
    o_ref, ko_ref, vo_ref, kvar_scr, vvar_scr = refs[-5:]

    ones_bd = _group_ones()

    @pl.when(pl.program_id(1) == 0)
    def _():
        for sq in range(ns):
            kv = kv_ref[sq * seq:(sq + 1) * seq, :]
            k_raw = kv[:, :KV_WIDTH]
            kn = k_raw * lax.rsqrt(_group_sum(k_raw * k_raw, ones_bd) * (1.0 / HEAD_DIM) + NORM_EPS) * kg_ref[...]
            if use_rope:
                kn = _rope(kn, ck_ref[...], sk_ref[...])
            vn = kv[:, KV_WIDTH:]
            ko_ref[sq] = kn
            vo_ref[sq] = vn
            pieces = [(past, seq, kn, vn)]
            if past:
                pieces.append((0, past, pk_ref[sq], pv_ref[sq]))
            for start, n, kx, vx in pieces:
                left = lax.broadcasted_iota(jnp.int32, kx.shape, 1) < HEAD_DIM
                for x, scr in ((kx, kvar_scr), (vx, vvar_scr)):
                    sw = pltpu.roll(x, HEAD_DIM, 1)
                    scr[4 * sq + 0, start:start + n, :] = jnp.where(left, x, 0.0).astype(BF16)
                    scr[4 * sq + 1, start:start + n, :] = jnp.where(left, 0.0, sw).astype(BF16)
                    scr[4 * sq + 2, start:start + n, :] = jnp.where(left, sw, 0.0).astype(BF16)
                    scr[4 * sq + 3, start:start + n, :] = jnp.where(left, 0.0, x).astype(BF16)

    scale = HEAD_DIM ** -0.5
    n_blk = ATTN_WIDTH // LANES
    blocks = [(sq, jb) for sq in range(ns) for jb in range(n_blk)]
    qn = []
    for sq, jb in blocks:
        qb = q_ref[sq * tq:(sq + 1) * tq, jb * LANES:(jb + 1) * LANES]
        x = qb * lax.rsqrt(_group_sum(qb * qb, ones_bd) * (1.0 / HEAD_DIM) + NORM_EPS) * qg_ref[...]
        if use_rope:
            x = _rope(x, cq_ref[...], sq_ref[...])
        qn.append((x * scale).astype(BF16))
    heads = [(i, 4 * sq + 2 * (jb // (n_blk // N_KV_HEADS)) + side)
             for i, (sq, jb) in enumerate(blocks) for side in (0, 1)]
    scores = [lax.dot_general(qn[i], kvar_scr[var], NT, preferred_element_type=F32) for i, var in heads]
    exps = [jnp.exp(s - jnp.max(s, axis=-1, keepdims=True)) for s in scores]
    outs = [lax.dot_general(e.astype(BF16), vvar_scr[var], NN, preferred_element_type=F32)
            for e, (i, var) in zip(exps, heads)]
    outs = [o / jnp.sum(e, axis=-1, keepdims=True) for o, e in zip(outs, exps)]
    for i, (sq, jb) in enumerate(blocks):
        o_ref[sq * tq:(sq + 1) * tq, jb * LANES:(jb + 1) * LANES] = (outs[2 * i] + outs[2 * i + 1]).astype(BF16)


def _attention(rest, n_batch, seq, P, l, rope128, past_kv, kv_all):
    tq = min(seq, 512)
    nq = seq // tq
    ns = 4 if nq == 1 else 1
    n_tok = n_batch * seq
    use_rope = rope128 is not None
    past = 0 if past_kv is None else past_kv[0].shape[2]
    in_specs = [
        pl.BlockSpec((ns * tq, ATTN_WIDTH), lambda b, i: (b * nq + i, COL_Q // ATTN_WIDTH)),
        pl.BlockSpec((ns * seq, 2 * KV_WIDTH), lambda b, i: (b, COL_KV // (2 * KV_WIDTH))),
        _lspec(l, (1, LANES)),
        _lspec(l, (1, LANES)),
    ]
    args = [rest, rest, P['q_gain'], P['k_gain']]
    if use_rope:
        cos, sin = rope128
        in_specs += [pl.BlockSpec((tq, LANES), lambda b, i: (i, 0)),
                     pl.BlockSpec((tq, LANES), lambda b, i: (i, 0)),
                     pl.BlockSpec((seq, LANES), lambda b, i: (0, 0)),
                     pl.BlockSpec((seq, LANES), lambda b, i: (0, 0))]
        args += [cos, sin, cos, sin]
    if past:
        in_specs += [pl.BlockSpec((ns, None, past, KV_WIDTH), lambda b, i: (b, l, 0, 0)),
                     pl.BlockSpec((ns, None, past, KV_WIDTH), lambda b, i: (b, l, 0, 0))]
        args += list(past_kv)
    aliases = {}
    if kv_all is not None:
        in_specs += [pl.BlockSpec(memory_space=pl.ANY), pl.BlockSpec(memory_space=pl.ANY)]
        aliases = {len(args): 1, len(args) + 1: 2}
        args += list(kv_all)
    cache_spec = pl.BlockSpec((ns, None, seq, KV_WIDTH), lambda b, i: (b, l, 0, 0))
    out_specs = [pl.BlockSpec((ns * tq, ATTN_WIDTH), lambda b, i: (b * nq + i, 0)), cache_spec, cache_spec]
    return pl.pallas_call(
        functools.partial(_attn_kernel, tq=tq, seq=seq, past=past, use_rope=use_rope, ns=ns),
        grid=(n_batch // ns, nq),
        in_specs=in_specs,
        out_specs=out_specs,
        out_shape=[jax.ShapeDtypeStruct((n_tok, ATTN_WIDTH), BF16),
                   jax.ShapeDtypeStruct((n_batch, DEPTH, seq, KV_WIDTH), F32),
                   jax.ShapeDtypeStruct((n_batch, DEPTH, seq, KV_WIDTH), F32)],
        input_output_aliases=aliases,
        scratch_shapes=[pltpu.VMEM((ns * 2 * N_KV_HEADS, past + seq, KV_WIDTH), BF16),
                        pltpu.VMEM((ns * 2 * N_KV_HEADS, past + seq, KV_WIDTH), BF16)],
        compiler_params=pltpu.CompilerParams(
            dimension_semantics=("parallel", "arbitrary"), vmem_limit_bytes=VMEM_LIMIT_BYTES),
        name="attention",
    )(*args)


def _post_mlp_kernel(y0_ref, y1_ref, bo0_ref, bo1_ref, lor_ref, oa_ref, gr_ref, ga_ref, x_ref, mod_ref,
                     gup_ref, gnw_ref, gnb_ref, wbr_ref, wout_ref, ng_ref, up_ref, down_ref, o_ref):
    ones_bd = _group_ones()
    y = y0_ref[...].astype(F32) + y1_ref[...].astype(F32)
    mean = _group_sum(y, ones_bd) * (1.0 / HEAD_DIM)
    yc = y - mean
    var = _group_sum(yc * yc, ones_bd) * (1.0 / HEAD_DIM)
    yn = yc * lax.rsqrt(var + GN_EPS) * gnw_ref[...] + gnb_ref[...]
    lg = lor_ref[:, LANES:]
    g = _mm(jax.nn.sigmoid(lg), gup_ref[...])
    o_r = (yn + (bo0_ref[...].astype(F32) + bo1_ref[...].astype(F32))) * g
    merged = (jax.nn.sigmoid(gr_ref[...].astype(F32)) * _mm(o_r, wbr_ref[0])
              + jax.nn.sigmoid(ga_ref[...].astype(F32)) * _mm(oa_ref[...], wbr_ref[1]))
    m = _mm(merged, wout_ref[...])
    g1 = mod_ref[0, 2:3, :]
    sh2 = mod_ref[0, 3:4, :]
    sc2 = mod_ref[0, 4:5, :]
    x1 = x_ref[...] + g1 * _rms(m, ng_ref[1:2, :])
    h = (_rms(x1, ng_ref[2:3, :]) * (1.0 + sc2) + sh2).astype(BF16)

    f = None
    ff = D_MODEL
    for j in range(D_FF // ff):
        u = lax.dot_general(h, up_ref[:, j * ff:(j + 1) * ff], NN, preferred_element_type=F32)
        u = jnp.square(jnp.maximum(u, 0.0)).astype(BF16)
        part = lax.dot_general(u, down_ref[j * ff:(j + 1) * ff, :], NN, preferred_element_type=F32)
        f = part if f is None else f + part
    g2 = mod_ref[0, 5:6, :]
    o_ref[...] = x1 + g2 * _rms(f, ng_ref[3:4, :])


def _post_mlp(y0, y1, bo0, bo1, rest, gates, o_a, x2d, P, l, mod_row0, rows_per_mod):
    n_tok = x2d.shape[0]
    tm = 512
    tok = lambda w: pl.BlockSpec((tm, w), lambda i: (i, 0))
    return pl.pallas_call(
        _post_mlp_kernel,
        grid=(n_tok // tm,),
        in_specs=[
            tok(RWKV_WIDTH), tok(RWKV_WIDTH), tok(RWKV_WIDTH), tok(RWKV_WIDTH),
            pl.BlockSpec((tm, 2 * LANES), lambda i: (i, COL_LORA // (2 * LANES))),
            tok(ATTN_WIDTH),
            pl.BlockSpec((tm, D_MODEL), lambda i: (i, 0)),
            pl.BlockSpec((tm, D_MODEL), lambda i: (i, 1)),
            tok(D_MODEL),
            _mod_spec(l, mod_row0, rows_per_mod, tm),
            _lspec(l, (GATE_RANK, RWKV_WIDTH)), _lspec(l, (1, RWKV_WIDTH)), _lspec(l, (1, RWKV_WIDTH)),
            _lspec(l, (2, RWKV_WIDTH, D_MODEL), single=True), _lspec(l, (D_MODEL, D_MODEL), single=True),
            _lspec(l, (4, D_MODEL)),
            _lspec(l, (D_MODEL, D_FF), single=True), _lspec(l, (D_FF, D_MODEL), single=True),
        ],
        out_specs=tok(D_MODEL),
        out_shape=jax.ShapeDtypeStruct((n_tok, D_MODEL), F32),
        compiler_params=pltpu.CompilerParams(
            dimension_semantics=("parallel",), vmem_limit_bytes=VMEM_LIMIT_BYTES),
        name="merge_mlp",
    )(y0, y1, bo0, bo1, rest, o_a, gates, gates, x2d, P['mod'], P['gup'], P['gn_w'], P['gn_b'], P['w_br'], P['w_out'],
      P['ng'], P['up'], P['down'])


def _rope_tables(seq):
    n_rows = seq // GRID_W
    rows = np.repeat(np.arange(n_rows, dtype=np.float32), GRID_W)
    cols = np.tile(np.arange(GRID_W, dtype=np.float32), n_rows)
    half = HEAD_DIM // 2
    freqs = 1.0 / (jnp.asarray(ROPE_THETA, F32) ** (jnp.arange(0, half, 2, dtype=F32) / half))
    ang_r = jnp.asarray(rows)[:, None] * freqs
    ang_c = jnp.asarray(cols)[:, None] * freqs
    cr, sr, cc, sc = jnp.cos(ang_r), jnp.sin(ang_r), jnp.cos(ang_c), jnp.sin(ang_c)
    cos64 = jnp.concatenate([cr, cr, cc, cc], axis=1)
    sin64 = jnp.concatenate([-sr, sr, -sc, sc], axis=1)
    return jnp.tile(cos64, (1, LANES // HEAD_DIM)), jnp.tile(sin64, (1, LANES // HEAD_DIM))


def _pairs_from_state(s):
    b = s.shape[0]
    s = s.reshape(b, 2, N_PAIRS, 2, HEAD_DIM, HEAD_DIM)
    return jnp.concatenate([s[:, :, :, 0], s[:, :, :, 1]], axis=-1)


def _layer(x2d, n_batch, seq, P, l, mod_row0, rows_per_mod, rope128, past_kv, s0, carry):
    kv_all, s_all = (None, None) if carry is None else (carry[:2], carry[2])
    rkv, gates, rest, kh = _in_projection(x2d, P, l, mod_row0, rows_per_mod, seq)
    s0_pairs = None if s0 is None else _pairs_from_state(s0)
    y0, y1, bo0, bo1, s_all = _wkv(rkv, rest, kh, s0_pairs, n_batch, seq, P, l, s_all)
    o_a, k_all, v_all = _attention(rest, n_batch, seq, P, l, rope128, past_kv, kv_all)
    x2 = _post_mlp(y0, y1, bo0, bo1, rest, gates, o_a, x2d, P, l, mod_row0, rows_per_mod)
    return x2, (k_all, v_all, s_all)


def kernel(x_prompt, x_sample, cache_k, cache_v, state_wkv, c, c_ctx, w_in, w_br, w_out, w_mod, b_mod, norm_g, mlp_up, mlp_down, rwkv_mu, rwkv_k_k, rwkv_k_a, rwkv_r_k, decay_w0, decay_up, iclr_a0, iclr_up, gate_up, gn_w, gn_b, q_gain, k_gain):
    n_ctx, seq_ctx, _ = x_prompt.shape
    n_dec, seq_dec, _ = x_sample.shape
    past = cache_k.shape[2]

    cvec8 = jnp.zeros((SUBLANES, D_MODEL), F32).at[0].set(c_ctx).at[1:1 + n_dec].set(c)
    mod = _modulation(cvec8, w_mod, b_mod)
    mod = jnp.pad(mod.reshape(DEPTH, SUBLANES, N_MOD, D_MODEL), ((0, 0), (0, 0), (0, SUBLANES - N_MOD), (0, 0)))

    o = np.cumsum((0, RWKV_WIDTH, RWKV_WIDTH, RWKV_WIDTH, DECAY_RANK, ICLR_RANK, GATE_RANK,
                   ATTN_WIDTH, KV_WIDTH, KV_WIDTH, 2 * D_MODEL))
    w_rest = jnp.concatenate([w_in[:, :, o[9]:o[10]], w_in[:, :, o[6]:o[7]], w_in[:, :, o[3]:o[6]],
                              w_in[:, :, o[7]:o[9]]], axis=-1)
    zpad = jnp.zeros((DEPTH, 2, DECAY_RANK, RWKV_WIDTH), F32)
    row = lambda a: a.reshape(DEPTH, 1, -1)
    tile2 = lambda a: jnp.tile(a, (1, LANES // HEAD_DIM)).reshape(DEPTH, 1, LANES)
    P = dict(
        mod=mod, ng=norm_g, w_rkv=w_in[:, :, o[0]:o[3]].astype(BF16), w_rest=w_rest.astype(BF16),
        mu=rwkv_mu, k_k=row(rwkv_k_k), k_a=row(rwkv_k_a), r_k=row(rwkv_r_k),
        w0=decay_w0, wup=jnp.concatenate([decay_up, zpad], axis=2).astype(BF16),
        a0=iclr_a0, aup=jnp.concatenate([zpad, iclr_up], axis=2).astype(BF16),
        gup=gate_up.astype(BF16), gn_w=row(gn_w), gn_b=row(gn_b),
        w_br=w_br.astype(BF16), w_out=w_out.astype(BF16), up=mlp_up.astype(BF16), down=mlp_down.astype(BF16),
        q_gain=tile2(q_gain), k_gain=tile2(k_gain),
    )

    x = x_prompt.reshape(n_ctx * seq_ctx, D_MODEL)
    carry = None
    for l in range(DEPTH):
        x, carry = _layer(x, n_ctx, seq_ctx, P, l, 0, 0, None, None, None, carry)
    y_prompt = x.reshape(n_ctx, seq_ctx, D_MODEL)
    new_k, new_v, new_s = carry
    cache_shape = (n_ctx, DEPTH, seq_ctx, N_KV_HEADS, HEAD_DIM)

    rope128 = _rope_tables(seq_dec)
    past_kv = (cache_k.reshape(n_dec, DEPTH, past, KV_WIDTH), cache_v.reshape(n_dec, DEPTH, past, KV_WIDTH))
    x = x_sample.reshape(n_dec * seq_dec, D_MODEL)
    for l in range(DEPTH):
        x, _ = _layer(x, n_dec, seq_dec, P, l, 1, seq_dec, rope128, past_kv, state_wkv[:, l], None)
    y_sample = x.reshape(n_dec, seq_dec, D_MODEL)

    return (y_prompt, y_sample, new_k.reshape(cache_shape), new_v.reshape(cache_shape), new_s)
```
